```python
import jax
import jax.numpy as jnp
from jax import lax
import numpy as np

D_MODEL = 1024
BATCH = 16
SEQ = 4096
DEPTH = 2

CTX_LEN = 256
GRID_W = 64
HEAD_DIM = 64
A_HEADS = 8
A_KV_HEADS = 2
A_GROUP = A_HEADS // A_KV_HEADS
B_HEADS = 4
B_Q_RANK = 192
B_KV_RANK = 128
B_NOPE = 64
B_ROPE = 32
B_V = 64
C_HEADS = 4
C_DK = 64
C_DV = 64
D_FF = 4 * D_MODEL
A_OUT = A_HEADS * HEAD_DIM
B_OUT = B_HEADS * B_V
C_OUT = C_HEADS * C_DV
D_MIX = A_OUT + B_OUT + C_OUT
IN_SIZES = (A_HEADS * HEAD_DIM, A_KV_HEADS * HEAD_DIM, A_KV_HEADS * HEAD_DIM,
            B_Q_RANK, B_KV_RANK, B_ROPE,
            C_HEADS * C_DK, C_HEADS * C_DK, C_HEADS * C_DK, C_HEADS * C_DV, C_HEADS * C_DV)
D_IN = sum(IN_SIZES)
SPLIT_AT = tuple(int(v) for v in np.cumsum(IN_SIZES)[:-1])
Q_BLOCK = 128
SCAN_CHUNK = 64
ROPE_THETA = 10000.0
EPS = 1e-6
F_TINY = 1e-30
F32 = jnp.float32

kernel_name = 'hybrid_parallel_heads_dit_block'


def rms_norm(x, gain):
    xf = x.astype(F32)
    y = xf * lax.rsqrt(jnp.mean(xf * xf, axis=-1, keepdims=True) + EPS)
    return (y * gain.astype(F32)).astype(x.dtype)


def modulate(h, shift, scale):
    return h * (1 + scale) + shift


def axial_rope(row_ids, col_ids, rot_dim):
    n_freq = rot_dim // 4
    inv = ROPE_THETA ** (-jnp.arange(n_freq, dtype=F32) / n_freq)
    ang = jnp.concatenate([row_ids.astype(F32)[:, None] * inv, col_ids.astype(F32)[:, None] * inv], axis=-1)
    return jnp.cos(ang)[:, None, :], jnp.sin(ang)[:, None, :]


def apply_rope(x, rope):
    cos, sin = rope
    xf = x.astype(F32)
    x1, x2 = jnp.split(xf, 2, axis=-1)
    return jnp.concatenate([x1 * cos - x2 * sin, x2 * cos + x1 * sin], axis=-1).astype(x.dtype)


def forget_gate(z, lb):
    zf = z.astype(F32)
    f = lb + (1.0 - lb) * jax.nn.sigmoid(zf)
    log_f = jnp.log(jnp.maximum(f, F_TINY))
    k = (1.0 - lb) * jax.nn.sigmoid(-zf)
    return log_f, k.astype(z.dtype)


def block_attention(q, k, v, scale):
    bsz, t = q.shape[:2]
    nb = t // Q_BLOCK
    qb = jnp.moveaxis(q.reshape(bsz, nb, Q_BLOCK, *q.shape[2:]), 1, 0)

    def one_block(qi):
        s = jnp.einsum('bqhgd,bkhd->bhgqk', qi, k, preferred_element_type=F32) * scale
        p = jax.nn.softmax(s, axis=-1).astype(v.dtype)
        return jnp.einsum('bhgqk,bkhv->bqhgv', p, v)

    out = lax.map(one_block, qb)
    return jnp.moveaxis(out, 0, 1).reshape(bsz, t, *out.shape[3:])


def gla_chunk_scan(q, k, v, g, s0):
    bsz, n_tok, n_h, dk = q.shape
    dv = v.shape[-1]
    n_chunk = n_tok // SCAN_CHUNK

    def to_chunks(a):
        return a.astype(F32).reshape(bsz, n_chunk, SCAN_CHUNK, n_h, a.shape[-1]).transpose(1, 0, 3, 2, 4)

    lower_tri = jnp.tril(jnp.ones((SCAN_CHUNK, SCAN_CHUNK), dtype=bool))[:, :, None]

    def step(state, inp):
        q_, k_, v_, g_ = inp
        b = jnp.cumsum(g_, axis=-2)
        diff = b[..., :, None, :] - b[..., None, :, :]
        decay = jnp.where(lower_tri, jnp.exp(jnp.where(lower_tri, diff, 0.0)), 0.0)
        scores = jnp.einsum('bhtd,bhsd,bhtsd->bhts', q_, k_, decay)
        o = jnp.einsum('bhts,bhsv->bhtv', scores, v_) + jnp.einsum('bhtd,bhdv->bhtv', q_ * jnp.exp(b), state)
        b_last = b[..., -1:, :]
        new_state = state * jnp.exp(b_last[..., 0, :])[..., None] + jnp.einsum('bhsd,bhsv->bhdv', k_ * jnp.exp(b_last - b), v_)
        return new_state, o

    s_fin, o = lax.scan(step, s0, (to_chunks(q), to_chunks(k), to_chunks(v), to_chunks(g)))
    o = o.transpose(1, 0, 3, 2, 4).reshape(bsz, n_tok, n_h, dv)
    return o.astype(v.dtype), s_fin


def mixer_features(h, w_in, a_q_norm, a_k_norm, b_q_norm, w_q_up, b_kv_norm, w_kv_up, lb, rope_a, rope_b):
    bsz, t, _ = h.shape
    aq, ak, av, bqd, bkvd, bkr, cq, cff, cfb, ci, cg = jnp.split(h @ w_in, SPLIT_AT, axis=-1)
    aq = rms_norm(aq.reshape(bsz, t, A_HEADS, HEAD_DIM), a_q_norm)
    ak = rms_norm(ak.reshape(bsz, t, A_KV_HEADS, HEAD_DIM), a_k_norm)
    av = av.reshape(bsz, t, A_KV_HEADS, HEAD_DIM)
    bq = (rms_norm(bqd, b_q_norm) @ w_q_up).reshape(bsz, t, B_HEADS, B_NOPE + B_ROPE)
    bkv = (rms_norm(bkvd, b_kv_norm) @ w_kv_up).reshape(bsz, t, B_HEADS, B_NOPE + B_V)
    bq_nope, bq_pe = jnp.split(bq, [B_NOPE], axis=-1)
    bk_nope, bv = jnp.split(bkv, [B_NOPE], axis=-1)
    bk_pe = bkr[:, :, None, :]
    if rope_a is not None:
        aq = apply_rope(aq, rope_a)
        ak = apply_rope(ak, rope_a)
        bq_pe = apply_rope(bq_pe, rope_b)
        bk_pe = apply_rope(bk_pe, rope_b)
    bq = jnp.concatenate([bq_nope, bq_pe], axis=-1)
    bk = jnp.concatenate([bk_nope, jnp.broadcast_to(bk_pe, (bsz, t, B_HEADS, B_ROPE))], axis=-1)
    heads = lambda a: a.reshape(bsz, t, C_HEADS, -1)
    g_f, k_f = forget_gate(cff, lb[0])
    g_b, k_b = forget_gate(cfb, lb[1])
    return {'aq': aq, 'ak': ak, 'av': av, 'bq': bq, 'bk': bk, 'bv': bv,
            'cq': heads(jax.nn.silu(cq)), 'cv': heads(ci), 'cgate': heads(cg),
            'ck_f': heads(k_f), 'cg_f': heads(g_f), 'ck_b': heads(k_b), 'cg_b': heads(g_b)}


def attention_groups(fq, fkv_list):
    bsz, t = fq['aq'].shape[:2]
    cat = lambda name: jnp.concatenate([f[name] for f in fkv_list], axis=1)
    ya = block_attention(fq['aq'].reshape(bsz, t, A_KV_HEADS, A_GROUP, HEAD_DIM), cat('ak'), cat('av'), HEAD_DIM ** -0.5)
    yb = block_attention(fq['bq'][:, :, :, None, :], cat('bk'), cat('bv'), (B_NOPE + B_ROPE) ** -0.5)
    return ya.reshape(bsz, t, A_OUT), yb.reshape(bsz, t, B_OUT)


def hgrn2_bidirectional(fl, fc, out_norm, need_ctx_out):
    bsz = fl['cq'].shape[0]
    zero = jnp.zeros((bsz, C_HEADS, C_DK, C_DV), F32)
    rev = lambda a: jnp.flip(a, axis=1)
    oc_f, s_f = gla_chunk_scan(fc['cq'], fc['ck_f'], fc['cv'], fc['cg_f'], zero)
    oc_b, s_b = gla_chunk_scan(rev(fc['cq']), rev(fc['ck_b']), rev(fc['cv']), rev(fc['cg_b']), zero)
    ol_f, _ = gla_chunk_scan(fl['cq'], fl['ck_f'], fl['cv'], fl['cg_f'], s_f)
    ol_b, _ = gla_chunk_scan(rev(fl['cq']), rev(fl['ck_b']), rev(fl['cv']), rev(fl['cg_b']), s_b)

    def readout(o, gate):
        y = rms_norm(o, out_norm) * jax.nn.silu(gate)
        return y.reshape(*y.shape[:2], C_OUT)

    y_lat = readout(ol_f + rev(ol_b), fl['cgate'])
    y_ctx = readout(oc_f + rev(oc_b), fc['cgate']) if need_ctx_out else None
    return y_lat, y_ctx


def sq_relu_mlp(h, w1, w2):
    return jnp.square(jax.nn.relu(h @ w1)) @ w2


def _fwd_setup_inputs(seed: int = 0) -> dict:
    key = jax.random.key(seed)
    ks = jax.random.split(key, 24)
    nrm = lambda k, shape, scale: jax.random.normal(k, shape, F32) * scale
    gain = lambda k, shape: 1.0 + 0.05 * jax.random.normal(k, shape, F32)
    return {
        'x': nrm(ks[0], (BATCH, SEQ, D_MODEL), 1.0),
        'c': nrm(ks[1], (BATCH, D_MODEL), 1.0),
        'ctx': nrm(ks[2], (BATCH, CTX_LEN, D_MODEL), 1.0),
        'c_ctx': nrm(ks[3], (D_MODEL,), 1.0),
        'w_ada': nrm(ks[4], (DEPTH, D_MODEL, 6 * D_MODEL), 0.5 * D_MODEL ** -0.5),
        'b_ada': nrm(ks[5], (DEPTH, 6 * D_MODEL), 0.02),
        'g_pre_mix': gain(ks[6], (DEPTH, D_MODEL)),
        'g_post_mix': gain(ks[7], (DEPTH, D_MODEL)),
        'g_pre_ffn': gain(ks[8], (DEPTH, D_MODEL)),
        'g_post_ffn': gain(ks[9], (DEPTH, D_MODEL)),
        'w_in': nrm(ks[10], (DEPTH, D_MODEL, D_IN), D_MODEL ** -0.5),
        'a_q_norm': gain(ks[11], (DEPTH, HEAD_DIM)),
        'a_k_norm': gain(ks[12], (DEPTH, HEAD_DIM)),
        'b_q_norm': gain(ks[13], (DEPTH, B_Q_RANK)),
        'w_q_up': nrm(ks[14], (DEPTH, B_Q_RANK, B_HEADS * (B_NOPE + B_ROPE)), B_Q_RANK ** -0.5),
        'b_kv_norm': gain(ks[15], (DEPTH, B_KV_RANK)),
        'w_kv_up': nrm(ks[16], (DEPTH, B_KV_RANK, B_HEADS * (B_NOPE + B_V)), B_KV_RANK ** -0.5),
        'c_lower_bounds': nrm(ks[17], (DEPTH, 2, C_HEADS * C_DK), 0.5),
        'c_out_norm': gain(ks[18], (DEPTH, C_DV)),
        'w_out': nrm(ks[19], (DEPTH, D_MIX, D_MODEL), D_MIX ** -0.5),
        'w_ff1': nrm(ks[20], (DEPTH, D_MODEL, D_FF), D_MODEL ** -0.5),
        'w_ff2': nrm(ks[21], (DEPTH, D_FF, D_MODEL), D_FF ** -0.5),
    }


def _fwd_reference(x, c, ctx, c_ctx, w_ada, b_ada, g_pre_mix, g_post_mix, g_pre_ffn, g_post_ffn,
              w_in, a_q_norm, a_k_norm, b_q_norm, w_q_up, b_kv_norm, w_kv_up,
              c_lower_bounds, c_out_norm, w_out, w_ff1, w_ff2):
    n_lat = x.shape[1]
    rows = n_lat // GRID_W
    row_ids = jnp.repeat(jnp.arange(rows, dtype=jnp.int32), GRID_W)
    col_ids = jnp.tile(jnp.arange(GRID_W, dtype=jnp.int32), rows)
    rope_a = axial_rope(row_ids, col_ids, HEAD_DIM)
    rope_b = axial_rope(row_ids, col_ids, B_ROPE)
    p_lb = jax.nn.softmax(c_lower_bounds.astype(F32), axis=0)
    lower = jnp.cumsum(p_lb, axis=0) - p_lb[:1]

    xc = ctx
    for l in range(DEPTH):
        need_ctx = l < DEPTH - 1
        sh_m, sc_m, gt_m, sh_f, sc_f, gt_f = jnp.split((jax.nn.silu(c) @ w_ada[l] + b_ada[l])[:, None, :], 6, axis=-1)
        csh_m, csc_m, cgt_m, csh_f, csc_f, cgt_f = jnp.split((jax.nn.silu(c_ctx) @ w_ada[l] + b_ada[l])[None, None, :], 6, axis=-1)
        feat_args = (w_in[l], a_q_norm[l], a_k_norm[l], b_q_norm[l], w_q_up[l], b_kv_norm[l], w_kv_up[l], lower[l])

        f_lat = mixer_features(modulate(rms_norm(x, g_pre_mix[l]), sh_m, sc_m), *feat_args, rope_a, rope_b)
        f_ctx = mixer_features(modulate(rms_norm(xc, g_pre_mix[l]), csh_m, csc_m), *feat_args, None, None)
        ya, yb = attention_groups(f_lat, [f_lat, f_ctx])
        yc, yc_ctx = hgrn2_bidirectional(f_lat, f_ctx, c_out_norm[l], need_ctx)
        x = x + gt_m * rms_norm(jnp.concatenate([ya, yb, yc], axis=-1) @ w_out[l], g_post_mix[l])

        x = x + gt_f * rms_norm(sq_relu_mlp(modulate(rms_norm(x, g_pre_ffn[l]), sh_f, sc_f), w_ff1[l], w_ff2[l]), g_post_ffn[l])

        if need_ctx:
            ya_c, yb_c = attention_groups(f_ctx, [f_ctx])
            xc = xc + cgt_m * rms_norm(jnp.concatenate([ya_c, yb_c, yc_ctx], axis=-1) @ w_out[l], g_post_mix[l])
            xc = xc + cgt_f * rms_norm(sq_relu_mlp(modulate(rms_norm(xc, g_pre_ffn[l]), csh_f, csc_f), w_ff1[l], w_ff2[l]), g_post_ffn[l])
    return x


import jax as _jax
import jax.numpy as _jnp

TWIN_FORMAT = 'train_step'
FWD_PARAMS = ['x', 'c', 'ctx', 'c_ctx', 'w_ada', 'b_ada', 'g_pre_mix', 'g_post_mix', 'g_pre_ffn', 'g_post_ffn', 'w_in', 'a_q_norm', 'a_k_norm', 'b_q_norm', 'w_q_up', 'b_kv_norm', 'w_kv_up', 'c_lower_bounds', 'c_out_norm', 'w_out', 'w_ff1', 'w_ff2']
TWIN_WEIGHTS = ['c_ctx', 'w_ada', 'b_ada', 'g_pre_mix', 'g_post_mix', 'g_pre_ffn', 'g_post_ffn', 'w_in', 'a_q_norm', 'a_k_norm', 'b_q_norm', 'w_q_up', 'b_kv_norm', 'w_kv_up', 'c_lower_bounds', 'c_out_norm', 'w_out', 'w_ff1', 'w_ff2']
TWIN_DIFF_INPUT = 'x'
TWIN_INPUTS = ['x', 'c', 'ctx', 'c_ctx', 'w_ada', 'b_ada', 'g_pre_mix', 'g_post_mix', 'g_pre_ffn', 'g_post_ffn', 'w_in', 'a_q_norm', 'a_k_norm', 'b_q_norm', 'w_q_up', 'b_kv_norm', 'w_kv_up', 'c_lower_bounds', 'c_out_norm', 'w_out', 'w_ff1', 'w_ff2', 'loss_target', 'm_c_ctx', 'm_w_ada', 'm_b_ada', 'm_g_pre_mix', 'm_g_post_mix', 'm_g_pre_ffn', 'm_g_post_ffn', 'm_w_in', 'm_a_q_norm', 'm_a_k_norm', 'm_b_q_norm', 'm_w_q_up', 'm_b_kv_norm', 'm_w_kv_up', 'm_c_lower_bounds', 'm_c_out_norm', 'm_w_out', 'm_w_ff1', 'm_w_ff2', 'v_c_ctx', 'v_w_ada', 'v_b_ada', 'v_g_pre_mix', 'v_g_post_mix', 'v_g_pre_ffn', 'v_g_post_ffn', 'v_w_in', 'v_a_q_norm', 'v_a_k_norm', 'v_b_q_norm', 'v_w_q_up', 'v_b_kv_norm', 'v_w_kv_up', 'v_c_lower_bounds', 'v_c_out_norm', 'v_w_out', 'v_w_ff1', 'v_w_ff2']
TWIN_OUTPUTS = ['loss', 'grad_x', 'grad_c_ctx', 'grad_w_ada', 'grad_b_ada', 'grad_g_pre_mix', 'grad_g_post_mix', 'grad_g_pre_ffn', 'grad_g_post_ffn', 'grad_w_in', 'grad_a_q_norm', 'grad_a_k_norm', 'grad_b_q_norm', 'grad_w_q_up', 'grad_b_kv_norm', 'grad_w_kv_up', 'grad_c_lower_bounds', 'grad_c_out_norm', 'grad_w_out', 'grad_w_ff1', 'grad_w_ff2', 'delta_c_ctx', 'delta_w_ada', 'delta_b_ada', 'delta_g_pre_mix', 'delta_g_post_mix', 'delta_g_pre_ffn', 'delta_g_post_ffn', 'delta_w_in', 'delta_a_q_norm', 'delta_a_k_norm', 'delta_b_q_norm', 'delta_w_q_up', 'delta_b_kv_norm', 'delta_w_kv_up', 'delta_c_lower_bounds', 'delta_c_out_norm', 'delta_w_out', 'delta_w_ff1', 'delta_w_ff2', 'new_m_c_ctx', 'new_m_w_ada', 'new_m_b_ada', 'new_m_g_pre_mix', 'new_m_g_post_mix', 'new_m_g_pre_ffn', 'new_m_g_post_ffn', 'new_m_w_in', 'new_m_a_q_norm', 'new_m_a_k_norm', 'new_m_b_q_norm', 'new_m_w_q_up', 'new_m_b_kv_norm', 'new_m_w_kv_up', 'new_m_c_lower_bounds', 'new_m_c_out_norm', 'new_m_w_out', 'new_m_w_ff1', 'new_m_w_ff2', 'new_v_c_ctx', 'new_v_w_ada', 'new_v_b_ada', 'new_v_g_pre_mix', 'new_v_g_post_mix', 'new_v_g_pre_ffn', 'new_v_g_post_ffn', 'new_v_w_in', 'new_v_a_q_norm', 'new_v_a_k_norm', 'new_v_b_q_norm', 'new_v_w_q_up', 'new_v_b_kv_norm', 'new_v_w_kv_up', 'new_v_c_lower_bounds', 'new_v_c_out_norm', 'new_v_w_out', 'new_v_w_ff1', 'new_v_w_ff2']
TWIN_LEAF_KINDS = {'loss': 'loss', 'grad_x': 'grad_x', 'grad_c_ctx': 'grad_w', 'grad_w_ada': 'grad_w', 'grad_b_ada': 'grad_w', 'grad_g_pre_mix': 'grad_w', 'grad_g_post_mix': 'grad_w', 'grad_g_pre_ffn': 'grad_w', 'grad_g_post_ffn': 'grad_w', 'grad_w_in': 'grad_w', 'grad_a_q_norm': 'grad_w', 'grad_a_k_norm': 'grad_w', 'grad_b_q_norm': 'grad_w', 'grad_w_q_up': 'grad_w', 'grad_b_kv_norm': 'grad_w', 'grad_w_kv_up': 'grad_w', 'grad_c_lower_bounds': 'grad_w', 'grad_c_out_norm': 'grad_w', 'grad_w_out': 'grad_w', 'grad_w_ff1': 'grad_w', 'grad_w_ff2': 'grad_w', 'delta_c_ctx': 'delta_w', 'delta_w_ada': 'delta_w', 'delta_b_ada': 'delta_w', 'delta_g_pre_mix': 'delta_w', 'delta_g_post_mix': 'delta_w', 'delta_g_pre_ffn': 'delta_w', 'delta_g_post_ffn': 'delta_w', 'delta_w_in': 'delta_w', 'delta_a_q_norm': 'delta_w', 'delta_a_k_norm': 'delta_w', 'delta_b_q_norm': 'delta_w', 'delta_w_q_up': 'delta_w', 'delta_b_kv_norm': 'delta_w', 'delta_w_kv_up': 'delta_w', 'delta_c_lower_bounds': 'delta_w', 'delta_c_out_norm': 'delta_w', 'delta_w_out': 'delta_w', 'delta_w_ff1': 'delta_w', 'delta_w_ff2': 'delta_w', 'new_m_c_ctx': 'new_m', 'new_m_w_ada': 'new_m', 'new_m_b_ada': 'new_m', 'new_m_g_pre_mix': 'new_m', 'new_m_g_post_mix': 'new_m', 'new_m_g_pre_ffn': 'new_m', 'new_m_g_post_ffn': 'new_m', 'new_m_w_in': 'new_m', 'new_m_a_q_norm': 'new_m', 'new_m_a_k_norm': 'new_m', 'new_m_b_q_norm': 'new_m', 'new_m_w_q_up': 'new_m', 'new_m_b_kv_norm': 'new_m', 'new_m_w_kv_up': 'new_m', 'new_m_c_lower_bounds': 'new_m', 'new_m_c_out_norm': 'new_m', 'new_m_w_out': 'new_m', 'new_m_w_ff1': 'new_m', 'new_m_w_ff2': 'new_m', 'new_v_c_ctx': 'new_v', 'new_v_w_ada': 'new_v', 'new_v_b_ada': 'new_v', 'new_v_g_pre_mix': 'new_v', 'new_v_g_post_mix': 'new_v', 'new_v_g_pre_ffn': 'new_v', 'new_v_g_post_ffn': 'new_v', 'new_v_w_in': 'new_v', 'new_v_a_q_norm': 'new_v', 'new_v_a_k_norm': 'new_v', 'new_v_b_q_norm': 'new_v', 'new_v_w_q_up': 'new_v', 'new_v_b_kv_norm': 'new_v', 'new_v_w_kv_up': 'new_v', 'new_v_c_lower_bounds': 'new_v', 'new_v_c_out_norm': 'new_v', 'new_v_w_out': 'new_v', 'new_v_w_ff1': 'new_v', 'new_v_w_ff2': 'new_v'}


def _forward(args):
    return _fwd_reference(*[args[k] for k in FWD_PARAMS])


def _output_shape():
    out = _jax.eval_shape(lambda: _forward(_fwd_setup_inputs(0)))
    return out.shape, out.dtype

N_MICROBATCH = 1
ADAM_LR = 0.001
ADAM_B1 = 0.9
ADAM_B2 = 0.999
ADAM_EPS = 1e-08
ADAM_WD = 0.01
ADAM_STEP = 10
PER_EXAMPLE_BATCH_AXIS = {'x': 0, 'c': 0, 'ctx': 0, 'loss_target': 0}
SHARED_INPUTS = []
_WEIGHT_DTYPES = {'c_ctx': _jnp.float32, 'w_ada': _jnp.float32, 'b_ada': _jnp.float32, 'g_pre_mix': _jnp.float32, 'g_post_mix': _jnp.float32, 'g_pre_ffn': _jnp.float32, 'g_post_ffn': _jnp.float32, 'w_in': _jnp.float32, 'a_q_norm': _jnp.float32, 'a_k_norm': _jnp.float32, 'b_q_norm': _jnp.float32, 'w_q_up': _jnp.float32, 'b_kv_norm': _jnp.float32, 'w_kv_up': _jnp.float32, 'c_lower_bounds': _jnp.float32, 'c_out_norm': _jnp.float32, 'w_out': _jnp.float32, 'w_ff1': _jnp.float32, 'w_ff2': _jnp.float32}
MOMENT_SCALE = {'c_ctx': 2.916438e-01, 'w_ada': 5.039154e+00, 'b_ada': 9.131040e+00, 'g_pre_mix': 8.264251e-01, 'g_post_mix': 8.441950e+00, 'g_pre_ffn': 4.585704e-01, 'g_post_ffn': 8.242968e+00, 'w_in': 1.856833e+00, 'a_q_norm': 1.946901e-01, 'a_k_norm': 1.864240e-01, 'b_q_norm': 2.490899e-01, 'w_q_up': 1.528836e-01, 'b_kv_norm': 3.651285e+00, 'w_kv_up': 1.841327e+00, 'c_lower_bounds': 1.703590e-02, 'c_out_norm': 2.473226e+00, 'w_out': 2.597129e+00, 'w_ff1': 4.233895e-01, 'w_ff2': 2.064561e+00}


def _to_microbatches(a, axis):
    t = _jnp.moveaxis(a, axis, 0)
    t = t.reshape((N_MICROBATCH, t.shape[0] // N_MICROBATCH) + t.shape[1:])
    return _jnp.moveaxis(t, 1, axis + 1)


def setup_inputs(seed: int = 0) -> dict:
    inp = _fwd_setup_inputs(seed)
    key = _jax.random.fold_in(_jax.random.key(seed), 7919)
    shape, _ = _output_shape()
    out = dict(inp)
    out["loss_target"] = _jax.random.normal(_jax.random.fold_in(key, 0), shape, _jnp.float32)
    for i, name in enumerate(TWIN_WEIGHTS):
        w = inp[name].astype(_jnp.float32)
        if MOMENT_SCALE is None:
            s = _jnp.sqrt(_jnp.mean(_jnp.square(w)) + 1e-30)
        else:
            s = MOMENT_SCALE[name]
        km, kv = _jax.random.split(_jax.random.fold_in(key, i + 1))
        out[name] = w
        out["m_" + name] = s * _jax.random.normal(km, w.shape, _jnp.float32)
        out["v_" + name] = (s * s) * _jax.random.uniform(kv, w.shape, _jnp.float32, 0.5, 1.5)
    if N_MICROBATCH > 1:
        for name, axis in PER_EXAMPLE_BATCH_AXIS.items():
            out[name] = _to_microbatches(out[name], axis)
    return {'x': out['x'], 'c': out['c'], 'ctx': out['ctx'], 'c_ctx': out['c_ctx'], 'w_ada': out['w_ada'], 'b_ada': out['b_ada'], 'g_pre_mix': out['g_pre_mix'], 'g_post_mix': out['g_post_mix'], 'g_pre_ffn': out['g_pre_ffn'], 'g_post_ffn': out['g_post_ffn'], 'w_in': out['w_in'], 'a_q_norm': out['a_q_norm'], 'a_k_norm': out['a_k_norm'], 'b_q_norm': out['b_q_norm'], 'w_q_up': out['w_q_up'], 'b_kv_norm': out['b_kv_norm'], 'w_kv_up': out['w_kv_up'], 'c_lower_bounds': out['c_lower_bounds'], 'c_out_norm': out['c_out_norm'], 'w_out': out['w_out'], 'w_ff1': out['w_ff1'], 'w_ff2': out['w_ff2'], 'loss_target': out['loss_target'], 'm_c_ctx': out['m_c_ctx'], 'm_w_ada': out['m_w_ada'], 'm_b_ada': out['m_b_ada'], 'm_g_pre_mix': out['m_g_pre_mix'], 'm_g_post_mix': out['m_g_post_mix'], 'm_g_pre_ffn': out['m_g_pre_ffn'], 'm_g_post_ffn': out['m_g_post_ffn'], 'm_w_in': out['m_w_in'], 'm_a_q_norm': out['m_a_q_norm'], 'm_a_k_norm': out['m_a_k_norm'], 'm_b_q_norm': out['m_b_q_norm'], 'm_w_q_up': out['m_w_q_up'], 'm_b_kv_norm': out['m_b_kv_norm'], 'm_w_kv_up': out['m_w_kv_up'], 'm_c_lower_bounds': out['m_c_lower_bounds'], 'm_c_out_norm': out['m_c_out_norm'], 'm_w_out': out['m_w_out'], 'm_w_ff1': out['m_w_ff1'], 'm_w_ff2': out['m_w_ff2'], 'v_c_ctx': out['v_c_ctx'], 'v_w_ada': out['v_w_ada'], 'v_b_ada': out['v_b_ada'], 'v_g_pre_mix': out['v_g_pre_mix'], 'v_g_post_mix': out['v_g_post_mix'], 'v_g_pre_ffn': out['v_g_pre_ffn'], 'v_g_post_ffn': out['v_g_post_ffn'], 'v_w_in': out['v_w_in'], 'v_a_q_norm': out['v_a_q_norm'], 'v_a_k_norm': out['v_a_k_norm'], 'v_b_q_norm': out['v_b_q_norm'], 'v_w_q_up': out['v_w_q_up'], 'v_b_kv_norm': out['v_b_kv_norm'], 'v_w_kv_up': out['v_w_kv_up'], 'v_c_lower_bounds': out['v_c_lower_bounds'], 'v_c_out_norm': out['v_c_out_norm'], 'v_w_out': out['v_w_out'], 'v_w_ff1': out['v_w_ff1'], 'v_w_ff2': out['v_w_ff2']}


def _loss(weights, diff, rest, loss_target):
    with _jax.named_scope("forward"):
        args = {**rest, TWIN_DIFF_INPUT: diff, **{k: w.astype(_WEIGHT_DTYPES[k]) for k, w in weights.items()}}
        y = _forward(args)
    with _jax.named_scope("loss_head"):
        err = _jnp.square(y.astype(_jnp.float32) - loss_target)
        return 0.5 * _jnp.sum(_jnp.mean(err, axis=-1)) if err.ndim else 0.5 * err


def _adamw(w, g, m, v):
    m = ADAM_B1 * m + (1.0 - ADAM_B1) * g
    v = ADAM_B2 * v + (1.0 - ADAM_B2) * _jnp.square(g)
    m_hat = m / (1.0 - ADAM_B1 ** ADAM_STEP)
    v_hat = v / (1.0 - ADAM_B2 ** ADAM_STEP)
    delta = -ADAM_LR * (m_hat / (_jnp.sqrt(v_hat) + ADAM_EPS) + ADAM_WD * w)
    return delta, m, v


def reference(x, c, ctx, c_ctx, w_ada, b_ada, g_pre_mix, g_post_mix, g_pre_ffn, g_post_ffn, w_in, a_q_norm, a_k_norm, b_q_norm, w_q_up, b_kv_norm, w_kv_up, c_lower_bounds, c_out_norm, w_out, w_ff1, w_ff2, loss_target, m_c_ctx, m_w_ada, m_b_ada, m_g_pre_mix, m_g_post_mix, m_g_pre_ffn, m_g_post_ffn, m_w_in, m_a_q_norm, m_a_k_norm, m_b_q_norm, m_w_q_up, m_b_kv_norm, m_w_kv_up, m_c_lower_bounds, m_c_out_norm, m_w_out, m_w_ff1, m_w_ff2, v_c_ctx, v_w_ada, v_b_ada, v_g_pre_mix, v_g_post_mix, v_g_pre_ffn, v_g_post_ffn, v_w_in, v_a_q_norm, v_a_k_norm, v_b_q_norm, v_w_q_up, v_b_kv_norm, v_w_kv_up, v_c_lower_bounds, v_c_out_norm, v_w_out, v_w_ff1, v_w_ff2):
    given = dict(x=x, c=c, ctx=ctx, c_ctx=c_ctx, w_ada=w_ada, b_ada=b_ada, g_pre_mix=g_pre_mix, g_post_mix=g_post_mix, g_pre_ffn=g_pre_ffn, g_post_ffn=g_post_ffn, w_in=w_in, a_q_norm=a_q_norm, a_k_norm=a_k_norm, b_q_norm=b_q_norm, w_q_up=w_q_up, b_kv_norm=b_kv_norm, w_kv_up=w_kv_up, c_lower_bounds=c_lower_bounds, c_out_norm=c_out_norm, w_out=w_out, w_ff1=w_ff1, w_ff2=w_ff2, loss_target=loss_target, m_c_ctx=m_c_ctx, m_w_ada=m_w_ada, m_b_ada=m_b_ada, m_g_pre_mix=m_g_pre_mix, m_g_post_mix=m_g_post_mix, m_g_pre_ffn=m_g_pre_ffn, m_g_post_ffn=m_g_post_ffn, m_w_in=m_w_in, m_a_q_norm=m_a_q_norm, m_a_k_norm=m_a_k_norm, m_b_q_norm=m_b_q_norm, m_w_q_up=m_w_q_up, m_b_kv_norm=m_b_kv_norm, m_w_kv_up=m_w_kv_up, m_c_lower_bounds=m_c_lower_bounds, m_c_out_norm=m_c_out_norm, m_w_out=m_w_out, m_w_ff1=m_w_ff1, m_w_ff2=m_w_ff2, v_c_ctx=v_c_ctx, v_w_ada=v_w_ada, v_b_ada=v_b_ada, v_g_pre_mix=v_g_pre_mix, v_g_post_mix=v_g_post_mix, v_g_pre_ffn=v_g_pre_ffn, v_g_post_ffn=v_g_post_ffn, v_w_in=v_w_in, v_a_q_norm=v_a_q_norm, v_a_k_norm=v_a_k_norm, v_b_q_norm=v_b_q_norm, v_w_q_up=v_w_q_up, v_b_kv_norm=v_b_kv_norm, v_w_kv_up=v_w_kv_up, v_c_lower_bounds=v_c_lower_bounds, v_c_out_norm=v_c_out_norm, v_w_out=v_w_out, v_w_ff1=v_w_ff1, v_w_ff2=v_w_ff2)
    weights = {n: given[n] for n in TWIN_WEIGHTS}
    shared = {n: given[n] for n in SHARED_INPUTS}
    per_example = {n: given[n] for n in ['x', 'c', 'ctx']}
    grad_fn = _jax.value_and_grad(_loss, argnums=(0, 1))

    def one_microbatch(ex, loss_target):
        ex = dict(ex)
        diff = ex.pop(TWIN_DIFF_INPUT)
        return grad_fn(weights, diff, {**shared, **ex}, loss_target)

    if N_MICROBATCH == 1:
        loss, (grad_w, grad_x) = one_microbatch(per_example, given["loss_target"])
    else:
        def body(carry, xs):
            loss_sum, grad_sum = carry
            l_k, (gw_k, gx_k) = one_microbatch(xs[0], xs[1])
            with _jax.named_scope("update"):
                return (loss_sum + l_k, _jax.tree.map(_jnp.add, grad_sum, gw_k)), gx_k

        init = (_jnp.zeros((), _jnp.float32), _jax.tree.map(_jnp.zeros_like, weights))
        (loss, grad_w), grad_x = _jax.lax.scan(body, init, (per_example, given["loss_target"]))
    with _jax.named_scope("update"):
        delta_w, new_m, new_v = {}, {}, {}
        for n in TWIN_WEIGHTS:
            delta_w[n], new_m[n], new_v[n] = _adamw(weights[n], grad_w[n], given["m_" + n], given["v_" + n])
    return (loss, grad_x, *[grad_w[n] for n in TWIN_WEIGHTS], *[delta_w[n] for n in TWIN_WEIGHTS],
            *[new_m[n] for n in TWIN_WEIGHTS], *[new_v[n] for n in TWIN_WEIGHTS])
```

```python
import functools
import math

import numpy as np

import jax
import jax.numpy as jnp
from jax import lax
from jax.experimental import pallas as pl
from jax.experimental.pallas import tpu as pltpu

F32 = jnp.float32
BF16 = jnp.bfloat16

A_HEADS, A_KV, HD = 8, 2, 64
A_GROUP = A_HEADS // A_KV
B_HEADS, B_QR, B_KVR, B_NOPE, B_ROPE, B_V = 4, 192, 128, 64, 32, 64
C_HEADS, C_DK = 4, 64
C_W = C_HEADS * C_DK
GRID_W = 64
CHUNK = 64
ROPE_THETA = 10000.0
EPS = 1e-6
F_TINY = 1e-30
IN_SIZES = (512, 128, 128, 192, 128, 32, 256, 256, 256, 256, 256)
D_IN = sum(IN_SIZES)
D_IN_PAD = 2560
N_DEV = 8
LANES = 128

ADAM_LR, ADAM_B1, ADAM_B2, ADAM_EPS, ADAM_WD, ADAM_STEP = 0.001, 0.9, 0.999, 1e-08, 0.01, 10

VMEM_LIMIT = 56 * 1024 * 1024
MESH = pl.DeviceIdType.MESH


def _pallas(body, **kw):
    return pl.pallas_call(body, **kw)


def _cparams(sem):
    return pltpu.CompilerParams(dimension_semantics=sem, vmem_limit_bytes=VMEM_LIMIT)


def _split3(x):
    hi = x.astype(BF16)
    r = x - hi.astype(F32)
    mid = r.astype(BF16)
    lo = (r - mid.astype(F32)).astype(BF16)
    return hi, mid, lo


def _nn(a, b):
    return jnp.dot(a, b, preferred_element_type=F32)


def _nt(a, b):
    return lax.dot_general(a, b, (((1,), (1,)), ((), ())), preferred_element_type=F32)


def _tn(a, b):
    return lax.dot_general(a, b, (((0,), (0,)), ((), ())), preferred_element_type=F32)


@jax.custom_vjp
def xdotr(x, m):
    return sum(_nn(p, m) for p in _split3(x))


def _xdotr_fwd(x, m):
    return xdotr(x, m), m


def _xdotr_bwd(m, ct):
    return sum(_nt(p, m) for p in _split3(ct)), None


xdotr.defvjp(_xdotr_fwd, _xdotr_bwd)


@jax.custom_vjp
def xdotl(m, mt, x):
    return sum(_nn(m, p) for p in _split3(x))


def _xdotl_fwd(m, mt, x):
    return xdotl(m, mt, x), (m, mt)


def _xdotl_bwd(res, ct):
    m, mt = res
    return None, None, sum(_nn(mt, p) for p in _split3(ct))


xdotl.defvjp(_xdotl_fwd, _xdotl_bwd)


def _sigmoid(x):
    return 1.0 / (1.0 + jnp.exp(-x))


def _silu(x):
    return x * _sigmoid(x)


def _rms(x, gain, n=None):
    n = x.shape[-1] if n is None else n
    ms = jnp.sum(x * x, axis=-1, keepdims=True) * (1.0 / n)
    return x * lax.rsqrt(ms + EPS) * gain


def _head_rms(x, seg, gain):
    ms = xdotr(x * x, seg) * (1.0 / HD)
    return x * lax.rsqrt(ms + EPS) * gain


class Lay:
    def __init__(self, B, T, Tc, tm, with_ctx):
        self.B, self.T, self.Tc, self.tm = B, T, Tc, tm
        self.nl = T // tm
        self.nc = max(Tc // tm, 1)
        self.n_lat = B * self.nl
        self.n_tiles = self.n_lat + (B * self.nc if with_ctx else 0)
        self.n_seg = 2 * B if with_ctx else B
        self.rows = self.n_tiles * tm

    def seg(self, i):
        return jnp.where(i < self.n_lat, i // self.nl, self.B + (i - self.n_lat) // self.nc)

    def pos(self, i):
        return jnp.where(i < self.n_lat, i % self.nl, self.nl)

    def first(self, i):
        return jnp.where(i < self.n_lat, i % self.nl == 0, (i - self.n_lat) % self.nc == 0)


def _ew_spec(kind, a, lay):
    if kind == "tok":
        return pl.BlockSpec((lay.tm, a.shape[1]), lambda i: (i, 0))
    if kind == "seg":
        return pl.BlockSpec((1, 1, a.shape[2]), lambda i: (lay.seg(i), 0, 0))
    if kind == "pos":
        return pl.BlockSpec((lay.tm, a.shape[1]), lambda i: (lay.pos(i), 0))
    return pl.BlockSpec(a.shape, lambda i: (0,) * a.ndim)


def _ew_load(ref, kind):
    return ref[0] if kind == "seg" else ref[...]


def _ew_fwd(f, lay, kinds, arrays, outs, name):
    n_in = len(arrays)

    def body(*refs):
        vals = [_ew_load(r, k) for r, k in zip(refs[:n_in], kinds)]
        res = f(*vals)
        for r, o in zip(res, refs[n_in:]):
            o[...] = r.astype(o.dtype)

    return _pallas(
        body, name=name, grid=(lay.n_tiles,),
        in_specs=[_ew_spec(k, a, lay) for k, a in zip(kinds, arrays)],
        out_specs=[pl.BlockSpec((lay.tm, c), lambda i: (i, 0)) for c, _ in outs],
        out_shape=[jax.ShapeDtypeStruct((lay.rows, c), dt) for c, dt in outs],
        compiler_params=_cparams(("parallel",)),
    )(*arrays)


def _ew_bwd(f, lay, kinds, diffs, arrays, cts, name):
    n_in, n_ct = len(arrays), len(cts)
    d_idx = [i for i, d in enumerate(diffs) if d]

    g_shapes, g_specs = [], []
    for i in d_idx:
        a, k = arrays[i], kinds[i]
        if k == "tok":
            g_shapes.append(jax.ShapeDtypeStruct((lay.rows, a.shape[1]), F32))
            g_specs.append(pl.BlockSpec((lay.tm, a.shape[1]), lambda t: (t, 0)))
        elif k == "seg":
            g_shapes.append(jax.ShapeDtypeStruct((lay.n_seg, 1, a.shape[2]), F32))
            g_specs.append(pl.BlockSpec((1, 1, a.shape[2]), lambda t: (lay.seg(t), 0, 0)))
        else:
            g_shapes.append(jax.ShapeDtypeStruct(a.shape, F32))
            g_specs.append(pl.BlockSpec(a.shape, lambda t, nd=a.ndim: (0,) * nd))

    def body(*refs):
        vals = [_ew_load(r, k) for r, k in zip(refs[:n_in], kinds)]
        cvals = tuple(r[...].astype(F32) for r in refs[n_in:n_in + n_ct])
        g_refs = refs[n_in + n_ct:]

        def g(*dv):
            full = list(vals)
            for j, i in enumerate(d_idx):
                full[i] = dv[j]
            return tuple(o.astype(F32) for o in f(*full))

        _, vjp = jax.vjp(g, *[vals[i] for i in d_idx])
        grads = vjp(cvals)
        t = pl.program_id(0)
        for gref, grad, i in zip(g_refs, grads, d_idx):
            k = kinds[i]
            if k == "tok":
                gref[...] = grad
            elif k == "seg":
                @pl.when(lay.first(t))
                def _():
                    gref[...] = jnp.zeros_like(gref)

                gref[0] += grad
            else:
                @pl.when(t == 0)
                def _():
                    gref[...] = jnp.zeros_like(gref)

                gref[...] += grad

    tok = pl.BlockSpec
    res = _pallas(
        body, name=name + "_bwd", grid=(lay.n_tiles,),
        in_specs=[_ew_spec(k, a, lay) for k, a in zip(kinds, arrays)]
        + [tok((lay.tm, c.shape[1]), lambda i: (i, 0)) for c in cts],
        out_specs=g_specs, out_shape=g_shapes,
        compiler_params=_cparams(("arbitrary",)),
    )(*arrays, *cts)
    out = [None] * n_in
    for gr, i in zip(res, d_idx):
        a = arrays[i]
        if gr.shape != a.shape:
            pad = [(0, a.shape[0] - gr.shape[0])] + [(0, 0)] * (a.ndim - 1)
            gr = jnp.pad(gr, pad)
        out[i] = gr
    return tuple(out)


def ew_op(f, lay, kinds, diffs, outs, name):
    kinds, diffs, outs = tuple(kinds), tuple(diffs), tuple(outs)

    @jax.custom_vjp
    def op(*arrays):
        return tuple(_ew_fwd(f, lay, kinds, arrays, outs, name))

    def fwd(*arrays):
        return op(*arrays), arrays

    def bwd(arrays, cts):
        return _ew_bwd(f, lay, kinds, diffs, arrays, tuple(cts), name)

    op.defvjp(fwd, bwd)
    return op


def _tile(n, cands):
    for c in cands:
        if n % c == 0:
            return c
    return n


def _mm_nn(x, w, name):
    M, K = x.shape
    N = w.shape[1]
    tm, tn = _tile(M, (512, 256)), _tile(N, (1024, 512, 256, 128))

    def body(x_ref, w_ref, o_ref):
        o_ref[...] = _nn(x_ref[...].astype(BF16), w_ref[...].astype(BF16))

    return _pallas(
        body, name=name, grid=(N // tn, M // tm),
        in_specs=[pl.BlockSpec((tm, K), lambda j, i: (i, 0)), pl.BlockSpec((K, tn), lambda j, i: (0, j))],
        out_specs=pl.BlockSpec((tm, tn), lambda j, i: (i, j)),
        out_shape=jax.ShapeDtypeStruct((M, N), F32),
        compiler_params=_cparams(("parallel", "parallel")),
    )(x, w)


def _mm_nt(dy, w, name):
    M, N = dy.shape
    K = w.shape[0]
    tm, tk = _tile(M, (512, 256)), _tile(K, (1024, 512, 256, 128))

    def body(dy_ref, w_ref, o_ref):
        o_ref[...] = _nt(dy_ref[...].astype(BF16), w_ref[...].astype(BF16))

    return _pallas(
        body, name=name, grid=(K // tk, M // tm),
        in_specs=[pl.BlockSpec((tm, N), lambda j, i: (i, 0)), pl.BlockSpec((tk, N), lambda j, i: (j, 0))],
        out_specs=pl.BlockSpec((tm, tk), lambda j, i: (i, j)),
        out_shape=jax.ShapeDtypeStruct((M, K), F32),
        compiler_params=_cparams(("parallel", "parallel")),
    )(dy, w)


def _mm_tn(x, dy, out_dtype, name):
    M, K = x.shape
    N = dy.shape[1]
    tm = _tile(M, (512, 256))
    tk, tn = _tile(K, (1024, 512, 256, 128)), _tile(N, (1024, 512, 256, 128))
    n_m = M // tm

    def body(x_ref, dy_ref, o_ref, acc_ref):
        m = pl.program_id(2)

        @pl.when(m == 0)
        def _():
            acc_ref[...] = jnp.zeros_like(acc_ref)

        acc_ref[...] += _tn(x_ref[...].astype(BF16), dy_ref[...].astype(BF16))

        @pl.when(m == n_m - 1)
        def _():
            o_ref[...] = acc_ref[...].astype(o_ref.dtype)

    return _pallas(
        body, name=name, grid=(K // tk, N // tn, n_m),
        in_specs=[pl.BlockSpec((tm, tk), lambda a, b, m: (m, a)), pl.BlockSpec((tm, tn), lambda a, b, m: (m, b))],
        out_specs=pl.BlockSpec((tk, tn), lambda a, b, m: (a, b)),
        out_shape=jax.ShapeDtypeStruct((K, N), out_dtype),
        scratch_shapes=[pltpu.VMEM((tk, tn), F32)],
        compiler_params=_cparams(("parallel", "parallel", "arbitrary")),
    )(x, dy)


def matmul_op(name):
    @jax.custom_vjp
    def op(x, w):
        return _mm_nn(x, w, name)

    def fwd(x, w):
        return op(x, w), (x, w)

    def bwd(res, dy):
        x, w = res
        return _mm_nt(dy, w, name + "_dx"), _mm_tn(x, dy, w.dtype, name + "_dw")

    op.defvjp(fwd, bwd)
    return op


def _attn_fwd(q, k, v, group, scale, name):
    B, H, T, dq = q.shape
    Hk, L, dv = k.shape[1], k.shape[2], v.shape[3]
    tq = min(T, 512 // group)
    rows = group * tq

    def body(q_ref, k_ref, v_ref, o_ref, lse_ref):
        qv = q_ref[0].reshape(rows, dq).astype(BF16)
        kv = k_ref[0, 0].astype(BF16)
        vv = v_ref[0, 0].astype(BF16)
        s = _nt(qv, kv) * scale
        m = jnp.max(s, axis=1, keepdims=True)
        p = jnp.exp(s - m)
        l = jnp.sum(p, axis=1, keepdims=True)
        o = _nn(p.astype(BF16), vv) / l
        o_ref[0] = o.reshape(group, tq, dv)
        lse_ref[0] = (m + jnp.log(l)).reshape(group, tq, 1)

    return _pallas(
        body, name=name, grid=(B, Hk, T // tq),
        in_specs=[pl.BlockSpec((1, group, tq, dq), lambda b, h, i: (b, h, i, 0)),
                  pl.BlockSpec((1, 1, L, dq), lambda b, h, i: (b, h, 0, 0)),
                  pl.BlockSpec((1, 1, L, dv), lambda b, h, i: (b, h, 0, 0))],
        out_specs=[pl.BlockSpec((1, group, tq, dv), lambda b, h, i: (b, h, i, 0)),
                   pl.BlockSpec((1, group, tq, 1), lambda b, h, i: (b, h, i, 0))],
        out_shape=[jax.ShapeDtypeStruct((B, H, T, dv), F32), jax.ShapeDtypeStruct((B, H, T, 1), F32)],
        compiler_params=_cparams(("parallel", "parallel", "parallel")),
    )(q, k, v)


def _attn_bwd(q, k, v, o, lse, do, group, scale, name):
    B, H, T, dq = q.shape
    Hk, L, dv = k.shape[1], k.shape[2], v.shape[3]
    tq = min(T, 512 // group)
    rows = group * tq

    def body(q_ref, k_ref, v_ref, o_ref, lse_ref, do_ref, dq_ref, dk_ref, dv_ref):
        i = pl.program_id(2)
        qv = q_ref[0].reshape(rows, dq).astype(BF16)
        kv = k_ref[0, 0].astype(BF16)
        vv = v_ref[0, 0].astype(BF16)
        dov = do_ref[0].reshape(rows, dv)
        ov = o_ref[0].reshape(rows, dv)
        s = _nt(qv, kv) * scale
        p = jnp.exp(s - lse_ref[0].reshape(rows, 1))
        dob = dov.astype(BF16)
        dp = _nt(dob, vv)
        dl = jnp.sum(dov * ov, axis=1, keepdims=True)
        ds = (p * (dp - dl) * scale).astype(BF16)
        dq_ref[0] = _nn(ds, kv).reshape(group, tq, dq)

        @pl.when(i == 0)
        def _():
            dk_ref[...] = jnp.zeros_like(dk_ref)
            dv_ref[...] = jnp.zeros_like(dv_ref)

        dk_ref[0, 0] += _tn(ds, qv)
        dv_ref[0, 0] += _tn(p.astype(BF16), dob)

    qspec = lambda d: pl.BlockSpec((1, group, tq, d), lambda b, h, i: (b, h, i, 0))
    kspec = lambda d: pl.BlockSpec((1, 1, L, d), lambda b, h, i: (b, h, 0, 0))
    return _pallas(
        body, name=name + "_bwd", grid=(B, Hk, T // tq),
        in_specs=[qspec(dq), kspec(dq), kspec(dv), qspec(dv), qspec(1), qspec(dv)],
        out_specs=[qspec(dq), kspec(dq), kspec(dv)],
        out_shape=[jax.ShapeDtypeStruct(q.shape, F32), jax.ShapeDtypeStruct(k.shape, F32),
                   jax.ShapeDtypeStruct(v.shape, F32)],
        compiler_params=_cparams(("parallel", "parallel", "arbitrary")),
    )(q, k, v, o, lse, do)


def attn_op(group, scale, name):
    @jax.custom_vjp
    def op(q, k, v):
        return _attn_fwd(q, k, v, group, scale, name)[0]

    def fwd(q, k, v):
        o, lse = _attn_fwd(q, k, v, group, scale, name)
        return o, (q, k, v, o, lse)

    def bwd(res, do):
        return tuple(_attn_bwd(*res, do, group, scale, name))

    op.defvjp(fwd, bwd)
    return op


SCAN_WIDTHS = (32, 16, 8, 4, 2, 1)
N_CM = 2 + 2 * len(SCAN_WIDTHS)


def _scan_consts(reverse):
    C = CHUNK
    t = np.arange(C)[:, None]
    s = np.arange(C)[None, :]
    blocks = [(s <= t), (s > t)]
    for w in SCAN_WIDTHS:
        blocks.append((s <= t) & (s // w == t // w))
    for w in SCAN_WIDTHS:
        blocks.append((s > t) & (s // w == t // w))
    masks = [np.eye(C, dtype=bool)]
    for w in SCAN_WIDTHS:
        masks.append((t // (2 * w) == s // (2 * w)) & ((t // w) % 2 == 1) & ((s // w) % 2 == 0))
    if reverse:
        blocks = [b[::-1, ::-1] for b in blocks]
        masks = [m[::-1, ::-1] for m in masks]
    cm = np.concatenate([b.astype(np.float32) for b in blocks] + [np.ones((8, C), np.float32)], axis=0)
    mw = np.stack([np.tile(m.astype(np.float32), (C_HEADS, 1)) for m in masks])
    rows = np.arange(C_HEADS * C)[:, None] // C
    lane = np.arange(C_W)[None, :] // C_DK
    hm = (rows == lane).astype(np.float32)
    bd = (np.arange(C_W)[:, None] // C_DK == lane).astype(np.float32)
    return (jnp.asarray(cm, BF16), jnp.asarray(cm.T.copy(), BF16), jnp.asarray(mw, F32),
            jnp.asarray(hm, F32), jnp.asarray(bd, F32))


def _scan_chunk(st, q, k, v, g, cm, cmt, mw, hm, bd):
    C = CHUNK
    cs = xdotl(cm, cmt, g)
    b = cs[0:C]
    rest = cs[C:2 * C]
    tot = cs[N_CM * C:N_CM * C + 1]
    kb = k.astype(BF16)

    def stack(a):
        return (jnp.concatenate([a] * C_HEADS, axis=0) * hm).astype(BF16)

    a = _nt(stack(q), kb) * mw[0]
    for i in range(len(SCAN_WIDTHS)):
        eq = jnp.exp(jnp.minimum(cs[(2 + i) * C:(3 + i) * C], 0.0))
        ek = jnp.exp(jnp.minimum(cs[(2 + len(SCAN_WIDTHS) + i) * C:(3 + len(SCAN_WIDTHS) + i) * C], 0.0))
        a = a + _nt(stack(q * eq), (k * ek).astype(BF16)) * mw[i + 1]
    oh = _nn(a.astype(BF16), v.astype(BF16)) * hm
    o = oh[0:C]
    for h in range(1, C_HEADS):
        o = o + oh[h * C:(h + 1) * C]
    o = o + _nt((q * jnp.exp(b)).astype(BF16), st.astype(BF16))
    st_new = st * jnp.exp(tot) + _tn(v.astype(BF16), (k * jnp.exp(rest)).astype(BF16)) * bd
    return o, st_new


class ScanLay:
    def __init__(self, B, T, Tc, reverse):
        self.B, self.reverse = B, reverse
        self.ncc, self.ncl = Tc // CHUNK, T // CHUNK
        self.ntot = self.ncc + self.ncl
        self.ctx0, self.lat_per, self.ctx_per = B * T // CHUNK, T // CHUNK, Tc // CHUNK

    def row(self, b, j):
        if self.reverse:
            jc, jl = self.ncc - 1 - j, self.ncl - 1 - (j - self.ncc)
        else:
            jc, jl = j, j - self.ncc
        return jnp.where(j < self.ncc, self.ctx0 + b * self.ctx_per + jc, b * self.lat_per + jl)


def _scan_fwd(q, k, v, g, sl, name):
    N = q.shape[0]
    consts = _scan_consts(sl.reverse)
    tok = pl.BlockSpec((CHUNK, C_W), lambda b, j: (sl.row(b, j), 0))
    cspecs = [pl.BlockSpec(c.shape, lambda b, j, nd=c.ndim: (0,) * nd) for c in consts]

    def body(q_ref, k_ref, v_ref, g_ref, cm, cmt, mw, hm, bd, o_ref, st_ref, st):
        @pl.when(pl.program_id(1) == 0)
        def _():
            st[...] = jnp.zeros_like(st)

        st_ref[0, 0] = st[...]
        o, st_new = _scan_chunk(st[...], q_ref[...], k_ref[...], v_ref[...], g_ref[...],
                                cm[...], cmt[...], mw[...], hm[...], bd[...])
        o_ref[...] = o
        st[...] = st_new

    return _pallas(
        body, name=name, grid=(sl.B, sl.ntot),
        in_specs=[tok] * 4 + cspecs,
        out_specs=[tok, pl.BlockSpec((1, 1, C_W, C_W), lambda b, j: (b, j, 0, 0))],
        out_shape=[jax.ShapeDtypeStruct((N, C_W), F32), jax.ShapeDtypeStruct((sl.B, sl.ntot, C_W, C_W), F32)],
        scratch_shapes=[pltpu.VMEM((C_W, C_W), F32)],
        compiler_params=_cparams(("parallel", "arbitrary")),
    )(q, k, v, g, *consts)


def _scan_bwd(q, k, v, g, states, do, sl, name):
    N = q.shape[0]
    consts = _scan_consts(sl.reverse)
    last = sl.ntot - 1
    tok = pl.BlockSpec((CHUNK, C_W), lambda b, j: (sl.row(b, last - j), 0))
    cspecs = [pl.BlockSpec(c.shape, lambda b, j, nd=c.ndim: (0,) * nd) for c in consts]

    def body(q_ref, k_ref, v_ref, g_ref, st_ref, do_ref, cm, cmt, mw, hm, bd,
             dq_ref, dk_ref, dv_ref, dg_ref, dst):
        @pl.when(pl.program_id(1) == 0)
        def _():
            dst[...] = jnp.zeros_like(dst)

        cv = (cm[...], cmt[...], mw[...], hm[...], bd[...])
        _, vjp = jax.vjp(lambda s_, q_, k_, v_, g_: _scan_chunk(s_, q_, k_, v_, g_, *cv),
                         st_ref[0, 0], q_ref[...], k_ref[...], v_ref[...], g_ref[...])
        ds, dq, dk, dv, dg = vjp((do_ref[...], dst[...]))
        dq_ref[...] = dq
        dk_ref[...] = dk
        dv_ref[...] = dv
        dg_ref[...] = dg
        dst[...] = ds

    return _pallas(
        body, name=name + "_bwd", grid=(sl.B, sl.ntot),
        in_specs=[tok] * 4 + [pl.BlockSpec((1, 1, C_W, C_W), lambda b, j: (b, last - j, 0, 0)), tok] + cspecs,
        out_specs=[tok] * 4,
        out_shape=[jax.ShapeDtypeStruct((N, C_W), F32)] * 4,
        scratch_shapes=[pltpu.VMEM((C_W, C_W), F32)],
        compiler_params=_cparams(("parallel", "arbitrary")),
    )(q, k, v, g, states, do, *consts)


def scan_op(sl, name):
    @jax.custom_vjp
    def op(q, k, v, g):
        return _scan_fwd(q, k, v, g, sl, name)[0]

    def fwd(q, k, v, g):
        o, states = _scan_fwd(q, k, v, g, sl, name)
        return o, (q, k, v, g, states)

    def bwd(res, do):
        return tuple(_scan_bwd(*res, do, sl, name))

    op.defvjp(fwd, bwd)
    return op


def loss_and_grad(y, target, tm):
    N, D = y.shape

    def body(y_ref, t_ref, dy_ref, l_ref):
        @pl.when(pl.program_id(0) == 0)
        def _():
            l_ref[...] = jnp.zeros_like(l_ref)

        e = y_ref[...] - t_ref[...]
        dy_ref[...] = e * (1.0 / D)
        l_ref[...] += 0.5 * jnp.sum(jnp.sum(e * e, axis=1, keepdims=True) * (1.0 / D), axis=0, keepdims=True)

    dy, lp = _pallas(
        body, name="loss_head", grid=(N // tm,),
        in_specs=[pl.BlockSpec((tm, D), lambda i: (i, 0))] * 2,
        out_specs=[pl.BlockSpec((tm, D), lambda i: (i, 0)), pl.BlockSpec((8, LANES), lambda i: (0, 0))],
        out_shape=[jax.ShapeDtypeStruct((N, D), F32), jax.ShapeDtypeStruct((8, LANES), F32)],
        compiler_params=_cparams(("arbitrary",)),
    )(y, target)
    return lp, dy


def adamw(w, g, m, v, name):
    R, C = w.shape
    tr = _tile(R, (512, 256, 128, 64, 32, 16, 8))

    def body(w_ref, g_ref, m_ref, v_ref, d_ref, mo_ref, vo_ref):
        gv = g_ref[...]
        mn = ADAM_B1 * m_ref[...] + (1.0 - ADAM_B1) * gv
        vn = ADAM_B2 * v_ref[...] + (1.0 - ADAM_B2) * jnp.square(gv)
        m_hat = mn / (1.0 - ADAM_B1 ** ADAM_STEP)
        v_hat = vn / (1.0 - ADAM_B2 ** ADAM_STEP)
        d_ref[...] = -ADAM_LR * (m_hat / (jnp.sqrt(v_hat) + ADAM_EPS) + ADAM_WD * w_ref[...])
        mo_ref[...] = mn
        vo_ref[...] = vn

    spec = pl.BlockSpec((tr, C), lambda i: (i, 0))
    return _pallas(
        body, name=name, grid=(R // tr,), in_specs=[spec] * 4, out_specs=[spec] * 3,
        out_shape=[jax.ShapeDtypeStruct((R, C), F32)] * 3,
        compiler_params=_cparams(("parallel",)),
    )(w, g, m, v)


def _me():
    return lax.axis_index("x"), lax.axis_index("y"), lax.axis_index("c")


def _two_level_gather(x_ref, out_ref, send_sems, recv_sems, local_sem):
    x, y, c = _me()
    me, sibling = (x, y, c), (x, y, 1 - c)
    chips = [(1 - x, y), (x, 1 - y), (1 - x, 1 - y)]

    def slot(px, py, pc):
        return out_ref.at[4 * px + 2 * py + pc]

    def copy(k, block, to, src=None):
        return pltpu.make_async_remote_copy(
            src_ref=slot(*block) if src is None else src, dst_ref=slot(*block),
            send_sem=send_sems.at[k], recv_sem=recv_sems.at[k], device_id=to, device_id_type=MESH)

    mine = pltpu.make_async_copy(x_ref, slot(*me), local_sem)
    mine.start()
    first = [copy(0, me, sibling, src=x_ref)]
    first += [copy(1 + j, me, (*chip, c), src=x_ref) for j, chip in enumerate(chips)]
    for cp in first:
        cp.start()
    passed = [copy(4 + j, (*chip, c), sibling) for j, chip in enumerate(chips)]
    for j, chip in enumerate(chips):
        copy(1 + j, (*chip, c), me).wait_recv()
        passed[j].start()
    copy(0, sibling, me).wait_recv()
    for j, chip in enumerate(chips):
        copy(4 + j, (*chip, 1 - c), me).wait_recv()
    for cp in first + passed:
        cp.wait_send()
    mine.wait()


def small_gather(xb, name):
    R = xb.shape[0]

    def body(x_ref, out_ref, sum_ref, send_sems, recv_sems, local_sem):
        _two_level_gather(x_ref, out_ref, send_sems, recv_sems, local_sem)
        acc = out_ref[0]
        for k in range(1, N_DEV):
            acc = acc + out_ref[k]
        sum_ref[...] = acc

    vm = pl.BlockSpec(memory_space=pltpu.VMEM)
    return _pallas(
        body, name=name, in_specs=[vm], out_specs=[vm, vm],
        out_shape=[jax.ShapeDtypeStruct((N_DEV, R, LANES), xb.dtype), jax.ShapeDtypeStruct((R, LANES), xb.dtype)],
        scratch_shapes=[pltpu.SemaphoreType.DMA((7,)), pltpu.SemaphoreType.DMA((7,)), pltpu.SemaphoreType.DMA],
        compiler_params=pltpu.CompilerParams(vmem_limit_bytes=VMEM_LIMIT),
    )(xb)


def big_gather(xb, name):
    R = xb.shape[0]

    def body(x_ref, out_ref, send_sems, recv_sems, local_sem):
        _two_level_gather(x_ref, out_ref, send_sems, recv_sems, local_sem)

    hbm = pl.BlockSpec(memory_space=pl.ANY)
    return _pallas(
        body, name=name, in_specs=[hbm], out_specs=hbm,
        out_shape=jax.ShapeDtypeStruct((N_DEV, R, LANES), xb.dtype),
        scratch_shapes=[pltpu.SemaphoreType.DMA((7,)), pltpu.SemaphoreType.DMA((7,)), pltpu.SemaphoreType.DMA],
    )(xb)


def scatter_exchange(gb, name):
    R = gb.shape[1]
    rels = [(dx, dy, dc) for dx in (0, 1) for dy in (0, 1) for dc in (0, 1) if (dx, dy, dc) != (0, 0, 0)]

    def body(g_ref, recv_ref, send_sems, recv_sems, local_sem):
        x, y, c = _me()
        me = 4 * x + 2 * y + c
        mine = pltpu.make_async_copy(g_ref.at[me], recv_ref.at[me], local_sem)
        mine.start()
        copies = []
        for r, (dx, dy, dc) in enumerate(rels):
            px, py, pc = (x + dx) % 2, (y + dy) % 2, (c + dc) % 2
            copies.append(pltpu.make_async_remote_copy(
                src_ref=g_ref.at[4 * px + 2 * py + pc], dst_ref=recv_ref.at[me],
                send_sem=send_sems.at[r], recv_sem=recv_sems.at[r], device_id=(px, py, pc), device_id_type=MESH))
        for cp in copies:
            cp.start()
        for cp in copies:
            cp.wait()
        mine.wait()

    hbm = pl.BlockSpec(memory_space=pl.ANY)
    return _pallas(
        body, name=name, in_specs=[hbm], out_specs=hbm,
        out_shape=jax.ShapeDtypeStruct(gb.shape, gb.dtype),
        scratch_shapes=[pltpu.SemaphoreType.DMA((7,)), pltpu.SemaphoreType.DMA((7,)), pltpu.SemaphoreType.DMA],
    )(gb)


def sum_slots(rb, name):
    R = rb.shape[1]
    tr = _tile(R, (512, 256, 128, 64, 32, 16))

    def body(r_ref, o_ref):
        acc = r_ref[0].astype(F32)
        for k in range(1, N_DEV):
            acc = acc + r_ref[k].astype(F32)
        o_ref[...] = acc

    return _pallas(
        body, name=name, grid=(R // tr,),
        in_specs=[pl.BlockSpec((N_DEV, tr, LANES), lambda i: (0, i, 0))],
        out_specs=pl.BlockSpec((tr, LANES), lambda i: (i, 0)),
        out_shape=jax.ShapeDtypeStruct((R, LANES), F32),
        compiler_params=_cparams(("parallel",)),
    )(rb)


def _pack(arrs, dtype, row_mult):
    flat = jnp.concatenate([a.astype(dtype).reshape(-1) for a in arrs])
    n = flat.shape[0]
    per = LANES * row_mult
    pad = (-n) % per
    if pad:
        flat = jnp.concatenate([flat, jnp.zeros((pad,), dtype)])
    return flat.reshape(-1, LANES)


def _unpack(buf, shapes, lead=()):
    nl = len(lead)
    flat = buf.reshape(*lead, -1)
    out, off = [], 0
    for s in shapes:
        n = int(np.prod(s))
        out.append(flat[..., off:off + n].reshape(*lead, *s))
        off += n
    return out


def _rope_tables(T, tm):
    pos = np.arange(T)
    row, col = pos // GRID_W, pos % GRID_W

    def tab(rot_dim):
        nf = rot_dim // 4
        inv = ROPE_THETA ** (-np.arange(nf, dtype=np.float32) / nf)
        ang = np.concatenate([row[:, None].astype(np.float32) * inv, col[:, None].astype(np.float32) * inv], axis=-1)
        ang = ang.astype(np.float32)
        cos, sin = np.cos(ang), np.sin(ang)
        return np.concatenate([cos, cos], -1), np.concatenate([-sin, sin], -1)

    c64, s64 = tab(HD)
    c32, s32 = tab(B_ROPE)
    ca, sa = np.tile(c64, (1, A_HEADS)), np.tile(s64, (1, A_HEADS))
    cb = np.concatenate([c32, np.ones((T, LANES - B_ROPE), np.float32)], -1)
    sb = np.concatenate([s32, np.zeros((T, LANES - B_ROPE), np.float32)], -1)
    one, zero = np.ones((T, B_NOPE), np.float32), np.zeros((T, B_NOPE), np.float32)
    tail1, tail0 = np.ones((T, LANES - B_NOPE - B_ROPE), np.float32), np.zeros((T, LANES - B_NOPE - B_ROPE), np.float32)
    cq = np.tile(np.concatenate([one, c32, tail1], -1), (1, B_HEADS))
    sq = np.tile(np.concatenate([zero, s32, tail0], -1), (1, B_HEADS))

    def fin(a, ident):
        return jnp.asarray(np.concatenate([a, np.full((tm, a.shape[1]), ident, np.float32)], 0), F32)

    return fin(ca, 1.0), fin(sa, 0.0), fin(cb, 1.0), fin(sb, 0.0), fin(cq, 1.0), fin(sq, 0.0)


def _swap_matrix(width, starts, half):
    p = np.zeros((width, width), np.float32)
    for s in starts:
        for i in range(half):
            p[s + i, s + half + i] = 1.0
            p[s + half + i, s + i] = 1.0
    return jnp.asarray(p, BF16)


def _seg_matrix(width):
    h = np.arange(width) // HD
    return jnp.asarray((h[:, None] == h[None, :]).astype(np.float32), BF16)


def _f_premod(x, sh, sc, g):
    return (_rms(x, g) * (1.0 + sc) + sh,)


def _f_post(x, y, gt, g):
    return (x + gt * _rms(y, g),)


def _f_relu2(u):
    return (jnp.square(jnp.maximum(u, 0.0)),)


def _f_bias(raw, b):
    return (raw + b,)


def _f_silu(x):
    return (_silu(x),)


def _f_readout(of, ob, gate, gain, seg):
    return (_head_rms(of + ob, seg, gain) * _silu(gate),)


def _f_bq(bq, cq, sq, pq):
    return (bq * cq + xdotr(bq, pq) * sq,)


def _make_f_feat(layer):
    def f(feat, ca, sa, cb, sb, gaq, gak, gbq, gbkv, c00, c01, c10, c11, seg, pa, pb):
        aq = _head_rms(feat[:, 0:512], seg, gaq)
        ak = _head_rms(feat[:, 512:640], seg[0:128, 0:128], gak)
        av = feat[:, 640:768]
        aq = aq * ca + xdotr(aq, pa) * sa
        ak = ak * ca[:, 0:128] + xdotr(ak, pa[0:128, 0:128]) * sa[:, 0:128]
        bqn = _rms(feat[:, 768:1024], gbq, B_QR)
        bkvn = _rms(feat[:, 1024:1152], gbkv)
        bkr = feat[:, 1152:1280]
        bkr = bkr * cb + xdotr(bkr, pb) * sb
        cq = _silu(feat[:, 1280:1536])
        zf, zb = feat[:, 1536:1792], feat[:, 1792:2048]
        if layer == 0:
            lbf = lbb = 0.0
        else:
            def share(c0, c1):
                m = jnp.maximum(c0, c1)
                e0, e1 = jnp.exp(c0 - m), jnp.exp(c1 - m)
                return e1 / (e0 + e1)
            lbf, lbb = share(c00, c10), share(c01, c11)

        def gate(z, lb):
            f_ = lb + (1.0 - lb) * _sigmoid(z)
            return (1.0 - lb) * _sigmoid(-z), jnp.log(jnp.maximum(f_, F_TINY))

        kf, gf = gate(zf, lbf)
        kb, gb = gate(zb, lbb)
        return aq, ak, av, bqn, bkvn, bkr, cq, kf, gf, kb, gb, feat[:, 2048:2304], feat[:, 2304:2560]

    return f


def _pad_w_in(w):
    z = lambda n: jnp.zeros((w.shape[0], n), w.dtype)
    return jnp.concatenate([w[:, 0:960], z(64), w[:, 960:1120], z(96), w[:, 1120:2400]], axis=1)


def _pad_w_q_up(w):
    w4 = w.reshape(B_QR, B_HEADS, B_NOPE + B_ROPE)
    w4 = jnp.pad(w4, ((0, 256 - B_QR), (0, 0), (0, LANES - B_NOPE - B_ROPE)))
    return w4.reshape(256, B_HEADS * LANES)


def _order_w_kv_up(w):
    w4 = w.reshape(B_KVR, B_HEADS, B_NOPE + B_V)
    return jnp.concatenate([w4[:, :, :B_NOPE].reshape(B_KVR, -1), w4[:, :, B_NOPE:].reshape(B_KVR, -1)], axis=1)


def _heads(a, B, T, H, d):
    return a.reshape(B, T, H, d).transpose(0, 2, 1, 3)


def _unheads(a):
    B, H, T, d = a.shape
    return a.transpose(0, 2, 1, 3).reshape(B * T, H * d)


def _tile_gain(g, reps, width=None):
    t = jnp.tile(g, reps)
    if width is not None and width > t.shape[0]:
        t = jnp.pad(t, (0, width - t.shape[0]))
    return t[None, :]


def local_forward(dims, p):
    B, T, Tc, D = dims
    tm = min(256, Tc)
    NL, NC = B * T, B * Tc
    lay_all = Lay(B, T, Tc, tm, True)
    lay_lat = Lay(B, T, Tc, tm, False)
    ca, sa, cb, sb, cq, sq = _rope_tables(T, tm)
    seg512, seg256 = _seg_matrix(512), _seg_matrix(C_W)
    pa = _swap_matrix(512, range(0, 512, HD), HD // 2)
    pb = _swap_matrix(LANES, [0], B_ROPE // 2)
    pq = _swap_matrix(512, [h * LANES + B_NOPE for h in range(B_HEADS)], B_ROPE // 2)
    scan_f, scan_b = ScanLay(B, T, Tc, False), ScanLay(B, T, Tc, True)

    tok = jnp.concatenate([p["x"], p["ctx"]], axis=0)
    depth = p["modraw"].shape[0]
    for l in range(depth):
        last = l == depth - 1
        tag = f"l{l}_"
        bias_lay = Lay(1, 8, 0, 8, False)
        raw8 = jnp.pad(p["modraw"][l], ((0, 5), (0, 0)))
        mod = ew_op(_f_bias, bias_lay, ("tok", "par"), (True, True), ((6 * D, F32),), tag + "ada_bias")(
            raw8, p["b_ada"][l][None, :])[0]
        seg_rows = jnp.concatenate([mod[0:B], jnp.broadcast_to(mod[B:B + 1], (B, 6 * D))], axis=0)[:, None, :]
        sh_m, sc_m, gt_m, sh_f, sc_f, gt_f = [seg_rows[:, :, i * D:(i + 1) * D] for i in range(6)]

        h = ew_op(_f_premod, lay_all, ("tok", "seg", "seg", "par"), (True,) * 4, ((D, F32),), tag + "premix")(
            tok, sh_m, sc_m, p["g_pre_mix"][l][None, :])[0]
        feat = matmul_op(tag + "w_in")(h, _pad_w_in(p["w_in"][l]))
        clb = p["clb"]
        feats = ew_op(
            _make_f_feat(l), lay_all,
            ("tok", "pos", "pos", "pos", "pos") + ("par",) * 11,
            (True,) + (False,) * 4 + (True,) * 8 + (False,) * 3,
            ((512, F32), (128, F32), (128, F32), (256, F32), (128, F32), (128, F32)) + ((C_W, F32),) * 7,
            tag + "feat")(
            feat, ca, sa, cb, sb,
            _tile_gain(p["a_q_norm"][l], A_HEADS), _tile_gain(p["a_k_norm"][l], A_KV),
            _tile_gain(p["b_q_norm"][l], 1, 256), _tile_gain(p["b_kv_norm"][l], 1),
            clb[0, 0][None, :], clb[0, 1][None, :], clb[1, 0][None, :], clb[1, 1][None, :],
            seg512, pa, pb)
        aq, ak, av, bqn, bkvn, bkr, cqs, kf, gf, kb, gb, cv, cgate = feats
        bq = matmul_op(tag + "w_q_up")(bqn, _pad_w_q_up(p["w_q_up"][l]))
        bq = ew_op(_f_bq, lay_all, ("tok", "pos", "pos", "par"), (True, False, False, False), ((512, F32),),
                   tag + "bq_rope")(bq, cq, sq, pq)[0]
        bkv = matmul_op(tag + "w_kv_up")(bkvn, _order_w_kv_up(p["w_kv_up"][l]))
        bkn, bv = bkv[:, 0:256], bkv[:, 256:512]
        bk = jnp.concatenate([bkn.reshape(-1, B_HEADS, B_NOPE),
                              jnp.broadcast_to(bkr[:, None, 0:B_ROPE], (bkr.shape[0], B_HEADS, B_ROPE)),
                              jnp.zeros((bkr.shape[0], B_HEADS, LANES - B_NOPE - B_ROPE), F32)], axis=-1)
        bk = bk.reshape(-1, B_HEADS * LANES)

        def lat(a):
            return a[:NL]

        def cx(a):
            return a[NL:]

        def cat_kv(a, H, d):
            return jnp.concatenate([_heads(lat(a), B, T, H, d), _heads(cx(a), B, Tc, H, d)], axis=2)

        att_a = attn_op(A_GROUP, HD ** -0.5, tag + "attn_a")
        att_b = attn_op(1, (B_NOPE + B_ROPE) ** -0.5, tag + "attn_b")
        ya = _unheads(att_a(_heads(lat(aq), B, T, A_HEADS, HD), cat_kv(ak, A_KV, HD), cat_kv(av, A_KV, HD)))
        yb = _unheads(att_b(_heads(lat(bq), B, T, B_HEADS, LANES), cat_kv(bk, B_HEADS, LANES),
                            cat_kv(bv, B_HEADS, B_V)))
        of = scan_op(scan_f, tag + "scan_f")(cqs, kf, cv, gf)
        ob = scan_op(scan_b, tag + "scan_b")(cqs, kb, cv, gb)
        lay_out = lay_lat if last else lay_all
        yc = ew_op(_f_readout, lay_out, ("tok", "tok", "tok", "par", "par"), (True, True, True, True, False),
                   ((C_W, F32),), tag + "readout")(of, ob, cgate, _tile_gain(p["c_out_norm"][l], C_HEADS), seg256)[0]
        if last:
            ycat = jnp.concatenate([ya, yb, yc], axis=1)
        else:
            att_ac = attn_op(A_GROUP, HD ** -0.5, tag + "attn_a_ctx")
            att_bc = attn_op(1, (B_NOPE + B_ROPE) ** -0.5, tag + "attn_b_ctx")
            ya_c = _unheads(att_ac(_heads(cx(aq), B, Tc, A_HEADS, HD), _heads(cx(ak), B, Tc, A_KV, HD),
                                   _heads(cx(av), B, Tc, A_KV, HD)))
            yb_c = _unheads(att_bc(_heads(cx(bq), B, Tc, B_HEADS, LANES), _heads(cx(bk), B, Tc, B_HEADS, LANES),
                                   _heads(cx(bv), B, Tc, B_HEADS, B_V)))
            ycat = jnp.concatenate([jnp.concatenate([ya, ya_c], axis=0), jnp.concatenate([yb, yb_c], axis=0), yc],
                                   axis=1)
        mixo = matmul_op(tag + "w_out")(ycat, p["w_out"][l])
        tok = ew_op(_f_post, lay_out, ("tok", "tok", "seg", "par"), (True,) * 4, ((D, F32),), tag + "postmix")(
            tok, mixo, gt_m, p["g_post_mix"][l][None, :])[0]

        hf = ew_op(_f_premod, lay_out, ("tok", "seg", "seg", "par"), (True,) * 4, ((D, F32),), tag + "preffn")(
            tok, sh_f, sc_f, p["g_pre_ffn"][l][None, :])[0]
        u = matmul_op(tag + "w_ff1")(hf, p["w_ff1"][l])
        z = ew_op(_f_relu2, lay_out, ("tok",), (True,), ((u.shape[1], F32),), tag + "relu2")(u)[0]
        yf = matmul_op(tag + "w_ff2")(z, p["w_ff2"][l])
        tok = ew_op(_f_post, lay_out, ("tok", "tok", "seg", "par"), (True,) * 4, ((D, F32),), tag + "postffn")(
            tok, yf, gt_f, p["g_post_ffn"][l][None, :])[0]
    return tok[:NL]


BIG = ("w_in", "w_q_up", "w_kv_up", "w_out", "w_ff1", "w_ff2")
COL_SHARDED = ("w_in", "w_q_up", "w_kv_up", "w_ff1")
SMALL = ("c_ctx", "b_ada", "g_pre_mix", "g_post_mix", "g_pre_ffn", "g_post_ffn", "a_q_norm", "a_k_norm",
         "b_q_norm", "b_kv_norm", "c_out_norm")
WEIGHTS = ("c_ctx", "w_ada", "b_ada", "g_pre_mix", "g_post_mix", "g_pre_ffn", "g_post_ffn", "w_in", "a_q_norm",
           "a_k_norm", "b_q_norm", "w_q_up", "b_kv_norm", "w_kv_up", "c_lower_bounds", "c_out_norm", "w_out",
           "w_ff1", "w_ff2")


def _assemble_big(gathered, shard_shapes):
    parts = _unpack(gathered, [shard_shapes[n] for n in BIG], lead=(N_DEV,))
    full = {}
    for n, a in zip(BIG, parts):
        if n in COL_SHARDED:
            full[n] = a.transpose(1, 2, 0, 3).reshape(a.shape[1], a.shape[2], N_DEV * a.shape[3])
        else:
            full[n] = a.transpose(1, 0, 2, 3).reshape(a.shape[1], N_DEV * a.shape[2], a.shape[3])
    return full


def _rows128(a):
    return a.reshape(-1, LANES)


def kernel(x, c, ctx, c_ctx, w_ada, b_ada, g_pre_mix, g_post_mix, g_pre_ffn, g_post_ffn, w_in, a_q_norm, a_k_norm, b_q_norm, w_q_up, b_kv_norm, w_kv_up, c_lower_bounds, c_out_norm, w_out, w_ff1, w_ff2, loss_target, m_c_ctx, m_w_ada, m_b_ada, m_g_pre_mix, m_g_post_mix, m_g_pre_ffn, m_g_post_ffn, m_w_in, m_a_q_norm, m_a_k_norm, m_b_q_norm, m_w_q_up, m_b_kv_norm, m_w_kv_up, m_c_lower_bounds, m_c_out_norm, m_w_out, m_w_ff1, m_w_ff2, v_c_ctx, v_w_ada, v_b_ada, v_g_pre_mix, v_g_post_mix, v_g_pre_ffn, v_g_post_ffn, v_w_in, v_a_q_norm, v_a_k_norm, v_b_q_norm, v_w_q_up, v_b_kv_norm, v_w_kv_up, v_c_lower_bounds, v_c_out_norm, v_w_out, v_w_ff1, v_w_ff2):
    W = dict(c_ctx=c_ctx, w_ada=w_ada, b_ada=b_ada, g_pre_mix=g_pre_mix, g_post_mix=g_post_mix, g_pre_ffn=g_pre_ffn,
             g_post_ffn=g_post_ffn, w_in=w_in, a_q_norm=a_q_norm, a_k_norm=a_k_norm, b_q_norm=b_q_norm,
             w_q_up=w_q_up, b_kv_norm=b_kv_norm, w_kv_up=w_kv_up, c_lower_bounds=c_lower_bounds,
             c_out_norm=c_out_norm, w_out=w_out, w_ff1=w_ff1, w_ff2=w_ff2)
    M = dict(c_ctx=m_c_ctx, w_ada=m_w_ada, b_ada=m_b_ada, g_pre_mix=m_g_pre_mix, g_post_mix=m_g_post_mix,
             g_pre_ffn=m_g_pre_ffn, g_post_ffn=m_g_post_ffn, w_in=m_w_in, a_q_norm=m_a_q_norm, a_k_norm=m_a_k_norm,
             b_q_norm=m_b_q_norm, w_q_up=m_w_q_up, b_kv_norm=m_b_kv_norm, w_kv_up=m_w_kv_up,
             c_lower_bounds=m_c_lower_bounds, c_out_norm=m_c_out_norm, w_out=m_w_out, w_ff1=m_w_ff1, w_ff2=m_w_ff2)
    V = dict(c_ctx=v_c_ctx, w_ada=v_w_ada, b_ada=v_b_ada, g_pre_mix=v_g_pre_mix, g_post_mix=v_g_post_mix,
             g_pre_ffn=v_g_pre_ffn, g_post_ffn=v_g_post_ffn, w_in=v_w_in, a_q_norm=v_a_q_norm, a_k_norm=v_a_k_norm,
             b_q_norm=v_b_q_norm, w_q_up=v_w_q_up, b_kv_norm=v_b_kv_norm, w_kv_up=v_w_kv_up,
             c_lower_bounds=v_c_lower_bounds, c_out_norm=v_c_out_norm, w_out=v_w_out, w_ff1=v_w_ff1, w_ff2=v_w_ff2)

    B, T, D = x.shape
    Tc = ctx.shape[1]
    depth = w_ada.shape[0]
    ada_cols = w_ada.shape[2]
    idx = 4 * lax.axis_index("x") + 2 * lax.axis_index("y") + lax.axis_index("c")
    n_cond = N_DEV * B
    cond_rows = -(-(n_cond + 1) // 8) * 8

    clb_cols = c_lower_bounds.shape[2]
    pay1 = _pack([c, c_lower_bounds], F32, 8)
    g1, _ = small_gather(pay1, "gather_cond")
    c_parts, clb_parts = _unpack(g1, [c.shape, c_lower_bounds.shape], lead=(N_DEV,))
    c_all = c_parts.reshape(n_cond, D)
    clb_full = clb_parts.transpose(1, 2, 0, 3).reshape(depth, 2, N_DEV * clb_cols)

    cond_lay = Lay(1, cond_rows, 0, cond_rows, False)

    def ada_shard(c_ctx_, w_ada_):
        cond = jnp.concatenate([c_all, c_ctx_[None, :], jnp.zeros((cond_rows - n_cond - 1, D), F32)], axis=0)
        sc = ew_op(_f_silu, cond_lay, ("tok",), (True,), ((D, F32),), "cond_silu")(cond)[0]
        return jnp.stack([matmul_op(f"l{l}_w_ada")(sc, w_ada_[l]) for l in range(depth)])

    mod_shard, vjp_ada = jax.vjp(ada_shard, c_ctx, w_ada)

    g2, _ = small_gather(_pack([mod_shard], F32, 8), "gather_mod")
    mod_all = _unpack(g2, [mod_shard.shape], lead=(N_DEV,))[0]
    mod_all = mod_all.transpose(1, 2, 0, 3).reshape(depth, cond_rows, N_DEV * ada_cols)
    mine = lax.dynamic_slice_in_dim(mod_all, idx * B, B, axis=1)
    modraw = jnp.concatenate([mine, mod_all[:, n_cond:n_cond + 1]], axis=1)

    shard_shapes = {n: W[n].shape for n in BIG}
    gathered = big_gather(_pack([W[n] for n in BIG], BF16, 16), "gather_weights")

    dims = (B, T, Tc, D)
    small_in = {n: W[n] for n in SMALL if n != "c_ctx"}

    def fwd(x_, modraw_, small_, clb_, gathered_):
        p = dict(small_)
        p.update(_assemble_big(gathered_, shard_shapes))
        p.update(x=x_.reshape(B * T, D), ctx=ctx.reshape(B * Tc, D), modraw=modraw_, clb=clb_)
        return local_forward(dims, p)

    y, vjp_main = jax.vjp(fwd, x, modraw, small_in, clb_full, gathered)
    loss_part, dy = loss_and_grad(y, loss_target.reshape(B * T, D), min(256, Tc))
    dx, dmodraw, dsmall, dclb, dgathered = vjp_main(dy)

    small_names = [n for n in SMALL if n != "c_ctx"]
    pay3 = _pack([dmodraw] + [dsmall[n] for n in small_names] + [dclb, loss_part[0, 0:1]], F32, 8)
    g3, s3 = small_gather(pay3, "gather_small_grads")
    dmod_parts = _unpack(g3, [dmodraw.shape], lead=(N_DEV,))[0]
    tot = _unpack(s3, [dmodraw.shape] + [W[n].shape for n in small_names] + [clb_full.shape, (1,)])
    dmod_tot, small_tot, dclb_tot, loss = tot[0], dict(zip(small_names, tot[1:-2])), tot[-2], tot[-1]
    drows = dmod_parts[:, :, 0:B].transpose(1, 0, 2, 3).reshape(depth, n_cond, N_DEV * ada_cols)
    dcond = jnp.concatenate([drows, dmod_tot[:, B:B + 1], jnp.zeros((depth, cond_rows - n_cond - 1, N_DEV * ada_cols), F32)],
                            axis=1)
    dmod_shard = lax.dynamic_slice_in_dim(dcond, idx * ada_cols, ada_cols, axis=2)
    dc_ctx_part, dw_ada = vjp_ada(dmod_shard)

    _, s4 = small_gather(_pack([dc_ctx_part], F32, 8), "gather_c_ctx_grad")
    small_tot["c_ctx"] = _unpack(s4, [c_ctx.shape])[0]

    recv = scatter_exchange(dgathered, "scatter_grads")
    gsum = sum_slots(recv, "sum_grads")
    big_grads = dict(zip(BIG, _unpack(gsum, [shard_shapes[n] for n in BIG])))

    grads = dict(small_tot)
    grads.update(big_grads)
    grads["w_ada"] = dw_ada
    grads["c_lower_bounds"] = lax.dynamic_slice_in_dim(dclb_tot, idx * clb_cols, clb_cols, axis=2)

    delta, new_m, new_v = {}, {}, {}

    def packed_update(names, row_mult, name):
        shapes = [W[n].shape for n in names]
        res = adamw(*[_pack([src[n] for n in names], F32, row_mult) for src in (W, grads, M, V)], name)
        for dst, buf in zip((delta, new_m, new_v), res):
            dst.update(zip(names, _unpack(buf, shapes)))

    packed_update(list(BIG), 8, "adamw_big")
    packed_update(list(SMALL) + ["c_lower_bounds"], 8, "adamw_small")
    flat = lambda a: a.reshape(-1, a.shape[-1])
    d_, m_, v_ = adamw(flat(w_ada), flat(dw_ada), flat(m_w_ada), flat(v_w_ada), "adamw_ada")
    delta["w_ada"], new_m["w_ada"], new_v["w_ada"] = (a.reshape(w_ada.shape) for a in (d_, m_, v_))

    return (loss.reshape(()), dx, *[grads[n] for n in WEIGHTS], *[delta[n] for n in WEIGHTS],
            *[new_m[n] for n in WEIGHTS], *[new_v[n] for n in WEIGHTS])
```

```python
import numpy as np

import jax
import jax.numpy as jnp
from jax import lax
from jax.experimental import pallas as pl
from jax.experimental.pallas import tpu as pltpu

F32 = jnp.float32
BF16 = jnp.bfloat16

A_HEADS, A_KV, HD = 8, 2, 64
A_GROUP = A_HEADS // A_KV
B_HEADS, B_QR, B_KVR, B_NOPE, B_ROPE, B_V = 4, 192, 128, 64, 32, 64
C_HEADS, C_DK = 4, 64
C_W = C_HEADS * C_DK
GRID_W = 64
CHUNK = 64
ROPE_THETA = 10000.0
EPS = 1e-6
F_TINY = 1e-30
D_IN = 2400
D_IN_PAD = 2560
N_DEV = 8
LANES = 128
NEG = -1e30

ADAM_LR, ADAM_B1, ADAM_B2, ADAM_EPS, ADAM_WD, ADAM_STEP = 0.001, 0.9, 0.999, 1e-08, 0.01, 10

VMEM_LIMIT = 56 * 1024 * 1024
MESH = pl.DeviceIdType.MESH


def _pallas(body, **kw):
    return pl.pallas_call(body, **kw)


def _cparams(sem):
    return pltpu.CompilerParams(dimension_semantics=sem, vmem_limit_bytes=VMEM_LIMIT)


def _split3(x):
    hi = x.astype(BF16)
    r = x - hi.astype(F32)
    mid = r.astype(BF16)
    lo = (r - mid.astype(F32)).astype(BF16)
    return hi, mid, lo


def _nn(a, b):
    return jnp.dot(a, b, preferred_element_type=F32)


def _nt(a, b):
    return lax.dot_general(a, b, (((1,), (1,)), ((), ())), preferred_element_type=F32)


def _tn(a, b):
    return lax.dot_general(a, b, (((0,), (0,)), ((), ())), preferred_element_type=F32)


@jax.custom_vjp
def xdotr(x, m):
    return sum(_nn(p, m) for p in _split3(x))


def _xdotr_fwd(x, m):
    return xdotr(x, m), m


def _xdotr_bwd(m, ct):
    return sum(_nt(p, m) for p in _split3(ct)), None


xdotr.defvjp(_xdotr_fwd, _xdotr_bwd)


@jax.custom_vjp
def xdotl(m, mt, x):
    return sum(_nn(m, p) for p in _split3(x))


def _xdotl_fwd(m, mt, x):
    return xdotl(m, mt, x), (m, mt)


def _xdotl_bwd(res, ct):
    m, mt = res
    return None, None, sum(_nn(mt, p) for p in _split3(ct))


xdotl.defvjp(_xdotl_fwd, _xdotl_bwd)


def _sigmoid(x):
    return 1.0 / (1.0 + jnp.exp(-x))


def _silu(x):
    return x * _sigmoid(x)


def _rms(x, gain, n=None):
    n = x.shape[-1] if n is None else n
    ms = jnp.sum(x * x, axis=-1, keepdims=True) * (1.0 / n)
    return x * lax.rsqrt(ms + EPS) * gain


def _head_rms(x, seg, gain):
    ms = xdotr(x * x, seg) * (1.0 / HD)
    return x * lax.rsqrt(ms + EPS) * gain


class Lay:
    def __init__(self, B, T, Tc, tm, with_ctx):
        self.B, self.T, self.Tc, self.tm = B, T, Tc, tm
        self.nl = T // tm
        self.nc = max(Tc // tm, 1)
        self.n_lat = B * self.nl
        self.n_tiles = self.n_lat + (B * self.nc if with_ctx else 0)
        self.n_seg = 2 * B if with_ctx else B
        self.rows = self.n_tiles * tm

    def seg(self, i):
        return jnp.where(i < self.n_lat, i // self.nl, self.B + (i - self.n_lat) // self.nc)

    def pos(self, i):
        return jnp.where(i < self.n_lat, i % self.nl, self.nl)

    def first(self, i):
        return jnp.where(i < self.n_lat, i % self.nl == 0, (i - self.n_lat) % self.nc == 0)


def _ew_spec(kind, a, lay):
    if kind == "tok":
        return pl.BlockSpec((lay.tm, a.shape[1]), lambda i: (i, 0))
    if kind == "seg":
        return pl.BlockSpec((1, 1, a.shape[2]), lambda i: (lay.seg(i), 0, 0))
    if kind == "pos":
        return pl.BlockSpec((lay.tm, a.shape[1]), lambda i: (lay.pos(i), 0))
    return pl.BlockSpec(a.shape, lambda i: (0,) * a.ndim)


def _ew_load(ref, kind):
    if kind == "seg":
        return ref[0]
    if kind == "tok":
        return ref[...].astype(F32)
    return ref[...]


def _ew_fwd(f, lay, kinds, arrays, outs, name):
    n_in = len(arrays)

    def body(*refs):
        vals = [_ew_load(r, k) for r, k in zip(refs[:n_in], kinds)]
        res = f(*vals)
        for r, o in zip(res, refs[n_in:]):
            o[...] = r.astype(o.dtype)

    return _pallas(
        body, name=name, grid=(lay.n_tiles,),
        in_specs=[_ew_spec(k, a, lay) for k, a in zip(kinds, arrays)],
        out_specs=[pl.BlockSpec((lay.tm, c), lambda i: (i, 0)) for c, _ in outs],
        out_shape=[jax.ShapeDtypeStruct((lay.rows, c), dt) for c, dt in outs],
        compiler_params=_cparams(("parallel",)),
    )(*arrays)


def _ew_bwd(f, lay, kinds, diffs, arrays, cts, name):
    n_in, n_ct = len(arrays), len(cts)
    d_idx = [i for i, d in enumerate(diffs) if d]

    g_shapes, g_specs = [], []
    for i in d_idx:
        a, k = arrays[i], kinds[i]
        if k == "tok":
            g_shapes.append(jax.ShapeDtypeStruct((lay.rows, a.shape[1]), a.dtype))
            g_specs.append(pl.BlockSpec((lay.tm, a.shape[1]), lambda t: (t, 0)))
        elif k == "seg":
            g_shapes.append(jax.ShapeDtypeStruct((lay.n_seg, 1, a.shape[2]), F32))
            g_specs.append(pl.BlockSpec((1, 1, a.shape[2]), lambda t: (lay.seg(t), 0, 0)))
        else:
            g_shapes.append(jax.ShapeDtypeStruct(a.shape, F32))
            g_specs.append(pl.BlockSpec(a.shape, lambda t, nd=a.ndim: (0,) * nd))

    def body(*refs):
        vals = [_ew_load(r, k) for r, k in zip(refs[:n_in], kinds)]
        cvals = tuple(r[...].astype(F32) for r in refs[n_in:n_in + n_ct])
        g_refs = refs[n_in + n_ct:]

        def g(*dv):
            full = list(vals)
            for j, i in enumerate(d_idx):
                full[i] = dv[j]
            return tuple(o.astype(F32) for o in f(*full))

        _, vjp = jax.vjp(g, *[vals[i] for i in d_idx])
        grads = vjp(cvals)
        t = pl.program_id(0)
        for gref, grad, i in zip(g_refs, grads, d_idx):
            k = kinds[i]
            if k == "tok":
                gref[...] = grad.astype(gref.dtype)
            elif k == "seg":
                @pl.when(lay.first(t))
                def _():
                    gref[...] = jnp.zeros_like(gref)

                gref[0] += grad
            else:
                @pl.when(t == 0)
                def _():
                    gref[...] = jnp.zeros_like(gref)

                gref[...] += grad

    res = _pallas(
        body, name=name + "_bwd", grid=(lay.n_tiles,),
        in_specs=[_ew_spec(k, a, lay) for k, a in zip(kinds, arrays)]
        + [pl.BlockSpec((lay.tm, c.shape[1]), lambda i: (i, 0)) for c in cts],
        out_specs=g_specs, out_shape=g_shapes,
        compiler_params=_cparams(("arbitrary",)),
    )(*arrays, *cts)
    out = [None] * n_in
    for gr, i in zip(res, d_idx):
        a = arrays[i]
        if gr.shape != a.shape:
            pad = [(0, a.shape[0] - gr.shape[0])] + [(0, 0)] * (a.ndim - 1)
            gr = jnp.pad(gr, pad)
        out[i] = gr
    return tuple(out)


def ew_op(f, lay, kinds, diffs, outs, name):
    kinds, diffs, outs = tuple(kinds), tuple(diffs), tuple(outs)

    @jax.custom_vjp
    def op(*arrays):
        return tuple(_ew_fwd(f, lay, kinds, arrays, outs, name))

    def fwd(*arrays):
        return op(*arrays), arrays

    def bwd(arrays, cts):
        return _ew_bwd(f, lay, kinds, diffs, arrays, tuple(cts), name)

    op.defvjp(fwd, bwd)
    return op


def _tile(n, cands):
    for c in cands:
        if n % c == 0:
            return c
    return n


def _mm_nn(x, w, name, out_dtype, relu2):
    M, K = x.shape
    N = w.shape[1]
    tm, tn = _tile(M, (512, 256)), _tile(N, (1024, 512, 256, 128))

    def body(x_ref, w_ref, o_ref):
        acc = _nn(x_ref[...].astype(BF16), w_ref[...].astype(BF16))
        if relu2:
            acc = jnp.square(jnp.maximum(acc, 0.0))
        o_ref[...] = acc.astype(o_ref.dtype)

    return _pallas(
        body, name=name, grid=(N // tn, M // tm),
        in_specs=[pl.BlockSpec((tm, K), lambda j, i: (i, 0)), pl.BlockSpec((K, tn), lambda j, i: (0, j))],
        out_specs=pl.BlockSpec((tm, tn), lambda j, i: (i, j)),
        out_shape=jax.ShapeDtypeStruct((M, N), out_dtype),
        compiler_params=_cparams(("parallel", "parallel")),
    )(x, w)


def _through_relu2(dz_ref, z_ref):
    if z_ref is None:
        return dz_ref[...].astype(BF16)
    return (dz_ref[...].astype(F32) * (2.0 * jnp.sqrt(z_ref[...].astype(F32)))).astype(BF16)


def _mm_nt(dy, w, name, out_dtype, z=None):
    M, N = dy.shape
    K = w.shape[0]
    tm = _tile(M, (512, 256))
    tk = _tile(K, (512, 256, 128)) if z is not None else _tile(K, (1024, 512, 256, 128))
    row = pl.BlockSpec((tm, N), lambda j, i: (i, 0))

    def body(*refs):
        dy_ref, z_ref = (refs[0], refs[1]) if z is not None else (refs[0], None)
        w_ref, o_ref = refs[-2], refs[-1]
        o_ref[...] = _nt(_through_relu2(dy_ref, z_ref), w_ref[...].astype(BF16)).astype(o_ref.dtype)

    return _pallas(
        body, name=name, grid=(K // tk, M // tm),
        in_specs=[row] * (2 if z is not None else 1) + [pl.BlockSpec((tk, N), lambda j, i: (j, 0))],
        out_specs=pl.BlockSpec((tm, tk), lambda j, i: (i, j)),
        out_shape=jax.ShapeDtypeStruct((M, K), out_dtype),
        compiler_params=_cparams(("parallel", "parallel")),
    )(*((dy, z, w) if z is not None else (dy, w)))


def _mm_tn(x, dy, name, out_dtype, z=None):
    M, K = x.shape
    N = dy.shape[1]
    tm = _tile(M, (512, 256))
    tk, tn = _tile(K, (1024, 512, 256, 128)), _tile(N, (1024, 512, 256, 128))
    n_m = M // tm
    col = pl.BlockSpec((tm, tn), lambda a, b, m: (m, b))

    def body(*refs):
        x_ref = refs[0]
        dy_ref, z_ref = (refs[1], refs[2]) if z is not None else (refs[1], None)
        o_ref, acc_ref = refs[-2], refs[-1]
        m = pl.program_id(2)

        @pl.when(m == 0)
        def _():
            acc_ref[...] = jnp.zeros_like(acc_ref)

        acc_ref[...] += _tn(x_ref[...].astype(BF16), _through_relu2(dy_ref, z_ref))

        @pl.when(m == n_m - 1)
        def _():
            o_ref[...] = acc_ref[...].astype(o_ref.dtype)

    return _pallas(
        body, name=name, grid=(K // tk, N // tn, n_m),
        in_specs=[pl.BlockSpec((tm, tk), lambda a, b, m: (m, a))] + [col] * (2 if z is not None else 1),
        out_specs=pl.BlockSpec((tk, tn), lambda a, b, m: (a, b)),
        out_shape=jax.ShapeDtypeStruct((K, N), out_dtype),
        scratch_shapes=[pltpu.VMEM((tk, tn), F32)],
        compiler_params=_cparams(("parallel", "parallel", "arbitrary")),
    )(*((x, dy, z) if z is not None else (x, dy)))


def matmul_op(name, out_dtype=F32, relu2=False):
    @jax.custom_vjp
    def op(x, w):
        return _mm_nn(x, w, name, out_dtype, relu2)

    def fwd(x, w):
        y = op(x, w)
        return y, (x, w, y if relu2 else None)

    def bwd(res, dy):
        x, w, z = res
        return _mm_nt(dy, w, name + "_dx", x.dtype, z), _mm_tn(x, dy, name + "_dw", w.dtype, z)

    op.defvjp(fwd, bwd)
    return op


class AttnLay:
    def __init__(self, B, T, Tc, tq, ctx_queries):
        self.B, self.T, self.Tc, self.tq = B, T, Tc, tq
        self.nql, self.nqc = T // tq, Tc // tq
        self.nq = self.nql + (self.nqc if ctx_queries else 0)
        self.rows_q = B * T + (B * Tc if ctx_queries else 0)
        self.ctx0 = B * T // Tc

    def qrow(self, b, i):
        return jnp.where(i < self.nql, b * self.nql + i, self.B * self.nql + b * self.nqc + (i - self.nql))


def _attn_specs(al, wq, wk, wv):
    qs = lambda w: pl.BlockSpec((al.tq, w), lambda b, i: (al.qrow(b, i), 0))
    lat = lambda w: pl.BlockSpec((al.T, w), lambda b, i: (b, 0))
    ctx = lambda w: pl.BlockSpec((al.Tc, w), lambda b, i: (al.ctx0 + b, 0))
    return qs, lat, ctx


def _stack(ref, g, group, width, tq):
    parts = [ref[:, (g * group + j) * width:(g * group + j + 1) * width].astype(F32) for j in range(group)]
    return parts[0] if group == 1 else jnp.concatenate(parts, axis=0)


def _attn_fwd(q, k, v, cfg, al, name):
    n_kv, group, dq, dv, scale = cfg
    tq = al.tq
    wq, wk, wv, wo = q.shape[1], k.shape[1], v.shape[1], n_kv * group * dv
    qs, lat, ctx = _attn_specs(al, wq, wk, wv)

    def body(q_ref, kl_ref, kc_ref, vl_ref, vc_ref, o_ref, lse_ref):
        bias = jnp.where(pl.program_id(1) < al.nql, 0.0, NEG)
        lane = lax.broadcasted_iota(jnp.int32, (tq, LANES), 1)
        lse_all = jnp.zeros((tq, LANES), F32)
        for g in range(n_kv):
            q4 = _stack(q_ref, g, group, dq, tq).astype(BF16)
            ksl, vsl = slice(g * dq, (g + 1) * dq), slice(g * dv, (g + 1) * dv)
            s_l = _nt(q4, kl_ref[:, ksl]) * scale + bias
            s_c = _nt(q4, kc_ref[:, ksl]) * scale
            m = jnp.maximum(jnp.max(s_l, axis=1, keepdims=True), jnp.max(s_c, axis=1, keepdims=True))
            p_l, p_c = jnp.exp(s_l - m), jnp.exp(s_c - m)
            l = jnp.sum(p_l, axis=1, keepdims=True) + jnp.sum(p_c, axis=1, keepdims=True)
            o = (_nn(p_l.astype(BF16), vl_ref[:, vsl]) + _nn(p_c.astype(BF16), vc_ref[:, vsl])) / l
            lse = m + jnp.log(l)
            for j in range(group):
                h = g * group + j
                o_ref[:, h * dv:(h + 1) * dv] = o[j * tq:(j + 1) * tq].astype(o_ref.dtype)
                lse_all = jnp.where(lane == h, lse[j * tq:(j + 1) * tq], lse_all)
        lse_ref[...] = lse_all

    return _pallas(
        body, name=name, grid=(al.B, al.nq),
        in_specs=[qs(wq), lat(wk), ctx(wk), lat(wv), ctx(wv)],
        out_specs=[qs(wo), qs(LANES)],
        out_shape=[jax.ShapeDtypeStruct((al.rows_q, wo), BF16), jax.ShapeDtypeStruct((al.rows_q, LANES), F32)],
        compiler_params=_cparams(("parallel", "parallel")),
    )(q, k, k, v, v)


def _attn_bwd(q, k, v, o, lse, do, cfg, al, name):
    n_kv, group, dq, dv, scale = cfg
    tq = al.tq
    wq, wk, wv, wo = q.shape[1], k.shape[1], v.shape[1], n_kv * group * dv
    qs, lat, ctx = _attn_specs(al, wq, wk, wv)
    B, T, Tc = al.B, al.T, al.Tc

    def body(q_ref, kl_ref, kc_ref, vl_ref, vc_ref, o_ref, lse_ref, do_ref,
             dq_ref, dkl_ref, dkc_ref, dvl_ref, dvc_ref, akl, akc, avl, avc):
        i = pl.program_id(1)

        @pl.when(i == 0)
        def _():
            for acc in (akl, akc, avl, avc):
                acc[...] = jnp.zeros_like(acc)

        bias = jnp.where(i < al.nql, 0.0, NEG)
        lane = lax.broadcasted_iota(jnp.int32, (tq, LANES), 1)
        lse_tile = lse_ref[...]
        for g in range(n_kv):
            q4 = _stack(q_ref, g, group, dq, tq).astype(BF16)
            do4 = _stack(do_ref, g, group, dv, tq)
            o4 = _stack(o_ref, g, group, dv, tq)
            cols = [jnp.sum(jnp.where(lane == g * group + j, lse_tile, 0.0), axis=1, keepdims=True)
                    for j in range(group)]
            lse4 = cols[0] if group == 1 else jnp.concatenate(cols, axis=0)
            dl = jnp.sum(do4 * o4, axis=1, keepdims=True)
            dob = do4.astype(BF16)
            ksl, vsl = slice(g * dq, (g + 1) * dq), slice(g * dv, (g + 1) * dv)
            dq4 = jnp.zeros((group * tq, dq), F32)
            for k_ref, v_ref, ak, av, b_ in ((kl_ref, vl_ref, akl, avl, bias), (kc_ref, vc_ref, akc, avc, 0.0)):
                kk = k_ref[:, ksl]
                p = jnp.exp(_nt(q4, kk) * scale + b_ - lse4)
                dp = _nt(dob, v_ref[:, vsl])
                ds = (p * (dp - dl) * scale).astype(BF16)
                dq4 = dq4 + _nn(ds, kk)
                ak[:, ksl] += _tn(ds, q4)
                av[:, vsl] += _tn(p.astype(BF16), dob)
            for j in range(group):
                h = g * group + j
                dq_ref[:, h * dq:(h + 1) * dq] = dq4[j * tq:(j + 1) * tq]

        @pl.when(i == al.nq - 1)
        def _():
            dkl_ref[...] = akl[...].astype(dkl_ref.dtype)
            dkc_ref[...] = akc[...].astype(dkc_ref.dtype)
            dvl_ref[...] = avl[...].astype(dvl_ref.dtype)
            dvc_ref[...] = avc[...].astype(dvc_ref.dtype)

    own = lambda rows, w: pl.BlockSpec((rows, w), lambda b, i: (b, 0))
    dq_, dkl, dkc, dvl, dvc = _pallas(
        body, name=name + "_bwd", grid=(B, al.nq),
        in_specs=[qs(wq), lat(wk), ctx(wk), lat(wv), ctx(wv), qs(wo), qs(LANES), qs(wo)],
        out_specs=[qs(wq), own(T, wk), own(Tc, wk), own(T, wv), own(Tc, wv)],
        out_shape=[jax.ShapeDtypeStruct((al.rows_q, wq), F32),
                   jax.ShapeDtypeStruct((B * T, wk), k.dtype), jax.ShapeDtypeStruct((B * Tc, wk), k.dtype),
                   jax.ShapeDtypeStruct((B * T, wv), v.dtype), jax.ShapeDtypeStruct((B * Tc, wv), v.dtype)],
        scratch_shapes=[pltpu.VMEM((T, wk), F32), pltpu.VMEM((Tc, wk), F32),
                        pltpu.VMEM((T, wv), F32), pltpu.VMEM((Tc, wv), F32)],
        compiler_params=_cparams(("parallel", "arbitrary")),
    )(q, k, k, v, v, o, lse, do)
    if dq_.shape[0] != q.shape[0]:
        dq_ = jnp.pad(dq_, ((0, q.shape[0] - dq_.shape[0]), (0, 0)))
    return dq_, jnp.concatenate([dkl, dkc], axis=0), jnp.concatenate([dvl, dvc], axis=0)


def attn_op(cfg, al, name):
    @jax.custom_vjp
    def op(q, k, v):
        return _attn_fwd(q, k, v, cfg, al, name)[0]

    def fwd(q, k, v):
        o, lse = _attn_fwd(q, k, v, cfg, al, name)
        return o, (q, k, v, o, lse)

    def bwd(res, do):
        return _attn_bwd(*res, do, cfg, al, name)

    op.defvjp(fwd, bwd)
    return op


SCAN_WIDTHS = (32, 16, 8, 4, 2, 1)
N_CM = 2 + 2 * len(SCAN_WIDTHS)


def _scan_consts(reverse):
    C = CHUNK
    t = np.arange(C)[:, None]
    s = np.arange(C)[None, :]
    blocks = [(s <= t), (s > t)]
    for w in SCAN_WIDTHS:
        blocks.append((s <= t) & (s // w == t // w))
    for w in SCAN_WIDTHS:
        blocks.append((s > t) & (s // w == t // w))
    masks = [np.eye(C, dtype=bool)]
    for w in SCAN_WIDTHS:
        masks.append((t // (2 * w) == s // (2 * w)) & ((t // w) % 2 == 1) & ((s // w) % 2 == 0))
    if reverse:
        blocks = [b[::-1, ::-1] for b in blocks]
        masks = [m[::-1, ::-1] for m in masks]
    cm = np.concatenate([b.astype(np.float32) for b in blocks] + [np.ones((8, C), np.float32)], axis=0)
    mw = np.stack([np.tile(m.astype(np.float32), (C_HEADS, 1)) for m in masks])
    rows = np.arange(C_HEADS * C)[:, None] // C
    lane = np.arange(C_W)[None, :] // C_DK
    hm = (rows == lane).astype(np.float32)
    bd = (np.arange(C_W)[:, None] // C_DK == lane).astype(np.float32)
    return (jnp.asarray(cm, BF16), jnp.asarray(cm.T.copy(), BF16), jnp.asarray(mw, F32),
            jnp.asarray(hm, F32), jnp.asarray(bd, F32))


def _scan_chunk(st, q, k, v, g, cm, cmt, mw, hm, bd):
    C = CHUNK
    cs = xdotl(cm, cmt, g)
    b = cs[0:C]
    rest = cs[C:2 * C]
    tot = cs[N_CM * C:N_CM * C + 1]
    kb = k.astype(BF16)

    def stack(a):
        return (jnp.concatenate([a] * C_HEADS, axis=0) * hm).astype(BF16)

    a = _nt(stack(q), kb) * mw[0]
    for i in range(len(SCAN_WIDTHS)):
        eq = jnp.exp(jnp.minimum(cs[(2 + i) * C:(3 + i) * C], 0.0))
        ek = jnp.exp(jnp.minimum(cs[(2 + len(SCAN_WIDTHS) + i) * C:(3 + len(SCAN_WIDTHS) + i) * C], 0.0))
        a = a + _nt(stack(q * eq), (k * ek).astype(BF16)) * mw[i + 1]
    oh = _nn(a.astype(BF16), v.astype(BF16)) * hm
    o = oh[0:C]
    for h in range(1, C_HEADS):
        o = o + oh[h * C:(h + 1) * C]
    o = o + _nt((q * jnp.exp(b)).astype(BF16), st.astype(BF16))
    st_new = st * jnp.exp(tot) + _tn(v.astype(BF16), (k * jnp.exp(rest)).astype(BF16)) * bd
    return o, st_new


class ScanLay:
    def __init__(self, B, T, Tc, reverse):
        self.B, self.reverse = B, reverse
        self.ncc, self.ncl = Tc // CHUNK, T // CHUNK
        self.ntot = self.ncc + self.ncl
        self.ctx0, self.lat_per, self.ctx_per = B * T // CHUNK, T // CHUNK, Tc // CHUNK

    def row(self, b, j):
        if self.reverse:
            jc, jl = self.ncc - 1 - j, self.ncl - 1 - (j - self.ncc)
        else:
            jc, jl = j, j - self.ncc
        return jnp.where(j < self.ncc, self.ctx0 + b * self.ctx_per + jc, b * self.lat_per + jl)


def _scan_fwd(q, k, v, g, sl, name):
    N = q.shape[0]
    consts = _scan_consts(sl.reverse)
    tok = pl.BlockSpec((CHUNK, C_W), lambda b, j: (sl.row(b, j), 0))
    cspecs = [pl.BlockSpec(c.shape, lambda b, j, nd=c.ndim: (0,) * nd) for c in consts]

    def body(q_ref, k_ref, v_ref, g_ref, cm, cmt, mw, hm, bd, o_ref, st_ref, st):
        @pl.when(pl.program_id(1) == 0)
        def _():
            st[...] = jnp.zeros_like(st)

        st_ref[0, 0] = st[...]
        o, st_new = _scan_chunk(st[...], q_ref[...], k_ref[...], v_ref[...], g_ref[...],
                                cm[...], cmt[...], mw[...], hm[...], bd[...])
        o_ref[...] = o
        st[...] = st_new

    return _pallas(
        body, name=name, grid=(sl.B, sl.ntot),
        in_specs=[tok] * 4 + cspecs,
        out_specs=[tok, pl.BlockSpec((1, 1, C_W, C_W), lambda b, j: (b, j, 0, 0))],
        out_shape=[jax.ShapeDtypeStruct((N, C_W), F32), jax.ShapeDtypeStruct((sl.B, sl.ntot, C_W, C_W), F32)],
        scratch_shapes=[pltpu.VMEM((C_W, C_W), F32)],
        compiler_params=_cparams(("parallel", "arbitrary")),
    )(q, k, v, g, *consts)


def _scan_bwd(q, k, v, g, states, do, sl, name):
    N = q.shape[0]
    consts = _scan_consts(sl.reverse)
    last = sl.ntot - 1
    tok = pl.BlockSpec((CHUNK, C_W), lambda b, j: (sl.row(b, last - j), 0))
    cspecs = [pl.BlockSpec(c.shape, lambda b, j, nd=c.ndim: (0,) * nd) for c in consts]

    def body(q_ref, k_ref, v_ref, g_ref, st_ref, do_ref, cm, cmt, mw, hm, bd,
             dq_ref, dk_ref, dv_ref, dg_ref, dst):
        @pl.when(pl.program_id(1) == 0)
        def _():
            dst[...] = jnp.zeros_like(dst)

        cv = (cm[...], cmt[...], mw[...], hm[...], bd[...])
        _, vjp = jax.vjp(lambda s_, q_, k_, v_, g_: _scan_chunk(s_, q_, k_, v_, g_, *cv),
                         st_ref[0, 0], q_ref[...], k_ref[...], v_ref[...], g_ref[...])
        ds, dq, dk, dv, dg = vjp((do_ref[...], dst[...]))
        dq_ref[...] = dq
        dk_ref[...] = dk
        dv_ref[...] = dv
        dg_ref[...] = dg
        dst[...] = ds

    return _pallas(
        body, name=name + "_bwd", grid=(sl.B, sl.ntot),
        in_specs=[tok] * 4 + [pl.BlockSpec((1, 1, C_W, C_W), lambda b, j: (b, last - j, 0, 0)), tok] + cspecs,
        out_specs=[tok] * 4,
        out_shape=[jax.ShapeDtypeStruct((N, C_W), F32)] * 4,
        scratch_shapes=[pltpu.VMEM((C_W, C_W), F32)],
        compiler_params=_cparams(("parallel", "arbitrary")),
    )(q, k, v, g, states, do, *consts)


def scan_op(sl, name):
    @jax.custom_vjp
    def op(q, k, v, g):
        return _scan_fwd(q, k, v, g, sl, name)[0]

    def fwd(q, k, v, g):
        o, states = _scan_fwd(q, k, v, g, sl, name)
        return o, (q, k, v, g, states)

    def bwd(res, do):
        return tuple(_scan_bwd(*res, do, sl, name))

    op.defvjp(fwd, bwd)
    return op


def loss_and_grad(y, target, tm):
    N, D = y.shape

    def body(y_ref, t_ref, dy_ref, l_ref):
        @pl.when(pl.program_id(0) == 0)
        def _():
            l_ref[...] = jnp.zeros_like(l_ref)

        e = y_ref[...] - t_ref[...]
        dy_ref[...] = e * (1.0 / D)
        l_ref[...] += 0.5 * jnp.sum(jnp.sum(e * e, axis=1, keepdims=True) * (1.0 / D), axis=0, keepdims=True)

    dy, lp = _pallas(
        body, name="loss_head", grid=(N // tm,),
        in_specs=[pl.BlockSpec((tm, D), lambda i: (i, 0))] * 2,
        out_specs=[pl.BlockSpec((tm, D), lambda i: (i, 0)), pl.BlockSpec((8, LANES), lambda i: (0, 0))],
        out_shape=[jax.ShapeDtypeStruct((N, D), F32), jax.ShapeDtypeStruct((8, LANES), F32)],
        compiler_params=_cparams(("arbitrary",)),
    )(y, target)
    return lp, dy


def _adam_math(w, g, m, v):
    mn = ADAM_B1 * m + (1.0 - ADAM_B1) * g
    vn = ADAM_B2 * v + (1.0 - ADAM_B2) * jnp.square(g)
    m_hat = mn / (1.0 - ADAM_B1 ** ADAM_STEP)
    v_hat = vn / (1.0 - ADAM_B2 ** ADAM_STEP)
    return -ADAM_LR * (m_hat / (jnp.sqrt(v_hat) + ADAM_EPS) + ADAM_WD * w), mn, vn


ROW_TILES = (512, 256, 128, 64, 32, 16, 8)


def adamw(w, g, m, v, name):
    R, C = w.shape
    tr = _tile(R, ROW_TILES)

    def body(w_ref, g_ref, m_ref, v_ref, d_ref, mo_ref, vo_ref):
        d_ref[...], mo_ref[...], vo_ref[...] = _adam_math(w_ref[...], g_ref[...], m_ref[...], v_ref[...])

    spec = pl.BlockSpec((tr, C), lambda i: (i, 0))
    return _pallas(
        body, name=name, grid=(R // tr,), in_specs=[spec] * 4, out_specs=[spec] * 3,
        out_shape=[jax.ShapeDtypeStruct((R, C), F32)] * 3,
        compiler_params=_cparams(("parallel",)),
    )(w, g, m, v)


def adamw_slots(w, recv, m, v, name):
    R, C = w.shape
    tr = _tile(R, ROW_TILES[1:])

    def body(w_ref, r_ref, m_ref, v_ref, g_ref, d_ref, mo_ref, vo_ref):
        g = r_ref[0].astype(F32)
        for k in range(1, N_DEV):
            g = g + r_ref[k].astype(F32)
        g_ref[...] = g
        d_ref[...], mo_ref[...], vo_ref[...] = _adam_math(w_ref[...], g, m_ref[...], v_ref[...])

    spec = pl.BlockSpec((tr, C), lambda i: (i, 0))
    return _pallas(
        body, name=name, grid=(R // tr,),
        in_specs=[spec, pl.BlockSpec((N_DEV, tr, C), lambda i: (0, i, 0)), spec, spec], out_specs=[spec] * 4,
        out_shape=[jax.ShapeDtypeStruct((R, C), F32)] * 4,
        compiler_params=_cparams(("parallel",)),
    )(w, recv, m, v)


def _me():
    return lax.axis_index("x"), lax.axis_index("y"), lax.axis_index("c")


def _gather_many(x_refs, out_refs, send_sems, recv_sems, local_sems):
    x, y, c = _me()
    me, sibling = (x, y, c), (x, y, 1 - c)
    chips = [(1 - x, y), (x, 1 - y), (1 - x, 1 - y)]
    arrs = range(len(x_refs))

    def slot(a, px, py, pc):
        return out_refs[a].at[4 * px + 2 * py + pc]

    def copy(a, k, block, to, src=None):
        return pltpu.make_async_remote_copy(
            src_ref=slot(a, *block) if src is None else src, dst_ref=slot(a, *block),
            send_sem=send_sems.at[7 * a + k], recv_sem=recv_sems.at[7 * a + k], device_id=to, device_id_type=MESH)

    mine = [pltpu.make_async_copy(x_refs[a], slot(a, *me), local_sems.at[a]) for a in arrs]
    for cp in mine:
        cp.start()
    first = []
    for a in arrs:
        first.append(copy(a, 0, me, sibling, src=x_refs[a]))
        first += [copy(a, 1 + j, me, (*chip, c), src=x_refs[a]) for j, chip in enumerate(chips)]
    for cp in first:
        cp.start()
    passed = []
    for j, chip in enumerate(chips):
        for a in arrs:
            copy(a, 1 + j, (*chip, c), me).wait_recv()
            fwd = copy(a, 4 + j, (*chip, c), sibling)
            fwd.start()
            passed.append(fwd)
    for a in arrs:
        copy(a, 0, sibling, me).wait_recv()
    for j, chip in enumerate(chips):
        for a in arrs:
            copy(a, 4 + j, (*chip, 1 - c), me).wait_recv()
    for cp in first + passed:
        cp.wait_send()
    for cp in mine:
        cp.wait()


def _comm_scratch(n):
    return [pltpu.SemaphoreType.DMA((7 * n,)), pltpu.SemaphoreType.DMA((7 * n,)), pltpu.SemaphoreType.DMA((n,))]


def small_gather(xb, name):
    R = xb.shape[0]

    def body(x_ref, out_ref, sum_ref, send_sems, recv_sems, local_sems):
        _gather_many([x_ref], [out_ref], send_sems, recv_sems, local_sems)
        acc = out_ref[0]
        for k in range(1, N_DEV):
            acc = acc + out_ref[k]
        sum_ref[...] = acc

    vm = pl.BlockSpec(memory_space=pltpu.VMEM)
    return _pallas(
        body, name=name, in_specs=[vm], out_specs=[vm, vm],
        out_shape=[jax.ShapeDtypeStruct((N_DEV, R, LANES), xb.dtype), jax.ShapeDtypeStruct((R, LANES), xb.dtype)],
        scratch_shapes=_comm_scratch(1),
        compiler_params=pltpu.CompilerParams(vmem_limit_bytes=VMEM_LIMIT),
    )(xb)


def big_gather(xs, name):
    n = len(xs)

    def body(*refs):
        _gather_many(refs[:n], refs[n:2 * n], *refs[2 * n:])

    hbm = pl.BlockSpec(memory_space=pl.ANY)
    return _pallas(
        body, name=name, in_specs=[hbm] * n, out_specs=[hbm] * n,
        out_shape=[jax.ShapeDtypeStruct((N_DEV,) + a.shape, a.dtype) for a in xs],
        scratch_shapes=_comm_scratch(n),
    )(*xs)


def scatter_exchange(gs, name):
    n = len(gs)
    rels = [(dx, dy, dc) for dx in (0, 1) for dy in (0, 1) for dc in (0, 1) if (dx, dy, dc) != (0, 0, 0)]

    def body(*refs):
        g_refs, r_refs = refs[:n], refs[n:2 * n]
        send_sems, recv_sems, local_sems = refs[2 * n:]
        x, y, c = _me()
        me = 4 * x + 2 * y + c
        mine = [pltpu.make_async_copy(g_refs[a].at[me], r_refs[a].at[me], local_sems.at[a]) for a in range(n)]
        for cp in mine:
            cp.start()
        copies = []
        for r, (dx, dy, dc) in enumerate(rels):
            px, py, pc = (x + dx) % 2, (y + dy) % 2, (c + dc) % 2
            for a in range(n):
                copies.append(pltpu.make_async_remote_copy(
                    src_ref=g_refs[a].at[4 * px + 2 * py + pc], dst_ref=r_refs[a].at[me],
                    send_sem=send_sems.at[7 * a + r], recv_sem=recv_sems.at[7 * a + r],
                    device_id=(px, py, pc), device_id_type=MESH))
        for cp in copies:
            cp.start()
        for cp in copies:
            cp.wait()
        for cp in mine:
            cp.wait()

    hbm = pl.BlockSpec(memory_space=pl.ANY)
    return _pallas(
        body, name=name, in_specs=[hbm] * n, out_specs=[hbm] * n,
        out_shape=[jax.ShapeDtypeStruct(a.shape, a.dtype) for a in gs],
        scratch_shapes=_comm_scratch(n),
    )(*gs)


def _pack(arrs, dtype, row_mult):
    flat = jnp.concatenate([a.astype(dtype).reshape(-1) for a in arrs])
    pad = (-flat.shape[0]) % (LANES * row_mult)
    if pad:
        flat = jnp.concatenate([flat, jnp.zeros((pad,), dtype)])
    return flat.reshape(-1, LANES)


def _unpack(buf, shapes, lead=()):
    flat = buf.reshape(*lead, -1)
    out, off = [], 0
    for s in shapes:
        n = int(np.prod(s))
        out.append(flat[..., off:off + n].reshape(*lead, *s))
        off += n
    return out


def _rope_tables(T, tm):
    pos = np.arange(T)
    row, col = pos // GRID_W, pos % GRID_W

    def tab(rot_dim):
        nf = rot_dim // 4
        inv = ROPE_THETA ** (-np.arange(nf, dtype=np.float32) / nf)
        ang = np.concatenate([row[:, None].astype(np.float32) * inv, col[:, None].astype(np.float32) * inv], axis=-1)
        ang = ang.astype(np.float32)
        cos, sin = np.cos(ang), np.sin(ang)
        return np.concatenate([cos, cos], -1), np.concatenate([-sin, sin], -1)

    c64, s64 = tab(HD)
    c32, s32 = tab(B_ROPE)
    ca, sa = np.tile(c64, (1, A_HEADS)), np.tile(s64, (1, A_HEADS))
    cb = np.concatenate([c32, np.ones((T, LANES - B_ROPE), np.float32)], -1)
    sb = np.concatenate([s32, np.zeros((T, LANES - B_ROPE), np.float32)], -1)
    one, zero = np.ones((T, B_NOPE), np.float32), np.zeros((T, B_NOPE), np.float32)
    tail1, tail0 = np.ones((T, LANES - B_NOPE - B_ROPE), np.float32), np.zeros((T, LANES - B_NOPE - B_ROPE), np.float32)
    cq = np.tile(np.concatenate([one, c32, tail1], -1), (1, B_HEADS))
    sq = np.tile(np.concatenate([zero, s32, tail0], -1), (1, B_HEADS))

    def fin(a, ident):
        return jnp.asarray(np.concatenate([a, np.full((tm, a.shape[1]), ident, np.float32)], 0), F32)

    return fin(ca, 1.0), fin(sa, 0.0), fin(cb, 1.0), fin(sb, 0.0), fin(cq, 1.0), fin(sq, 0.0)


def _swap_matrix(width, starts, half):
    p = np.zeros((width, width), np.float32)
    for s in starts:
        for i in range(half):
            p[s + i, s + half + i] = 1.0
            p[s + half + i, s + i] = 1.0
    return jnp.asarray(p, BF16)


def _seg_matrix(width):
    h = np.arange(width) // HD
    return jnp.asarray((h[:, None] == h[None, :]).astype(np.float32), BF16)


def _key_slot_matrices():
    e1 = np.zeros((B_HEADS * B_NOPE, B_HEADS * LANES), np.float32)
    e2 = np.zeros((LANES, B_HEADS * LANES), np.float32)
    for h in range(B_HEADS):
        for i in range(B_NOPE):
            e1[h * B_NOPE + i, h * LANES + i] = 1.0
        for i in range(B_ROPE):
            e2[i, h * LANES + B_NOPE + i] = 1.0
    return jnp.asarray(e1, BF16), jnp.asarray(e2, BF16)


def _f_premod(x, sh, sc, g):
    return (_rms(x, g) * (1.0 + sc) + sh,)


def _f_post(x, y, gt, g):
    return (x + gt * _rms(y, g),)


def _f_bias(raw, b):
    return (raw + b,)


def _f_silu(x):
    return (_silu(x),)


def _f_readout(of, ob, gate, gain, seg):
    return (_head_rms(of + ob, seg, gain) * _silu(gate),)


def _f_bq(bq, cq, sq, pq):
    return (bq * cq + xdotr(bq, pq) * sq,)


def _f_bk(bkn, bkr, e1, e2):
    return (xdotr(bkn, e1) + xdotr(bkr, e2),)


def _make_f_feat(layer):
    def f(feat, ca, sa, cb, sb, gaq, gak, gbq, gbkv, c00, c01, c10, c11, seg, pa, pb):
        aq = _head_rms(feat[:, 0:512], seg, gaq)
        ak = _head_rms(feat[:, 512:640], seg[0:128, 0:128], gak)
        av = feat[:, 640:768]
        aq = aq * ca + xdotr(aq, pa) * sa
        ak = ak * ca[:, 0:128] + xdotr(ak, pa[0:128, 0:128]) * sa[:, 0:128]
        bqn = _rms(feat[:, 768:1024], gbq, B_QR)
        bkvn = _rms(feat[:, 1024:1152], gbkv)
        bkr = feat[:, 1152:1280]
        bkr = bkr * cb + xdotr(bkr, pb) * sb
        cq = _silu(feat[:, 1280:1536])
        zf, zb = feat[:, 1536:1792], feat[:, 1792:2048]
        if layer == 0:
            lbf = lbb = 0.0
        else:
            def share(c0, c1):
                m = jnp.maximum(c0, c1)
                e0, e1 = jnp.exp(c0 - m), jnp.exp(c1 - m)
                return e1 / (e0 + e1)
            lbf, lbb = share(c00, c10), share(c01, c11)

        def gate(z, lb):
            f_ = lb + (1.0 - lb) * _sigmoid(z)
            return (1.0 - lb) * _sigmoid(-z), jnp.log(jnp.maximum(f_, F_TINY))

        kf, gf = gate(zf, lbf)
        kb, gb = gate(zb, lbb)
        return aq, ak, av, bqn, bkvn, bkr, cq, kf, gf, kb, gb, feat[:, 2048:2304], feat[:, 2304:2560]

    return f


def _pad_w_in(w):
    z = lambda n: jnp.zeros((w.shape[0], n), w.dtype)
    return jnp.concatenate([w[:, 0:960], z(64), w[:, 960:1120], z(96), w[:, 1120:2400]], axis=1)


def _pad_w_q_up(w):
    w4 = w.reshape(B_QR, B_HEADS, B_NOPE + B_ROPE)
    w4 = jnp.pad(w4, ((0, 256 - B_QR), (0, 0), (0, LANES - B_NOPE - B_ROPE)))
    return w4.reshape(256, B_HEADS * LANES)


def _split_w_kv_up(w):
    w4 = w.reshape(B_KVR, B_HEADS, B_NOPE + B_V)
    return w4[:, :, :B_NOPE].reshape(B_KVR, -1), w4[:, :, B_NOPE:].reshape(B_KVR, -1)


def _tile_gain(g, reps, width=None):
    t = jnp.tile(g, reps)
    if width is not None and width > t.shape[0]:
        t = jnp.pad(t, (0, width - t.shape[0]))
    return t[None, :]


def local_forward(dims, p):
    B, T, Tc, D = dims
    tm = min(256, Tc)
    NL = B * T
    lay_all = Lay(B, T, Tc, tm, True)
    lay_lat = Lay(B, T, Tc, tm, False)
    ca, sa, cb, sb, cq, sq = _rope_tables(T, tm)
    seg512, seg256 = _seg_matrix(512), _seg_matrix(C_W)
    pa = _swap_matrix(512, range(0, 512, HD), HD // 2)
    pb = _swap_matrix(LANES, [0], B_ROPE // 2)
    pq = _swap_matrix(512, [h * LANES + B_NOPE for h in range(B_HEADS)], B_ROPE // 2)
    e1, e2 = _key_slot_matrices()
    scan_f, scan_b = ScanLay(B, T, Tc, False), ScanLay(B, T, Tc, True)
    cfg_a = (A_KV, A_GROUP, HD, HD, HD ** -0.5)
    cfg_b = (B_HEADS, 1, LANES, B_V, (B_NOPE + B_ROPE) ** -0.5)
    tq = min(128, Tc)

    tok = jnp.concatenate([p["x"], p["ctx"]], axis=0)
    depth = p["modraw"].shape[0]
    for l in range(depth):
        last = l == depth - 1
        tag = f"l{l}_"
        bias_lay = Lay(1, 8, 0, 8, False)
        raw8 = jnp.pad(p["modraw"][l], ((0, 8 - B - 1), (0, 0)))
        mod = ew_op(_f_bias, bias_lay, ("tok", "par"), (True, True), ((6 * D, F32),), tag + "ada_bias")(
            raw8, p["b_ada"][l][None, :])[0]
        seg_rows = jnp.concatenate([mod[0:B], jnp.broadcast_to(mod[B:B + 1], (B, 6 * D))], axis=0)[:, None, :]
        sh_m, sc_m, gt_m, sh_f, sc_f, gt_f = [seg_rows[:, :, i * D:(i + 1) * D] for i in range(6)]

        h = ew_op(_f_premod, lay_all, ("tok", "seg", "seg", "par"), (True,) * 4, ((D, BF16),), tag + "premix")(
            tok, sh_m, sc_m, p["g_pre_mix"][l][None, :])[0]
        feat = matmul_op(tag + "w_in")(h, _pad_w_in(p["w_in"][l]))
        clb = p["clb"]
        feats = ew_op(
            _make_f_feat(l), lay_all,
            ("tok", "pos", "pos", "pos", "pos") + ("par",) * 11,
            (True,) + (False,) * 4 + (True,) * 8 + (False,) * 3,
            ((512, F32), (128, BF16), (128, BF16), (256, BF16), (128, BF16), (128, F32)) + ((C_W, F32),) * 7,
            tag + "feat")(
            feat, ca, sa, cb, sb,
            _tile_gain(p["a_q_norm"][l], A_HEADS), _tile_gain(p["a_k_norm"][l], A_KV),
            _tile_gain(p["b_q_norm"][l], 1, 256), _tile_gain(p["b_kv_norm"][l], 1),
            clb[0, 0][None, :], clb[0, 1][None, :], clb[1, 0][None, :], clb[1, 1][None, :],
            seg512, pa, pb)
        aq, ak, av, bqn, bkvn, bkr, cqs, kf, gf, kb, gb, cv, cgate = feats
        bq = matmul_op(tag + "w_q_up")(bqn, _pad_w_q_up(p["w_q_up"][l]))
        bq = ew_op(_f_bq, lay_all, ("tok", "pos", "pos", "par"), (True, False, False, False), ((512, F32),),
                   tag + "bq_rope")(bq, cq, sq, pq)[0]
        w_kn, w_v = _split_w_kv_up(p["w_kv_up"][l])
        bkn = matmul_op(tag + "w_k_up")(bkvn, w_kn)
        bv = matmul_op(tag + "w_v_up", BF16)(bkvn, w_v)
        bk = ew_op(_f_bk, lay_all, ("tok", "tok", "par", "par"), (True, True, False, False),
                   ((B_HEADS * LANES, BF16),), tag + "bk_slots")(bkn, bkr, e1, e2)[0]

        al = AttnLay(B, T, Tc, tq, not last)
        ya = attn_op(cfg_a, al, tag + "attn_a")(aq, ak, av)
        yb = attn_op(cfg_b, al, tag + "attn_b")(bq, bk, bv)
        of = scan_op(scan_f, tag + "scan_f")(cqs, kf, cv, gf)
        ob = scan_op(scan_b, tag + "scan_b")(cqs, kb, cv, gb)
        lay_out = lay_lat if last else lay_all
        yc = ew_op(_f_readout, lay_out, ("tok", "tok", "tok", "par", "par"), (True, True, True, True, False),
                   ((C_W, BF16),), tag + "readout")(of, ob, cgate, _tile_gain(p["c_out_norm"][l], C_HEADS), seg256)[0]
        ycat = jnp.concatenate([ya, yb, yc], axis=1)
        mixo = matmul_op(tag + "w_out")(ycat, p["w_out"][l])
        tok = ew_op(_f_post, lay_out, ("tok", "tok", "seg", "par"), (True,) * 4, ((D, F32),), tag + "postmix")(
            tok, mixo, gt_m, p["g_post_mix"][l][None, :])[0]

        hf = ew_op(_f_premod, lay_out, ("tok", "seg", "seg", "par"), (True,) * 4, ((D, BF16),), tag + "preffn")(
            tok, sh_f, sc_f, p["g_pre_ffn"][l][None, :])[0]
        z = matmul_op(tag + "w_ff1", BF16, relu2=True)(hf, p["w_ff1"][l])
        yf = matmul_op(tag + "w_ff2")(z, p["w_ff2"][l])
        tok = ew_op(_f_post, lay_out, ("tok", "tok", "seg", "par"), (True,) * 4, ((D, F32),), tag + "postffn")(
            tok, yf, gt_f, p["g_post_ffn"][l][None, :])[0]
    return tok[:NL]


BIG = ("w_in", "w_q_up", "w_kv_up", "w_out", "w_ff1", "w_ff2")
COL_SHARDED = ("w_in", "w_q_up", "w_kv_up", "w_ff1")
SMALL = ("c_ctx", "b_ada", "g_pre_mix", "g_post_mix", "g_pre_ffn", "g_post_ffn", "a_q_norm", "a_k_norm",
         "b_q_norm", "b_kv_norm", "c_out_norm")
WEIGHTS = ("c_ctx", "w_ada", "b_ada", "g_pre_mix", "g_post_mix", "g_pre_ffn", "g_post_ffn", "w_in", "a_q_norm",
           "a_k_norm", "b_q_norm", "w_q_up", "b_kv_norm", "w_kv_up", "c_lower_bounds", "c_out_norm", "w_out",
           "w_ff1", "w_ff2")
SMALL_ROWS = 64


def _assemble(name, a):
    if name in COL_SHARDED:
        return a.transpose(1, 2, 0, 3).reshape(a.shape[1], a.shape[2], N_DEV * a.shape[3])
    return a.transpose(1, 0, 2, 3).reshape(a.shape[1], N_DEV * a.shape[2], a.shape[3])


def kernel(x, c, ctx, c_ctx, w_ada, b_ada, g_pre_mix, g_post_mix, g_pre_ffn, g_post_ffn, w_in, a_q_norm, a_k_norm, b_q_norm, w_q_up, b_kv_norm, w_kv_up, c_lower_bounds, c_out_norm, w_out, w_ff1, w_ff2, loss_target, m_c_ctx, m_w_ada, m_b_ada, m_g_pre_mix, m_g_post_mix, m_g_pre_ffn, m_g_post_ffn, m_w_in, m_a_q_norm, m_a_k_norm, m_b_q_norm, m_w_q_up, m_b_kv_norm, m_w_kv_up, m_c_lower_bounds, m_c_out_norm, m_w_out, m_w_ff1, m_w_ff2, v_c_ctx, v_w_ada, v_b_ada, v_g_pre_mix, v_g_post_mix, v_g_pre_ffn, v_g_post_ffn, v_w_in, v_a_q_norm, v_a_k_norm, v_b_q_norm, v_w_q_up, v_b_kv_norm, v_w_kv_up, v_c_lower_bounds, v_c_out_norm, v_w_out, v_w_ff1, v_w_ff2):
    W = dict(c_ctx=c_ctx, w_ada=w_ada, b_ada=b_ada, g_pre_mix=g_pre_mix, g_post_mix=g_post_mix, g_pre_ffn=g_pre_ffn,
             g_post_ffn=g_post_ffn, w_in=w_in, a_q_norm=a_q_norm, a_k_norm=a_k_norm, b_q_norm=b_q_norm,
             w_q_up=w_q_up, b_kv_norm=b_kv_norm, w_kv_up=w_kv_up, c_lower_bounds=c_lower_bounds,
             c_out_norm=c_out_norm, w_out=w_out, w_ff1=w_ff1, w_ff2=w_ff2)
    M = dict(c_ctx=m_c_ctx, w_ada=m_w_ada, b_ada=m_b_ada, g_pre_mix=m_g_pre_mix, g_post_mix=m_g_post_mix,
             g_pre_ffn=m_g_pre_ffn, g_post_ffn=m_g_post_ffn, w_in=m_w_in, a_q_norm=m_a_q_norm, a_k_norm=m_a_k_norm,
             b_q_norm=m_b_q_norm, w_q_up=m_w_q_up, b_kv_norm=m_b_kv_norm, w_kv_up=m_w_kv_up,
             c_lower_bounds=m_c_lower_bounds, c_out_norm=m_c_out_norm, w_out=m_w_out, w_ff1=m_w_ff1, w_ff2=m_w_ff2)
    V = dict(c_ctx=v_c_ctx, w_ada=v_w_ada, b_ada=v_b_ada, g_pre_mix=v_g_pre_mix, g_post_mix=v_g_post_mix,
             g_pre_ffn=v_g_pre_ffn, g_post_ffn=v_g_post_ffn, w_in=v_w_in, a_q_norm=v_a_q_norm, a_k_norm=v_a_k_norm,
             b_q_norm=v_b_q_norm, w_q_up=v_w_q_up, b_kv_norm=v_b_kv_norm, w_kv_up=v_w_kv_up,
             c_lower_bounds=v_c_lower_bounds, c_out_norm=v_c_out_norm, w_out=v_w_out, w_ff1=v_w_ff1, w_ff2=v_w_ff2)

    B, T, D = x.shape
    Tc = ctx.shape[1]
    depth = w_ada.shape[0]
    ada_cols = w_ada.shape[2]
    idx = 4 * lax.axis_index("x") + 2 * lax.axis_index("y") + lax.axis_index("c")
    n_cond = N_DEV * B
    cond_rows = -(-(n_cond + 1) // 8) * 8

    clb_cols = c_lower_bounds.shape[2]
    g1, _ = small_gather(_pack([c, c_lower_bounds], F32, 8), "gather_cond")
    c_parts, clb_parts = _unpack(g1, [c.shape, c_lower_bounds.shape], lead=(N_DEV,))
    c_all = c_parts.reshape(n_cond, D)
    clb_full = clb_parts.transpose(1, 2, 0, 3).reshape(depth, 2, N_DEV * clb_cols)

    cond_lay = Lay(1, cond_rows, 0, cond_rows, False)

    def ada_shard(c_ctx_, w_ada_):
        cond = jnp.concatenate([c_all, c_ctx_[None, :], jnp.zeros((cond_rows - n_cond - 1, D), F32)], axis=0)
        sc = ew_op(_f_silu, cond_lay, ("tok",), (True,), ((D, F32),), "cond_silu")(cond)[0]
        return jnp.stack([matmul_op(f"l{l}_w_ada")(sc, w_ada_[l]) for l in range(depth)])

    mod_shard, vjp_ada = jax.vjp(ada_shard, c_ctx, w_ada)

    g2, _ = small_gather(_pack([mod_shard], F32, 8), "gather_mod")
    mod_all = _unpack(g2, [mod_shard.shape], lead=(N_DEV,))[0]
    mod_all = mod_all.transpose(1, 2, 0, 3).reshape(depth, cond_rows, N_DEV * ada_cols)
    mine = lax.dynamic_slice_in_dim(mod_all, idx * B, B, axis=1)
    modraw = jnp.concatenate([mine, mod_all[:, n_cond:n_cond + 1]], axis=1)

    gathered = dict(zip(BIG, big_gather([W[n].astype(BF16) for n in BIG], "gather_weights")))

    dims = (B, T, Tc, D)
    small_names = [n for n in SMALL if n != "c_ctx"]
    small_in = {n: W[n] for n in small_names}

    def fwd(x_, modraw_, small_, clb_, gathered_):
        p = dict(small_)
        p.update({n: _assemble(n, a) for n, a in gathered_.items()})
        p.update(x=x_.reshape(B * T, D), ctx=ctx.reshape(B * Tc, D), modraw=modraw_, clb=clb_)
        return local_forward(dims, p)

    y, vjp_main = jax.vjp(fwd, x, modraw, small_in, clb_full, gathered)
    loss_part, dy = loss_and_grad(y, loss_target.reshape(B * T, D), min(256, Tc))
    dx, dmodraw, dsmall, dclb, dgathered = vjp_main(dy)

    pay3 = _pack([dmodraw] + [dsmall[n] for n in small_names] + [dclb, loss_part[0, 0:1]], F32, 8)
    g3, s3 = small_gather(pay3, "gather_small_grads")
    dmod_parts = _unpack(g3, [dmodraw.shape], lead=(N_DEV,))[0]
    tot = _unpack(s3, [dmodraw.shape] + [W[n].shape for n in small_names] + [clb_full.shape, (1,)])
    dmod_tot, small_tot, dclb_tot, loss = tot[0], dict(zip(small_names, tot[1:-2])), tot[-2], tot[-1]
    drows = dmod_parts[:, :, 0:B].transpose(1, 0, 2, 3).reshape(depth, n_cond, N_DEV * ada_cols)
    dcond = jnp.concatenate(
        [drows, dmod_tot[:, B:B + 1], jnp.zeros((depth, cond_rows - n_cond - 1, N_DEV * ada_cols), F32)], axis=1)
    dmod_shard = lax.dynamic_slice_in_dim(dcond, idx * ada_cols, ada_cols, axis=2)
    dc_ctx_part, dw_ada = vjp_ada(dmod_shard)

    _, s4 = small_gather(_pack([dc_ctx_part], F32, 8), "gather_c_ctx_grad")
    small_tot["c_ctx"] = _unpack(s4, [c_ctx.shape])[0]

    recv = dict(zip(BIG, scatter_exchange([dgathered[n] for n in BIG], "scatter_grads")))
    grads, delta, new_m, new_v = dict(small_tot), {}, {}, {}
    for n in BIG:
        shape, cols = W[n].shape, W[n].shape[-1]
        flat = lambda a: a.reshape(-1, cols)
        res = adamw_slots(flat(W[n]), recv[n].reshape(N_DEV, -1, cols), flat(M[n]), flat(V[n]), "adamw_" + n)
        grads[n], delta[n], new_m[n], new_v[n] = (a.reshape(shape) for a in res)
    grads["w_ada"] = dw_ada
    grads["c_lower_bounds"] = lax.dynamic_slice_in_dim(dclb_tot, idx * clb_cols, clb_cols, axis=2)

    flat_a = lambda a: a.reshape(-1, ada_cols)
    res = adamw(flat_a(w_ada), flat_a(dw_ada), flat_a(m_w_ada), flat_a(v_w_ada), "adamw_w_ada")
    delta["w_ada"], new_m["w_ada"], new_v["w_ada"] = (a.reshape(w_ada.shape) for a in res)
    names = list(SMALL) + ["c_lower_bounds"]
    shapes = [W[n].shape for n in names]
    res = adamw(*[_pack([src[n] for n in names], F32, SMALL_ROWS) for src in (W, grads, M, V)], "adamw_small")
    for dst, buf in zip((delta, new_m, new_v), res):
        dst.update(zip(names, _unpack(buf, shapes)))

    return (loss.reshape(()), dx, *[grads[n] for n in WEIGHTS], *[delta[n] for n in WEIGHTS],
            *[new_m[n] for n in WEIGHTS], *[new_v[n] for n in WEIGHTS])
```

```python
import math

import numpy as np

import jax
import jax.numpy as jnp
from jax import lax
from jax.experimental import pallas as pl
from jax.experimental.pallas import tpu as pltpu

F32 = jnp.float32
BF16 = jnp.bfloat16

A_HEADS, A_KV, HD = 8, 2, 64
A_GROUP = A_HEADS // A_KV
B_HEADS, B_QR, B_KVR, B_NOPE, B_ROPE, B_V = 4, 192, 128, 64, 32, 64
C_HEADS, C_DK = 4, 64
C_W = C_HEADS * C_DK
GRID_W = 64
CHUNK = 64
ROPE_THETA = 10000.0
EPS = 1e-6
F_TINY = 1e-30
D_IN = 2400
D_IN_PAD = 2560
N_DEV = 8
LANES = 128
NEG = -1e30

ADAM_LR, ADAM_B1, ADAM_B2, ADAM_EPS, ADAM_WD, ADAM_STEP = 0.001, 0.9, 0.999, 1e-08, 0.01, 10

VMEM_LIMIT = 56 * 1024 * 1024
MESH = pl.DeviceIdType.MESH


def _pallas(body, **kw):
    return pl.pallas_call(body, **kw)


def _cparams(sem):
    return pltpu.CompilerParams(dimension_semantics=sem, vmem_limit_bytes=VMEM_LIMIT)


def _split3(x):
    hi = x.astype(BF16)
    r = x - hi.astype(F32)
    mid = r.astype(BF16)
    lo = (r - mid.astype(F32)).astype(BF16)
    return hi, mid, lo


def _nn(a, b):
    return jnp.dot(a, b, preferred_element_type=F32)


def _nt(a, b):
    return lax.dot_general(a, b, (((1,), (1,)), ((), ())), preferred_element_type=F32)


def _tn(a, b):
    return lax.dot_general(a, b, (((0,), (0,)), ((), ())), preferred_element_type=F32)


@jax.custom_vjp
def xdotr(x, m):
    return sum(_nn(p, m) for p in _split3(x))


def _xdotr_fwd(x, m):
    return xdotr(x, m), m


def _xdotr_bwd(m, ct):
    return sum(_nt(p, m) for p in _split3(ct)), None


xdotr.defvjp(_xdotr_fwd, _xdotr_bwd)


@jax.custom_vjp
def xdotl(m, mt, x):
    return sum(_nn(m, p) for p in _split3(x))


def _xdotl_fwd(m, mt, x):
    return xdotl(m, mt, x), (m, mt)


def _xdotl_bwd(res, ct):
    m, mt = res
    return None, None, sum(_nn(mt, p) for p in _split3(ct))


xdotl.defvjp(_xdotl_fwd, _xdotl_bwd)


def _sigmoid(x):
    return 1.0 / (1.0 + jnp.exp(-x))


def _silu(x):
    return x * _sigmoid(x)


def _rms(x, gain, n=None):
    n = x.shape[-1] if n is None else n
    ms = jnp.sum(x * x, axis=-1, keepdims=True) * (1.0 / n)
    return x * lax.rsqrt(ms + EPS) * gain


def _head_rms(x, seg, gain):
    ms = xdotr(x * x, seg) * (1.0 / HD)
    return x * lax.rsqrt(ms + EPS) * gain


class Lay:
    def __init__(self, B, T, Tc, tm):
        self.B, self.T, self.Tc, self.tm = B, T, Tc, tm
        self.nl, self.nc = T // tm, Tc // tm
        self.per = self.nl + self.nc
        self.n_tiles = B * self.per
        self.n_seg = 2 * B
        self.rows = self.n_tiles * tm

    def seg(self, i):
        b, w = i // self.per, i % self.per
        return jnp.where(w < self.nc, self.B + b, b)

    def pos(self, i):
        w = i % self.per
        return jnp.where(w < self.nc, self.nl, w - self.nc)

    def first(self, i):
        w = i % self.per
        return jnp.logical_or(w == 0, w == self.nc)


def _ew_spec(kind, a, lay):
    if kind == "tok":
        return pl.BlockSpec((lay.tm, a.shape[1]), lambda i: (i, 0))
    if kind == "seg":
        return pl.BlockSpec((1, 1, a.shape[2]), lambda i: (lay.seg(i), 0, 0))
    if kind == "pos":
        return pl.BlockSpec((lay.tm, a.shape[1]), lambda i: (lay.pos(i), 0))
    return pl.BlockSpec(a.shape, lambda i: (0,) * a.ndim)


def _ew_load(ref, kind):
    if kind == "seg":
        return ref[0]
    if kind == "tok":
        return ref[...].astype(F32)
    return ref[...]


def _ew_fwd(f, lay, kinds, arrays, outs, name):
    n_in = len(arrays)

    def body(*refs):
        vals = [_ew_load(r, k) for r, k in zip(refs[:n_in], kinds)]
        res = f(*vals)
        for r, o in zip(res, refs[n_in:]):
            o[...] = r.astype(o.dtype)

    return _pallas(
        body, name=name, grid=(lay.n_tiles,),
        in_specs=[_ew_spec(k, a, lay) for k, a in zip(kinds, arrays)],
        out_specs=[pl.BlockSpec((lay.tm, c), lambda i: (i, 0)) for c, _ in outs],
        out_shape=[jax.ShapeDtypeStruct((lay.rows, c), dt) for c, dt in outs],
        compiler_params=_cparams(("parallel",)),
    )(*arrays)


def _ew_bwd(f, lay, kinds, diffs, arrays, cts, name):
    n_in, n_ct = len(arrays), len(cts)
    d_idx = [i for i, d in enumerate(diffs) if d]

    g_shapes, g_specs = [], []
    for i in d_idx:
        a, k = arrays[i], kinds[i]
        if k == "tok":
            g_shapes.append(jax.ShapeDtypeStruct((lay.rows, a.shape[1]), a.dtype))
            g_specs.append(pl.BlockSpec((lay.tm, a.shape[1]), lambda t: (t, 0)))
        elif k == "seg":
            g_shapes.append(jax.ShapeDtypeStruct((lay.n_seg, 1, a.shape[2]), F32))
            g_specs.append(pl.BlockSpec((1, 1, a.shape[2]), lambda t: (lay.seg(t), 0, 0)))
        else:
            g_shapes.append(jax.ShapeDtypeStruct(a.shape, F32))
            g_specs.append(pl.BlockSpec(a.shape, lambda t, nd=a.ndim: (0,) * nd))

    def body(*refs):
        vals = [_ew_load(r, k) for r, k in zip(refs[:n_in], kinds)]
        cvals = tuple(r[...].astype(F32) for r in refs[n_in:n_in + n_ct])
        g_refs = refs[n_in + n_ct:]

        def g(*dv):
            full = list(vals)
            for j, i in enumerate(d_idx):
                full[i] = dv[j]
            return tuple(o.astype(F32) for o in f(*full))

        _, vjp = jax.vjp(g, *[vals[i] for i in d_idx])
        grads = vjp(cvals)
        t = pl.program_id(0)
        for gref, grad, i in zip(g_refs, grads, d_idx):
            k = kinds[i]
            if k == "tok":
                gref[...] = grad.astype(gref.dtype)
            elif k == "seg":
                @pl.when(lay.first(t))
                def _():
                    gref[...] = jnp.zeros_like(gref)

                gref[0] += grad
            else:
                @pl.when(t == 0)
                def _():
                    gref[...] = jnp.zeros_like(gref)

                gref[...] += grad

    res = _pallas(
        body, name=name + "_bwd", grid=(lay.n_tiles,),
        in_specs=[_ew_spec(k, a, lay) for k, a in zip(kinds, arrays)]
        + [pl.BlockSpec((lay.tm, c.shape[1]), lambda i: (i, 0)) for c in cts],
        out_specs=g_specs, out_shape=g_shapes,
        compiler_params=_cparams(("arbitrary",)),
    )(*arrays, *cts)
    out = [None] * n_in
    for gr, i in zip(res, d_idx):
        a = arrays[i]
        if gr.shape != a.shape:
            pad = [(0, a.shape[0] - gr.shape[0])] + [(0, 0)] * (a.ndim - 1)
            gr = jnp.pad(gr, pad)
        out[i] = gr
    return tuple(out)


def ew_op(f, lay, kinds, diffs, outs, name):
    kinds, diffs, outs = tuple(kinds), tuple(diffs), tuple(outs)

    @jax.custom_vjp
    def op(*arrays):
        return tuple(_ew_fwd(f, lay, kinds, arrays, outs, name))

    def fwd(*arrays):
        return op(*arrays), arrays

    def bwd(arrays, cts):
        return _ew_bwd(f, lay, kinds, diffs, arrays, tuple(cts), name)

    op.defvjp(fwd, bwd)
    return op


def _tile(n, cands):
    for c in cands:
        if n % c == 0:
            return c
    return n


def _mm_nn(x, w, name, out_dtype, relu2):
    M, K = x.shape
    N = w.shape[1]
    tm, tn = _tile(M, (512, 256)), _tile(N, (1024, 512, 256, 128))

    def body(x_ref, w_ref, o_ref):
        acc = _nn(x_ref[...].astype(BF16), w_ref[...].astype(BF16))
        if relu2:
            acc = jnp.square(jnp.maximum(acc, 0.0))
        o_ref[...] = acc.astype(o_ref.dtype)

    return _pallas(
        body, name=name, grid=(N // tn, M // tm),
        in_specs=[pl.BlockSpec((tm, K), lambda j, i: (i, 0)), pl.BlockSpec((K, tn), lambda j, i: (0, j))],
        out_specs=pl.BlockSpec((tm, tn), lambda j, i: (i, j)),
        out_shape=jax.ShapeDtypeStruct((M, N), out_dtype),
        compiler_params=_cparams(("parallel", "parallel")),
    )(x, w)


def _through_relu2(dz_ref, z_ref):
    if z_ref is None:
        return dz_ref[...].astype(BF16)
    return (dz_ref[...].astype(F32) * (2.0 * jnp.sqrt(z_ref[...].astype(F32)))).astype(BF16)


def _mm_nt(dy, w, name, out_dtype, z=None):
    M, N = dy.shape
    K = w.shape[0]
    tm = _tile(M, (512, 256))
    tk = _tile(K, (512, 256, 128)) if z is not None else _tile(K, (1024, 512, 256, 128))
    row = pl.BlockSpec((tm, N), lambda j, i: (i, 0))

    def body(*refs):
        dy_ref, z_ref = (refs[0], refs[1]) if z is not None else (refs[0], None)
        w_ref, o_ref = refs[-2], refs[-1]
        o_ref[...] = _nt(_through_relu2(dy_ref, z_ref), w_ref[...].astype(BF16)).astype(o_ref.dtype)

    return _pallas(
        body, name=name, grid=(K // tk, M // tm),
        in_specs=[row] * (2 if z is not None else 1) + [pl.BlockSpec((tk, N), lambda j, i: (j, 0))],
        out_specs=pl.BlockSpec((tm, tk), lambda j, i: (i, j)),
        out_shape=jax.ShapeDtypeStruct((M, K), out_dtype),
        compiler_params=_cparams(("parallel", "parallel")),
    )(*((dy, z, w) if z is not None else (dy, w)))


def _mm_tn(x, dy, name, out_dtype, z=None):
    M, K = x.shape
    N = dy.shape[1]
    tm = _tile(M, (512, 256))
    tk, tn = _tile(K, (1024, 512, 256, 128)), _tile(N, (1024, 512, 256, 128))
    n_m = M // tm
    col = pl.BlockSpec((tm, tn), lambda a, b, m: (m, b))

    def body(*refs):
        x_ref = refs[0]
        dy_ref, z_ref = (refs[1], refs[2]) if z is not None else (refs[1], None)
        o_ref, acc_ref = refs[-2], refs[-1]
        m = pl.program_id(2)

        @pl.when(m == 0)
        def _():
            acc_ref[...] = jnp.zeros_like(acc_ref)

        acc_ref[...] += _tn(x_ref[...].astype(BF16), _through_relu2(dy_ref, z_ref))

        @pl.when(m == n_m - 1)
        def _():
            o_ref[...] = acc_ref[...].astype(o_ref.dtype)

    return _pallas(
        body, name=name, grid=(K // tk, N // tn, n_m),
        in_specs=[pl.BlockSpec((tm, tk), lambda a, b, m: (m, a))] + [col] * (2 if z is not None else 1),
        out_specs=pl.BlockSpec((tk, tn), lambda a, b, m: (a, b)),
        out_shape=jax.ShapeDtypeStruct((K, N), out_dtype),
        scratch_shapes=[pltpu.VMEM((tk, tn), F32)],
        compiler_params=_cparams(("parallel", "parallel", "arbitrary")),
    )(*((x, dy, z) if z is not None else (x, dy)))


def matmul_op(name, out_dtype=F32, relu2=False):
    @jax.custom_vjp
    def op(x, w):
        return _mm_nn(x, w, name, out_dtype, relu2)

    def fwd(x, w):
        y = op(x, w)
        return y, (x, w, y if relu2 else None)

    def bwd(res, dy):
        x, w, z = res
        return _mm_nt(dy, w, name + "_dx", x.dtype, z), _mm_tn(x, dy, name + "_dw", w.dtype, z)

    op.defvjp(fwd, bwd)
    return op


LOG2E = math.log2(math.e)


class AttnLay:
    def __init__(self, B, T, Tc, tq):
        self.B, self.T, self.Tc, self.tq = B, T, Tc, tq
        self.S = T + Tc
        self.nq, self.nqc = self.S // tq, Tc // tq


def _attn_specs(al):
    qs = lambda w: pl.BlockSpec((al.tq, w), lambda b, i: (b * al.nq + i, 0))
    ks = lambda w: pl.BlockSpec((al.S, w), lambda b, i: (b, 0))
    return qs, ks


def _lane_fold(acc, x, op):
    for j in range(x.shape[1] // LANES):
        acc = op(acc, x[:, j * LANES:(j + 1) * LANES])
    return acc


def _key_chunks(n, kc):
    return [(c0, min(kc, n - c0)) for c0 in range(0, n, kc)]


def _stack(ref, g, group, width, tq):
    parts = [ref[:, (g * group + j) * width:(g * group + j + 1) * width].astype(F32) for j in range(group)]
    return parts[0] if group == 1 else jnp.concatenate(parts, axis=0)


def _attn_fwd(q, k, v, cfg, al, name):
    n_kv, group, dq, dv, scale, kc, _ = cfg
    tq = al.tq
    rows = group * tq
    wq, wk, wv, wo = q.shape[1], k.shape[1], v.shape[1], n_kv * group * dv
    qs, ks = _attn_specs(al)

    def body(q_ref, k_ref, v_ref, o_ref, lse_ref, s_scr):
        lane = lax.broadcasted_iota(jnp.int32, (tq, LANES), 1)

        def run(n_keys):
            chunks = _key_chunks(n_keys, kc)
            lse_all = jnp.zeros((tq, LANES), F32)
            for g in range(n_kv):
                q4 = (_stack(q_ref, g, group, dq, tq) * (scale * LOG2E)).astype(BF16)
                ksl, vsl = slice(g * dq, (g + 1) * dq), slice(g * dv, (g + 1) * dv)
                m_part = jnp.full((rows, LANES), -jnp.inf, F32)
                for c0, w in chunks:
                    s = _nt(q4, k_ref[c0:c0 + w, ksl])
                    s_scr[:, c0:c0 + w] = s
                    m_part = _lane_fold(m_part, s, jnp.maximum)
                m = jnp.max(m_part, axis=1, keepdims=True)
                l_part = jnp.zeros((rows, LANES), F32)
                acc = jnp.zeros((rows, dv), F32)
                for c0, w in chunks:
                    p = jnp.exp2(s_scr[:, c0:c0 + w] - m)
                    l_part = _lane_fold(l_part, p, jnp.add)
                    acc = acc + _nn(p.astype(BF16), v_ref[c0:c0 + w, vsl])
                l = jnp.sum(l_part, axis=1, keepdims=True)
                o = acc / l
                lse = m + jnp.log2(l)
                for j in range(group):
                    h = g * group + j
                    o_ref[:, h * dv:(h + 1) * dv] = o[j * tq:(j + 1) * tq].astype(o_ref.dtype)
                    lse_all = jnp.where(lane == h, lse[j * tq:(j + 1) * tq], lse_all)
            lse_ref[...] = lse_all

        is_ctx = pl.program_id(1) < al.nqc

        @pl.when(is_ctx)
        def _():
            run(al.Tc)

        @pl.when(jnp.logical_not(is_ctx))
        def _():
            run(al.S)

    return _pallas(
        body, name=name, grid=(al.B, al.nq),
        in_specs=[qs(wq), ks(wk), ks(wv)],
        out_specs=[qs(wo), qs(LANES)],
        out_shape=[jax.ShapeDtypeStruct((q.shape[0], wo), BF16), jax.ShapeDtypeStruct((q.shape[0], LANES), F32)],
        scratch_shapes=[pltpu.VMEM((rows, al.S), F32)],
        compiler_params=_cparams(("parallel", "parallel")),
    )(q, k, v)


def _attn_bwd(q, k, v, o, lse, do, cfg, al, name):
    n_kv, group, dq, dv, scale, _, kc = cfg
    tq = al.tq
    rows = group * tq
    wq, wk, wv, wo = q.shape[1], k.shape[1], v.shape[1], n_kv * group * dv
    qs, ks = _attn_specs(al)

    def body(q_ref, k_ref, v_ref, o_ref, lse_ref, do_ref, dq_ref, dk_ref, dv_ref, ak, av):
        i = pl.program_id(1)

        @pl.when(i == 0)
        def _():
            ak[...] = jnp.zeros_like(ak)
            av[...] = jnp.zeros_like(av)

        lane = lax.broadcasted_iota(jnp.int32, (tq, LANES), 1)

        def run(n_keys):
            lse_tile = lse_ref[...]
            for g in range(n_kv):
                qf = _stack(q_ref, g, group, dq, tq)
                q4l = (qf * (scale * LOG2E)).astype(BF16)
                q4s = (qf * scale).astype(BF16)
                do4 = _stack(do_ref, g, group, dv, tq)
                o4 = _stack(o_ref, g, group, dv, tq)
                cols = [jnp.sum(jnp.where(lane == g * group + j, lse_tile, 0.0), axis=1, keepdims=True)
                        for j in range(group)]
                lse4 = cols[0] if group == 1 else jnp.concatenate(cols, axis=0)
                dl = jnp.sum(do4 * o4, axis=1, keepdims=True)
                dob = do4.astype(BF16)
                ksl, vsl = slice(g * dq, (g + 1) * dq), slice(g * dv, (g + 1) * dv)
                dq4 = jnp.zeros((rows, dq), F32)
                for c0, w in _key_chunks(n_keys, kc):
                    kk = k_ref[c0:c0 + w, ksl]
                    p = jnp.exp2(_nt(q4l, kk) - lse4)
                    dp = _nt(dob, v_ref[c0:c0 + w, vsl])
                    ds = (p * (dp - dl)).astype(BF16)
                    dq4 = dq4 + _nn(ds, kk)
                    ak[c0:c0 + w, ksl] += _tn(ds, q4s)
                    av[c0:c0 + w, vsl] += _tn(p.astype(BF16), dob)
                dq4 = dq4 * scale
                for j in range(group):
                    h = g * group + j
                    dq_ref[:, h * dq:(h + 1) * dq] = dq4[j * tq:(j + 1) * tq]

        is_ctx = i < al.nqc

        @pl.when(is_ctx)
        def _():
            run(al.Tc)

        @pl.when(jnp.logical_not(is_ctx))
        def _():
            run(al.S)

        @pl.when(i == al.nq - 1)
        def _():
            dk_ref[...] = ak[...].astype(dk_ref.dtype)
            dv_ref[...] = av[...].astype(dv_ref.dtype)

    return _pallas(
        body, name=name + "_bwd", grid=(al.B, al.nq),
        in_specs=[qs(wq), ks(wk), ks(wv), qs(wo), qs(LANES), qs(wo)],
        out_specs=[qs(wq), ks(wk), ks(wv)],
        out_shape=[jax.ShapeDtypeStruct(q.shape, F32), jax.ShapeDtypeStruct(k.shape, k.dtype),
                   jax.ShapeDtypeStruct(v.shape, v.dtype)],
        scratch_shapes=[pltpu.VMEM((al.S, wk), F32), pltpu.VMEM((al.S, wv), F32)],
        compiler_params=_cparams(("parallel", "arbitrary")),
    )(q, k, v, o, lse, do)


def attn_op(cfg, al, name):
    @jax.custom_vjp
    def op(q, k, v):
        return _attn_fwd(q, k, v, cfg, al, name)[0]

    def fwd(q, k, v):
        o, lse = _attn_fwd(q, k, v, cfg, al, name)
        return o, (q, k, v, o, lse)

    def bwd(res, do):
        return tuple(_attn_bwd(*res, do, cfg, al, name))

    op.defvjp(fwd, bwd)
    return op


SCAN_WIDTHS = (32, 16, 8, 4, 2, 1)
N_CM = 2 + 2 * len(SCAN_WIDTHS)


def _scan_consts(reverse):
    C = CHUNK
    t = np.arange(C)[:, None]
    s = np.arange(C)[None, :]
    blocks = [(s <= t), (s > t)]
    for w in SCAN_WIDTHS:
        blocks.append((s <= t) & (s // w == t // w))
    for w in SCAN_WIDTHS:
        blocks.append((s > t) & (s // w == t // w))
    masks = [np.eye(C, dtype=bool)]
    for w in SCAN_WIDTHS:
        masks.append((t // (2 * w) == s // (2 * w)) & ((t // w) % 2 == 1) & ((s // w) % 2 == 0))
    if reverse:
        blocks = [b[::-1, ::-1] for b in blocks]
        masks = [m[::-1, ::-1] for m in masks]
    cm = np.concatenate([b.astype(np.float32) for b in blocks] + [np.ones((8, C), np.float32)], axis=0)
    mw = np.stack([np.tile(m.astype(np.float32), (C_HEADS, 1)) for m in masks])
    rows = np.arange(C_HEADS * C)[:, None] // C
    lane = np.arange(C_W)[None, :] // C_DK
    hm = (rows == lane).astype(np.float32)
    bd = (np.arange(C_W)[:, None] // C_DK == lane).astype(np.float32)
    return (jnp.asarray(cm, BF16), jnp.asarray(cm.T.copy(), BF16), jnp.asarray(mw, F32),
            jnp.asarray(hm, F32), jnp.asarray(bd, F32))


def _scan_chunk(st, q, k, v, g, cm, cmt, mw, hm, bd):
    C = CHUNK
    cs = xdotl(cm, cmt, g)
    b = cs[0:C]
    rest = cs[C:2 * C]
    tot = cs[N_CM * C:N_CM * C + 1]
    kb = k.astype(BF16)

    def stack(a):
        return (jnp.concatenate([a] * C_HEADS, axis=0) * hm).astype(BF16)

    a = _nt(stack(q), kb) * mw[0]
    for i in range(len(SCAN_WIDTHS)):
        eq = jnp.exp(jnp.minimum(cs[(2 + i) * C:(3 + i) * C], 0.0))
        ek = jnp.exp(jnp.minimum(cs[(2 + len(SCAN_WIDTHS) + i) * C:(3 + len(SCAN_WIDTHS) + i) * C], 0.0))
        a = a + _nt(stack(q * eq), (k * ek).astype(BF16)) * mw[i + 1]
    oh = _nn(a.astype(BF16), v.astype(BF16)) * hm
    o = oh[0:C]
    for h in range(1, C_HEADS):
        o = o + oh[h * C:(h + 1) * C]
    o = o + _nt((q * jnp.exp(b)).astype(BF16), st.astype(BF16))
    st_new = st * jnp.exp(tot) + _tn(v.astype(BF16), (k * jnp.exp(rest)).astype(BF16)) * bd
    return o, st_new


class ScanLay:
    def __init__(self, B, T, Tc):
        self.B, self.S = B, T + Tc
        self.ncc, self.ntot = Tc // CHUNK, (T + Tc) // CHUNK

    def chunk(self, j, reverse):
        if not reverse:
            return j
        return jnp.where(j < self.ncc, self.ncc - 1 - j, self.ntot - 1 - (j - self.ncc))


def _scan_specs(sl, step):
    f = pl.BlockSpec((sl.B, CHUNK, C_W), lambda j: (0, sl.chunk(step(j), False), 0))
    r = pl.BlockSpec((sl.B, CHUNK, C_W), lambda j: (0, sl.chunk(step(j), True), 0))
    return f, r


def _scan_fwd(q, kf, gf, kb, gb, v, sl, name):
    B, S = sl.B, sl.S
    view = lambda a: a.reshape(B, S, C_W)
    cf, cr = _scan_consts(False), _scan_consts(True)
    nc = len(cf)
    f, r = _scan_specs(sl, lambda j: j)
    cspecs = [pl.BlockSpec(c.shape, lambda j, nd=c.ndim: (0,) * nd) for c in cf + cr]

    def body(*refs):
        (qf_ref, kf_ref, gf_ref, vf_ref, qr_ref, kr_ref, gr_ref, vr_ref), refs = refs[:8], refs[8:]
        cfv, crv = [c[...] for c in refs[:nc]], [c[...] for c in refs[nc:2 * nc]]
        of_ref, or_ref, st_ref, st = refs[2 * nc:]

        @pl.when(pl.program_id(0) == 0)
        def _():
            st[...] = jnp.zeros_like(st)

        st_ref[0] = st[...]
        for d, (q_, k_, g_, v_, o_, cv) in enumerate(((qf_ref, kf_ref, gf_ref, vf_ref, of_ref, cfv),
                                                     (qr_ref, kr_ref, gr_ref, vr_ref, or_ref, crv))):
            for b in range(B):
                o, st_new = _scan_chunk(st[d * B + b], q_[b], k_[b], v_[b], g_[b], *cv)
                o_[b] = o
                st[d * B + b] = st_new

    of, ob, states = _pallas(
        body, name=name, grid=(sl.ntot,),
        in_specs=[f] * 4 + [r] * 4 + cspecs,
        out_specs=[f, r, pl.BlockSpec((1, 2 * B, C_W, C_W), lambda j: (j, 0, 0, 0))],
        out_shape=[jax.ShapeDtypeStruct((B, S, C_W), F32)] * 2
        + [jax.ShapeDtypeStruct((sl.ntot, 2 * B, C_W, C_W), F32)],
        scratch_shapes=[pltpu.VMEM((2 * B, C_W, C_W), F32)],
        compiler_params=_cparams(("arbitrary",)),
    )(view(q), view(kf), view(gf), view(v), view(q), view(kb), view(gb), view(v), *cf, *cr)
    return of.reshape(B * S, C_W), ob.reshape(B * S, C_W), states


def _scan_bwd(q, kf, gf, kb, gb, v, states, dof, dob, sl, name):
    B, S = sl.B, sl.S
    view = lambda a: a.reshape(B, S, C_W)
    cf, cr = _scan_consts(False), _scan_consts(True)
    nc = len(cf)
    last = sl.ntot - 1
    f, r = _scan_specs(sl, lambda j: last - j)
    cspecs = [pl.BlockSpec(c.shape, lambda j, nd=c.ndim: (0,) * nd) for c in cf + cr]

    def body(*refs):
        ins, refs = refs[:11], refs[11:]
        qf_ref, kf_ref, gf_ref, vf_ref, dof_ref, qr_ref, kr_ref, gr_ref, vr_ref, dor_ref, st_ref = ins
        cfv, crv = [c[...] for c in refs[:nc]], [c[...] for c in refs[nc:2 * nc]]
        outs, dst = refs[2 * nc:-1], refs[-1]

        @pl.when(pl.program_id(0) == 0)
        def _():
            dst[...] = jnp.zeros_like(dst)

        for d, (q_, k_, g_, v_, do_, cv) in enumerate(((qf_ref, kf_ref, gf_ref, vf_ref, dof_ref, cfv),
                                                      (qr_ref, kr_ref, gr_ref, vr_ref, dor_ref, crv))):
            dq_, dk_, dg_, dv_ = outs[4 * d:4 * d + 4]
            for b in range(B):
                _, vjp = jax.vjp(lambda s_, a_, b_, c_, e_, cv=cv: _scan_chunk(s_, a_, b_, c_, e_, *cv),
                                 st_ref[0, d * B + b], q_[b], k_[b], v_[b], g_[b])
                ds, dq, dk, dv, dg = vjp((do_[b], dst[d * B + b]))
                dq_[b], dk_[b], dg_[b], dv_[b] = dq, dk, dg, dv
                dst[d * B + b] = ds

    res = _pallas(
        body, name=name + "_bwd", grid=(sl.ntot,),
        in_specs=[f] * 5 + [r] * 5 + [pl.BlockSpec((1, 2 * B, C_W, C_W), lambda j: (last - j, 0, 0, 0))] + cspecs,
        out_specs=[f] * 4 + [r] * 4,
        out_shape=[jax.ShapeDtypeStruct((B, S, C_W), F32)] * 8,
        scratch_shapes=[pltpu.VMEM((2 * B, C_W, C_W), F32)],
        compiler_params=_cparams(("arbitrary",)),
    )(view(q), view(kf), view(gf), view(v), view(dof), view(q), view(kb), view(gb), view(v), view(dob),
      states, *cf, *cr)
    dq_f, dk_f, dg_f, dv_f, dq_r, dk_r, dg_r, dv_r = [a.reshape(B * S, C_W) for a in res]
    return dq_f + dq_r, dk_f, dg_f, dk_r, dg_r, dv_f + dv_r


def scan_op(sl, name):
    @jax.custom_vjp
    def op(q, kf, gf, kb, gb, v):
        return _scan_fwd(q, kf, gf, kb, gb, v, sl, name)[:2]

    def fwd(q, kf, gf, kb, gb, v):
        of, ob, states = _scan_fwd(q, kf, gf, kb, gb, v, sl, name)
        return (of, ob), (q, kf, gf, kb, gb, v, states)

    def bwd(res, cts):
        return _scan_bwd(*res, cts[0], cts[1], sl, name)

    op.defvjp(fwd, bwd)
    return op


def loss_and_grad(y, target, lay):
    N, D = y.shape

    def body(y_ref, t_ref, dy_ref, l_ref):
        i = pl.program_id(0)

        @pl.when(i == 0)
        def _():
            l_ref[...] = jnp.zeros_like(l_ref)

        is_ctx = i % lay.per < lay.nc

        @pl.when(is_ctx)
        def _():
            dy_ref[...] = jnp.zeros_like(dy_ref)

        @pl.when(jnp.logical_not(is_ctx))
        def _():
            e = y_ref[...] - t_ref[...]
            dy_ref[...] = e * (1.0 / D)
            l_ref[...] += 0.5 * jnp.sum(jnp.sum(e * e, axis=1, keepdims=True) * (1.0 / D), axis=0, keepdims=True)

    def t_index(i):
        return ((i // lay.per) * lay.nl + jnp.maximum(i % lay.per - lay.nc, 0), 0)

    dy, lp = _pallas(
        body, name="loss_head", grid=(lay.n_tiles,),
        in_specs=[pl.BlockSpec((lay.tm, D), lambda i: (i, 0)), pl.BlockSpec((lay.tm, D), t_index)],
        out_specs=[pl.BlockSpec((lay.tm, D), lambda i: (i, 0)), pl.BlockSpec((8, LANES), lambda i: (0, 0))],
        out_shape=[jax.ShapeDtypeStruct((N, D), F32), jax.ShapeDtypeStruct((8, LANES), F32)],
        compiler_params=_cparams(("arbitrary",)),
    )(y, target)
    return lp, dy


def _adam_math(w, g, m, v):
    mn = ADAM_B1 * m + (1.0 - ADAM_B1) * g
    vn = ADAM_B2 * v + (1.0 - ADAM_B2) * jnp.square(g)
    m_hat = mn / (1.0 - ADAM_B1 ** ADAM_STEP)
    v_hat = vn / (1.0 - ADAM_B2 ** ADAM_STEP)
    return -ADAM_LR * (m_hat / (jnp.sqrt(v_hat) + ADAM_EPS) + ADAM_WD * w), mn, vn


ROW_TILES = (512, 256, 128, 64, 32, 16, 8)


def adamw(w, g, m, v, name):
    R, C = w.shape
    tr = _tile(R, ROW_TILES)

    def body(w_ref, g_ref, m_ref, v_ref, d_ref, mo_ref, vo_ref):
        d_ref[...], mo_ref[...], vo_ref[...] = _adam_math(w_ref[...], g_ref[...], m_ref[...], v_ref[...])

    spec = pl.BlockSpec((tr, C), lambda i: (i, 0))
    return _pallas(
        body, name=name, grid=(R // tr,), in_specs=[spec] * 4, out_specs=[spec] * 3,
        out_shape=[jax.ShapeDtypeStruct((R, C), F32)] * 3,
        compiler_params=_cparams(("parallel",)),
    )(w, g, m, v)


def adamw_slots(w, recv, m, v, name):
    R, C = w.shape
    tr = _tile(R, ROW_TILES[1:])

    def body(w_ref, r_ref, m_ref, v_ref, g_ref, d_ref, mo_ref, vo_ref):
        g = r_ref[0].astype(F32)
        for k in range(1, N_DEV):
            g = g + r_ref[k].astype(F32)
        g_ref[...] = g
        d_ref[...], mo_ref[...], vo_ref[...] = _adam_math(w_ref[...], g, m_ref[...], v_ref[...])

    spec = pl.BlockSpec((tr, C), lambda i: (i, 0))
    return _pallas(
        body, name=name, grid=(R // tr,),
        in_specs=[spec, pl.BlockSpec((N_DEV, tr, C), lambda i: (0, i, 0)), spec, spec], out_specs=[spec] * 4,
        out_shape=[jax.ShapeDtypeStruct((R, C), F32)] * 4,
        compiler_params=_cparams(("parallel",)),
    )(w, recv, m, v)


def _me():
    return lax.axis_index("x"), lax.axis_index("y"), lax.axis_index("c")


def _gather_many(x_refs, out_refs, send_sems, recv_sems, local_sems):
    x, y, c = _me()
    me, sibling = (x, y, c), (x, y, 1 - c)
    chips = [(1 - x, y), (x, 1 - y), (1 - x, 1 - y)]
    arrs = range(len(x_refs))

    def slot(a, px, py, pc):
        return out_refs[a].at[4 * px + 2 * py + pc]

    def copy(a, k, block, to, src=None):
        return pltpu.make_async_remote_copy(
            src_ref=slot(a, *block) if src is None else src, dst_ref=slot(a, *block),
            send_sem=send_sems.at[7 * a + k], recv_sem=recv_sems.at[7 * a + k], device_id=to, device_id_type=MESH)

    mine = [pltpu.make_async_copy(x_refs[a], slot(a, *me), local_sems.at[a]) for a in arrs]
    for cp in mine:
        cp.start()
    first = []
    for a in arrs:
        first.append(copy(a, 0, me, sibling, src=x_refs[a]))
        first += [copy(a, 1 + j, me, (*chip, c), src=x_refs[a]) for j, chip in enumerate(chips)]
    for cp in first:
        cp.start()
    passed = []
    for j, chip in enumerate(chips):
        for a in arrs:
            copy(a, 1 + j, (*chip, c), me).wait_recv()
            fwd = copy(a, 4 + j, (*chip, c), sibling)
            fwd.start()
            passed.append(fwd)
    for a in arrs:
        copy(a, 0, sibling, me).wait_recv()
    for j, chip in enumerate(chips):
        for a in arrs:
            copy(a, 4 + j, (*chip, 1 - c), me).wait_recv()
    for cp in first + passed:
        cp.wait_send()
    for cp in mine:
        cp.wait()


def _comm_scratch(n):
    return [pltpu.SemaphoreType.DMA((7 * n,)), pltpu.SemaphoreType.DMA((7 * n,)), pltpu.SemaphoreType.DMA((n,))]


def small_gather(xb, name):
    R = xb.shape[0]

    def body(x_ref, out_ref, sum_ref, send_sems, recv_sems, local_sems):
        _gather_many([x_ref], [out_ref], send_sems, recv_sems, local_sems)
        acc = out_ref[0]
        for k in range(1, N_DEV):
            acc = acc + out_ref[k]
        sum_ref[...] = acc

    vm = pl.BlockSpec(memory_space=pltpu.VMEM)
    return _pallas(
        body, name=name, in_specs=[vm], out_specs=[vm, vm],
        out_shape=[jax.ShapeDtypeStruct((N_DEV, R, LANES), xb.dtype), jax.ShapeDtypeStruct((R, LANES), xb.dtype)],
        scratch_shapes=_comm_scratch(1),
        compiler_params=pltpu.CompilerParams(vmem_limit_bytes=VMEM_LIMIT),
    )(xb)


def big_gather(xs, name):
    n = len(xs)

    def body(*refs):
        _gather_many(refs[:n], refs[n:2 * n], *refs[2 * n:])

    hbm = pl.BlockSpec(memory_space=pl.ANY)
    return _pallas(
        body, name=name, in_specs=[hbm] * n, out_specs=[hbm] * n,
        out_shape=[jax.ShapeDtypeStruct((N_DEV,) + a.shape, a.dtype) for a in xs],
        scratch_shapes=_comm_scratch(n),
    )(*xs)


def scatter_exchange(gs, name):
    n = len(gs)
    rels = [(dx, dy, dc) for dx in (0, 1) for dy in (0, 1) for dc in (0, 1) if (dx, dy, dc) != (0, 0, 0)]

    def body(*refs):
        g_refs, r_refs = refs[:n], refs[n:2 * n]
        send_sems, recv_sems, local_sems = refs[2 * n:]
        x, y, c = _me()
        me = 4 * x + 2 * y + c
        mine = [pltpu.make_async_copy(g_refs[a].at[me], r_refs[a].at[me], local_sems.at[a]) for a in range(n)]
        for cp in mine:
            cp.start()
        copies = []
        for r, (dx, dy, dc) in enumerate(rels):
            px, py, pc = (x + dx) % 2, (y + dy) % 2, (c + dc) % 2
            for a in range(n):
                copies.append(pltpu.make_async_remote_copy(
                    src_ref=g_refs[a].at[4 * px + 2 * py + pc], dst_ref=r_refs[a].at[me],
                    send_sem=send_sems.at[7 * a + r], recv_sem=recv_sems.at[7 * a + r],
                    device_id=(px, py, pc), device_id_type=MESH))
        for cp in copies:
            cp.start()
        for cp in copies:
            cp.wait()
        for cp in mine:
            cp.wait()

    hbm = pl.BlockSpec(memory_space=pl.ANY)
    return _pallas(
        body, name=name, in_specs=[hbm] * n, out_specs=[hbm] * n,
        out_shape=[jax.ShapeDtypeStruct(a.shape, a.dtype) for a in gs],
        scratch_shapes=_comm_scratch(n),
    )(*gs)


def _pack(arrs, dtype, row_mult):
    flat = jnp.concatenate([a.astype(dtype).reshape(-1) for a in arrs])
    pad = (-flat.shape[0]) % (LANES * row_mult)
    if pad:
        flat = jnp.concatenate([flat, jnp.zeros((pad,), dtype)])
    return flat.reshape(-1, LANES)


def _unpack(buf, shapes, lead=()):
    flat = buf.reshape(*lead, -1)
    out, off = [], 0
    for s in shapes:
        n = int(np.prod(s))
        out.append(flat[..., off:off + n].reshape(*lead, *s))
        off += n
    return out


def _rope_tables(T, tm):
    pos = np.arange(T)
    row, col = pos // GRID_W, pos % GRID_W

    def tab(rot_dim):
        nf = rot_dim // 4
        inv = ROPE_THETA ** (-np.arange(nf, dtype=np.float32) / nf)
        ang = np.concatenate([row[:, None].astype(np.float32) * inv, col[:, None].astype(np.float32) * inv], axis=-1)
        ang = ang.astype(np.float32)
        cos, sin = np.cos(ang), np.sin(ang)
        return np.concatenate([cos, cos], -1), np.concatenate([-sin, sin], -1)

    c64, s64 = tab(HD)
    c32, s32 = tab(B_ROPE)
    ca, sa = np.tile(c64, (1, A_HEADS)), np.tile(s64, (1, A_HEADS))
    cb = np.concatenate([c32, np.ones((T, LANES - B_ROPE), np.float32)], -1)
    sb = np.concatenate([s32, np.zeros((T, LANES - B_ROPE), np.float32)], -1)
    one, zero = np.ones((T, B_NOPE), np.float32), np.zeros((T, B_NOPE), np.float32)
    tail1, tail0 = np.ones((T, LANES - B_NOPE - B_ROPE), np.float32), np.zeros((T, LANES - B_NOPE - B_ROPE), np.float32)
    cq = np.tile(np.concatenate([one, c32, tail1], -1), (1, B_HEADS))
    sq = np.tile(np.concatenate([zero, s32, tail0], -1), (1, B_HEADS))

    def fin(a, ident):
        return jnp.asarray(np.concatenate([a, np.full((tm, a.shape[1]), ident, np.float32)], 0), F32)

    return fin(ca, 1.0), fin(sa, 0.0), fin(cb, 1.0), fin(sb, 0.0), fin(cq, 1.0), fin(sq, 0.0)


def _swap_matrix(width, starts, half):
    p = np.zeros((width, width), np.float32)
    for s in starts:
        for i in range(half):
            p[s + i, s + half + i] = 1.0
            p[s + half + i, s + i] = 1.0
    return jnp.asarray(p, BF16)


def _seg_matrix(width):
    h = np.arange(width) // HD
    return jnp.asarray((h[:, None] == h[None, :]).astype(np.float32), BF16)


def _key_slot_matrices():
    e1 = np.zeros((B_HEADS * B_NOPE, B_HEADS * LANES), np.float32)
    e2 = np.zeros((LANES, B_HEADS * LANES), np.float32)
    for h in range(B_HEADS):
        for i in range(B_NOPE):
            e1[h * B_NOPE + i, h * LANES + i] = 1.0
        for i in range(B_ROPE):
            e2[i, h * LANES + B_NOPE + i] = 1.0
    return jnp.asarray(e1, BF16), jnp.asarray(e2, BF16)


def _f_premod(x, sh, sc, g):
    return (_rms(x, g) * (1.0 + sc) + sh,)


def _f_post(x, y, gt, g):
    return (x + gt * _rms(y, g),)


def _f_post_pre(x, y, gt, g_post, sh, sc, g_pre):
    x1 = x + gt * _rms(y, g_post)
    return x1, _rms(x1, g_pre) * (1.0 + sc) + sh


def _f_bias(raw, b):
    return (raw + b,)


def _f_silu(x):
    return (_silu(x),)


def _f_readout(of, ob, gate, gain, seg):
    return (_head_rms(of + ob, seg, gain) * _silu(gate),)


def _f_bq(bq, cq, sq, pq):
    return (bq * cq + xdotr(bq, pq) * sq,)


def _f_bk(bkn, bkr, e1, e2):
    return (xdotr(bkn, e1) + xdotr(bkr, e2),)


def _make_f_feat(layer):
    def f(feat, ca, sa, cb, sb, gaq, gak, gbq, gbkv, c00, c01, c10, c11, seg, pa, pb):
        aq = _head_rms(feat[:, 0:512], seg, gaq)
        ak = _head_rms(feat[:, 512:640], seg[0:128, 0:128], gak)
        av = feat[:, 640:768]
        aq = aq * ca + xdotr(aq, pa) * sa
        ak = ak * ca[:, 0:128] + xdotr(ak, pa[0:128, 0:128]) * sa[:, 0:128]
        bqn = _rms(feat[:, 768:1024], gbq, B_QR)
        bkvn = _rms(feat[:, 1024:1152], gbkv)
        bkr = feat[:, 1152:1280]
        bkr = bkr * cb + xdotr(bkr, pb) * sb
        cq = _silu(feat[:, 1280:1536])
        zf, zb = feat[:, 1536:1792], feat[:, 1792:2048]
        if layer == 0:
            lbf = lbb = 0.0
        else:
            def share(c0, c1):
                m = jnp.maximum(c0, c1)
                e0, e1 = jnp.exp(c0 - m), jnp.exp(c1 - m)
                return e1 / (e0 + e1)
            lbf, lbb = share(c00, c10), share(c01, c11)

        def gate(z, lb):
            f_ = lb + (1.0 - lb) * _sigmoid(z)
            return (1.0 - lb) * _sigmoid(-z), jnp.log(jnp.maximum(f_, F_TINY))

        kf, gf = gate(zf, lbf)
        kb, gb = gate(zb, lbb)
        return aq, ak, av, bqn, bkvn, bkr, cq, kf, gf, kb, gb, feat[:, 2048:2304], feat[:, 2304:2560]

    return f


def _pad_w_in(w):
    z = lambda n: jnp.zeros((w.shape[0], n), w.dtype)
    return jnp.concatenate([w[:, 0:960], z(64), w[:, 960:1120], z(96), w[:, 1120:2400]], axis=1)


def _pad_w_q_up(w):
    w4 = w.reshape(B_QR, B_HEADS, B_NOPE + B_ROPE)
    w4 = jnp.pad(w4, ((0, 256 - B_QR), (0, 0), (0, LANES - B_NOPE - B_ROPE)))
    return w4.reshape(256, B_HEADS * LANES)


def _split_w_kv_up(w):
    w4 = w.reshape(B_KVR, B_HEADS, B_NOPE + B_V)
    return w4[:, :, :B_NOPE].reshape(B_KVR, -1), w4[:, :, B_NOPE:].reshape(B_KVR, -1)


def _tile_gain(g, reps, width=None):
    t = jnp.tile(g, reps)
    if width is not None and width > t.shape[0]:
        t = jnp.pad(t, (0, width - t.shape[0]))
    return t[None, :]


def local_forward(dims, p):
    B, T, Tc, D = dims
    tm = min(256, Tc)
    lay_all = Lay(B, T, Tc, tm)
    ca, sa, cb, sb, cq, sq = _rope_tables(T, tm)
    seg512, seg256 = _seg_matrix(512), _seg_matrix(C_W)
    pa = _swap_matrix(512, range(0, 512, HD), HD // 2)
    pb = _swap_matrix(LANES, [0], B_ROPE // 2)
    pq = _swap_matrix(512, [h * LANES + B_NOPE for h in range(B_HEADS)], B_ROPE // 2)
    e1, e2 = _key_slot_matrices()
    sl = ScanLay(B, T, Tc)
    al_a, al_b = AttnLay(B, T, Tc, min(128, Tc)), AttnLay(B, T, Tc, min(256, Tc))
    cfg_a = (A_KV, A_GROUP, HD, HD, HD ** -0.5, 512, 1024)
    cfg_b = (B_HEADS, 1, LANES, B_V, (B_NOPE + B_ROPE) ** -0.5, 512, T + Tc)

    tok = p["tok"]
    depth = p["modraw"].shape[0]
    mods = []
    for l in range(depth):
        bias_lay = Lay(1, 8, 0, 8)
        raw8 = jnp.pad(p["modraw"][l], ((0, 8 - B - 1), (0, 0)))
        mod = ew_op(_f_bias, bias_lay, ("tok", "par"), (True, True), ((6 * D, F32),), f"l{l}_ada_bias")(
            raw8, p["b_ada"][l][None, :])[0]
        seg_rows = jnp.concatenate([mod[0:B], jnp.broadcast_to(mod[B:B + 1], (B, 6 * D))], axis=0)[:, None, :]
        mods.append([seg_rows[:, :, i * D:(i + 1) * D] for i in range(6)])

    post_pre_kinds = ("tok", "tok", "seg", "par", "seg", "seg", "par")
    h = ew_op(_f_premod, lay_all, ("tok", "seg", "seg", "par"), (True,) * 4, ((D, BF16),), "l0_premix")(
        tok, mods[0][0], mods[0][1], p["g_pre_mix"][0][None, :])[0]
    for l in range(depth):
        tag = f"l{l}_"
        sh_m, sc_m, gt_m, sh_f, sc_f, gt_f = mods[l]

        feat = matmul_op(tag + "w_in")(h, _pad_w_in(p["w_in"][l]))
        clb = p["clb"]
        feats = ew_op(
            _make_f_feat(l), lay_all,
            ("tok", "pos", "pos", "pos", "pos") + ("par",) * 11,
            (True,) + (False,) * 4 + (True,) * 8 + (False,) * 3,
            ((512, F32), (128, BF16), (128, BF16), (256, BF16), (128, BF16), (128, F32)) + ((C_W, F32),) * 7,
            tag + "feat")(
            feat, ca, sa, cb, sb,
            _tile_gain(p["a_q_norm"][l], A_HEADS), _tile_gain(p["a_k_norm"][l], A_KV),
            _tile_gain(p["b_q_norm"][l], 1, 256), _tile_gain(p["b_kv_norm"][l], 1),
            clb[0, 0][None, :], clb[0, 1][None, :], clb[1, 0][None, :], clb[1, 1][None, :],
            seg512, pa, pb)
        aq, ak, av, bqn, bkvn, bkr, cqs, kf, gf, kb, gb, cv, cgate = feats
        bq = matmul_op(tag + "w_q_up")(bqn, _pad_w_q_up(p["w_q_up"][l]))
        bq = ew_op(_f_bq, lay_all, ("tok", "pos", "pos", "par"), (True, False, False, False), ((512, F32),),
                   tag + "bq_rope")(bq, cq, sq, pq)[0]
        w_kn, w_v = _split_w_kv_up(p["w_kv_up"][l])
        bkn = matmul_op(tag + "w_k_up")(bkvn, w_kn)
        bv = matmul_op(tag + "w_v_up", BF16)(bkvn, w_v)
        bk = ew_op(_f_bk, lay_all, ("tok", "tok", "par", "par"), (True, True, False, False),
                   ((B_HEADS * LANES, BF16),), tag + "bk_slots")(bkn, bkr, e1, e2)[0]

        ya = attn_op(cfg_a, al_a, tag + "attn_a")(aq, ak, av)
        yb = attn_op(cfg_b, al_b, tag + "attn_b")(bq, bk, bv)
        of, ob = scan_op(sl, tag + "scan")(cqs, kf, gf, kb, gb, cv)
        lay_out = lay_all
        yc = ew_op(_f_readout, lay_out, ("tok", "tok", "tok", "par", "par"), (True, True, True, True, False),
                   ((C_W, BF16),), tag + "readout")(of, ob, cgate, _tile_gain(p["c_out_norm"][l], C_HEADS), seg256)[0]
        ycat = jnp.concatenate([ya, yb, yc], axis=1)
        mixo = matmul_op(tag + "w_out")(ycat, p["w_out"][l])
        tok, hf = ew_op(_f_post_pre, lay_out, post_pre_kinds, (True,) * 7, ((D, F32), (D, BF16)),
                        tag + "postmix_preffn")(
            tok, mixo, gt_m, p["g_post_mix"][l][None, :], sh_f, sc_f, p["g_pre_ffn"][l][None, :])

        z = matmul_op(tag + "w_ff1", BF16, relu2=True)(hf, p["w_ff1"][l])
        yf = matmul_op(tag + "w_ff2")(z, p["w_ff2"][l])
        if l + 1 < depth:
            tok, h = ew_op(_f_post_pre, lay_out, post_pre_kinds, (True,) * 7, ((D, F32), (D, BF16)),
                           tag + "postffn_premix")(
                tok, yf, gt_f, p["g_post_ffn"][l][None, :], mods[l + 1][0], mods[l + 1][1],
                p["g_pre_mix"][l + 1][None, :])
        else:
            tok = ew_op(_f_post, lay_out, ("tok", "tok", "seg", "par"), (True,) * 4, ((D, F32),), tag + "postffn")(
                tok, yf, gt_f, p["g_post_ffn"][l][None, :])[0]
    return tok


BIG = ("w_in", "w_q_up", "w_kv_up", "w_out", "w_ff1", "w_ff2")
COL_SHARDED = ("w_in", "w_q_up", "w_kv_up", "w_ff1")
SMALL = ("c_ctx", "b_ada", "g_pre_mix", "g_post_mix", "g_pre_ffn", "g_post_ffn", "a_q_norm", "a_k_norm",
         "b_q_norm", "b_kv_norm", "c_out_norm")
WEIGHTS = ("c_ctx", "w_ada", "b_ada", "g_pre_mix", "g_post_mix", "g_pre_ffn", "g_post_ffn", "w_in", "a_q_norm",
           "a_k_norm", "b_q_norm", "w_q_up", "b_kv_norm", "w_kv_up", "c_lower_bounds", "c_out_norm", "w_out",
           "w_ff1", "w_ff2")
SMALL_ROWS = 64


def _assemble(name, a):
    if name in COL_SHARDED:
        return a.transpose(1, 2, 0, 3).reshape(a.shape[1], a.shape[2], N_DEV * a.shape[3])
    return a.transpose(1, 0, 2, 3).reshape(a.shape[1], N_DEV * a.shape[2], a.shape[3])


def kernel(x, c, ctx, c_ctx, w_ada, b_ada, g_pre_mix, g_post_mix, g_pre_ffn, g_post_ffn, w_in, a_q_norm, a_k_norm, b_q_norm, w_q_up, b_kv_norm, w_kv_up, c_lower_bounds, c_out_norm, w_out, w_ff1, w_ff2, loss_target, m_c_ctx, m_w_ada, m_b_ada, m_g_pre_mix, m_g_post_mix, m_g_pre_ffn, m_g_post_ffn, m_w_in, m_a_q_norm, m_a_k_norm, m_b_q_norm, m_w_q_up, m_b_kv_norm, m_w_kv_up, m_c_lower_bounds, m_c_out_norm, m_w_out, m_w_ff1, m_w_ff2, v_c_ctx, v_w_ada, v_b_ada, v_g_pre_mix, v_g_post_mix, v_g_pre_ffn, v_g_post_ffn, v_w_in, v_a_q_norm, v_a_k_norm, v_b_q_norm, v_w_q_up, v_b_kv_norm, v_w_kv_up, v_c_lower_bounds, v_c_out_norm, v_w_out, v_w_ff1, v_w_ff2):
    W = dict(c_ctx=c_ctx, w_ada=w_ada, b_ada=b_ada, g_pre_mix=g_pre_mix, g_post_mix=g_post_mix, g_pre_ffn=g_pre_ffn,
             g_post_ffn=g_post_ffn, w_in=w_in, a_q_norm=a_q_norm, a_k_norm=a_k_norm, b_q_norm=b_q_norm,
             w_q_up=w_q_up, b_kv_norm=b_kv_norm, w_kv_up=w_kv_up, c_lower_bounds=c_lower_bounds,
             c_out_norm=c_out_norm, w_out=w_out, w_ff1=w_ff1, w_ff2=w_ff2)
    M = dict(c_ctx=m_c_ctx, w_ada=m_w_ada, b_ada=m_b_ada, g_pre_mix=m_g_pre_mix, g_post_mix=m_g_post_mix,
             g_pre_ffn=m_g_pre_ffn, g_post_ffn=m_g_post_ffn, w_in=m_w_in, a_q_norm=m_a_q_norm, a_k_norm=m_a_k_norm,
             b_q_norm=m_b_q_norm, w_q_up=m_w_q_up, b_kv_norm=m_b_kv_norm, w_kv_up=m_w_kv_up,
             c_lower_bounds=m_c_lower_bounds, c_out_norm=m_c_out_norm, w_out=m_w_out, w_ff1=m_w_ff1, w_ff2=m_w_ff2)
    V = dict(c_ctx=v_c_ctx, w_ada=v_w_ada, b_ada=v_b_ada, g_pre_mix=v_g_pre_mix, g_post_mix=v_g_post_mix,
             g_pre_ffn=v_g_pre_ffn, g_post_ffn=v_g_post_ffn, w_in=v_w_in, a_q_norm=v_a_q_norm, a_k_norm=v_a_k_norm,
             b_q_norm=v_b_q_norm, w_q_up=v_w_q_up, b_kv_norm=v_b_kv_norm, w_kv_up=v_w_kv_up,
             c_lower_bounds=v_c_lower_bounds, c_out_norm=v_c_out_norm, w_out=v_w_out, w_ff1=v_w_ff1, w_ff2=v_w_ff2)

    B, T, D = x.shape
    Tc = ctx.shape[1]
    depth = w_ada.shape[0]
    ada_cols = w_ada.shape[2]
    idx = 4 * lax.axis_index("x") + 2 * lax.axis_index("y") + lax.axis_index("c")
    n_cond = N_DEV * B
    cond_rows = -(-(n_cond + 1) // 8) * 8

    clb_cols = c_lower_bounds.shape[2]
    g1, _ = small_gather(_pack([c, c_lower_bounds], F32, 8), "gather_cond")
    c_parts, clb_parts = _unpack(g1, [c.shape, c_lower_bounds.shape], lead=(N_DEV,))
    c_all = c_parts.reshape(n_cond, D)
    clb_full = clb_parts.transpose(1, 2, 0, 3).reshape(depth, 2, N_DEV * clb_cols)

    cond_lay = Lay(1, cond_rows, 0, cond_rows)

    def ada_shard(c_ctx_, w_ada_):
        cond = jnp.concatenate([c_all, c_ctx_[None, :], jnp.zeros((cond_rows - n_cond - 1, D), F32)], axis=0)
        sc = ew_op(_f_silu, cond_lay, ("tok",), (True,), ((D, F32),), "cond_silu")(cond)[0]
        return jnp.stack([matmul_op(f"l{l}_w_ada")(sc, w_ada_[l]) for l in range(depth)])

    mod_shard, vjp_ada = jax.vjp(ada_shard, c_ctx, w_ada)

    g2, _ = small_gather(_pack([mod_shard], F32, 8), "gather_mod")
    mod_all = _unpack(g2, [mod_shard.shape], lead=(N_DEV,))[0]
    mod_all = mod_all.transpose(1, 2, 0, 3).reshape(depth, cond_rows, N_DEV * ada_cols)
    mine = lax.dynamic_slice_in_dim(mod_all, idx * B, B, axis=1)
    modraw = jnp.concatenate([mine, mod_all[:, n_cond:n_cond + 1]], axis=1)

    gathered = dict(zip(BIG, big_gather([W[n].astype(BF16) for n in BIG], "gather_weights")))

    dims = (B, T, Tc, D)
    small_names = [n for n in SMALL if n != "c_ctx"]
    small_in = {n: W[n] for n in small_names}

    def fwd(x_, modraw_, small_, clb_, gathered_):
        p = dict(small_)
        p.update({n: _assemble(n, a) for n, a in gathered_.items()})
        p.update(tok=jnp.concatenate([ctx, x_], axis=1).reshape(B * (Tc + T), D), modraw=modraw_, clb=clb_)
        return local_forward(dims, p)

    y, vjp_main = jax.vjp(fwd, x, modraw, small_in, clb_full, gathered)
    loss_part, dy = loss_and_grad(y, loss_target.reshape(B * T, D), Lay(B, T, Tc, min(256, Tc)))
    dx, dmodraw, dsmall, dclb, dgathered = vjp_main(dy)

    pay3 = _pack([dmodraw] + [dsmall[n] for n in small_names] + [dclb, loss_part[0, 0:1]], F32, 8)
    g3, s3 = small_gather(pay3, "gather_small_grads")
    dmod_parts = _unpack(g3, [dmodraw.shape], lead=(N_DEV,))[0]
    tot = _unpack(s3, [dmodraw.shape] + [W[n].shape for n in small_names] + [clb_full.shape, (1,)])
    dmod_tot, small_tot, dclb_tot, loss = tot[0], dict(zip(small_names, tot[1:-2])), tot[-2], tot[-1]
    drows = dmod_parts[:, :, 0:B].transpose(1, 0, 2, 3).reshape(depth, n_cond, N_DEV * ada_cols)
    dcond = jnp.concatenate(
        [drows, dmod_tot[:, B:B + 1], jnp.zeros((depth, cond_rows - n_cond - 1, N_DEV * ada_cols), F32)], axis=1)
    dmod_shard = lax.dynamic_slice_in_dim(dcond, idx * ada_cols, ada_cols, axis=2)
    dc_ctx_part, dw_ada = vjp_ada(dmod_shard)

    _, s4 = small_gather(_pack([dc_ctx_part], F32, 8), "gather_c_ctx_grad")
    small_tot["c_ctx"] = _unpack(s4, [c_ctx.shape])[0]

    recv = dict(zip(BIG, scatter_exchange([dgathered[n] for n in BIG], "scatter_grads")))
    grads, delta, new_m, new_v = dict(small_tot), {}, {}, {}
    for n in BIG:
        shape, cols = W[n].shape, W[n].shape[-1]
        flat = lambda a: a.reshape(-1, cols)
        res = adamw_slots(flat(W[n]), recv[n].reshape(N_DEV, -1, cols), flat(M[n]), flat(V[n]), "adamw_" + n)
        grads[n], delta[n], new_m[n], new_v[n] = (a.reshape(shape) for a in res)
    grads["w_ada"] = dw_ada
    grads["c_lower_bounds"] = lax.dynamic_slice_in_dim(dclb_tot, idx * clb_cols, clb_cols, axis=2)

    flat_a = lambda a: a.reshape(-1, ada_cols)
    res = adamw(flat_a(w_ada), flat_a(dw_ada), flat_a(m_w_ada), flat_a(v_w_ada), "adamw_w_ada")
    delta["w_ada"], new_m["w_ada"], new_v["w_ada"] = (a.reshape(w_ada.shape) for a in res)
    names = list(SMALL) + ["c_lower_bounds"]
    shapes = [W[n].shape for n in names]
    res = adamw(*[_pack([src[n] for n in names], F32, SMALL_ROWS) for src in (W, grads, M, V)], "adamw_small")
    for dst, buf in zip((delta, new_m, new_v), res):
        dst.update(zip(names, _unpack(buf, shapes)))

    return (loss.reshape(()), dx, *[grads[n] for n in WEIGHTS], *[delta[n] for n in WEIGHTS],
            *[new_m[n] for n in WEIGHTS], *[new_v[n] for n in WEIGHTS])
```

```python
import math

import numpy as np

import jax
import jax.numpy as jnp
from jax import lax
from jax.experimental import pallas as pl
from jax.experimental.pallas import tpu as pltpu

F32 = jnp.float32
BF16 = jnp.bfloat16

A_HEADS, A_KV, HD = 8, 2, 64
A_GROUP = A_HEADS // A_KV
B_HEADS, B_QR, B_KVR, B_NOPE, B_ROPE, B_V = 4, 192, 128, 64, 32, 64
C_HEADS, C_DK = 4, 64
C_W = C_HEADS * C_DK
GRID_W = 64
CHUNK = 64
ROPE_THETA = 10000.0
EPS = 1e-6
F_TINY = 1e-30
D_IN = 2400
D_IN_PAD = 2560
N_DEV = 8
LANES = 128
NEG = -1e30

ADAM_LR, ADAM_B1, ADAM_B2, ADAM_EPS, ADAM_WD, ADAM_STEP = 0.001, 0.9, 0.999, 1e-08, 0.01, 10

VMEM_LIMIT = 56 * 1024 * 1024
MESH = pl.DeviceIdType.MESH


def _pallas(body, **kw):
    return pl.pallas_call(body, **kw)


def _cparams(sem):
    return pltpu.CompilerParams(dimension_semantics=sem, vmem_limit_bytes=VMEM_LIMIT)


def _split3(x):
    hi = x.astype(BF16)
    r = x - hi.astype(F32)
    mid = r.astype(BF16)
    lo = (r - mid.astype(F32)).astype(BF16)
    return hi, mid, lo


def _nn(a, b):
    return jnp.dot(a, b, preferred_element_type=F32)


def _nt(a, b):
    return lax.dot_general(a, b, (((1,), (1,)), ((), ())), preferred_element_type=F32)


def _tn(a, b):
    return lax.dot_general(a, b, (((0,), (0,)), ((), ())), preferred_element_type=F32)


def _make_xdotr(pieces):
    @jax.custom_vjp
    def op(x, m):
        return sum(_nn(p, m) for p in _split3(x)[:pieces])

    def fwd(x, m):
        return op(x, m), m

    def bwd(m, ct):
        return sum(_nt(p, m) for p in _split3(ct)[:pieces]), None

    op.defvjp(fwd, bwd)
    return op


xdotr, xdotr2, xdotr1 = _make_xdotr(3), _make_xdotr(2), _make_xdotr(1)


@jax.custom_vjp
def xdotl(m, mt, x):
    return sum(_nn(m, p) for p in _split3(x))


def _xdotl_fwd(m, mt, x):
    return xdotl(m, mt, x), (m, mt)


def _xdotl_bwd(res, ct):
    m, mt = res
    return None, None, sum(_nn(mt, p) for p in _split3(ct))


xdotl.defvjp(_xdotl_fwd, _xdotl_bwd)


def _sigmoid(x):
    return 1.0 / (1.0 + jnp.exp(-x))


def _silu(x):
    return x * _sigmoid(x)


def _rms(x, gain, n=None):
    n = x.shape[-1] if n is None else n
    ms = jnp.sum(x * x, axis=-1, keepdims=True) * (1.0 / n)
    return x * lax.rsqrt(ms + EPS) * gain


def _head_rms(x, seg, gain):
    ms = xdotr2(x * x, seg) * (1.0 / HD)
    return x * lax.rsqrt(ms + EPS) * gain


class Lay:
    def __init__(self, B, T, Tc, tm):
        self.B, self.T, self.Tc, self.tm = B, T, Tc, tm
        self.nl, self.nc = T // tm, Tc // tm
        self.per = self.nl + self.nc
        self.n_tiles = B * self.per
        self.n_seg = 2 * B
        self.rows = self.n_tiles * tm

    def seg(self, i):
        b, w = i // self.per, i % self.per
        return jnp.where(w < self.nc, self.B + b, b)

    def pos(self, i):
        w = i % self.per
        return jnp.where(w < self.nc, self.nl, w - self.nc)

    def first(self, i):
        w = i % self.per
        return jnp.logical_or(w == 0, w == self.nc)


def _ew_spec(kind, a, lay):
    if kind == "tok":
        return pl.BlockSpec((lay.tm, a.shape[1]), lambda i: (i, 0))
    if kind == "seg":
        return pl.BlockSpec((1, 1, a.shape[2]), lambda i: (lay.seg(i), 0, 0))
    if kind == "pos":
        return pl.BlockSpec((lay.tm, a.shape[1]), lambda i: (lay.pos(i), 0))
    return pl.BlockSpec(a.shape, lambda i: (0,) * a.ndim)


def _ew_load(ref, kind):
    if kind == "seg":
        return ref[0]
    if kind == "tok":
        return ref[...].astype(F32)
    return ref[...]


def _ew_fwd(f, lay, kinds, arrays, outs, name):
    n_in = len(arrays)

    def body(*refs):
        vals = [_ew_load(r, k) for r, k in zip(refs[:n_in], kinds)]
        res = f(*vals)
        for r, o in zip(res, refs[n_in:]):
            o[...] = r.astype(o.dtype)

    return _pallas(
        body, name=name, grid=(lay.n_tiles,),
        in_specs=[_ew_spec(k, a, lay) for k, a in zip(kinds, arrays)],
        out_specs=[pl.BlockSpec((lay.tm, c), lambda i: (i, 0)) for c, _ in outs],
        out_shape=[jax.ShapeDtypeStruct((lay.rows, c), dt) for c, dt in outs],
        compiler_params=_cparams(("parallel",)),
    )(*arrays)


def _ew_bwd(f, lay, kinds, diffs, arrays, cts, name):
    n_in, n_ct = len(arrays), len(cts)
    d_idx = [i for i, d in enumerate(diffs) if d]

    g_shapes, g_specs = [], []
    for i in d_idx:
        a, k = arrays[i], kinds[i]
        if k == "tok":
            g_shapes.append(jax.ShapeDtypeStruct((lay.rows, a.shape[1]), a.dtype))
            g_specs.append(pl.BlockSpec((lay.tm, a.shape[1]), lambda t: (t, 0)))
        elif k == "seg":
            g_shapes.append(jax.ShapeDtypeStruct((lay.n_seg, 1, a.shape[2]), F32))
            g_specs.append(pl.BlockSpec((1, 1, a.shape[2]), lambda t: (lay.seg(t), 0, 0)))
        else:
            g_shapes.append(jax.ShapeDtypeStruct(a.shape, F32))
            g_specs.append(pl.BlockSpec(a.shape, lambda t, nd=a.ndim: (0,) * nd))

    def body(*refs):
        vals = [_ew_load(r, k) for r, k in zip(refs[:n_in], kinds)]
        cvals = tuple(r[...].astype(F32) for r in refs[n_in:n_in + n_ct])
        g_refs = refs[n_in + n_ct:]

        def g(*dv):
            full = list(vals)
            for j, i in enumerate(d_idx):
                full[i] = dv[j]
            return tuple(o.astype(F32) for o in f(*full))

        _, vjp = jax.vjp(g, *[vals[i] for i in d_idx])
        grads = vjp(cvals)
        t = pl.program_id(0)
        for gref, grad, i in zip(g_refs, grads, d_idx):
            k = kinds[i]
            if k == "tok":
                gref[...] = grad.astype(gref.dtype)
            elif k == "seg":
                @pl.when(lay.first(t))
                def _():
                    gref[...] = jnp.zeros_like(gref)

                gref[0] += grad
            else:
                @pl.when(t == 0)
                def _():
                    gref[...] = jnp.zeros_like(gref)

                gref[...] += grad

    res = _pallas(
        body, name=name + "_bwd", grid=(lay.n_tiles,),
        in_specs=[_ew_spec(k, a, lay) for k, a in zip(kinds, arrays)]
        + [pl.BlockSpec((lay.tm, c.shape[1]), lambda i: (i, 0)) for c in cts],
        out_specs=g_specs, out_shape=g_shapes,
        compiler_params=_cparams(("arbitrary",)),
    )(*arrays, *cts)
    out = [None] * n_in
    for gr, i in zip(res, d_idx):
        a = arrays[i]
        if gr.shape != a.shape:
            pad = [(0, a.shape[0] - gr.shape[0])] + [(0, 0)] * (a.ndim - 1)
            gr = jnp.pad(gr, pad)
        out[i] = gr
    return tuple(out)


def ew_op(f, lay, kinds, diffs, outs, name):
    kinds, diffs, outs = tuple(kinds), tuple(diffs), tuple(outs)

    @jax.custom_vjp
    def op(*arrays):
        return tuple(_ew_fwd(f, lay, kinds, arrays, outs, name))

    def fwd(*arrays):
        return op(*arrays), arrays

    def bwd(arrays, cts):
        return _ew_bwd(f, lay, kinds, diffs, arrays, tuple(cts), name)

    op.defvjp(fwd, bwd)
    return op


def _tile(n, cands):
    for c in cands:
        if n % c == 0:
            return c
    return n


TN_ROW_TILES = (2176, 1024, 512, 256)


def _mm_nn(x, w, name, out_dtype, relu2):
    M, K = x.shape
    N = w.shape[1]
    tm, tn = _tile(M, (512, 256)), _tile(N, (1024, 512, 256, 128))

    def body(x_ref, w_ref, o_ref):
        acc = _nn(x_ref[...].astype(BF16), w_ref[...].astype(BF16))
        if relu2:
            acc = jnp.square(jnp.maximum(acc, 0.0))
        o_ref[...] = acc.astype(o_ref.dtype)

    return _pallas(
        body, name=name, grid=(N // tn, M // tm),
        in_specs=[pl.BlockSpec((tm, K), lambda j, i: (i, 0)), pl.BlockSpec((K, tn), lambda j, i: (0, j))],
        out_specs=pl.BlockSpec((tm, tn), lambda j, i: (i, j)),
        out_shape=jax.ShapeDtypeStruct((M, N), out_dtype),
        compiler_params=_cparams(("parallel", "parallel")),
    )(x, w)


def _through_relu2(dz_ref, z_ref):
    if z_ref is None:
        return dz_ref[...].astype(BF16)
    z = z_ref[...].astype(F32)
    root = z * lax.rsqrt(jnp.maximum(z, F_TINY))
    return (dz_ref[...].astype(F32) * (2.0 * root)).astype(BF16)


def _mm_nt(dy, w, name, out_dtype, z=None):
    M, N = dy.shape
    K = w.shape[0]
    tm = _tile(M, (256,)) if z is not None else _tile(M, (512, 256))
    tk = K if z is not None else _tile(K, (1024, 512, 256, 128))
    row = pl.BlockSpec((tm, N), lambda j, i: (i, 0))

    def body(*refs):
        dy_ref, z_ref = (refs[0], refs[1]) if z is not None else (refs[0], None)
        w_ref, o_ref = refs[-2], refs[-1]
        o_ref[...] = _nt(_through_relu2(dy_ref, z_ref), w_ref[...].astype(BF16)).astype(o_ref.dtype)

    return _pallas(
        body, name=name, grid=(K // tk, M // tm),
        in_specs=[row] * (2 if z is not None else 1) + [pl.BlockSpec((tk, N), lambda j, i: (j, 0))],
        out_specs=pl.BlockSpec((tm, tk), lambda j, i: (i, j)),
        out_shape=jax.ShapeDtypeStruct((M, K), out_dtype),
        compiler_params=_cparams(("parallel", "parallel")),
    )(*((dy, z, w) if z is not None else (dy, w)))


def _mm_tn(x, dy, name, out_dtype, z=None):
    M, K = x.shape
    N = dy.shape[1]
    tm = _tile(M, TN_ROW_TILES)
    tk, tn = _tile(K, (1024, 512, 256, 128)), _tile(N, (1024, 512, 256, 128))
    n_m = M // tm
    col = pl.BlockSpec((tm, tn), lambda a, b, m: (m, b))

    def body(*refs):
        x_ref = refs[0]
        dy_ref, z_ref = (refs[1], refs[2]) if z is not None else (refs[1], None)
        o_ref, acc_ref = refs[-2], refs[-1]
        m = pl.program_id(2)

        @pl.when(m == 0)
        def _():
            acc_ref[...] = jnp.zeros_like(acc_ref)

        acc_ref[...] += _tn(x_ref[...].astype(BF16), _through_relu2(dy_ref, z_ref))

        @pl.when(m == n_m - 1)
        def _():
            o_ref[...] = acc_ref[...].astype(o_ref.dtype)

    return _pallas(
        body, name=name, grid=(K // tk, N // tn, n_m),
        in_specs=[pl.BlockSpec((tm, tk), lambda a, b, m: (m, a))] + [col] * (2 if z is not None else 1),
        out_specs=pl.BlockSpec((tk, tn), lambda a, b, m: (a, b)),
        out_shape=jax.ShapeDtypeStruct((K, N), out_dtype),
        scratch_shapes=[pltpu.VMEM((tk, tn), F32)],
        compiler_params=_cparams(("parallel", "parallel", "arbitrary")),
    )(*((x, dy, z) if z is not None else (x, dy)))


def matmul_op(name, out_dtype=F32, relu2=False):
    @jax.custom_vjp
    def op(x, w):
        return _mm_nn(x, w, name, out_dtype, relu2)

    def fwd(x, w):
        y = op(x, w)
        return y, (x, w, y if relu2 else None)

    def bwd(res, dy):
        x, w, z = res
        return _mm_nt(dy, w, name + "_dx", x.dtype, z), _mm_tn(x, dy, name + "_dw", w.dtype, z)

    op.defvjp(fwd, bwd)
    return op


LOG2E = math.log2(math.e)


class AttnLay:
    def __init__(self, B, T, Tc, tq):
        self.B, self.T, self.Tc, self.tq = B, T, Tc, tq
        self.S = T + Tc
        self.nq, self.nqc = self.S // tq, Tc // tq


def _attn_specs(al):
    qs = lambda w: pl.BlockSpec((al.tq, w), lambda b, i: (b * al.nq + i, 0))
    ks = lambda w: pl.BlockSpec((al.S, w), lambda b, i: (b, 0))
    return qs, ks


def _lane_fold(acc, x, op):
    for j in range(x.shape[1] // LANES):
        acc = op(acc, x[:, j * LANES:(j + 1) * LANES])
    return acc


def _key_chunks(n, kc):
    return [(c0, min(kc, n - c0)) for c0 in range(0, n, kc)]


def _stack(ref, g, group, width, tq):
    parts = [ref[:, (g * group + j) * width:(g * group + j + 1) * width].astype(F32) for j in range(group)]
    return parts[0] if group == 1 else jnp.concatenate(parts, axis=0)


def _attn_fwd(q, k, v, cfg, al, name):
    n_kv, group, dq, dv, scale, kc, _ = cfg
    tq = al.tq
    rows = group * tq
    wq, wk, wv, wo = q.shape[1], k.shape[1], v.shape[1], n_kv * group * dv
    qs, ks = _attn_specs(al)

    def body(q_ref, k_ref, v_ref, o_ref, lse_ref, s_scr):
        lane = lax.broadcasted_iota(jnp.int32, (tq, LANES), 1)

        def run(n_keys):
            chunks = _key_chunks(n_keys, kc)
            lse_all = jnp.zeros((tq, LANES), F32)
            for g in range(n_kv):
                q4 = (_stack(q_ref, g, group, dq, tq) * (scale * LOG2E)).astype(BF16)
                ksl, vsl = slice(g * dq, (g + 1) * dq), slice(g * dv, (g + 1) * dv)
                m_part = jnp.full((rows, LANES), -jnp.inf, F32)
                for c0, w in chunks:
                    s = _nt(q4, k_ref[c0:c0 + w, ksl])
                    s_scr[:, c0:c0 + w] = s
                    m_part = _lane_fold(m_part, s, jnp.maximum)
                m = jnp.max(m_part, axis=1, keepdims=True)
                l_part = jnp.zeros((rows, LANES), F32)
                acc = jnp.zeros((rows, dv), F32)
                for c0, w in chunks:
                    p = jnp.exp2(s_scr[:, c0:c0 + w] - m)
                    l_part = _lane_fold(l_part, p, jnp.add)
                    acc = acc + _nn(p.astype(BF16), v_ref[c0:c0 + w, vsl])
                l = jnp.sum(l_part, axis=1, keepdims=True)
                o = acc / l
                lse = m + jnp.log2(l)
                for j in range(group):
                    h = g * group + j
                    o_ref[:, h * dv:(h + 1) * dv] = o[j * tq:(j + 1) * tq].astype(o_ref.dtype)
                    lse_all = jnp.where(lane == h, lse[j * tq:(j + 1) * tq], lse_all)
            lse_ref[...] = lse_all

        is_ctx = pl.program_id(1) < al.nqc

        @pl.when(is_ctx)
        def _():
            run(al.Tc)

        @pl.when(jnp.logical_not(is_ctx))
        def _():
            run(al.S)

    return _pallas(
        body, name=name, grid=(al.B, al.nq),
        in_specs=[qs(wq), ks(wk), ks(wv)],
        out_specs=[qs(wo), qs(LANES)],
        out_shape=[jax.ShapeDtypeStruct((q.shape[0], wo), BF16), jax.ShapeDtypeStruct((q.shape[0], LANES), F32)],
        scratch_shapes=[pltpu.VMEM((rows, al.S), F32)],
        compiler_params=_cparams(("parallel", "parallel")),
    )(q, k, v)


def _attn_bwd(q, k, v, o, lse, do, cfg, al, name):
    n_kv, group, dq, dv, scale, _, kc = cfg
    tq = al.tq
    rows = group * tq
    wq, wk, wv, wo = q.shape[1], k.shape[1], v.shape[1], n_kv * group * dv
    qs, ks = _attn_specs(al)

    def body(q_ref, k_ref, v_ref, o_ref, lse_ref, do_ref, dq_ref, dk_ref, dv_ref, ak, av):
        i = pl.program_id(1)

        @pl.when(i == 0)
        def _():
            ak[...] = jnp.zeros_like(ak)
            av[...] = jnp.zeros_like(av)

        lane = lax.broadcasted_iota(jnp.int32, (tq, LANES), 1)

        def run(n_keys):
            lse_tile = lse_ref[...]
            for g in range(n_kv):
                qf = _stack(q_ref, g, group, dq, tq)
                q4l = (qf * (scale * LOG2E)).astype(BF16)
                q4s = (qf * scale).astype(BF16)
                do4 = _stack(do_ref, g, group, dv, tq)
                o4 = _stack(o_ref, g, group, dv, tq)
                cols = [jnp.sum(jnp.where(lane == g * group + j, lse_tile, 0.0), axis=1, keepdims=True)
                        for j in range(group)]
                lse4 = cols[0] if group == 1 else jnp.concatenate(cols, axis=0)
                dl = jnp.sum(do4 * o4, axis=1, keepdims=True)
                dob = do4.astype(BF16)
                ksl, vsl = slice(g * dq, (g + 1) * dq), slice(g * dv, (g + 1) * dv)
                dq4 = jnp.zeros((rows, dq), F32)
                for c0, w in _key_chunks(n_keys, kc):
                    kk = k_ref[c0:c0 + w, ksl]
                    p = jnp.exp2(_nt(q4l, kk) - lse4)
                    dp = _nt(dob, v_ref[c0:c0 + w, vsl])
                    ds = (p * (dp - dl)).astype(BF16)
                    dq4 = dq4 + _nn(ds, kk)
                    ak[c0:c0 + w, ksl] += _tn(ds, q4s)
                    av[c0:c0 + w, vsl] += _tn(p.astype(BF16), dob)
                dq4 = dq4 * scale
                for j in range(group):
                    h = g * group + j
                    dq_ref[:, h * dq:(h + 1) * dq] = dq4[j * tq:(j + 1) * tq]

        is_ctx = i < al.nqc

        @pl.when(is_ctx)
        def _():
            run(al.Tc)

        @pl.when(jnp.logical_not(is_ctx))
        def _():
            run(al.S)

        @pl.when(i == al.nq - 1)
        def _():
            dk_ref[...] = ak[...].astype(dk_ref.dtype)
            dv_ref[...] = av[...].astype(dv_ref.dtype)

    return _pallas(
        body, name=name + "_bwd", grid=(al.B, al.nq),
        in_specs=[qs(wq), ks(wk), ks(wv), qs(wo), qs(LANES), qs(wo)],
        out_specs=[qs(wq), ks(wk), ks(wv)],
        out_shape=[jax.ShapeDtypeStruct(q.shape, F32), jax.ShapeDtypeStruct(k.shape, k.dtype),
                   jax.ShapeDtypeStruct(v.shape, v.dtype)],
        scratch_shapes=[pltpu.VMEM((al.S, wk), F32), pltpu.VMEM((al.S, wv), F32)],
        compiler_params=_cparams(("parallel", "arbitrary")),
    )(q, k, v, o, lse, do)


def attn_op(cfg, al, name):
    @jax.custom_vjp
    def op(q, k, v):
        return _attn_fwd(q, k, v, cfg, al, name)[0]

    def fwd(q, k, v):
        o, lse = _attn_fwd(q, k, v, cfg, al, name)
        return o, (q, k, v, o, lse)

    def bwd(res, do):
        return tuple(_attn_bwd(*res, do, cfg, al, name))

    op.defvjp(fwd, bwd)
    return op


SCAN_WIDTHS = (32, 16, 8, 4, 2, 1)
N_CM = 2 + 2 * len(SCAN_WIDTHS)


def _scan_consts(reverse):
    C = CHUNK
    t = np.arange(C)[:, None]
    s = np.arange(C)[None, :]
    blocks = [(s <= t), (s > t)]
    for w in SCAN_WIDTHS:
        blocks.append((s <= t) & (s // w == t // w))
    for w in SCAN_WIDTHS:
        blocks.append((s > t) & (s // w == t // w))
    masks = [np.eye(C, dtype=bool)]
    for w in SCAN_WIDTHS:
        masks.append((t // (2 * w) == s // (2 * w)) & ((t // w) % 2 == 1) & ((s // w) % 2 == 0))
    if reverse:
        blocks = [b[::-1, ::-1] for b in blocks]
        masks = [m[::-1, ::-1] for m in masks]
    cm = np.concatenate([b.astype(np.float32) for b in blocks] + [np.ones((8, C), np.float32)], axis=0)
    mw = np.stack([np.tile(m.astype(np.float32), (C_HEADS, 1)) for m in masks])
    rows = np.arange(C_HEADS * C)[:, None] // C
    lane = np.arange(C_W)[None, :] // C_DK
    hm = (rows == lane).astype(np.float32)
    bd = (np.arange(C_W)[:, None] // C_DK == lane).astype(np.float32)
    return (jnp.asarray(cm, BF16), jnp.asarray(cm.T.copy(), BF16), jnp.asarray(mw, F32),
            jnp.asarray(hm, F32), jnp.asarray(bd, F32))


def _scan_chunk(st, q, k, v, g, cm, cmt, mw, hm, bd):
    C = CHUNK
    cs = xdotl(cm, cmt, g)
    b = cs[0:C]
    rest = cs[C:2 * C]
    tot = cs[N_CM * C:N_CM * C + 1]
    kb = k.astype(BF16)

    def stack(a):
        return (jnp.concatenate([a] * C_HEADS, axis=0) * hm).astype(BF16)

    a = _nt(stack(q), kb) * mw[0]
    for i in range(len(SCAN_WIDTHS)):
        eq = jnp.exp(jnp.minimum(cs[(2 + i) * C:(3 + i) * C], 0.0))
        ek = jnp.exp(jnp.minimum(cs[(2 + len(SCAN_WIDTHS) + i) * C:(3 + len(SCAN_WIDTHS) + i) * C], 0.0))
        a = a + _nt(stack(q * eq), (k * ek).astype(BF16)) * mw[i + 1]
    oh = _nn(a.astype(BF16), v.astype(BF16)) * hm
    o = oh[0:C]
    for h in range(1, C_HEADS):
        o = o + oh[h * C:(h + 1) * C]
    o = o + _nt((q * jnp.exp(b)).astype(BF16), st.astype(BF16))
    st_new = st * jnp.exp(tot) + _tn(v.astype(BF16), (k * jnp.exp(rest)).astype(BF16)) * bd
    return o, st_new


class ScanLay:
    def __init__(self, B, T, Tc):
        self.B, self.S = B, T + Tc
        self.ncc, self.ntot = Tc // CHUNK, (T + Tc) // CHUNK

    def chunk(self, j, reverse):
        if not reverse:
            return j
        return jnp.where(j < self.ncc, self.ncc - 1 - j, self.ntot - 1 - (j - self.ncc))


def _scan_specs(sl, step):
    f = pl.BlockSpec((sl.B, CHUNK, C_W), lambda j: (0, sl.chunk(step(j), False), 0))
    r = pl.BlockSpec((sl.B, CHUNK, C_W), lambda j: (0, sl.chunk(step(j), True), 0))
    return f, r


def _scan_fwd(q, kf, gf, kb, gb, v, sl, name):
    B, S = sl.B, sl.S
    view = lambda a: a.reshape(B, S, C_W)
    cf, cr = _scan_consts(False), _scan_consts(True)
    nc = len(cf)
    f, r = _scan_specs(sl, lambda j: j)
    cspecs = [pl.BlockSpec(c.shape, lambda j, nd=c.ndim: (0,) * nd) for c in cf + cr]

    def body(*refs):
        (qf_ref, kf_ref, gf_ref, vf_ref, qr_ref, kr_ref, gr_ref, vr_ref), refs = refs[:8], refs[8:]
        cfv, crv = [c[...] for c in refs[:nc]], [c[...] for c in refs[nc:2 * nc]]
        of_ref, or_ref, st_ref, st = refs[2 * nc:]

        @pl.when(pl.program_id(0) == 0)
        def _():
            st[...] = jnp.zeros_like(st)

        st_ref[0] = st[...]
        for d, (q_, k_, g_, v_, o_, cv) in enumerate(((qf_ref, kf_ref, gf_ref, vf_ref, of_ref, cfv),
                                                     (qr_ref, kr_ref, gr_ref, vr_ref, or_ref, crv))):
            for b in range(B):
                o, st_new = _scan_chunk(st[d * B + b], q_[b], k_[b], v_[b], g_[b], *cv)
                o_[b] = o
                st[d * B + b] = st_new

    of, ob, states = _pallas(
        body, name=name, grid=(sl.ntot,),
        in_specs=[f] * 4 + [r] * 4 + cspecs,
        out_specs=[f, r, pl.BlockSpec((1, 2 * B, C_W, C_W), lambda j: (j, 0, 0, 0))],
        out_shape=[jax.ShapeDtypeStruct((B, S, C_W), F32)] * 2
        + [jax.ShapeDtypeStruct((sl.ntot, 2 * B, C_W, C_W), F32)],
        scratch_shapes=[pltpu.VMEM((2 * B, C_W, C_W), F32)],
        compiler_params=_cparams(("arbitrary",)),
    )(view(q), view(kf), view(gf), view(v), view(q), view(kb), view(gb), view(v), *cf, *cr)
    return of.reshape(B * S, C_W), ob.reshape(B * S, C_W), states


def _scan_bwd(q, kf, gf, kb, gb, v, states, dof, dob, sl, name):
    B, S = sl.B, sl.S
    view = lambda a: a.reshape(B, S, C_W)
    cf, cr = _scan_consts(False), _scan_consts(True)
    nc = len(cf)
    last = sl.ntot - 1
    f, r = _scan_specs(sl, lambda j: last - j)
    cspecs = [pl.BlockSpec(c.shape, lambda j, nd=c.ndim: (0,) * nd) for c in cf + cr]

    def body(*refs):
        ins, refs = refs[:11], refs[11:]
        qf_ref, kf_ref, gf_ref, vf_ref, dof_ref, qr_ref, kr_ref, gr_ref, vr_ref, dor_ref, st_ref = ins
        cfv, crv = [c[...] for c in refs[:nc]], [c[...] for c in refs[nc:2 * nc]]
        outs, dst = refs[2 * nc:-1], refs[-1]

        @pl.when(pl.program_id(0) == 0)
        def _():
            dst[...] = jnp.zeros_like(dst)

        for d, (q_, k_, g_, v_, do_, cv) in enumerate(((qf_ref, kf_ref, gf_ref, vf_ref, dof_ref, cfv),
                                                      (qr_ref, kr_ref, gr_ref, vr_ref, dor_ref, crv))):
            dq_, dk_, dg_, dv_ = outs[4 * d:4 * d + 4]
            for b in range(B):
                _, vjp = jax.vjp(lambda s_, a_, b_, c_, e_, cv=cv: _scan_chunk(s_, a_, b_, c_, e_, *cv),
                                 st_ref[0, d * B + b], q_[b], k_[b], v_[b], g_[b])
                ds, dq, dk, dv, dg = vjp((do_[b], dst[d * B + b]))
                dq_[b], dk_[b], dg_[b], dv_[b] = dq, dk, dg, dv
                dst[d * B + b] = ds

    res = _pallas(
        body, name=name + "_bwd", grid=(sl.ntot,),
        in_specs=[f] * 5 + [r] * 5 + [pl.BlockSpec((1, 2 * B, C_W, C_W), lambda j: (last - j, 0, 0, 0))] + cspecs,
        out_specs=[f] * 4 + [r] * 4,
        out_shape=[jax.ShapeDtypeStruct((B, S, C_W), F32)] * 8,
        scratch_shapes=[pltpu.VMEM((2 * B, C_W, C_W), F32)],
        compiler_params=_cparams(("arbitrary",)),
    )(view(q), view(kf), view(gf), view(v), view(dof), view(q), view(kb), view(gb), view(v), view(dob),
      states, *cf, *cr)
    dq_f, dk_f, dg_f, dv_f, dq_r, dk_r, dg_r, dv_r = [a.reshape(B * S, C_W) for a in res]
    return dq_f + dq_r, dk_f, dg_f, dk_r, dg_r, dv_f + dv_r


def scan_op(sl, name):
    @jax.custom_vjp
    def op(q, kf, gf, kb, gb, v):
        return _scan_fwd(q, kf, gf, kb, gb, v, sl, name)[:2]

    def fwd(q, kf, gf, kb, gb, v):
        of, ob, states = _scan_fwd(q, kf, gf, kb, gb, v, sl, name)
        return (of, ob), (q, kf, gf, kb, gb, v, states)

    def bwd(res, cts):
        return _scan_bwd(*res, cts[0], cts[1], sl, name)

    op.defvjp(fwd, bwd)
    return op


def loss_and_grad(y, target, lay):
    N, D = y.shape

    def body(y_ref, t_ref, dy_ref, l_ref):
        i = pl.program_id(0)

        @pl.when(i == 0)
        def _():
            l_ref[...] = jnp.zeros_like(l_ref)

        is_ctx = i % lay.per < lay.nc

        @pl.when(is_ctx)
        def _():
            dy_ref[...] = jnp.zeros_like(dy_ref)

        @pl.when(jnp.logical_not(is_ctx))
        def _():
            e = y_ref[...] - t_ref[...]
            dy_ref[...] = e * (1.0 / D)
            l_ref[...] += 0.5 * jnp.sum(jnp.sum(e * e, axis=1, keepdims=True) * (1.0 / D), axis=0, keepdims=True)

    def t_index(i):
        return ((i // lay.per) * lay.nl + jnp.maximum(i % lay.per - lay.nc, 0), 0)

    dy, lp = _pallas(
        body, name="loss_head", grid=(lay.n_tiles,),
        in_specs=[pl.BlockSpec((lay.tm, D), lambda i: (i, 0)), pl.BlockSpec((lay.tm, D), t_index)],
        out_specs=[pl.BlockSpec((lay.tm, D), lambda i: (i, 0)), pl.BlockSpec((8, LANES), lambda i: (0, 0))],
        out_shape=[jax.ShapeDtypeStruct((N, D), F32), jax.ShapeDtypeStruct((8, LANES), F32)],
        compiler_params=_cparams(("arbitrary",)),
    )(y, target)
    return lp, dy


def _adam_math(w, g, m, v):
    mn = ADAM_B1 * m + (1.0 - ADAM_B1) * g
    vn = ADAM_B2 * v + (1.0 - ADAM_B2) * jnp.square(g)
    m_hat = mn / (1.0 - ADAM_B1 ** ADAM_STEP)
    v_hat = vn / (1.0 - ADAM_B2 ** ADAM_STEP)
    return -ADAM_LR * (m_hat / (jnp.sqrt(v_hat) + ADAM_EPS) + ADAM_WD * w), mn, vn


ROW_TILES = (512, 256, 128, 64, 32, 16, 8)


def adamw(w, g, m, v, name):
    R, C = w.shape
    tr = _tile(R, ROW_TILES)

    def body(w_ref, g_ref, m_ref, v_ref, d_ref, mo_ref, vo_ref):
        d_ref[...], mo_ref[...], vo_ref[...] = _adam_math(w_ref[...], g_ref[...], m_ref[...], v_ref[...])

    spec = pl.BlockSpec((tr, C), lambda i: (i, 0))
    return _pallas(
        body, name=name, grid=(R // tr,), in_specs=[spec] * 4, out_specs=[spec] * 3,
        out_shape=[jax.ShapeDtypeStruct((R, C), F32)] * 3,
        compiler_params=_cparams(("parallel",)),
    )(w, g, m, v)


def adamw_slots(w, recv, m, v, name):
    R, C = w.shape
    tr = _tile(R, ROW_TILES[1:])

    def body(w_ref, r_ref, m_ref, v_ref, g_ref, d_ref, mo_ref, vo_ref):
        g = r_ref[0].astype(F32)
        for k in range(1, N_DEV):
            g = g + r_ref[k].astype(F32)
        g_ref[...] = g
        d_ref[...], mo_ref[...], vo_ref[...] = _adam_math(w_ref[...], g, m_ref[...], v_ref[...])

    spec = pl.BlockSpec((tr, C), lambda i: (i, 0))
    return _pallas(
        body, name=name, grid=(R // tr,),
        in_specs=[spec, pl.BlockSpec((N_DEV, tr, C), lambda i: (0, i, 0)), spec, spec], out_specs=[spec] * 4,
        out_shape=[jax.ShapeDtypeStruct((R, C), F32)] * 4,
        compiler_params=_cparams(("parallel",)),
    )(w, recv, m, v)


def _me():
    return lax.axis_index("x"), lax.axis_index("y"), lax.axis_index("c")


def _gather_many(x_refs, out_refs, send_sems, recv_sems, local_sems):
    x, y, c = _me()
    me, sibling = (x, y, c), (x, y, 1 - c)
    chips = [(1 - x, y), (x, 1 - y), (1 - x, 1 - y)]
    arrs = range(len(x_refs))

    def slot(a, px, py, pc):
        return out_refs[a].at[4 * px + 2 * py + pc]

    def copy(a, k, block, to, src=None):
        return pltpu.make_async_remote_copy(
            src_ref=slot(a, *block) if src is None else src, dst_ref=slot(a, *block),
            send_sem=send_sems.at[7 * a + k], recv_sem=recv_sems.at[7 * a + k], device_id=to, device_id_type=MESH)

    mine = [pltpu.make_async_copy(x_refs[a], slot(a, *me), local_sems.at[a]) for a in arrs]
    for cp in mine:
        cp.start()
    first = []
    for a in arrs:
        first.append(copy(a, 0, me, sibling, src=x_refs[a]))
        first += [copy(a, 1 + j, me, (*chip, c), src=x_refs[a]) for j, chip in enumerate(chips)]
    for cp in first:
        cp.start()
    passed = []
    for j, chip in enumerate(chips):
        for a in arrs:
            copy(a, 1 + j, (*chip, c), me).wait_recv()
            fwd = copy(a, 4 + j, (*chip, c), sibling)
            fwd.start()
            passed.append(fwd)
    for a in arrs:
        copy(a, 0, sibling, me).wait_recv()
    for j, chip in enumerate(chips):
        for a in arrs:
            copy(a, 4 + j, (*chip, 1 - c), me).wait_recv()
    for cp in first + passed:
        cp.wait_send()
    for cp in mine:
        cp.wait()


def _comm_scratch(n):
    return [pltpu.SemaphoreType.DMA((7 * n,)), pltpu.SemaphoreType.DMA((7 * n,)), pltpu.SemaphoreType.DMA((n,))]


def small_gather(xb, name):
    R = xb.shape[0]

    def body(x_ref, out_ref, sum_ref, send_sems, recv_sems, local_sems):
        _gather_many([x_ref], [out_ref], send_sems, recv_sems, local_sems)
        acc = out_ref[0]
        for k in range(1, N_DEV):
            acc = acc + out_ref[k]
        sum_ref[...] = acc

    vm = pl.BlockSpec(memory_space=pltpu.VMEM)
    return _pallas(
        body, name=name, in_specs=[vm], out_specs=[vm, vm],
        out_shape=[jax.ShapeDtypeStruct((N_DEV, R, LANES), xb.dtype), jax.ShapeDtypeStruct((R, LANES), xb.dtype)],
        scratch_shapes=_comm_scratch(1),
        compiler_params=pltpu.CompilerParams(vmem_limit_bytes=VMEM_LIMIT),
    )(xb)


def big_gather(xs, name):
    n = len(xs)

    def body(*refs):
        _gather_many(refs[:n], refs[n:2 * n], *refs[2 * n:])

    hbm = pl.BlockSpec(memory_space=pl.ANY)
    return _pallas(
        body, name=name, in_specs=[hbm] * n, out_specs=[hbm] * n,
        out_shape=[jax.ShapeDtypeStruct((N_DEV,) + a.shape, a.dtype) for a in xs],
        scratch_shapes=_comm_scratch(n),
    )(*xs)


def scatter_exchange(gs, name):
    n = len(gs)
    rels = [(dx, dy, dc) for dx in (0, 1) for dy in (0, 1) for dc in (0, 1) if (dx, dy, dc) != (0, 0, 0)]

    def body(*refs):
        g_refs, r_refs = refs[:n], refs[n:2 * n]
        send_sems, recv_sems, local_sems = refs[2 * n:]
        x, y, c = _me()
        me = 4 * x + 2 * y + c
        mine = [pltpu.make_async_copy(g_refs[a].at[me], r_refs[a].at[me], local_sems.at[a]) for a in range(n)]
        for cp in mine:
            cp.start()
        copies = []
        for r, (dx, dy, dc) in enumerate(rels):
            px, py, pc = (x + dx) % 2, (y + dy) % 2, (c + dc) % 2
            for a in range(n):
                copies.append(pltpu.make_async_remote_copy(
                    src_ref=g_refs[a].at[4 * px + 2 * py + pc], dst_ref=r_refs[a].at[me],
                    send_sem=send_sems.at[7 * a + r], recv_sem=recv_sems.at[7 * a + r],
                    device_id=(px, py, pc), device_id_type=MESH))
        for cp in copies:
            cp.start()
        for cp in copies:
            cp.wait()
        for cp in mine:
            cp.wait()

    hbm = pl.BlockSpec(memory_space=pl.ANY)
    return _pallas(
        body, name=name, in_specs=[hbm] * n, out_specs=[hbm] * n,
        out_shape=[jax.ShapeDtypeStruct(a.shape, a.dtype) for a in gs],
        scratch_shapes=_comm_scratch(n),
    )(*gs)


def _pack(arrs, dtype, row_mult):
    flat = jnp.concatenate([a.astype(dtype).reshape(-1) for a in arrs])
    pad = (-flat.shape[0]) % (LANES * row_mult)
    if pad:
        flat = jnp.concatenate([flat, jnp.zeros((pad,), dtype)])
    return flat.reshape(-1, LANES)


def _unpack(buf, shapes, lead=()):
    flat = buf.reshape(*lead, -1)
    out, off = [], 0
    for s in shapes:
        n = int(np.prod(s))
        out.append(flat[..., off:off + n].reshape(*lead, *s))
        off += n
    return out


def _rope_tables(T, tm):
    pos = np.arange(T)
    row, col = pos // GRID_W, pos % GRID_W

    def tab(rot_dim):
        nf = rot_dim // 4
        inv = ROPE_THETA ** (-np.arange(nf, dtype=np.float32) / nf)
        ang = np.concatenate([row[:, None].astype(np.float32) * inv, col[:, None].astype(np.float32) * inv], axis=-1)
        ang = ang.astype(np.float32)
        cos, sin = np.cos(ang), np.sin(ang)
        return np.concatenate([cos, cos], -1), np.concatenate([-sin, sin], -1)

    c64, s64 = tab(HD)
    c32, s32 = tab(B_ROPE)
    ca, sa = np.tile(c64, (1, A_HEADS)), np.tile(s64, (1, A_HEADS))
    cb = np.concatenate([c32, np.ones((T, LANES - B_ROPE), np.float32)], -1)
    sb = np.concatenate([s32, np.zeros((T, LANES - B_ROPE), np.float32)], -1)
    one, zero = np.ones((T, B_NOPE), np.float32), np.zeros((T, B_NOPE), np.float32)
    tail1, tail0 = np.ones((T, LANES - B_NOPE - B_ROPE), np.float32), np.zeros((T, LANES - B_NOPE - B_ROPE), np.float32)
    cq = np.tile(np.concatenate([one, c32, tail1], -1), (1, B_HEADS))
    sq = np.tile(np.concatenate([zero, s32, tail0], -1), (1, B_HEADS))

    def fin(a, ident):
        return jnp.asarray(np.concatenate([a, np.full((tm, a.shape[1]), ident, np.float32)], 0), F32)

    return fin(ca, 1.0), fin(sa, 0.0), fin(cb, 1.0), fin(sb, 0.0), fin(cq, 1.0), fin(sq, 0.0)


def _swap_matrix(width, starts, half):
    p = np.zeros((width, width), np.float32)
    for s in starts:
        for i in range(half):
            p[s + i, s + half + i] = 1.0
            p[s + half + i, s + i] = 1.0
    return jnp.asarray(p, BF16)


def _seg_matrix(width):
    h = np.arange(width) // HD
    return jnp.asarray((h[:, None] == h[None, :]).astype(np.float32), BF16)


def _key_slot_matrices():
    e1 = np.zeros((B_HEADS * B_NOPE, B_HEADS * LANES), np.float32)
    e2 = np.zeros((LANES, B_HEADS * LANES), np.float32)
    for h in range(B_HEADS):
        for i in range(B_NOPE):
            e1[h * B_NOPE + i, h * LANES + i] = 1.0
        for i in range(B_ROPE):
            e2[i, h * LANES + B_NOPE + i] = 1.0
    return jnp.asarray(e1, BF16), jnp.asarray(e2, BF16)


def _f_premod(x, sh, sc, g):
    return (_rms(x, g) * (1.0 + sc) + sh,)


def _f_post(x, y, gt, g):
    return (x + gt * _rms(y, g),)


def _f_post_pre(x, y, gt, g_post, sh, sc, g_pre):
    x1 = x + gt * _rms(y, g_post)
    return x1, _rms(x1, g_pre) * (1.0 + sc) + sh


def _f_bias(raw, b):
    return (raw + b,)


def _f_silu(x):
    return (_silu(x),)


def _f_readout(of, ob, gate, gain, seg):
    return (_head_rms(of + ob, seg, gain) * _silu(gate),)


def _f_bq(bq, cq, sq, pq):
    return (bq * cq + xdotr2(bq, pq) * sq,)


def _f_bk(bkn, bkr, e1, e2):
    return (xdotr1(bkn, e1) + xdotr1(bkr, e2),)


def _make_f_feat(layer):
    def f(feat, ca, sa, cb, sb, gaq, gak, gbq, gbkv, c00, c01, c10, c11, seg, pa, pb):
        aq = _head_rms(feat[:, 0:512], seg, gaq)
        ak = _head_rms(feat[:, 512:640], seg[0:128, 0:128], gak)
        av = feat[:, 640:768]
        aq = aq * ca + xdotr2(aq, pa) * sa
        ak = ak * ca[:, 0:128] + xdotr2(ak, pa[0:128, 0:128]) * sa[:, 0:128]
        bqn = _rms(feat[:, 768:1024], gbq, B_QR)
        bkvn = _rms(feat[:, 1024:1152], gbkv)
        bkr = feat[:, 1152:1280]
        bkr = bkr * cb + xdotr2(bkr, pb) * sb
        cq = _silu(feat[:, 1280:1536])
        zf, zb = feat[:, 1536:1792], feat[:, 1792:2048]
        if layer == 0:
            lbf = lbb = 0.0
        else:
            def share(c0, c1):
                m = jnp.maximum(c0, c1)
                e0, e1 = jnp.exp(c0 - m), jnp.exp(c1 - m)
                return e1 / (e0 + e1)
            lbf, lbb = share(c00, c10), share(c01, c11)

        def gate(z, lb):
            f_ = lb + (1.0 - lb) * _sigmoid(z)
            return (1.0 - lb) * _sigmoid(-z), jnp.log(jnp.maximum(f_, F_TINY))

        kf, gf = gate(zf, lbf)
        kb, gb = gate(zb, lbb)
        return aq, ak, av, bqn, bkvn, bkr, cq, kf, gf, kb, gb, feat[:, 2048:2304], feat[:, 2304:2560]

    return f


def _pad_w_in(w):
    z = lambda n: jnp.zeros((w.shape[0], n), w.dtype)
    return jnp.concatenate([w[:, 0:960], z(64), w[:, 960:1120], z(96), w[:, 1120:2400]], axis=1)


def _pad_w_q_up(w):
    w4 = w.reshape(B_QR, B_HEADS, B_NOPE + B_ROPE)
    w4 = jnp.pad(w4, ((0, 256 - B_QR), (0, 0), (0, LANES - B_NOPE - B_ROPE)))
    return w4.reshape(256, B_HEADS * LANES)


def _split_w_kv_up(w):
    w4 = w.reshape(B_KVR, B_HEADS, B_NOPE + B_V)
    return w4[:, :, :B_NOPE].reshape(B_KVR, -1), w4[:, :, B_NOPE:].reshape(B_KVR, -1)


def _tile_gain(g, reps, width=None):
    t = jnp.tile(g, reps)
    if width is not None and width > t.shape[0]:
        t = jnp.pad(t, (0, width - t.shape[0]))
    return t[None, :]


def local_forward(dims, p):
    B, T, Tc, D = dims
    tm = min(256, Tc)
    lay_all = Lay(B, T, Tc, tm)
    ca, sa, cb, sb, cq, sq = _rope_tables(T, tm)
    seg512, seg256 = _seg_matrix(512), _seg_matrix(C_W)
    pa = _swap_matrix(512, range(0, 512, HD), HD // 2)
    pb = _swap_matrix(LANES, [0], B_ROPE // 2)
    pq = _swap_matrix(512, [h * LANES + B_NOPE for h in range(B_HEADS)], B_ROPE // 2)
    e1, e2 = _key_slot_matrices()
    sl = ScanLay(B, T, Tc)
    al_a, al_b = AttnLay(B, T, Tc, min(128, Tc)), AttnLay(B, T, Tc, min(256, Tc))
    cfg_a = (A_KV, A_GROUP, HD, HD, HD ** -0.5, 512, 1024)
    cfg_b = (B_HEADS, 1, LANES, B_V, (B_NOPE + B_ROPE) ** -0.5, 512, T + Tc)

    tok = p["tok"]
    depth = p["modraw"].shape[0]
    mods = []
    for l in range(depth):
        bias_lay = Lay(1, 8, 0, 8)
        raw8 = jnp.pad(p["modraw"][l], ((0, 8 - B - 1), (0, 0)))
        mod = ew_op(_f_bias, bias_lay, ("tok", "par"), (True, True), ((6 * D, F32),), f"l{l}_ada_bias")(
            raw8, p["b_ada"][l][None, :])[0]
        seg_rows = jnp.concatenate([mod[0:B], jnp.broadcast_to(mod[B:B + 1], (B, 6 * D))], axis=0)[:, None, :]
        mods.append([seg_rows[:, :, i * D:(i + 1) * D] for i in range(6)])

    post_pre_kinds = ("tok", "tok", "seg", "par", "seg", "seg", "par")
    h = ew_op(_f_premod, lay_all, ("tok", "seg", "seg", "par"), (True,) * 4, ((D, BF16),), "l0_premix")(
        tok, mods[0][0], mods[0][1], p["g_pre_mix"][0][None, :])[0]
    for l in range(depth):
        tag = f"l{l}_"
        sh_m, sc_m, gt_m, sh_f, sc_f, gt_f = mods[l]

        feat = matmul_op(tag + "w_in")(h, _pad_w_in(p["w_in"][l]))
        clb = p["clb"]
        feats = ew_op(
            _make_f_feat(l), lay_all,
            ("tok", "pos", "pos", "pos", "pos") + ("par",) * 11,
            (True,) + (False,) * 4 + (True,) * 8 + (False,) * 3,
            ((512, F32), (128, BF16), (128, BF16), (256, BF16), (128, BF16), (128, F32)) + ((C_W, F32),) * 7,
            tag + "feat")(
            feat, ca, sa, cb, sb,
            _tile_gain(p["a_q_norm"][l], A_HEADS), _tile_gain(p["a_k_norm"][l], A_KV),
            _tile_gain(p["b_q_norm"][l], 1, 256), _tile_gain(p["b_kv_norm"][l], 1),
            clb[0, 0][None, :], clb[0, 1][None, :], clb[1, 0][None, :], clb[1, 1][None, :],
            seg512, pa, pb)
        aq, ak, av, bqn, bkvn, bkr, cqs, kf, gf, kb, gb, cv, cgate = feats
        bq = matmul_op(tag + "w_q_up")(bqn, _pad_w_q_up(p["w_q_up"][l]))
        bq = ew_op(_f_bq, lay_all, ("tok", "pos", "pos", "par"), (True, False, False, False), ((512, F32),),
                   tag + "bq_rope")(bq, cq, sq, pq)[0]
        w_kn, w_v = _split_w_kv_up(p["w_kv_up"][l])
        bkn = matmul_op(tag + "w_k_up")(bkvn, w_kn)
        bv = matmul_op(tag + "w_v_up", BF16)(bkvn, w_v)
        bk = ew_op(_f_bk, lay_all, ("tok", "tok", "par", "par"), (True, True, False, False),
                   ((B_HEADS * LANES, BF16),), tag + "bk_slots")(bkn, bkr, e1, e2)[0]

        ya = attn_op(cfg_a, al_a, tag + "attn_a")(aq, ak, av)
        yb = attn_op(cfg_b, al_b, tag + "attn_b")(bq, bk, bv)
        of, ob = scan_op(sl, tag + "scan")(cqs, kf, gf, kb, gb, cv)
        lay_out = lay_all
        yc = ew_op(_f_readout, lay_out, ("tok", "tok", "tok", "par", "par"), (True, True, True, True, False),
                   ((C_W, BF16),), tag + "readout")(of, ob, cgate, _tile_gain(p["c_out_norm"][l], C_HEADS), seg256)[0]
        ycat = jnp.concatenate([ya, yb, yc], axis=1)
        mixo = matmul_op(tag + "w_out")(ycat, p["w_out"][l])
        tok, hf = ew_op(_f_post_pre, lay_out, post_pre_kinds, (True,) * 7, ((D, F32), (D, BF16)),
                        tag + "postmix_preffn")(
            tok, mixo, gt_m, p["g_post_mix"][l][None, :], sh_f, sc_f, p["g_pre_ffn"][l][None, :])

        z = matmul_op(tag + "w_ff1", BF16, relu2=True)(hf, p["w_ff1"][l])
        yf = matmul_op(tag + "w_ff2")(z, p["w_ff2"][l])
        if l + 1 < depth:
            tok, h = ew_op(_f_post_pre, lay_out, post_pre_kinds, (True,) * 7, ((D, F32), (D, BF16)),
                           tag + "postffn_premix")(
                tok, yf, gt_f, p["g_post_ffn"][l][None, :], mods[l + 1][0], mods[l + 1][1],
                p["g_pre_mix"][l + 1][None, :])
        else:
            tok = ew_op(_f_post, lay_out, ("tok", "tok", "seg", "par"), (True,) * 4, ((D, F32),), tag + "postffn")(
                tok, yf, gt_f, p["g_post_ffn"][l][None, :])[0]
    return tok


BIG = ("w_in", "w_q_up", "w_kv_up", "w_out", "w_ff1", "w_ff2")
COL_SHARDED = ("w_in", "w_q_up", "w_kv_up", "w_ff1")
SMALL = ("c_ctx", "b_ada", "g_pre_mix", "g_post_mix", "g_pre_ffn", "g_post_ffn", "a_q_norm", "a_k_norm",
         "b_q_norm", "b_kv_norm", "c_out_norm")
WEIGHTS = ("c_ctx", "w_ada", "b_ada", "g_pre_mix", "g_post_mix", "g_pre_ffn", "g_post_ffn", "w_in", "a_q_norm",
           "a_k_norm", "b_q_norm", "w_q_up", "b_kv_norm", "w_kv_up", "c_lower_bounds", "c_out_norm", "w_out",
           "w_ff1", "w_ff2")
SMALL_ROWS = 64


def _assemble(name, a):
    if name in COL_SHARDED:
        return a.transpose(1, 2, 0, 3).reshape(a.shape[1], a.shape[2], N_DEV * a.shape[3])
    return a.transpose(1, 0, 2, 3).reshape(a.shape[1], N_DEV * a.shape[2], a.shape[3])


def kernel(x, c, ctx, c_ctx, w_ada, b_ada, g_pre_mix, g_post_mix, g_pre_ffn, g_post_ffn, w_in, a_q_norm, a_k_norm, b_q_norm, w_q_up, b_kv_norm, w_kv_up, c_lower_bounds, c_out_norm, w_out, w_ff1, w_ff2, loss_target, m_c_ctx, m_w_ada, m_b_ada, m_g_pre_mix, m_g_post_mix, m_g_pre_ffn, m_g_post_ffn, m_w_in, m_a_q_norm, m_a_k_norm, m_b_q_norm, m_w_q_up, m_b_kv_norm, m_w_kv_up, m_c_lower_bounds, m_c_out_norm, m_w_out, m_w_ff1, m_w_ff2, v_c_ctx, v_w_ada, v_b_ada, v_g_pre_mix, v_g_post_mix, v_g_pre_ffn, v_g_post_ffn, v_w_in, v_a_q_norm, v_a_k_norm, v_b_q_norm, v_w_q_up, v_b_kv_norm, v_w_kv_up, v_c_lower_bounds, v_c_out_norm, v_w_out, v_w_ff1, v_w_ff2):
    W = dict(c_ctx=c_ctx, w_ada=w_ada, b_ada=b_ada, g_pre_mix=g_pre_mix, g_post_mix=g_post_mix, g_pre_ffn=g_pre_ffn,
             g_post_ffn=g_post_ffn, w_in=w_in, a_q_norm=a_q_norm, a_k_norm=a_k_norm, b_q_norm=b_q_norm,
             w_q_up=w_q_up, b_kv_norm=b_kv_norm, w_kv_up=w_kv_up, c_lower_bounds=c_lower_bounds,
             c_out_norm=c_out_norm, w_out=w_out, w_ff1=w_ff1, w_ff2=w_ff2)
    M = dict(c_ctx=m_c_ctx, w_ada=m_w_ada, b_ada=m_b_ada, g_pre_mix=m_g_pre_mix, g_post_mix=m_g_post_mix,
             g_pre_ffn=m_g_pre_ffn, g_post_ffn=m_g_post_ffn, w_in=m_w_in, a_q_norm=m_a_q_norm, a_k_norm=m_a_k_norm,
             b_q_norm=m_b_q_norm, w_q_up=m_w_q_up, b_kv_norm=m_b_kv_norm, w_kv_up=m_w_kv_up,
             c_lower_bounds=m_c_lower_bounds, c_out_norm=m_c_out_norm, w_out=m_w_out, w_ff1=m_w_ff1, w_ff2=m_w_ff2)
    V = dict(c_ctx=v_c_ctx, w_ada=v_w_ada, b_ada=v_b_ada, g_pre_mix=v_g_pre_mix, g_post_mix=v_g_post_mix,
             g_pre_ffn=v_g_pre_ffn, g_post_ffn=v_g_post_ffn, w_in=v_w_in, a_q_norm=v_a_q_norm, a_k_norm=v_a_k_norm,
             b_q_norm=v_b_q_norm, w_q_up=v_w_q_up, b_kv_norm=v_b_kv_norm, w_kv_up=v_w_kv_up,
             c_lower_bounds=v_c_lower_bounds, c_out_norm=v_c_out_norm, w_out=v_w_out, w_ff1=v_w_ff1, w_ff2=v_w_ff2)

    B, T, D = x.shape
    Tc = ctx.shape[1]
    depth = w_ada.shape[0]
    ada_cols = w_ada.shape[2]
    idx = 4 * lax.axis_index("x") + 2 * lax.axis_index("y") + lax.axis_index("c")
    n_cond = N_DEV * B
    cond_rows = -(-(n_cond + 1) // 8) * 8

    clb_cols = c_lower_bounds.shape[2]
    g1, _ = small_gather(_pack([c, c_lower_bounds], F32, 8), "gather_cond")
    c_parts, clb_parts = _unpack(g1, [c.shape, c_lower_bounds.shape], lead=(N_DEV,))
    c_all = c_parts.reshape(n_cond, D)
    clb_full = clb_parts.transpose(1, 2, 0, 3).reshape(depth, 2, N_DEV * clb_cols)

    cond_lay = Lay(1, cond_rows, 0, cond_rows)

    def ada_shard(c_ctx_, w_ada_):
        cond = jnp.concatenate([c_all, c_ctx_[None, :], jnp.zeros((cond_rows - n_cond - 1, D), F32)], axis=0)
        sc = ew_op(_f_silu, cond_lay, ("tok",), (True,), ((D, F32),), "cond_silu")(cond)[0]
        return jnp.stack([matmul_op(f"l{l}_w_ada")(sc, w_ada_[l]) for l in range(depth)])

    mod_shard, vjp_ada = jax.vjp(ada_shard, c_ctx, w_ada)

    g2, _ = small_gather(_pack([mod_shard], F32, 8), "gather_mod")
    mod_all = _unpack(g2, [mod_shard.shape], lead=(N_DEV,))[0]
    mod_all = mod_all.transpose(1, 2, 0, 3).reshape(depth, cond_rows, N_DEV * ada_cols)
    mine = lax.dynamic_slice_in_dim(mod_all, idx * B, B, axis=1)
    modraw = jnp.concatenate([mine, mod_all[:, n_cond:n_cond + 1]], axis=1)

    gathered = dict(zip(BIG, big_gather([W[n].astype(BF16) for n in BIG], "gather_weights")))

    dims = (B, T, Tc, D)
    small_names = [n for n in SMALL if n != "c_ctx"]
    small_in = {n: W[n] for n in small_names}

    def fwd(x_, modraw_, small_, clb_, gathered_):
        p = dict(small_)
        p.update({n: _assemble(n, a) for n, a in gathered_.items()})
        p.update(tok=jnp.concatenate([ctx, x_], axis=1).reshape(B * (Tc + T), D), modraw=modraw_, clb=clb_)
        return local_forward(dims, p)

    y, vjp_main = jax.vjp(fwd, x, modraw, small_in, clb_full, gathered)
    loss_part, dy = loss_and_grad(y, loss_target.reshape(B * T, D), Lay(B, T, Tc, min(256, Tc)))
    dx, dmodraw, dsmall, dclb, dgathered = vjp_main(dy)

    pay3 = _pack([dmodraw] + [dsmall[n] for n in small_names] + [dclb, loss_part[0, 0:1]], F32, 8)
    g3, s3 = small_gather(pay3, "gather_small_grads")
    dmod_parts = _unpack(g3, [dmodraw.shape], lead=(N_DEV,))[0]
    tot = _unpack(s3, [dmodraw.shape] + [W[n].shape for n in small_names] + [clb_full.shape, (1,)])
    dmod_tot, small_tot, dclb_tot, loss = tot[0], dict(zip(small_names, tot[1:-2])), tot[-2], tot[-1]
    drows = dmod_parts[:, :, 0:B].transpose(1, 0, 2, 3).reshape(depth, n_cond, N_DEV * ada_cols)
    dcond = jnp.concatenate(
        [drows, dmod_tot[:, B:B + 1], jnp.zeros((depth, cond_rows - n_cond - 1, N_DEV * ada_cols), F32)], axis=1)
    dmod_shard = lax.dynamic_slice_in_dim(dcond, idx * ada_cols, ada_cols, axis=2)
    dc_ctx_part, dw_ada = vjp_ada(dmod_shard)

    _, s4 = small_gather(_pack([dc_ctx_part], F32, 8), "gather_c_ctx_grad")
    small_tot["c_ctx"] = _unpack(s4, [c_ctx.shape])[0]

    recv = dict(zip(BIG, scatter_exchange([dgathered[n] for n in BIG], "scatter_grads")))
    grads, delta, new_m, new_v = dict(small_tot), {}, {}, {}
    for n in BIG:
        shape, cols = W[n].shape, W[n].shape[-1]
        flat = lambda a: a.reshape(-1, cols)
        res = adamw_slots(flat(W[n]), recv[n].reshape(N_DEV, -1, cols), flat(M[n]), flat(V[n]), "adamw_" + n)
        grads[n], delta[n], new_m[n], new_v[n] = (a.reshape(shape) for a in res)
    grads["w_ada"] = dw_ada
    grads["c_lower_bounds"] = lax.dynamic_slice_in_dim(dclb_tot, idx * clb_cols, clb_cols, axis=2)

    flat_a = lambda a: a.reshape(-1, ada_cols)
    res = adamw(flat_a(w_ada), flat_a(dw_ada), flat_a(m_w_ada), flat_a(v_w_ada), "adamw_w_ada")
    delta["w_ada"], new_m["w_ada"], new_v["w_ada"] = (a.reshape(w_ada.shape) for a in res)
    names = list(SMALL) + ["c_lower_bounds"]
    shapes = [W[n].shape for n in names]
    res = adamw(*[_pack([src[n] for n in names], F32, SMALL_ROWS) for src in (W, grads, M, V)], "adamw_small")
    for dst, buf in zip((delta, new_m, new_v), res):
        dst.update(zip(names, _unpack(buf, shapes)))

    return (loss.reshape(()), dx, *[grads[n] for n in WEIGHTS], *[delta[n] for n in WEIGHTS],
            *[new_m[n] for n in WEIGHTS], *[new_v[n] for n in WEIGHTS])
```

```python
import math

import numpy as np

import jax
import jax.numpy as jnp
from jax import lax
from jax.experimental import pallas as pl
from jax.experimental.pallas import tpu as pltpu

F32 = jnp.float32
BF16 = jnp.bfloat16

A_HEADS, A_KV, HD = 8, 2, 64
A_GROUP = A_HEADS // A_KV
B_HEADS, B_QR, B_KVR, B_NOPE, B_ROPE, B_V = 4, 192, 128, 64, 32, 64
C_HEADS, C_DK = 4, 64
C_W = C_HEADS * C_DK
GRID_W = 64
CHUNK = 64
ROPE_THETA = 10000.0
EPS = 1e-6
F_TINY = 1e-30
D_IN = 2400
D_IN_PAD = 2560
N_DEV = 8
LANES = 128
NEG = -1e30

ADAM_LR, ADAM_B1, ADAM_B2, ADAM_EPS, ADAM_WD, ADAM_STEP = 0.001, 0.9, 0.999, 1e-08, 0.01, 10

VMEM_LIMIT = 56 * 1024 * 1024
MESH = pl.DeviceIdType.MESH


def _pallas(body, **kw):
    return pl.pallas_call(body, **kw)


def _cparams(sem):
    return pltpu.CompilerParams(dimension_semantics=sem, vmem_limit_bytes=VMEM_LIMIT)


def _split3(x):
    hi = x.astype(BF16)
    r = x - hi.astype(F32)
    mid = r.astype(BF16)
    lo = (r - mid.astype(F32)).astype(BF16)
    return hi, mid, lo


def _nn(a, b):
    return jnp.dot(a, b, preferred_element_type=F32)


def _nt(a, b):
    return lax.dot_general(a, b, (((1,), (1,)), ((), ())), preferred_element_type=F32)


def _tn(a, b):
    return lax.dot_general(a, b, (((0,), (0,)), ((), ())), preferred_element_type=F32)


def _make_xdotr(pieces):
    @jax.custom_vjp
    def op(x, m):
        return sum(_nn(p, m) for p in _split3(x)[:pieces])

    def fwd(x, m):
        return op(x, m), m

    def bwd(m, ct):
        return sum(_nt(p, m) for p in _split3(ct)[:pieces]), None

    op.defvjp(fwd, bwd)
    return op


xdotr, xdotr2, xdotr1 = _make_xdotr(3), _make_xdotr(2), _make_xdotr(1)


@jax.custom_vjp
def xdotl(m, mt, x):
    return sum(_nn(m, p) for p in _split3(x))


def _xdotl_fwd(m, mt, x):
    return xdotl(m, mt, x), (m, mt)


def _xdotl_bwd(res, ct):
    m, mt = res
    return None, None, sum(_nn(mt, p) for p in _split3(ct))


xdotl.defvjp(_xdotl_fwd, _xdotl_bwd)


def _sigmoid(x):
    return 1.0 / (1.0 + jnp.exp(-x))


def _silu(x):
    return x * _sigmoid(x)


def _rms(x, gain, n=None):
    n = x.shape[-1] if n is None else n
    ms = jnp.sum(x * x, axis=-1, keepdims=True) * (1.0 / n)
    return x * lax.rsqrt(ms + EPS) * gain


def _head_rms(x, seg, gain):
    ms = xdotr2(x * x, seg) * (1.0 / HD)
    return x * lax.rsqrt(ms + EPS) * gain


class Lay:
    def __init__(self, B, T, Tc, tm):
        self.B, self.T, self.Tc, self.tm = B, T, Tc, tm
        self.nl, self.nc = T // tm, Tc // tm
        self.per = self.nl + self.nc
        self.n_tiles = B * self.per
        self.n_seg = 2 * B
        self.rows = self.n_tiles * tm

    def seg(self, i):
        b, w = i // self.per, i % self.per
        return jnp.where(w < self.nc, self.B + b, b)

    def pos(self, i):
        w = i % self.per
        return jnp.where(w < self.nc, self.nl, w - self.nc)

    def first(self, i):
        w = i % self.per
        return jnp.logical_or(w == 0, w == self.nc)


def _ew_spec(kind, a, lay):
    if kind == "tok":
        return pl.BlockSpec((lay.tm, a.shape[1]), lambda i: (i, 0))
    if kind == "seg":
        return pl.BlockSpec((1, 1, a.shape[2]), lambda i: (lay.seg(i), 0, 0))
    if kind == "pos":
        return pl.BlockSpec((lay.tm, a.shape[1]), lambda i: (lay.pos(i), 0))
    return pl.BlockSpec(a.shape, lambda i: (0,) * a.ndim)


def _ew_load(ref, kind):
    if kind == "seg":
        return ref[0]
    if kind == "tok":
        return ref[...].astype(F32)
    return ref[...]


def _ew_fwd(f, lay, kinds, arrays, outs, name):
    n_in = len(arrays)

    def body(*refs):
        vals = [_ew_load(r, k) for r, k in zip(refs[:n_in], kinds)]
        res = f(*vals)
        for r, o in zip(res, refs[n_in:]):
            o[...] = r.astype(o.dtype)

    return _pallas(
        body, name=name, grid=(lay.n_tiles,),
        in_specs=[_ew_spec(k, a, lay) for k, a in zip(kinds, arrays)],
        out_specs=[pl.BlockSpec((lay.tm, c), lambda i: (i, 0)) for c, _ in outs],
        out_shape=[jax.ShapeDtypeStruct((lay.rows, c), dt) for c, dt in outs],
        compiler_params=_cparams(("parallel",)),
    )(*arrays)


def _ew_bwd(f, lay, kinds, diffs, arrays, cts, name):
    n_in, n_ct = len(arrays), len(cts)
    d_idx = [i for i, d in enumerate(diffs) if d]

    g_shapes, g_specs = [], []
    for i in d_idx:
        a, k = arrays[i], kinds[i]
        if k == "tok":
            g_shapes.append(jax.ShapeDtypeStruct((lay.rows, a.shape[1]), a.dtype))
            g_specs.append(pl.BlockSpec((lay.tm, a.shape[1]), lambda t: (t, 0)))
        elif k == "seg":
            g_shapes.append(jax.ShapeDtypeStruct((lay.n_seg, 1, a.shape[2]), F32))
            g_specs.append(pl.BlockSpec((1, 1, a.shape[2]), lambda t: (lay.seg(t), 0, 0)))
        else:
            g_shapes.append(jax.ShapeDtypeStruct(a.shape, F32))
            g_specs.append(pl.BlockSpec(a.shape, lambda t, nd=a.ndim: (0,) * nd))

    def body(*refs):
        vals = [_ew_load(r, k) for r, k in zip(refs[:n_in], kinds)]
        cvals = tuple(r[...].astype(F32) for r in refs[n_in:n_in + n_ct])
        g_refs = refs[n_in + n_ct:]

        def g(*dv):
            full = list(vals)
            for j, i in enumerate(d_idx):
                full[i] = dv[j]
            return tuple(o.astype(F32) for o in f(*full))

        _, vjp = jax.vjp(g, *[vals[i] for i in d_idx])
        grads = vjp(cvals)
        t = pl.program_id(0)
        for gref, grad, i in zip(g_refs, grads, d_idx):
            k = kinds[i]
            if k == "tok":
                gref[...] = grad.astype(gref.dtype)
            elif k == "seg":
                @pl.when(lay.first(t))
                def _():
                    gref[...] = jnp.zeros_like(gref)

                gref[0] += grad
            else:
                @pl.when(t == 0)
                def _():
                    gref[...] = jnp.zeros_like(gref)

                gref[...] += grad

    res = _pallas(
        body, name=name + "_bwd", grid=(lay.n_tiles,),
        in_specs=[_ew_spec(k, a, lay) for k, a in zip(kinds, arrays)]
        + [pl.BlockSpec((lay.tm, c.shape[1]), lambda i: (i, 0)) for c in cts],
        out_specs=g_specs, out_shape=g_shapes,
        compiler_params=_cparams(("arbitrary",)),
    )(*arrays, *cts)
    out = [None] * n_in
    for gr, i in zip(res, d_idx):
        a = arrays[i]
        if gr.shape != a.shape:
            pad = [(0, a.shape[0] - gr.shape[0])] + [(0, 0)] * (a.ndim - 1)
            gr = jnp.pad(gr, pad)
        out[i] = gr
    return tuple(out)


def ew_op(f, lay, kinds, diffs, outs, name):
    kinds, diffs, outs = tuple(kinds), tuple(diffs), tuple(outs)

    @jax.custom_vjp
    def op(*arrays):
        return tuple(_ew_fwd(f, lay, kinds, arrays, outs, name))

    def fwd(*arrays):
        return op(*arrays), arrays

    def bwd(arrays, cts):
        return _ew_bwd(f, lay, kinds, diffs, arrays, tuple(cts), name)

    op.defvjp(fwd, bwd)
    return op


def _tile(n, cands):
    for c in cands:
        if n % c == 0:
            return c
    return n


TN_ROW_TILES = (2176, 1024, 512, 256)


def _mm_nn(x, w, name, out_dtype, relu2):
    M, K = x.shape
    N = w.shape[1]
    tm, tn = _tile(M, (512, 256)), _tile(N, (1024, 512, 256, 128))

    def body(x_ref, w_ref, o_ref):
        acc = _nn(x_ref[...].astype(BF16), w_ref[...].astype(BF16))
        if relu2:
            acc = jnp.square(jnp.maximum(acc, 0.0))
        o_ref[...] = acc.astype(o_ref.dtype)

    return _pallas(
        body, name=name, grid=(N // tn, M // tm),
        in_specs=[pl.BlockSpec((tm, K), lambda j, i: (i, 0)), pl.BlockSpec((K, tn), lambda j, i: (0, j))],
        out_specs=pl.BlockSpec((tm, tn), lambda j, i: (i, j)),
        out_shape=jax.ShapeDtypeStruct((M, N), out_dtype),
        compiler_params=_cparams(("parallel", "parallel")),
    )(x, w)


def _through_relu2(dz_ref, z_ref):
    if z_ref is None:
        return dz_ref[...].astype(BF16)
    z = z_ref[...].astype(F32)
    root = z * lax.rsqrt(jnp.maximum(z, F_TINY))
    return (dz_ref[...].astype(F32) * (2.0 * root)).astype(BF16)


def _mm_nt(dy, w, name, out_dtype, z=None):
    M, N = dy.shape
    K = w.shape[0]
    tm = _tile(M, (256,)) if z is not None else _tile(M, (512, 256))
    tk = K if z is not None else _tile(K, (1024, 512, 256, 128))
    row = pl.BlockSpec((tm, N), lambda j, i: (i, 0))

    def body(*refs):
        dy_ref, z_ref = (refs[0], refs[1]) if z is not None else (refs[0], None)
        w_ref, o_ref = refs[-2], refs[-1]
        o_ref[...] = _nt(_through_relu2(dy_ref, z_ref), w_ref[...].astype(BF16)).astype(o_ref.dtype)

    return _pallas(
        body, name=name, grid=(K // tk, M // tm),
        in_specs=[row] * (2 if z is not None else 1) + [pl.BlockSpec((tk, N), lambda j, i: (j, 0))],
        out_specs=pl.BlockSpec((tm, tk), lambda j, i: (i, j)),
        out_shape=jax.ShapeDtypeStruct((M, K), out_dtype),
        compiler_params=_cparams(("parallel", "parallel")),
    )(*((dy, z, w) if z is not None else (dy, w)))


def _mm_tn(x, dy, name, out_dtype, z=None):
    M, K = x.shape
    N = dy.shape[1]
    tm = _tile(M, TN_ROW_TILES)
    tk, tn = _tile(K, (1024, 512, 256, 128)), _tile(N, (1024, 512, 256, 128))
    n_m = M // tm
    col = pl.BlockSpec((tm, tn), lambda a, b, m: (m, b))

    def body(*refs):
        x_ref = refs[0]
        dy_ref, z_ref = (refs[1], refs[2]) if z is not None else (refs[1], None)
        o_ref, acc_ref = refs[-2], refs[-1]
        m = pl.program_id(2)

        @pl.when(m == 0)
        def _():
            acc_ref[...] = jnp.zeros_like(acc_ref)

        acc_ref[...] += _tn(x_ref[...].astype(BF16), _through_relu2(dy_ref, z_ref))

        @pl.when(m == n_m - 1)
        def _():
            o_ref[...] = acc_ref[...].astype(o_ref.dtype)

    return _pallas(
        body, name=name, grid=(K // tk, N // tn, n_m),
        in_specs=[pl.BlockSpec((tm, tk), lambda a, b, m: (m, a))] + [col] * (2 if z is not None else 1),
        out_specs=pl.BlockSpec((tk, tn), lambda a, b, m: (a, b)),
        out_shape=jax.ShapeDtypeStruct((K, N), out_dtype),
        scratch_shapes=[pltpu.VMEM((tk, tn), F32)],
        compiler_params=_cparams(("parallel", "parallel", "arbitrary")),
    )(*((x, dy, z) if z is not None else (x, dy)))


def matmul_op(name, out_dtype=F32, relu2=False):
    @jax.custom_vjp
    def op(x, w):
        return _mm_nn(x, w, name, out_dtype, relu2)

    def fwd(x, w):
        y = op(x, w)
        return y, (x, w, y if relu2 else None)

    def bwd(res, dy):
        x, w, z = res
        return _mm_nt(dy, w, name + "_dx", x.dtype, z), _mm_tn(x, dy, name + "_dw", w.dtype, z)

    op.defvjp(fwd, bwd)
    return op


LOG2E = math.log2(math.e)


class AttnLay:
    def __init__(self, B, T, Tc, tq):
        self.B, self.T, self.Tc, self.tq = B, T, Tc, tq
        self.S = T + Tc
        self.nq, self.nqc = self.S // tq, Tc // tq


def _attn_specs(al):
    qs = lambda w: pl.BlockSpec((al.tq, w), lambda b, i: (b * al.nq + i, 0))
    ks = lambda w: pl.BlockSpec((al.S, w), lambda b, i: (b, 0))
    return qs, ks


def _lane_fold(acc, x, op):
    for j in range(x.shape[1] // LANES):
        acc = op(acc, x[:, j * LANES:(j + 1) * LANES])
    return acc


def _key_chunks(n, kc):
    return [(c0, min(kc, n - c0)) for c0 in range(0, n, kc)]


def _stack(ref, g, group, width, tq):
    parts = [ref[:, (g * group + j) * width:(g * group + j + 1) * width].astype(F32) for j in range(group)]
    return parts[0] if group == 1 else jnp.concatenate(parts, axis=0)


PEER_RELATIONS = [(dx, dy, dc) for dx in (0, 1) for dy in (0, 1) for dc in (0, 1) if (dx, dy, dc) != (0, 0, 0)]


def _exchange_behind(al, src_refs, dst_refs, sems, gather):
    send_sems, recv_sems, local_sems = sems
    x, y, c = _me()
    me = 4 * x + 2 * y + c

    def copies():
        out = []
        for a, (s, d) in enumerate(zip(src_refs, dst_refs)):
            out.append(pltpu.make_async_copy(s if gather else s.at[me], d.at[me], local_sems.at[a]))
            for r, (dx, dy, dc) in enumerate(PEER_RELATIONS):
                px, py, pc = (x + dx) % 2, (y + dy) % 2, (c + dc) % 2
                out.append(pltpu.make_async_remote_copy(
                    src_ref=s if gather else s.at[4 * px + 2 * py + pc], dst_ref=d.at[me],
                    send_sem=send_sems.at[7 * a + r], recv_sem=recv_sems.at[7 * a + r],
                    device_id=(px, py, pc), device_id_type=MESH))
        return out

    b, i = pl.program_id(0), pl.program_id(1)

    @pl.when(jnp.logical_and(b == 0, i == 0))
    def _():
        for cp in copies():
            cp.start()

    @pl.when(jnp.logical_and(b == al.B - 1, i == al.nq - 1))
    def _():
        for cp in copies():
            cp.wait()


def _attn_fwd(q, k, v, cfg, al, name, shards=()):
    n_kv, group, dq, dv, scale, kc, _ = cfg
    tq = al.tq
    rows = group * tq
    wq, wk, wv, wo = q.shape[1], k.shape[1], v.shape[1], n_kv * group * dv
    qs, ks = _attn_specs(al)
    n = len(shards)

    def body(*refs):
        q_ref, k_ref, v_ref = refs[:3]
        x_refs = refs[3:3 + n]
        o_ref, lse_ref = refs[3 + n:5 + n]
        g_refs = refs[5 + n:5 + 2 * n]
        s_scr = refs[5 + 2 * n]
        if n:
            _exchange_behind(al, x_refs, g_refs, refs[6 + 2 * n:], gather=True)
        lane = lax.broadcasted_iota(jnp.int32, (tq, LANES), 1)

        def run(n_keys):
            chunks = _key_chunks(n_keys, kc)
            lse_all = jnp.zeros((tq, LANES), F32)
            for g in range(n_kv):
                q4 = (_stack(q_ref, g, group, dq, tq) * (scale * LOG2E)).astype(BF16)
                ksl, vsl = slice(g * dq, (g + 1) * dq), slice(g * dv, (g + 1) * dv)
                m_part = jnp.full((rows, LANES), -jnp.inf, F32)
                for c0, w in chunks:
                    s = _nt(q4, k_ref[c0:c0 + w, ksl])
                    s_scr[:, c0:c0 + w] = s
                    m_part = _lane_fold(m_part, s, jnp.maximum)
                m = jnp.max(m_part, axis=1, keepdims=True)
                l_part = jnp.zeros((rows, LANES), F32)
                acc = jnp.zeros((rows, dv), F32)
                for c0, w in chunks:
                    p = jnp.exp2(s_scr[:, c0:c0 + w] - m)
                    l_part = _lane_fold(l_part, p, jnp.add)
                    acc = acc + _nn(p.astype(BF16), v_ref[c0:c0 + w, vsl])
                l = jnp.sum(l_part, axis=1, keepdims=True)
                o = acc / l
                lse = m + jnp.log2(l)
                for j in range(group):
                    h = g * group + j
                    o_ref[:, h * dv:(h + 1) * dv] = o[j * tq:(j + 1) * tq].astype(o_ref.dtype)
                    lse_all = jnp.where(lane == h, lse[j * tq:(j + 1) * tq], lse_all)
            lse_ref[...] = lse_all

        is_ctx = pl.program_id(1) < al.nqc

        @pl.when(is_ctx)
        def _():
            run(al.Tc)

        @pl.when(jnp.logical_not(is_ctx))
        def _():
            run(al.S)

    hbm = pl.BlockSpec(memory_space=pl.ANY)
    res = _pallas(
        body, name=name, grid=(al.B, al.nq),
        in_specs=[qs(wq), ks(wk), ks(wv)] + [hbm] * n,
        out_specs=[qs(wo), qs(LANES)] + [hbm] * n,
        out_shape=[jax.ShapeDtypeStruct((q.shape[0], wo), BF16), jax.ShapeDtypeStruct((q.shape[0], LANES), F32)]
        + [jax.ShapeDtypeStruct((N_DEV,) + a.shape, a.dtype) for a in shards],
        scratch_shapes=[pltpu.VMEM((rows, al.S), F32)] + (_comm_scratch(n) if n else []),
        compiler_params=_cparams(("arbitrary", "arbitrary") if n else ("parallel", "parallel")),
    )(q, k, v, *shards)
    return res[0], res[1], list(res[2:])


def _attn_bwd(q, k, v, o, lse, do, cfg, al, name, partials=()):
    n_kv, group, dq, dv, scale, _, kc = cfg
    tq = al.tq
    rows = group * tq
    wq, wk, wv, wo = q.shape[1], k.shape[1], v.shape[1], n_kv * group * dv
    qs, ks = _attn_specs(al)
    n = len(partials)

    def body(*refs):
        q_ref, k_ref, v_ref, o_ref, lse_ref, do_ref = refs[:6]
        p_refs = refs[6:6 + n]
        dq_ref, dk_ref, dv_ref = refs[6 + n:9 + n]
        r_refs = refs[9 + n:9 + 2 * n]
        ak, av = refs[9 + 2 * n:11 + 2 * n]
        if n:
            _exchange_behind(al, p_refs, r_refs, refs[11 + 2 * n:], gather=False)
        i = pl.program_id(1)

        @pl.when(i == 0)
        def _():
            ak[...] = jnp.zeros_like(ak)
            av[...] = jnp.zeros_like(av)

        lane = lax.broadcasted_iota(jnp.int32, (tq, LANES), 1)

        def run(n_keys):
            lse_tile = lse_ref[...]
            for g in range(n_kv):
                qf = _stack(q_ref, g, group, dq, tq)
                q4l = (qf * (scale * LOG2E)).astype(BF16)
                q4s = (qf * scale).astype(BF16)
                do4 = _stack(do_ref, g, group, dv, tq)
                o4 = _stack(o_ref, g, group, dv, tq)
                cols = [jnp.sum(jnp.where(lane == g * group + j, lse_tile, 0.0), axis=1, keepdims=True)
                        for j in range(group)]
                lse4 = cols[0] if group == 1 else jnp.concatenate(cols, axis=0)
                dl = jnp.sum(do4 * o4, axis=1, keepdims=True)
                dob = do4.astype(BF16)
                ksl, vsl = slice(g * dq, (g + 1) * dq), slice(g * dv, (g + 1) * dv)
                dq4 = jnp.zeros((rows, dq), F32)
                for c0, w in _key_chunks(n_keys, kc):
                    kk = k_ref[c0:c0 + w, ksl]
                    p = jnp.exp2(_nt(q4l, kk) - lse4)
                    dp = _nt(dob, v_ref[c0:c0 + w, vsl])
                    ds = (p * (dp - dl)).astype(BF16)
                    dq4 = dq4 + _nn(ds, kk)
                    ak[c0:c0 + w, ksl] += _tn(ds, q4s)
                    av[c0:c0 + w, vsl] += _tn(p.astype(BF16), dob)
                dq4 = dq4 * scale
                for j in range(group):
                    h = g * group + j
                    dq_ref[:, h * dq:(h + 1) * dq] = dq4[j * tq:(j + 1) * tq]

        is_ctx = i < al.nqc

        @pl.when(is_ctx)
        def _():
            run(al.Tc)

        @pl.when(jnp.logical_not(is_ctx))
        def _():
            run(al.S)

        @pl.when(i == al.nq - 1)
        def _():
            dk_ref[...] = ak[...].astype(dk_ref.dtype)
            dv_ref[...] = av[...].astype(dv_ref.dtype)

    hbm = pl.BlockSpec(memory_space=pl.ANY)
    res = _pallas(
        body, name=name + "_bwd", grid=(al.B, al.nq),
        in_specs=[qs(wq), ks(wk), ks(wv), qs(wo), qs(LANES), qs(wo)] + [hbm] * n,
        out_specs=[qs(wq), ks(wk), ks(wv)] + [hbm] * n,
        out_shape=[jax.ShapeDtypeStruct(q.shape, F32), jax.ShapeDtypeStruct(k.shape, k.dtype),
                   jax.ShapeDtypeStruct(v.shape, v.dtype)]
        + [jax.ShapeDtypeStruct(a.shape, a.dtype) for a in partials],
        scratch_shapes=[pltpu.VMEM((al.S, wk), F32), pltpu.VMEM((al.S, wv), F32)] + (_comm_scratch(n) if n else []),
        compiler_params=_cparams(("arbitrary", "arbitrary") if n else ("parallel", "arbitrary")),
    )(q, k, v, o, lse, do, *partials)
    return res[0], res[1], res[2], list(res[3:])


def attn_op(cfg, al, name):
    @jax.custom_vjp
    def op(q, k, v):
        return _attn_fwd(q, k, v, cfg, al, name)[0]

    def fwd(q, k, v):
        o, lse, _ = _attn_fwd(q, k, v, cfg, al, name)
        return o, (q, k, v, o, lse)

    def bwd(res, do):
        return _attn_bwd(*res, do, cfg, al, name)[:3]

    op.defvjp(fwd, bwd)
    return op


def attn_gather_op(cfg, al, name):
    @jax.custom_vjp
    def op(q, k, v, *shards):
        o, _, gathered = _attn_fwd(q, k, v, cfg, al, name, [s.astype(BF16) for s in shards])
        return (o, *gathered)

    def fwd(q, k, v, *shards):
        o, lse, gathered = _attn_fwd(q, k, v, cfg, al, name, [s.astype(BF16) for s in shards])
        return (o, *gathered), (q, k, v, o, lse)

    def bwd(res, cts):
        dq_, dk_, dv_, received = _attn_bwd(*res, cts[0], cfg, al, name, list(cts[1:]))
        return (dq_, dk_, dv_, *[sum_slots(r, name + f"_sum{a}") for a, r in enumerate(received)])

    op.defvjp(fwd, bwd)
    return op


SCAN_WIDTHS = (32, 16, 8, 4, 2, 1)
N_CM = 2 + 2 * len(SCAN_WIDTHS)


def _scan_consts(reverse):
    C = CHUNK
    t = np.arange(C)[:, None]
    s = np.arange(C)[None, :]
    blocks = [(s <= t), (s > t)]
    for w in SCAN_WIDTHS:
        blocks.append((s <= t) & (s // w == t // w))
    for w in SCAN_WIDTHS:
        blocks.append((s > t) & (s // w == t // w))
    masks = [np.eye(C, dtype=bool)]
    for w in SCAN_WIDTHS:
        masks.append((t // (2 * w) == s // (2 * w)) & ((t // w) % 2 == 1) & ((s // w) % 2 == 0))
    if reverse:
        blocks = [b[::-1, ::-1] for b in blocks]
        masks = [m[::-1, ::-1] for m in masks]
    cm = np.concatenate([b.astype(np.float32) for b in blocks] + [np.ones((8, C), np.float32)], axis=0)
    mw = np.stack([np.tile(m.astype(np.float32), (C_HEADS, 1)) for m in masks])
    rows = np.arange(C_HEADS * C)[:, None] // C
    lane = np.arange(C_W)[None, :] // C_DK
    hm = (rows == lane).astype(np.float32)
    bd = (np.arange(C_W)[:, None] // C_DK == lane).astype(np.float32)
    return (jnp.asarray(cm, BF16), jnp.asarray(cm.T.copy(), BF16), jnp.asarray(mw, F32),
            jnp.asarray(hm, F32), jnp.asarray(bd, F32))


def _scan_chunk(st, q, k, v, g, cm, cmt, mw, hm, bd):
    C = CHUNK
    cs = xdotl(cm, cmt, g)
    b = cs[0:C]
    rest = cs[C:2 * C]
    tot = cs[N_CM * C:N_CM * C + 1]
    kb = k.astype(BF16)

    def stack(a):
        return (jnp.concatenate([a] * C_HEADS, axis=0) * hm).astype(BF16)

    a = _nt(stack(q), kb) * mw[0]
    for i in range(len(SCAN_WIDTHS)):
        eq = jnp.exp(jnp.minimum(cs[(2 + i) * C:(3 + i) * C], 0.0))
        ek = jnp.exp(jnp.minimum(cs[(2 + len(SCAN_WIDTHS) + i) * C:(3 + len(SCAN_WIDTHS) + i) * C], 0.0))
        a = a + _nt(stack(q * eq), (k * ek).astype(BF16)) * mw[i + 1]
    oh = _nn(a.astype(BF16), v.astype(BF16)) * hm
    o = oh[0:C]
    for h in range(1, C_HEADS):
        o = o + oh[h * C:(h + 1) * C]
    o = o + _nt((q * jnp.exp(b)).astype(BF16), st.astype(BF16))
    st_new = st * jnp.exp(tot) + _tn(v.astype(BF16), (k * jnp.exp(rest)).astype(BF16)) * bd
    return o, st_new


class ScanLay:
    def __init__(self, B, T, Tc):
        self.B, self.S = B, T + Tc
        self.ncc, self.ntot = Tc // CHUNK, (T + Tc) // CHUNK

    def chunk(self, j, reverse):
        if not reverse:
            return j
        return jnp.where(j < self.ncc, self.ncc - 1 - j, self.ntot - 1 - (j - self.ncc))


def _scan_specs(sl, step):
    f = pl.BlockSpec((sl.B, CHUNK, C_W), lambda j: (0, sl.chunk(step(j), False), 0))
    r = pl.BlockSpec((sl.B, CHUNK, C_W), lambda j: (0, sl.chunk(step(j), True), 0))
    return f, r


def _scan_fwd(q, kf, gf, kb, gb, v, sl, name):
    B, S = sl.B, sl.S
    view = lambda a: a.reshape(B, S, C_W)
    cf, cr = _scan_consts(False), _scan_consts(True)
    nc = len(cf)
    f, r = _scan_specs(sl, lambda j: j)
    cspecs = [pl.BlockSpec(c.shape, lambda j, nd=c.ndim: (0,) * nd) for c in cf + cr]

    def body(*refs):
        (qf_ref, kf_ref, gf_ref, vf_ref, qr_ref, kr_ref, gr_ref, vr_ref), refs = refs[:8], refs[8:]
        cfv, crv = [c[...] for c in refs[:nc]], [c[...] for c in refs[nc:2 * nc]]
        of_ref, or_ref, st_ref, st = refs[2 * nc:]

        @pl.when(pl.program_id(0) == 0)
        def _():
            st[...] = jnp.zeros_like(st)

        st_ref[0] = st[...]
        for d, (q_, k_, g_, v_, o_, cv) in enumerate(((qf_ref, kf_ref, gf_ref, vf_ref, of_ref, cfv),
                                                     (qr_ref, kr_ref, gr_ref, vr_ref, or_ref, crv))):
            for b in range(B):
                o, st_new = _scan_chunk(st[d * B + b], q_[b], k_[b], v_[b], g_[b], *cv)
                o_[b] = o
                st[d * B + b] = st_new

    of, ob, states = _pallas(
        body, name=name, grid=(sl.ntot,),
        in_specs=[f] * 4 + [r] * 4 + cspecs,
        out_specs=[f, r, pl.BlockSpec((1, 2 * B, C_W, C_W), lambda j: (j, 0, 0, 0))],
        out_shape=[jax.ShapeDtypeStruct((B, S, C_W), F32)] * 2
        + [jax.ShapeDtypeStruct((sl.ntot, 2 * B, C_W, C_W), F32)],
        scratch_shapes=[pltpu.VMEM((2 * B, C_W, C_W), F32)],
        compiler_params=_cparams(("arbitrary",)),
    )(view(q), view(kf), view(gf), view(v), view(q), view(kb), view(gb), view(v), *cf, *cr)
    return of.reshape(B * S, C_W), ob.reshape(B * S, C_W), states


def _scan_bwd(q, kf, gf, kb, gb, v, states, dof, dob, sl, name):
    B, S = sl.B, sl.S
    view = lambda a: a.reshape(B, S, C_W)
    cf, cr = _scan_consts(False), _scan_consts(True)
    nc = len(cf)
    last = sl.ntot - 1
    f, r = _scan_specs(sl, lambda j: last - j)
    cspecs = [pl.BlockSpec(c.shape, lambda j, nd=c.ndim: (0,) * nd) for c in cf + cr]

    def body(*refs):
        ins, refs = refs[:11], refs[11:]
        qf_ref, kf_ref, gf_ref, vf_ref, dof_ref, qr_ref, kr_ref, gr_ref, vr_ref, dor_ref, st_ref = ins
        cfv, crv = [c[...] for c in refs[:nc]], [c[...] for c in refs[nc:2 * nc]]
        outs, dst = refs[2 * nc:-1], refs[-1]

        @pl.when(pl.program_id(0) == 0)
        def _():
            dst[...] = jnp.zeros_like(dst)

        for d, (q_, k_, g_, v_, do_, cv) in enumerate(((qf_ref, kf_ref, gf_ref, vf_ref, dof_ref, cfv),
                                                      (qr_ref, kr_ref, gr_ref, vr_ref, dor_ref, crv))):
            dq_, dk_, dg_, dv_ = outs[4 * d:4 * d + 4]
            for b in range(B):
                _, vjp = jax.vjp(lambda s_, a_, b_, c_, e_, cv=cv: _scan_chunk(s_, a_, b_, c_, e_, *cv),
                                 st_ref[0, d * B + b], q_[b], k_[b], v_[b], g_[b])
                ds, dq, dk, dv, dg = vjp((do_[b], dst[d * B + b]))
                dq_[b], dk_[b], dg_[b], dv_[b] = dq, dk, dg, dv
                dst[d * B + b] = ds

    res = _pallas(
        body, name=name + "_bwd", grid=(sl.ntot,),
        in_specs=[f] * 5 + [r] * 5 + [pl.BlockSpec((1, 2 * B, C_W, C_W), lambda j: (last - j, 0, 0, 0))] + cspecs,
        out_specs=[f] * 4 + [r] * 4,
        out_shape=[jax.ShapeDtypeStruct((B, S, C_W), F32)] * 8,
        scratch_shapes=[pltpu.VMEM((2 * B, C_W, C_W), F32)],
        compiler_params=_cparams(("arbitrary",)),
    )(view(q), view(kf), view(gf), view(v), view(dof), view(q), view(kb), view(gb), view(v), view(dob),
      states, *cf, *cr)
    dq_f, dk_f, dg_f, dv_f, dq_r, dk_r, dg_r, dv_r = [a.reshape(B * S, C_W) for a in res]
    return dq_f + dq_r, dk_f, dg_f, dk_r, dg_r, dv_f + dv_r


def scan_op(sl, name):
    @jax.custom_vjp
    def op(q, kf, gf, kb, gb, v):
        return _scan_fwd(q, kf, gf, kb, gb, v, sl, name)[:2]

    def fwd(q, kf, gf, kb, gb, v):
        of, ob, states = _scan_fwd(q, kf, gf, kb, gb, v, sl, name)
        return (of, ob), (q, kf, gf, kb, gb, v, states)

    def bwd(res, cts):
        return _scan_bwd(*res, cts[0], cts[1], sl, name)

    op.defvjp(fwd, bwd)
    return op


def loss_and_grad(y, target, lay):
    N, D = y.shape

    def body(y_ref, t_ref, dy_ref, l_ref):
        i = pl.program_id(0)

        @pl.when(i == 0)
        def _():
            l_ref[...] = jnp.zeros_like(l_ref)

        is_ctx = i % lay.per < lay.nc

        @pl.when(is_ctx)
        def _():
            dy_ref[...] = jnp.zeros_like(dy_ref)

        @pl.when(jnp.logical_not(is_ctx))
        def _():
            e = y_ref[...] - t_ref[...]
            dy_ref[...] = e * (1.0 / D)
            l_ref[...] += 0.5 * jnp.sum(jnp.sum(e * e, axis=1, keepdims=True) * (1.0 / D), axis=0, keepdims=True)

    def t_index(i):
        return ((i // lay.per) * lay.nl + jnp.maximum(i % lay.per - lay.nc, 0), 0)

    dy, lp = _pallas(
        body, name="loss_head", grid=(lay.n_tiles,),
        in_specs=[pl.BlockSpec((lay.tm, D), lambda i: (i, 0)), pl.BlockSpec((lay.tm, D), t_index)],
        out_specs=[pl.BlockSpec((lay.tm, D), lambda i: (i, 0)), pl.BlockSpec((8, LANES), lambda i: (0, 0))],
        out_shape=[jax.ShapeDtypeStruct((N, D), F32), jax.ShapeDtypeStruct((8, LANES), F32)],
        compiler_params=_cparams(("arbitrary",)),
    )(y, target)
    return lp, dy


def _adam_math(w, g, m, v):
    mn = ADAM_B1 * m + (1.0 - ADAM_B1) * g
    vn = ADAM_B2 * v + (1.0 - ADAM_B2) * jnp.square(g)
    m_hat = mn / (1.0 - ADAM_B1 ** ADAM_STEP)
    v_hat = vn / (1.0 - ADAM_B2 ** ADAM_STEP)
    return -ADAM_LR * (m_hat / (jnp.sqrt(v_hat) + ADAM_EPS) + ADAM_WD * w), mn, vn


ROW_TILES = (512, 256, 128, 64, 32, 16, 8)


def adamw(w, g, m, v, name):
    R, C = w.shape
    tr = _tile(R, ROW_TILES)

    def body(w_ref, g_ref, m_ref, v_ref, d_ref, mo_ref, vo_ref):
        d_ref[...], mo_ref[...], vo_ref[...] = _adam_math(w_ref[...], g_ref[...], m_ref[...], v_ref[...])

    spec = pl.BlockSpec((tr, C), lambda i: (i, 0))
    return _pallas(
        body, name=name, grid=(R // tr,), in_specs=[spec] * 4, out_specs=[spec] * 3,
        out_shape=[jax.ShapeDtypeStruct((R, C), F32)] * 3,
        compiler_params=_cparams(("parallel",)),
    )(w, g, m, v)


def sum_slots(recv, name):
    shape = recv.shape[1:]
    C = shape[-1]
    r3 = recv.reshape(N_DEV, -1, C)
    R = r3.shape[1]
    tr = _tile(R, ROW_TILES[1:])

    def body(r_ref, o_ref):
        g = r_ref[0].astype(F32)
        for k in range(1, N_DEV):
            g = g + r_ref[k].astype(F32)
        o_ref[...] = g

    out = _pallas(
        body, name=name, grid=(R // tr,),
        in_specs=[pl.BlockSpec((N_DEV, tr, C), lambda i: (0, i, 0))],
        out_specs=pl.BlockSpec((tr, C), lambda i: (i, 0)),
        out_shape=jax.ShapeDtypeStruct((R, C), F32),
        compiler_params=_cparams(("parallel",)),
    )(r3)
    return out.reshape(shape)


def adamw_slots(w, recv, m, v, name):
    R, C = w.shape
    tr = _tile(R, ROW_TILES[1:])

    def body(w_ref, r_ref, m_ref, v_ref, g_ref, d_ref, mo_ref, vo_ref):
        g = r_ref[0].astype(F32)
        for k in range(1, N_DEV):
            g = g + r_ref[k].astype(F32)
        g_ref[...] = g
        d_ref[...], mo_ref[...], vo_ref[...] = _adam_math(w_ref[...], g, m_ref[...], v_ref[...])

    spec = pl.BlockSpec((tr, C), lambda i: (i, 0))
    return _pallas(
        body, name=name, grid=(R // tr,),
        in_specs=[spec, pl.BlockSpec((N_DEV, tr, C), lambda i: (0, i, 0)), spec, spec], out_specs=[spec] * 4,
        out_shape=[jax.ShapeDtypeStruct((R, C), F32)] * 4,
        compiler_params=_cparams(("parallel",)),
    )(w, recv, m, v)


def _me():
    return lax.axis_index("x"), lax.axis_index("y"), lax.axis_index("c")


def _gather_many(x_refs, out_refs, send_sems, recv_sems, local_sems):
    x, y, c = _me()
    me, sibling = (x, y, c), (x, y, 1 - c)
    chips = [(1 - x, y), (x, 1 - y), (1 - x, 1 - y)]
    arrs = range(len(x_refs))

    def slot(a, px, py, pc):
        return out_refs[a].at[4 * px + 2 * py + pc]

    def copy(a, k, block, to, src=None):
        return pltpu.make_async_remote_copy(
            src_ref=slot(a, *block) if src is None else src, dst_ref=slot(a, *block),
            send_sem=send_sems.at[7 * a + k], recv_sem=recv_sems.at[7 * a + k], device_id=to, device_id_type=MESH)

    mine = [pltpu.make_async_copy(x_refs[a], slot(a, *me), local_sems.at[a]) for a in arrs]
    for cp in mine:
        cp.start()
    first = []
    for a in arrs:
        first.append(copy(a, 0, me, sibling, src=x_refs[a]))
        first += [copy(a, 1 + j, me, (*chip, c), src=x_refs[a]) for j, chip in enumerate(chips)]
    for cp in first:
        cp.start()
    passed = []
    for j, chip in enumerate(chips):
        for a in arrs:
            copy(a, 1 + j, (*chip, c), me).wait_recv()
            fwd = copy(a, 4 + j, (*chip, c), sibling)
            fwd.start()
            passed.append(fwd)
    for a in arrs:
        copy(a, 0, sibling, me).wait_recv()
    for j, chip in enumerate(chips):
        for a in arrs:
            copy(a, 4 + j, (*chip, 1 - c), me).wait_recv()
    for cp in first + passed:
        cp.wait_send()
    for cp in mine:
        cp.wait()


def _comm_scratch(n):
    return [pltpu.SemaphoreType.DMA((7 * n,)), pltpu.SemaphoreType.DMA((7 * n,)), pltpu.SemaphoreType.DMA((n,))]


def small_gather(xb, name):
    R = xb.shape[0]

    def body(x_ref, out_ref, sum_ref, send_sems, recv_sems, local_sems):
        _gather_many([x_ref], [out_ref], send_sems, recv_sems, local_sems)
        acc = out_ref[0]
        for k in range(1, N_DEV):
            acc = acc + out_ref[k]
        sum_ref[...] = acc

    vm = pl.BlockSpec(memory_space=pltpu.VMEM)
    return _pallas(
        body, name=name, in_specs=[vm], out_specs=[vm, vm],
        out_shape=[jax.ShapeDtypeStruct((N_DEV, R, LANES), xb.dtype), jax.ShapeDtypeStruct((R, LANES), xb.dtype)],
        scratch_shapes=_comm_scratch(1),
        compiler_params=pltpu.CompilerParams(vmem_limit_bytes=VMEM_LIMIT),
    )(xb)


def big_gather(xs, name):
    n = len(xs)

    def body(*refs):
        _gather_many(refs[:n], refs[n:2 * n], *refs[2 * n:])

    hbm = pl.BlockSpec(memory_space=pl.ANY)
    return _pallas(
        body, name=name, in_specs=[hbm] * n, out_specs=[hbm] * n,
        out_shape=[jax.ShapeDtypeStruct((N_DEV,) + a.shape, a.dtype) for a in xs],
        scratch_shapes=_comm_scratch(n),
    )(*xs)


def scatter_exchange(gs, name):
    n = len(gs)
    rels = [(dx, dy, dc) for dx in (0, 1) for dy in (0, 1) for dc in (0, 1) if (dx, dy, dc) != (0, 0, 0)]

    def body(*refs):
        g_refs, r_refs = refs[:n], refs[n:2 * n]
        send_sems, recv_sems, local_sems = refs[2 * n:]
        x, y, c = _me()
        me = 4 * x + 2 * y + c
        mine = [pltpu.make_async_copy(g_refs[a].at[me], r_refs[a].at[me], local_sems.at[a]) for a in range(n)]
        for cp in mine:
            cp.start()
        copies = []
        for r, (dx, dy, dc) in enumerate(rels):
            px, py, pc = (x + dx) % 2, (y + dy) % 2, (c + dc) % 2
            for a in range(n):
                copies.append(pltpu.make_async_remote_copy(
                    src_ref=g_refs[a].at[4 * px + 2 * py + pc], dst_ref=r_refs[a].at[me],
                    send_sem=send_sems.at[7 * a + r], recv_sem=recv_sems.at[7 * a + r],
                    device_id=(px, py, pc), device_id_type=MESH))
        for cp in copies:
            cp.start()
        for cp in copies:
            cp.wait()
        for cp in mine:
            cp.wait()

    hbm = pl.BlockSpec(memory_space=pl.ANY)
    return _pallas(
        body, name=name, in_specs=[hbm] * n, out_specs=[hbm] * n,
        out_shape=[jax.ShapeDtypeStruct(a.shape, a.dtype) for a in gs],
        scratch_shapes=_comm_scratch(n),
    )(*gs)


def _pack(arrs, dtype, row_mult):
    flat = jnp.concatenate([a.astype(dtype).reshape(-1) for a in arrs])
    pad = (-flat.shape[0]) % (LANES * row_mult)
    if pad:
        flat = jnp.concatenate([flat, jnp.zeros((pad,), dtype)])
    return flat.reshape(-1, LANES)


def _unpack(buf, shapes, lead=()):
    flat = buf.reshape(*lead, -1)
    out, off = [], 0
    for s in shapes:
        n = int(np.prod(s))
        out.append(flat[..., off:off + n].reshape(*lead, *s))
        off += n
    return out


def _rope_tables(T, tm):
    pos = np.arange(T)
    row, col = pos // GRID_W, pos % GRID_W

    def tab(rot_dim):
        nf = rot_dim // 4
        inv = ROPE_THETA ** (-np.arange(nf, dtype=np.float32) / nf)
        ang = np.concatenate([row[:, None].astype(np.float32) * inv, col[:, None].astype(np.float32) * inv], axis=-1)
        ang = ang.astype(np.float32)
        cos, sin = np.cos(ang), np.sin(ang)
        return np.concatenate([cos, cos], -1), np.concatenate([-sin, sin], -1)

    c64, s64 = tab(HD)
    c32, s32 = tab(B_ROPE)
    ca, sa = np.tile(c64, (1, A_HEADS)), np.tile(s64, (1, A_HEADS))
    cb = np.concatenate([c32, np.ones((T, LANES - B_ROPE), np.float32)], -1)
    sb = np.concatenate([s32, np.zeros((T, LANES - B_ROPE), np.float32)], -1)
    one, zero = np.ones((T, B_NOPE), np.float32), np.zeros((T, B_NOPE), np.float32)
    tail1, tail0 = np.ones((T, LANES - B_NOPE - B_ROPE), np.float32), np.zeros((T, LANES - B_NOPE - B_ROPE), np.float32)
    cq = np.tile(np.concatenate([one, c32, tail1], -1), (1, B_HEADS))
    sq = np.tile(np.concatenate([zero, s32, tail0], -1), (1, B_HEADS))

    def fin(a, ident):
        return jnp.asarray(np.concatenate([a, np.full((tm, a.shape[1]), ident, np.float32)], 0), F32)

    return fin(ca, 1.0), fin(sa, 0.0), fin(cb, 1.0), fin(sb, 0.0), fin(cq, 1.0), fin(sq, 0.0)


def _swap_matrix(width, starts, half):
    p = np.zeros((width, width), np.float32)
    for s in starts:
        for i in range(half):
            p[s + i, s + half + i] = 1.0
            p[s + half + i, s + i] = 1.0
    return jnp.asarray(p, BF16)


def _seg_matrix(width):
    h = np.arange(width) // HD
    return jnp.asarray((h[:, None] == h[None, :]).astype(np.float32), BF16)


def _key_slot_matrices():
    e1 = np.zeros((B_HEADS * B_NOPE, B_HEADS * LANES), np.float32)
    e2 = np.zeros((LANES, B_HEADS * LANES), np.float32)
    for h in range(B_HEADS):
        for i in range(B_NOPE):
            e1[h * B_NOPE + i, h * LANES + i] = 1.0
        for i in range(B_ROPE):
            e2[i, h * LANES + B_NOPE + i] = 1.0
    return jnp.asarray(e1, BF16), jnp.asarray(e2, BF16)


def _f_premod(x, sh, sc, g):
    return (_rms(x, g) * (1.0 + sc) + sh,)


def _f_post(x, y, gt, g):
    return (x + gt * _rms(y, g),)


def _f_post_pre(x, y, gt, g_post, sh, sc, g_pre):
    x1 = x + gt * _rms(y, g_post)
    return x1, _rms(x1, g_pre) * (1.0 + sc) + sh


def _f_bias(raw, b):
    return (raw + b,)


def _f_silu(x):
    return (_silu(x),)


def _f_readout(of, ob, gate, gain, seg):
    return (_head_rms(of + ob, seg, gain) * _silu(gate),)


def _f_bq(bq, cq, sq, pq):
    return (bq * cq + xdotr2(bq, pq) * sq,)


def _f_bk(bkn, bkr, e1, e2):
    return (xdotr1(bkn, e1) + xdotr1(bkr, e2),)


def _make_f_feat(layer):
    def f(feat, ca, sa, cb, sb, gaq, gak, gbq, gbkv, c00, c01, c10, c11, seg, pa, pb):
        aq = _head_rms(feat[:, 0:512], seg, gaq)
        ak = _head_rms(feat[:, 512:640], seg[0:128, 0:128], gak)
        av = feat[:, 640:768]
        aq = aq * ca + xdotr2(aq, pa) * sa
        ak = ak * ca[:, 0:128] + xdotr2(ak, pa[0:128, 0:128]) * sa[:, 0:128]
        bqn = _rms(feat[:, 768:1024], gbq, B_QR)
        bkvn = _rms(feat[:, 1024:1152], gbkv)
        bkr = feat[:, 1152:1280]
        bkr = bkr * cb + xdotr2(bkr, pb) * sb
        cq = _silu(feat[:, 1280:1536])
        zf, zb = feat[:, 1536:1792], feat[:, 1792:2048]
        if layer == 0:
            lbf = lbb = 0.0
        else:
            def share(c0, c1):
                m = jnp.maximum(c0, c1)
                e0, e1 = jnp.exp(c0 - m), jnp.exp(c1 - m)
                return e1 / (e0 + e1)
            lbf, lbb = share(c00, c10), share(c01, c11)

        def gate(z, lb):
            f_ = lb + (1.0 - lb) * _sigmoid(z)
            return (1.0 - lb) * _sigmoid(-z), jnp.log(jnp.maximum(f_, F_TINY))

        kf, gf = gate(zf, lbf)
        kb, gb = gate(zb, lbb)
        return aq, ak, av, bqn, bkvn, bkr, cq, kf, gf, kb, gb, feat[:, 2048:2304], feat[:, 2304:2560]

    return f


def _pad_w_in(w):
    z = lambda n: jnp.zeros((w.shape[0], n), w.dtype)
    return jnp.concatenate([w[:, 0:960], z(64), w[:, 960:1120], z(96), w[:, 1120:2400]], axis=1)


def _pad_w_q_up(w):
    w4 = w.reshape(B_QR, B_HEADS, B_NOPE + B_ROPE)
    w4 = jnp.pad(w4, ((0, 256 - B_QR), (0, 0), (0, LANES - B_NOPE - B_ROPE)))
    return w4.reshape(256, B_HEADS * LANES)


def _split_w_kv_up(w):
    w4 = w.reshape(B_KVR, B_HEADS, B_NOPE + B_V)
    return w4[:, :, :B_NOPE].reshape(B_KVR, -1), w4[:, :, B_NOPE:].reshape(B_KVR, -1)


def _tile_gain(g, reps, width=None):
    t = jnp.tile(g, reps)
    if width is not None and width > t.shape[0]:
        t = jnp.pad(t, (0, width - t.shape[0]))
    return t[None, :]


def local_forward(dims, p):
    B, T, Tc, D = dims
    tm = min(256, Tc)
    lay_all = Lay(B, T, Tc, tm)
    ca, sa, cb, sb, cq, sq = _rope_tables(T, tm)
    seg512, seg256 = _seg_matrix(512), _seg_matrix(C_W)
    pa = _swap_matrix(512, range(0, 512, HD), HD // 2)
    pb = _swap_matrix(LANES, [0], B_ROPE // 2)
    pq = _swap_matrix(512, [h * LANES + B_NOPE for h in range(B_HEADS)], B_ROPE // 2)
    e1, e2 = _key_slot_matrices()
    sl = ScanLay(B, T, Tc)
    al_a, al_b = AttnLay(B, T, Tc, min(128, Tc)), AttnLay(B, T, Tc, min(256, Tc))
    cfg_a = (A_KV, A_GROUP, HD, HD, HD ** -0.5, 512, 1024)
    cfg_b = (B_HEADS, 1, LANES, B_V, (B_NOPE + B_ROPE) ** -0.5, 512, T + Tc)

    tok = p["tok"]
    depth = p["modraw"].shape[0]
    mods = []
    for l in range(depth):
        bias_lay = Lay(1, 8, 0, 8)
        raw8 = jnp.pad(p["modraw"][l], ((0, 8 - B - 1), (0, 0)))
        mod = ew_op(_f_bias, bias_lay, ("tok", "par"), (True, True), ((6 * D, F32),), f"l{l}_ada_bias")(
            raw8, p["b_ada"][l][None, :])[0]
        seg_rows = jnp.concatenate([mod[0:B], jnp.broadcast_to(mod[B:B + 1], (B, 6 * D))], axis=0)[:, None, :]
        mods.append([seg_rows[:, :, i * D:(i + 1) * D] for i in range(6)])

    post_pre_kinds = ("tok", "tok", "seg", "par", "seg", "seg", "par")
    h = ew_op(_f_premod, lay_all, ("tok", "seg", "seg", "par"), (True,) * 4, ((D, BF16),), "l0_premix")(
        tok, mods[0][0], mods[0][1], p["g_pre_mix"][0][None, :])[0]
    for l in range(depth):
        tag = f"l{l}_"
        sh_m, sc_m, gt_m, sh_f, sc_f, gt_f = mods[l]

        feat = matmul_op(tag + "w_in")(h, _pad_w_in(p["w_in"][l]))
        clb = p["clb"]
        feats = ew_op(
            _make_f_feat(l), lay_all,
            ("tok", "pos", "pos", "pos", "pos") + ("par",) * 11,
            (True,) + (False,) * 4 + (True,) * 8 + (False,) * 3,
            ((512, F32), (128, BF16), (128, BF16), (256, BF16), (128, BF16), (128, F32)) + ((C_W, F32),) * 7,
            tag + "feat")(
            feat, ca, sa, cb, sb,
            _tile_gain(p["a_q_norm"][l], A_HEADS), _tile_gain(p["a_k_norm"][l], A_KV),
            _tile_gain(p["b_q_norm"][l], 1, 256), _tile_gain(p["b_kv_norm"][l], 1),
            clb[0, 0][None, :], clb[0, 1][None, :], clb[1, 0][None, :], clb[1, 1][None, :],
            seg512, pa, pb)
        aq, ak, av, bqn, bkvn, bkr, cqs, kf, gf, kb, gb, cv, cgate = feats
        bq = matmul_op(tag + "w_q_up")(bqn, _pad_w_q_up(p["w_q_up"][l]))
        bq = ew_op(_f_bq, lay_all, ("tok", "pos", "pos", "par"), (True, False, False, False), ((512, F32),),
                   tag + "bq_rope")(bq, cq, sq, pq)[0]
        w_kn, w_v = _split_w_kv_up(p["w_kv_up"][l])
        bkn = matmul_op(tag + "w_k_up")(bkvn, w_kn)
        bv = matmul_op(tag + "w_v_up", BF16)(bkvn, w_v)
        bk = ew_op(_f_bk, lay_all, ("tok", "tok", "par", "par"), (True, True, False, False),
                   ((B_HEADS * LANES, BF16),), tag + "bk_slots")(bkn, bkr, e1, e2)[0]

        if l == 0:
            ya, *got = attn_gather_op(cfg_a, al_a, tag + "attn_a")(aq, ak, av, *[p["shard_" + n] for n in LATE])
            late = {n: _assemble(n, g) for n, g in zip(LATE, got)}
        else:
            ya = attn_op(cfg_a, al_a, tag + "attn_a")(aq, ak, av)
        yb = attn_op(cfg_b, al_b, tag + "attn_b")(bq, bk, bv)
        of, ob = scan_op(sl, tag + "scan")(cqs, kf, gf, kb, gb, cv)
        lay_out = lay_all
        yc = ew_op(_f_readout, lay_out, ("tok", "tok", "tok", "par", "par"), (True, True, True, True, False),
                   ((C_W, BF16),), tag + "readout")(of, ob, cgate, _tile_gain(p["c_out_norm"][l], C_HEADS), seg256)[0]
        ycat = jnp.concatenate([ya, yb, yc], axis=1)
        mixo = matmul_op(tag + "w_out")(ycat, late["w_out"][l])
        tok, hf = ew_op(_f_post_pre, lay_out, post_pre_kinds, (True,) * 7, ((D, F32), (D, BF16)),
                        tag + "postmix_preffn")(
            tok, mixo, gt_m, p["g_post_mix"][l][None, :], sh_f, sc_f, p["g_pre_ffn"][l][None, :])

        z = matmul_op(tag + "w_ff1", BF16, relu2=True)(hf, late["w_ff1"][l])
        yf = matmul_op(tag + "w_ff2")(z, late["w_ff2"][l])
        if l + 1 < depth:
            tok, h = ew_op(_f_post_pre, lay_out, post_pre_kinds, (True,) * 7, ((D, F32), (D, BF16)),
                           tag + "postffn_premix")(
                tok, yf, gt_f, p["g_post_ffn"][l][None, :], mods[l + 1][0], mods[l + 1][1],
                p["g_pre_mix"][l + 1][None, :])
        else:
            tok = ew_op(_f_post, lay_out, ("tok", "tok", "seg", "par"), (True,) * 4, ((D, F32),), tag + "postffn")(
                tok, yf, gt_f, p["g_post_ffn"][l][None, :])[0]
    return tok


EARLY = ("w_in", "w_q_up", "w_kv_up")
LATE = ("w_out", "w_ff1", "w_ff2")
COL_SHARDED = ("w_in", "w_q_up", "w_kv_up", "w_ff1")
SMALL = ("c_ctx", "b_ada", "g_pre_mix", "g_post_mix", "g_pre_ffn", "g_post_ffn", "a_q_norm", "a_k_norm",
         "b_q_norm", "b_kv_norm", "c_out_norm")
WEIGHTS = ("c_ctx", "w_ada", "b_ada", "g_pre_mix", "g_post_mix", "g_pre_ffn", "g_post_ffn", "w_in", "a_q_norm",
           "a_k_norm", "b_q_norm", "w_q_up", "b_kv_norm", "w_kv_up", "c_lower_bounds", "c_out_norm", "w_out",
           "w_ff1", "w_ff2")
SMALL_ROWS = 64


def _assemble(name, a):
    if name in COL_SHARDED:
        return a.transpose(1, 2, 0, 3).reshape(a.shape[1], a.shape[2], N_DEV * a.shape[3])
    return a.transpose(1, 0, 2, 3).reshape(a.shape[1], N_DEV * a.shape[2], a.shape[3])


def kernel(x, c, ctx, c_ctx, w_ada, b_ada, g_pre_mix, g_post_mix, g_pre_ffn, g_post_ffn, w_in, a_q_norm, a_k_norm, b_q_norm, w_q_up, b_kv_norm, w_kv_up, c_lower_bounds, c_out_norm, w_out, w_ff1, w_ff2, loss_target, m_c_ctx, m_w_ada, m_b_ada, m_g_pre_mix, m_g_post_mix, m_g_pre_ffn, m_g_post_ffn, m_w_in, m_a_q_norm, m_a_k_norm, m_b_q_norm, m_w_q_up, m_b_kv_norm, m_w_kv_up, m_c_lower_bounds, m_c_out_norm, m_w_out, m_w_ff1, m_w_ff2, v_c_ctx, v_w_ada, v_b_ada, v_g_pre_mix, v_g_post_mix, v_g_pre_ffn, v_g_post_ffn, v_w_in, v_a_q_norm, v_a_k_norm, v_b_q_norm, v_w_q_up, v_b_kv_norm, v_w_kv_up, v_c_lower_bounds, v_c_out_norm, v_w_out, v_w_ff1, v_w_ff2):
    W = dict(c_ctx=c_ctx, w_ada=w_ada, b_ada=b_ada, g_pre_mix=g_pre_mix, g_post_mix=g_post_mix, g_pre_ffn=g_pre_ffn,
             g_post_ffn=g_post_ffn, w_in=w_in, a_q_norm=a_q_norm, a_k_norm=a_k_norm, b_q_norm=b_q_norm,
             w_q_up=w_q_up, b_kv_norm=b_kv_norm, w_kv_up=w_kv_up, c_lower_bounds=c_lower_bounds,
             c_out_norm=c_out_norm, w_out=w_out, w_ff1=w_ff1, w_ff2=w_ff2)
    M = dict(c_ctx=m_c_ctx, w_ada=m_w_ada, b_ada=m_b_ada, g_pre_mix=m_g_pre_mix, g_post_mix=m_g_post_mix,
             g_pre_ffn=m_g_pre_ffn, g_post_ffn=m_g_post_ffn, w_in=m_w_in, a_q_norm=m_a_q_norm, a_k_norm=m_a_k_norm,
             b_q_norm=m_b_q_norm, w_q_up=m_w_q_up, b_kv_norm=m_b_kv_norm, w_kv_up=m_w_kv_up,
             c_lower_bounds=m_c_lower_bounds, c_out_norm=m_c_out_norm, w_out=m_w_out, w_ff1=m_w_ff1, w_ff2=m_w_ff2)
    V = dict(c_ctx=v_c_ctx, w_ada=v_w_ada, b_ada=v_b_ada, g_pre_mix=v_g_pre_mix, g_post_mix=v_g_post_mix,
             g_pre_ffn=v_g_pre_ffn, g_post_ffn=v_g_post_ffn, w_in=v_w_in, a_q_norm=v_a_q_norm, a_k_norm=v_a_k_norm,
             b_q_norm=v_b_q_norm, w_q_up=v_w_q_up, b_kv_norm=v_b_kv_norm, w_kv_up=v_w_kv_up,
             c_lower_bounds=v_c_lower_bounds, c_out_norm=v_c_out_norm, w_out=v_w_out, w_ff1=v_w_ff1, w_ff2=v_w_ff2)

    B, T, D = x.shape
    Tc = ctx.shape[1]
    depth = w_ada.shape[0]
    ada_cols = w_ada.shape[2]
    idx = 4 * lax.axis_index("x") + 2 * lax.axis_index("y") + lax.axis_index("c")
    n_cond = N_DEV * B
    cond_rows = -(-(n_cond + 1) // 8) * 8

    clb_cols = c_lower_bounds.shape[2]
    g1, _ = small_gather(_pack([c, c_lower_bounds], F32, 8), "gather_cond")
    c_parts, clb_parts = _unpack(g1, [c.shape, c_lower_bounds.shape], lead=(N_DEV,))
    c_all = c_parts.reshape(n_cond, D)
    clb_full = clb_parts.transpose(1, 2, 0, 3).reshape(depth, 2, N_DEV * clb_cols)

    cond_lay = Lay(1, cond_rows, 0, cond_rows)

    def ada_shard(c_ctx_, w_ada_):
        cond = jnp.concatenate([c_all, c_ctx_[None, :], jnp.zeros((cond_rows - n_cond - 1, D), F32)], axis=0)
        sc = ew_op(_f_silu, cond_lay, ("tok",), (True,), ((D, F32),), "cond_silu")(cond)[0]
        return jnp.stack([matmul_op(f"l{l}_w_ada")(sc, w_ada_[l]) for l in range(depth)])

    mod_shard, vjp_ada = jax.vjp(ada_shard, c_ctx, w_ada)

    g2, _ = small_gather(_pack([mod_shard], F32, 8), "gather_mod")
    mod_all = _unpack(g2, [mod_shard.shape], lead=(N_DEV,))[0]
    mod_all = mod_all.transpose(1, 2, 0, 3).reshape(depth, cond_rows, N_DEV * ada_cols)
    mine = lax.dynamic_slice_in_dim(mod_all, idx * B, B, axis=1)
    modraw = jnp.concatenate([mine, mod_all[:, n_cond:n_cond + 1]], axis=1)

    gathered = dict(zip(EARLY, big_gather([W[n].astype(BF16) for n in EARLY], "gather_weights")))

    dims = (B, T, Tc, D)
    small_names = [n for n in SMALL if n != "c_ctx"]
    small_in = {n: W[n] for n in small_names}
    late_in = {n: W[n] for n in LATE}

    def fwd(x_, modraw_, small_, clb_, gathered_, late_):
        p = dict(small_)
        p.update({n: _assemble(n, a) for n, a in gathered_.items()})
        p.update({"shard_" + n: a for n, a in late_.items()})
        p.update(tok=jnp.concatenate([ctx, x_], axis=1).reshape(B * (Tc + T), D), modraw=modraw_, clb=clb_)
        return local_forward(dims, p)

    y, vjp_main = jax.vjp(fwd, x, modraw, small_in, clb_full, gathered, late_in)
    loss_part, dy = loss_and_grad(y, loss_target.reshape(B * T, D), Lay(B, T, Tc, min(256, Tc)))
    dx, dmodraw, dsmall, dclb, dgathered, dlate = vjp_main(dy)

    pay3 = _pack([dmodraw] + [dsmall[n] for n in small_names] + [dclb, loss_part[0, 0:1]], F32, 8)
    g3, s3 = small_gather(pay3, "gather_small_grads")
    dmod_parts = _unpack(g3, [dmodraw.shape], lead=(N_DEV,))[0]
    tot = _unpack(s3, [dmodraw.shape] + [W[n].shape for n in small_names] + [clb_full.shape, (1,)])
    dmod_tot, small_tot, dclb_tot, loss = tot[0], dict(zip(small_names, tot[1:-2])), tot[-2], tot[-1]
    drows = dmod_parts[:, :, 0:B].transpose(1, 0, 2, 3).reshape(depth, n_cond, N_DEV * ada_cols)
    dcond = jnp.concatenate(
        [drows, dmod_tot[:, B:B + 1], jnp.zeros((depth, cond_rows - n_cond - 1, N_DEV * ada_cols), F32)], axis=1)
    dmod_shard = lax.dynamic_slice_in_dim(dcond, idx * ada_cols, ada_cols, axis=2)
    dc_ctx_part, dw_ada = vjp_ada(dmod_shard)

    _, s4 = small_gather(_pack([dc_ctx_part], F32, 8), "gather_c_ctx_grad")
    small_tot["c_ctx"] = _unpack(s4, [c_ctx.shape])[0]

    recv = dict(zip(EARLY, scatter_exchange([dgathered[n] for n in EARLY], "scatter_grads")))
    grads, delta, new_m, new_v = dict(small_tot), {}, {}, {}
    for n in EARLY:
        shape, cols = W[n].shape, W[n].shape[-1]
        flat = lambda a: a.reshape(-1, cols)
        res = adamw_slots(flat(W[n]), recv[n].reshape(N_DEV, -1, cols), flat(M[n]), flat(V[n]), "adamw_" + n)
        grads[n], delta[n], new_m[n], new_v[n] = (a.reshape(shape) for a in res)
    for n in LATE:
        shape, cols = W[n].shape, W[n].shape[-1]
        flat = lambda a: a.reshape(-1, cols)
        grads[n] = dlate[n]
        res = adamw(flat(W[n]), flat(dlate[n]), flat(M[n]), flat(V[n]), "adamw_" + n)
        delta[n], new_m[n], new_v[n] = (a.reshape(shape) for a in res)
    grads["w_ada"] = dw_ada
    grads["c_lower_bounds"] = lax.dynamic_slice_in_dim(dclb_tot, idx * clb_cols, clb_cols, axis=2)

    flat_a = lambda a: a.reshape(-1, ada_cols)
    res = adamw(flat_a(w_ada), flat_a(dw_ada), flat_a(m_w_ada), flat_a(v_w_ada), "adamw_w_ada")
    delta["w_ada"], new_m["w_ada"], new_v["w_ada"] = (a.reshape(w_ada.shape) for a in res)
    names = list(SMALL) + ["c_lower_bounds"]
    shapes = [W[n].shape for n in names]
    res = adamw(*[_pack([src[n] for n in names], F32, SMALL_ROWS) for src in (W, grads, M, V)], "adamw_small")
    for dst, buf in zip((delta, new_m, new_v), res):
        dst.update(zip(names, _unpack(buf, shapes)))

    return (loss.reshape(()), dx, *[grads[n] for n in WEIGHTS], *[delta[n] for n in WEIGHTS],
            *[new_m[n] for n in WEIGHTS], *[new_v[n] for n in WEIGHTS])
```

```python
import math

import numpy as np

import jax
import jax.numpy as jnp
from jax import lax
from jax.experimental import pallas as pl
from jax.experimental.pallas import tpu as pltpu

F32 = jnp.float32
BF16 = jnp.bfloat16

A_HEADS, A_KV, HD = 8, 2, 64
A_GROUP = A_HEADS // A_KV
B_HEADS, B_QR, B_KVR, B_NOPE, B_ROPE, B_V = 4, 192, 128, 64, 32, 64
C_HEADS, C_DK = 4, 64
C_W = C_HEADS * C_DK
GRID_W = 64
CHUNK = 64
ROPE_THETA = 10000.0
EPS = 1e-6
F_TINY = 1e-30
D_IN = 2400
D_IN_PAD = 2560
N_DEV = 8
LANES = 128
NEG = -1e30

ADAM_LR, ADAM_B1, ADAM_B2, ADAM_EPS, ADAM_WD, ADAM_STEP = 0.001, 0.9, 0.999, 1e-08, 0.01, 10

VMEM_LIMIT = 56 * 1024 * 1024
MESH = pl.DeviceIdType.MESH


def _pallas(body, **kw):
    return pl.pallas_call(body, **kw)


def _cparams(sem):
    return pltpu.CompilerParams(dimension_semantics=sem, vmem_limit_bytes=VMEM_LIMIT)


def _split3(x):
    hi = x.astype(BF16)
    r = x - hi.astype(F32)
    mid = r.astype(BF16)
    lo = (r - mid.astype(F32)).astype(BF16)
    return hi, mid, lo


def _nn(a, b):
    return jnp.dot(a, b, preferred_element_type=F32)


def _nt(a, b):
    return lax.dot_general(a, b, (((1,), (1,)), ((), ())), preferred_element_type=F32)


def _tn(a, b):
    return lax.dot_general(a, b, (((0,), (0,)), ((), ())), preferred_element_type=F32)


def _make_xdotr(pieces):
    @jax.custom_vjp
    def op(x, m):
        return sum(_nn(p, m) for p in _split3(x)[:pieces])

    def fwd(x, m):
        return op(x, m), m

    def bwd(m, ct):
        return sum(_nt(p, m) for p in _split3(ct)[:pieces]), None

    op.defvjp(fwd, bwd)
    return op


xdotr, xdotr2, xdotr1 = _make_xdotr(3), _make_xdotr(2), _make_xdotr(1)


@jax.custom_vjp
def xdotl(m, mt, x):
    return sum(_nn(m, p) for p in _split3(x)[:2])


def _xdotl_fwd(m, mt, x):
    return xdotl(m, mt, x), (m, mt)


def _xdotl_bwd(res, ct):
    m, mt = res
    return None, None, sum(_nn(mt, p) for p in _split3(ct)[:2])


xdotl.defvjp(_xdotl_fwd, _xdotl_bwd)


def _sigmoid(x):
    return 1.0 / (1.0 + jnp.exp(-x))


def _silu(x):
    return x * _sigmoid(x)


def _rms(x, gain, n=None):
    n = x.shape[-1] if n is None else n
    ms = jnp.sum(x * x, axis=-1, keepdims=True) * (1.0 / n)
    return x * lax.rsqrt(ms + EPS) * gain


def _head_rms(x, seg, gain):
    ms = xdotr2(x * x, seg) * (1.0 / HD)
    return x * lax.rsqrt(ms + EPS) * gain


class Lay:
    def __init__(self, B, T, Tc, tm):
        self.B, self.T, self.Tc, self.tm = B, T, Tc, tm
        self.nl, self.nc = T // tm, Tc // tm
        self.per = self.nl + self.nc
        self.n_tiles = B * self.per
        self.n_seg = 2 * B
        self.rows = self.n_tiles * tm

    def seg(self, i):
        b, w = i // self.per, i % self.per
        return jnp.where(w < self.nc, self.B + b, b)

    def pos(self, i):
        w = i % self.per
        return jnp.where(w < self.nc, self.nl, w - self.nc)

    def first(self, i):
        w = i % self.per
        return jnp.logical_or(w == 0, w == self.nc)


def _ew_spec(kind, a, lay):
    if kind == "tok":
        return pl.BlockSpec((lay.tm, a.shape[1]), lambda i: (i, 0))
    if kind == "seg":
        return pl.BlockSpec((1, 1, a.shape[2]), lambda i: (lay.seg(i), 0, 0))
    if kind == "pos":
        return pl.BlockSpec((lay.tm, a.shape[1]), lambda i: (lay.pos(i), 0))
    return pl.BlockSpec(a.shape, lambda i: (0,) * a.ndim)


def _ew_load(ref, kind):
    if kind == "seg":
        return ref[0]
    if kind == "tok":
        return ref[...].astype(F32)
    return ref[...]


def _ew_fwd(f, lay, kinds, arrays, outs, name):
    n_in = len(arrays)

    def body(*refs):
        vals = [_ew_load(r, k) for r, k in zip(refs[:n_in], kinds)]
        res = f(*vals)
        for r, o in zip(res, refs[n_in:]):
            o[...] = r.astype(o.dtype)

    return _pallas(
        body, name=name, grid=(lay.n_tiles,),
        in_specs=[_ew_spec(k, a, lay) for k, a in zip(kinds, arrays)],
        out_specs=[pl.BlockSpec((lay.tm, c), lambda i: (i, 0)) for c, _ in outs],
        out_shape=[jax.ShapeDtypeStruct((lay.rows, c), dt) for c, dt in outs],
        compiler_params=_cparams(("parallel",)),
    )(*arrays)


def _ew_bwd(f, lay, kinds, diffs, arrays, cts, name):
    n_in, n_ct = len(arrays), len(cts)
    d_idx = [i for i, d in enumerate(diffs) if d]

    g_shapes, g_specs = [], []
    for i in d_idx:
        a, k = arrays[i], kinds[i]
        if k == "tok":
            g_shapes.append(jax.ShapeDtypeStruct((lay.rows, a.shape[1]), a.dtype))
            g_specs.append(pl.BlockSpec((lay.tm, a.shape[1]), lambda t: (t, 0)))
        elif k == "seg":
            g_shapes.append(jax.ShapeDtypeStruct((lay.n_seg, 1, a.shape[2]), F32))
            g_specs.append(pl.BlockSpec((1, 1, a.shape[2]), lambda t: (lay.seg(t), 0, 0)))
        else:
            g_shapes.append(jax.ShapeDtypeStruct(a.shape, F32))
            g_specs.append(pl.BlockSpec(a.shape, lambda t, nd=a.ndim: (0,) * nd))

    def body(*refs):
        vals = [_ew_load(r, k) for r, k in zip(refs[:n_in], kinds)]
        cvals = tuple(r[...].astype(F32) for r in refs[n_in:n_in + n_ct])
        g_refs = refs[n_in + n_ct:]

        def g(*dv):
            full = list(vals)
            for j, i in enumerate(d_idx):
                full[i] = dv[j]
            return tuple(o.astype(F32) for o in f(*full))

        _, vjp = jax.vjp(g, *[vals[i] for i in d_idx])
        grads = vjp(cvals)
        t = pl.program_id(0)
        for gref, grad, i in zip(g_refs, grads, d_idx):
            k = kinds[i]
            if k == "tok":
                gref[...] = grad.astype(gref.dtype)
            elif k == "seg":
                @pl.when(lay.first(t))
                def _():
                    gref[...] = jnp.zeros_like(gref)

                gref[0] += grad
            else:
                @pl.when(t == 0)
                def _():
                    gref[...] = jnp.zeros_like(gref)

                gref[...] += grad

    res = _pallas(
        body, name=name + "_bwd", grid=(lay.n_tiles,),
        in_specs=[_ew_spec(k, a, lay) for k, a in zip(kinds, arrays)]
        + [pl.BlockSpec((lay.tm, c.shape[1]), lambda i: (i, 0)) for c in cts],
        out_specs=g_specs, out_shape=g_shapes,
        compiler_params=_cparams(("arbitrary",)),
    )(*arrays, *cts)
    out = [None] * n_in
    for gr, i in zip(res, d_idx):
        a = arrays[i]
        if gr.shape != a.shape:
            pad = [(0, a.shape[0] - gr.shape[0])] + [(0, 0)] * (a.ndim - 1)
            gr = jnp.pad(gr, pad)
        out[i] = gr
    return tuple(out)


def ew_op(f, lay, kinds, diffs, outs, name):
    kinds, diffs, outs = tuple(kinds), tuple(diffs), tuple(outs)

    @jax.custom_vjp
    def op(*arrays):
        return tuple(_ew_fwd(f, lay, kinds, arrays, outs, name))

    def fwd(*arrays):
        return op(*arrays), arrays

    def bwd(arrays, cts):
        return _ew_bwd(f, lay, kinds, diffs, arrays, tuple(cts), name)

    op.defvjp(fwd, bwd)
    return op


def _tile(n, cands):
    for c in cands:
        if n % c == 0:
            return c
    return n


TN_ROW_TILES = (2176, 1024, 512, 256)
WEIGHT_TILE_ELEMS = 4 * 1024 * 1024


def _wide_tile(n, depth):
    for c in (2048, 1280, 1024, 512, 256, 128):
        if n % c == 0 and c * depth <= WEIGHT_TILE_ELEMS:
            return c
    return n


def _mm_nn(x, w, name, out_dtype, relu2):
    M, K = x.shape
    N = w.shape[1]
    tm, tn = _tile(M, (512, 256)), _wide_tile(N, K)

    def body(x_ref, w_ref, o_ref):
        acc = _nn(x_ref[...].astype(BF16), w_ref[...].astype(BF16))
        if relu2:
            acc = jnp.square(jnp.maximum(acc, 0.0))
        o_ref[...] = acc.astype(o_ref.dtype)

    return _pallas(
        body, name=name, grid=(N // tn, M // tm),
        in_specs=[pl.BlockSpec((tm, K), lambda j, i: (i, 0)), pl.BlockSpec((K, tn), lambda j, i: (0, j))],
        out_specs=pl.BlockSpec((tm, tn), lambda j, i: (i, j)),
        out_shape=jax.ShapeDtypeStruct((M, N), out_dtype),
        compiler_params=_cparams(("parallel", "parallel")),
    )(x, w)


def _through_relu2(dz_ref, z_ref):
    if z_ref is None:
        return dz_ref[...].astype(BF16)
    z = z_ref[...].astype(F32)
    root = z * lax.rsqrt(jnp.maximum(z, F_TINY))
    return (dz_ref[...].astype(F32) * (2.0 * root)).astype(BF16)


def _mm_nt(dy, w, name, out_dtype, z=None):
    M, N = dy.shape
    K = w.shape[0]
    tm = _tile(M, (256,)) if z is not None else _tile(M, (512, 256))
    tk = K if z is not None else _wide_tile(K, N)
    row = pl.BlockSpec((tm, N), lambda j, i: (i, 0))

    def body(*refs):
        dy_ref, z_ref = (refs[0], refs[1]) if z is not None else (refs[0], None)
        w_ref, o_ref = refs[-2], refs[-1]
        o_ref[...] = _nt(_through_relu2(dy_ref, z_ref), w_ref[...].astype(BF16)).astype(o_ref.dtype)

    return _pallas(
        body, name=name, grid=(K // tk, M // tm),
        in_specs=[row] * (2 if z is not None else 1) + [pl.BlockSpec((tk, N), lambda j, i: (j, 0))],
        out_specs=pl.BlockSpec((tm, tk), lambda j, i: (i, j)),
        out_shape=jax.ShapeDtypeStruct((M, K), out_dtype),
        compiler_params=_cparams(("parallel", "parallel")),
    )(*((dy, z, w) if z is not None else (dy, w)))


def _mm_tn(x, dy, name, out_dtype, z=None):
    M, K = x.shape
    N = dy.shape[1]
    tm = _tile(M, TN_ROW_TILES)
    tk, tn = _tile(K, (1024, 512, 256, 128)), _tile(N, (1024, 512, 256, 128))
    n_m = M // tm
    col = pl.BlockSpec((tm, tn), lambda a, b, m: (m, b))

    def body(*refs):
        x_ref = refs[0]
        dy_ref, z_ref = (refs[1], refs[2]) if z is not None else (refs[1], None)
        o_ref, acc_ref = refs[-2], refs[-1]
        m = pl.program_id(2)

        @pl.when(m == 0)
        def _():
            acc_ref[...] = jnp.zeros_like(acc_ref)

        acc_ref[...] += _tn(x_ref[...].astype(BF16), _through_relu2(dy_ref, z_ref))

        @pl.when(m == n_m - 1)
        def _():
            o_ref[...] = acc_ref[...].astype(o_ref.dtype)

    return _pallas(
        body, name=name, grid=(K // tk, N // tn, n_m),
        in_specs=[pl.BlockSpec((tm, tk), lambda a, b, m: (m, a))] + [col] * (2 if z is not None else 1),
        out_specs=pl.BlockSpec((tk, tn), lambda a, b, m: (a, b)),
        out_shape=jax.ShapeDtypeStruct((K, N), out_dtype),
        scratch_shapes=[pltpu.VMEM((tk, tn), F32)],
        compiler_params=_cparams(("parallel", "parallel", "arbitrary")),
    )(*((x, dy, z) if z is not None else (x, dy)))


def matmul_op(name, out_dtype=F32, relu2=False):
    @jax.custom_vjp
    def op(x, w):
        return _mm_nn(x, w, name, out_dtype, relu2)

    def fwd(x, w):
        y = op(x, w)
        return y, (x, w, y if relu2 else None)

    def bwd(res, dy):
        x, w, z = res
        return _mm_nt(dy, w, name + "_dx", x.dtype, z), _mm_tn(x, dy, name + "_dw", w.dtype, z)

    op.defvjp(fwd, bwd)
    return op


LOG2E = math.log2(math.e)


class AttnLay:
    def __init__(self, B, T, Tc, tq):
        self.B, self.T, self.Tc, self.tq = B, T, Tc, tq
        self.S = T + Tc
        self.nq, self.nqc = self.S // tq, Tc // tq


def _attn_specs(al):
    qs = lambda w: pl.BlockSpec((al.tq, w), lambda b, i: (b * al.nq + i, 0))
    ks = lambda w: pl.BlockSpec((al.S, w), lambda b, i: (b, 0))
    return qs, ks


def _lane_fold(acc, x, op):
    for j in range(x.shape[1] // LANES):
        acc = op(acc, x[:, j * LANES:(j + 1) * LANES])
    return acc


def _key_chunks(n, kc):
    return [(c0, min(kc, n - c0)) for c0 in range(0, n, kc)]


def _stack(ref, g, group, width, tq):
    parts = [ref[:, (g * group + j) * width:(g * group + j + 1) * width].astype(F32) for j in range(group)]
    return parts[0] if group == 1 else jnp.concatenate(parts, axis=0)


PEER_RELATIONS = [(dx, dy, dc) for dx in (0, 1) for dy in (0, 1) for dc in (0, 1) if (dx, dy, dc) != (0, 0, 0)]


def _exchange_behind(al, src_refs, dst_refs, sems, gather):
    send_sems, recv_sems, local_sems = sems
    x, y, c = _me()
    me = 4 * x + 2 * y + c

    def copies():
        out = []
        for a, (s, d) in enumerate(zip(src_refs, dst_refs)):
            out.append(pltpu.make_async_copy(s if gather else s.at[me], d.at[me], local_sems.at[a]))
            for r, (dx, dy, dc) in enumerate(PEER_RELATIONS):
                px, py, pc = (x + dx) % 2, (y + dy) % 2, (c + dc) % 2
                out.append(pltpu.make_async_remote_copy(
                    src_ref=s if gather else s.at[4 * px + 2 * py + pc], dst_ref=d.at[me],
                    send_sem=send_sems.at[7 * a + r], recv_sem=recv_sems.at[7 * a + r],
                    device_id=(px, py, pc), device_id_type=MESH))
        return out

    b, i = pl.program_id(0), pl.program_id(1)

    @pl.when(jnp.logical_and(b == 0, i == 0))
    def _():
        for cp in copies():
            cp.start()

    @pl.when(jnp.logical_and(b == al.B - 1, i == al.nq - 1))
    def _():
        for cp in copies():
            cp.wait()


def _attn_fwd(q, k, v, cfg, al, name, shards=()):
    n_kv, group, dq, dv, scale, kc, _ = cfg
    tq = al.tq
    rows = group * tq
    wq, wk, wv, wo = q.shape[1], k.shape[1], v.shape[1], n_kv * group * dv
    qs, ks = _attn_specs(al)
    n = len(shards)

    def body(*refs):
        q_ref, k_ref, v_ref = refs[:3]
        x_refs = refs[3:3 + n]
        o_ref, lse_ref = refs[3 + n:5 + n]
        g_refs = refs[5 + n:5 + 2 * n]
        s_scr = refs[5 + 2 * n]
        if n:
            _exchange_behind(al, x_refs, g_refs, refs[6 + 2 * n:], gather=True)
        lane = lax.broadcasted_iota(jnp.int32, (tq, LANES), 1)

        def run(n_keys):
            chunks = _key_chunks(n_keys, kc)
            lse_all = jnp.zeros((tq, LANES), F32)
            for g in range(n_kv):
                q4 = (_stack(q_ref, g, group, dq, tq) * (scale * LOG2E)).astype(BF16)
                ksl, vsl = slice(g * dq, (g + 1) * dq), slice(g * dv, (g + 1) * dv)
                m_part = jnp.full((rows, LANES), -jnp.inf, F32)
                for c0, w in chunks:
                    s = _nt(q4, k_ref[c0:c0 + w, ksl])
                    s_scr[:, c0:c0 + w] = s
                    m_part = _lane_fold(m_part, s, jnp.maximum)
                m = jnp.max(m_part, axis=1, keepdims=True)
                l_part = jnp.zeros((rows, LANES), F32)
                acc = jnp.zeros((rows, dv), F32)
                for c0, w in chunks:
                    p = jnp.exp2(s_scr[:, c0:c0 + w] - m)
                    l_part = _lane_fold(l_part, p, jnp.add)
                    acc = acc + _nn(p.astype(BF16), v_ref[c0:c0 + w, vsl])
                l = jnp.sum(l_part, axis=1, keepdims=True)
                o = acc / l
                lse = m + jnp.log2(l)
                for j in range(group):
                    h = g * group + j
                    o_ref[:, h * dv:(h + 1) * dv] = o[j * tq:(j + 1) * tq].astype(o_ref.dtype)
                    lse_all = jnp.where(lane == h, lse[j * tq:(j + 1) * tq], lse_all)
            lse_ref[...] = lse_all

        is_ctx = pl.program_id(1) < al.nqc

        @pl.when(is_ctx)
        def _():
            run(al.Tc)

        @pl.when(jnp.logical_not(is_ctx))
        def _():
            run(al.S)

    hbm = pl.BlockSpec(memory_space=pl.ANY)
    res = _pallas(
        body, name=name, grid=(al.B, al.nq),
        in_specs=[qs(wq), ks(wk), ks(wv)] + [hbm] * n,
        out_specs=[qs(wo), qs(LANES)] + [hbm] * n,
        out_shape=[jax.ShapeDtypeStruct((q.shape[0], wo), BF16), jax.ShapeDtypeStruct((q.shape[0], LANES), F32)]
        + [jax.ShapeDtypeStruct((N_DEV,) + a.shape, a.dtype) for a in shards],
        scratch_shapes=[pltpu.VMEM((rows, al.S), F32)] + (_comm_scratch(n) if n else []),
        compiler_params=_cparams(("arbitrary", "arbitrary") if n else ("parallel", "parallel")),
    )(q, k, v, *shards)
    return res[0], res[1], list(res[2:])


def _attn_bwd(q, k, v, o, lse, do, cfg, al, name, partials=()):
    n_kv, group, dq, dv, scale, _, kc = cfg
    tq = al.tq
    rows = group * tq
    wq, wk, wv, wo = q.shape[1], k.shape[1], v.shape[1], n_kv * group * dv
    qs, ks = _attn_specs(al)
    n = len(partials)

    def body(*refs):
        q_ref, k_ref, v_ref, o_ref, lse_ref, do_ref = refs[:6]
        p_refs = refs[6:6 + n]
        dq_ref, dk_ref, dv_ref = refs[6 + n:9 + n]
        r_refs = refs[9 + n:9 + 2 * n]
        ak, av = refs[9 + 2 * n:11 + 2 * n]
        if n:
            _exchange_behind(al, p_refs, r_refs, refs[11 + 2 * n:], gather=False)
        i = pl.program_id(1)

        @pl.when(i == 0)
        def _():
            ak[...] = jnp.zeros_like(ak)
            av[...] = jnp.zeros_like(av)

        lane = lax.broadcasted_iota(jnp.int32, (tq, LANES), 1)

        def run(n_keys):
            lse_tile = lse_ref[...]
            for g in range(n_kv):
                qf = _stack(q_ref, g, group, dq, tq)
                q4l = (qf * (scale * LOG2E)).astype(BF16)
                q4s = (qf * scale).astype(BF16)
                do4 = _stack(do_ref, g, group, dv, tq)
                o4 = _stack(o_ref, g, group, dv, tq)
                cols = [jnp.sum(jnp.where(lane == g * group + j, lse_tile, 0.0), axis=1, keepdims=True)
                        for j in range(group)]
                lse4 = cols[0] if group == 1 else jnp.concatenate(cols, axis=0)
                dl = jnp.sum(do4 * o4, axis=1, keepdims=True)
                dob = do4.astype(BF16)
                ksl, vsl = slice(g * dq, (g + 1) * dq), slice(g * dv, (g + 1) * dv)
                dq4 = jnp.zeros((rows, dq), F32)
                for c0, w in _key_chunks(n_keys, kc):
                    kk = k_ref[c0:c0 + w, ksl]
                    p = jnp.exp2(_nt(q4l, kk) - lse4)
                    dp = _nt(dob, v_ref[c0:c0 + w, vsl])
                    ds = (p * (dp - dl)).astype(BF16)
                    dq4 = dq4 + _nn(ds, kk)
                    ak[c0:c0 + w, ksl] += _tn(ds, q4s)
                    av[c0:c0 + w, vsl] += _tn(p.astype(BF16), dob)
                dq4 = dq4 * scale
                for j in range(group):
                    h = g * group + j
                    dq_ref[:, h * dq:(h + 1) * dq] = dq4[j * tq:(j + 1) * tq]

        is_ctx = i < al.nqc

        @pl.when(is_ctx)
        def _():
            run(al.Tc)

        @pl.when(jnp.logical_not(is_ctx))
        def _():
            run(al.S)

        @pl.when(i == al.nq - 1)
        def _():
            dk_ref[...] = ak[...].astype(dk_ref.dtype)
            dv_ref[...] = av[...].astype(dv_ref.dtype)

    hbm = pl.BlockSpec(memory_space=pl.ANY)
    res = _pallas(
        body, name=name + "_bwd", grid=(al.B, al.nq),
        in_specs=[qs(wq), ks(wk), ks(wv), qs(wo), qs(LANES), qs(wo)] + [hbm] * n,
        out_specs=[qs(wq), ks(wk), ks(wv)] + [hbm] * n,
        out_shape=[jax.ShapeDtypeStruct(q.shape, F32), jax.ShapeDtypeStruct(k.shape, k.dtype),
                   jax.ShapeDtypeStruct(v.shape, v.dtype)]
        + [jax.ShapeDtypeStruct(a.shape, a.dtype) for a in partials],
        scratch_shapes=[pltpu.VMEM((al.S, wk), F32), pltpu.VMEM((al.S, wv), F32)] + (_comm_scratch(n) if n else []),
        compiler_params=_cparams(("arbitrary", "arbitrary") if n else ("parallel", "arbitrary")),
    )(q, k, v, o, lse, do, *partials)
    return res[0], res[1], res[2], list(res[3:])


def attn_op(cfg, al, name):
    @jax.custom_vjp
    def op(q, k, v):
        return _attn_fwd(q, k, v, cfg, al, name)[0]

    def fwd(q, k, v):
        o, lse, _ = _attn_fwd(q, k, v, cfg, al, name)
        return o, (q, k, v, o, lse)

    def bwd(res, do):
        return _attn_bwd(*res, do, cfg, al, name)[:3]

    op.defvjp(fwd, bwd)
    return op


def attn_gather_op(cfg, al, name):
    @jax.custom_vjp
    def op(q, k, v, *shards):
        o, _, gathered = _attn_fwd(q, k, v, cfg, al, name, [s.astype(BF16) for s in shards])
        return (o, *gathered)

    def fwd(q, k, v, *shards):
        o, lse, gathered = _attn_fwd(q, k, v, cfg, al, name, [s.astype(BF16) for s in shards])
        return (o, *gathered), (q, k, v, o, lse)

    def bwd(res, cts):
        dq_, dk_, dv_, received = _attn_bwd(*res, cts[0], cfg, al, name, list(cts[1:]))
        return (dq_, dk_, dv_, *[sum_slots(r, name + f"_sum{a}") for a, r in enumerate(received)])

    op.defvjp(fwd, bwd)
    return op


SCAN_WIDTHS = (32, 16, 8, 4, 2, 1)
N_CM = 2 + 2 * len(SCAN_WIDTHS)


def _scan_consts(reverse):
    C = CHUNK
    t = np.arange(C)[:, None]
    s = np.arange(C)[None, :]
    blocks = [(s <= t), (s > t)]
    for w in SCAN_WIDTHS:
        blocks.append((s <= t) & (s // w == t // w))
    for w in SCAN_WIDTHS:
        blocks.append((s > t) & (s // w == t // w))
    masks = [np.eye(C, dtype=bool)]
    for w in SCAN_WIDTHS:
        masks.append((t // (2 * w) == s // (2 * w)) & ((t // w) % 2 == 1) & ((s // w) % 2 == 0))
    if reverse:
        blocks = [b[::-1, ::-1] for b in blocks]
        masks = [m[::-1, ::-1] for m in masks]
    cm = np.concatenate([b.astype(np.float32) for b in blocks] + [np.ones((8, C), np.float32)], axis=0)
    mw = np.stack([np.tile(m.astype(np.float32), (C_HEADS, 1)) for m in masks])
    rows = np.arange(C_HEADS * C)[:, None] // C
    lane = np.arange(C_W)[None, :] // C_DK
    hm = (rows == lane).astype(np.float32)
    bd = (np.arange(C_W)[:, None] // C_DK == lane).astype(np.float32)
    return (jnp.asarray(cm, BF16), jnp.asarray(cm.T.copy(), BF16), jnp.asarray(mw, F32),
            jnp.asarray(hm, F32), jnp.asarray(bd, F32))


def _scan_chunk(st, q, k, v, g, cm, cmt, mw, hm, bd):
    C = CHUNK
    cs = xdotl(cm, cmt, g)
    b = cs[0:C]
    rest = cs[C:2 * C]
    tot = cs[N_CM * C:N_CM * C + 1]
    kb = k.astype(BF16)

    def stack(a):
        return (jnp.concatenate([a] * C_HEADS, axis=0) * hm).astype(BF16)

    a = _nt(stack(q), kb) * mw[0]
    for i in range(len(SCAN_WIDTHS)):
        eq = jnp.exp(jnp.minimum(cs[(2 + i) * C:(3 + i) * C], 0.0))
        ek = jnp.exp(jnp.minimum(cs[(2 + len(SCAN_WIDTHS) + i) * C:(3 + len(SCAN_WIDTHS) + i) * C], 0.0))
        a = a + _nt(stack(q * eq), (k * ek).astype(BF16)) * mw[i + 1]
    oh = _nn(a.astype(BF16), v.astype(BF16)) * hm
    o = oh[0:C]
    for h in range(1, C_HEADS):
        o = o + oh[h * C:(h + 1) * C]
    o = o + _nt((q * jnp.exp(b)).astype(BF16), st.astype(BF16))
    st_new = st * jnp.exp(tot) + _tn(v.astype(BF16), (k * jnp.exp(rest)).astype(BF16)) * bd
    return o, st_new


class ScanLay:
    def __init__(self, B, T, Tc):
        self.B, self.S = B, T + Tc
        self.ncc, self.ntot = Tc // CHUNK, (T + Tc) // CHUNK

    def chunk(self, j, reverse):
        if not reverse:
            return j
        return jnp.where(j < self.ncc, self.ncc - 1 - j, self.ntot - 1 - (j - self.ncc))


def _scan_specs(sl, step):
    f = pl.BlockSpec((sl.B, CHUNK, C_W), lambda j: (0, sl.chunk(step(j), False), 0))
    r = pl.BlockSpec((sl.B, CHUNK, C_W), lambda j: (0, sl.chunk(step(j), True), 0))
    return f, r


def _scan_fwd(q, kf, gf, kb, gb, v, sl, name):
    B, S = sl.B, sl.S
    view = lambda a: a.reshape(B, S, C_W)
    cf, cr = _scan_consts(False), _scan_consts(True)
    nc = len(cf)
    f, r = _scan_specs(sl, lambda j: j)
    cspecs = [pl.BlockSpec(c.shape, lambda j, nd=c.ndim: (0,) * nd) for c in cf + cr]

    def body(*refs):
        (qf_ref, kf_ref, gf_ref, vf_ref, qr_ref, kr_ref, gr_ref, vr_ref), refs = refs[:8], refs[8:]
        cfv, crv = [c[...] for c in refs[:nc]], [c[...] for c in refs[nc:2 * nc]]
        of_ref, or_ref, st_ref, st = refs[2 * nc:]

        @pl.when(pl.program_id(0) == 0)
        def _():
            st[...] = jnp.zeros_like(st)

        st_ref[0] = st[...]
        for d, (q_, k_, g_, v_, o_, cv) in enumerate(((qf_ref, kf_ref, gf_ref, vf_ref, of_ref, cfv),
                                                     (qr_ref, kr_ref, gr_ref, vr_ref, or_ref, crv))):
            for b in range(B):
                o, st_new = _scan_chunk(st[d * B + b], q_[b], k_[b], v_[b], g_[b], *cv)
                o_[b] = o
                st[d * B + b] = st_new

    of, ob, states = _pallas(
        body, name=name, grid=(sl.ntot,),
        in_specs=[f] * 4 + [r] * 4 + cspecs,
        out_specs=[f, r, pl.BlockSpec((1, 2 * B, C_W, C_W), lambda j: (j, 0, 0, 0))],
        out_shape=[jax.ShapeDtypeStruct((B, S, C_W), F32)] * 2
        + [jax.ShapeDtypeStruct((sl.ntot, 2 * B, C_W, C_W), F32)],
        scratch_shapes=[pltpu.VMEM((2 * B, C_W, C_W), F32)],
        compiler_params=_cparams(("arbitrary",)),
    )(view(q), view(kf), view(gf), view(v), view(q), view(kb), view(gb), view(v), *cf, *cr)
    return of.reshape(B * S, C_W), ob.reshape(B * S, C_W), states


def _scan_bwd(q, kf, gf, kb, gb, v, states, dof, dob, sl, name):
    B, S = sl.B, sl.S
    view = lambda a: a.reshape(B, S, C_W)
    cf, cr = _scan_consts(False), _scan_consts(True)
    nc = len(cf)
    last = sl.ntot - 1
    f, r = _scan_specs(sl, lambda j: last - j)
    cspecs = [pl.BlockSpec(c.shape, lambda j, nd=c.ndim: (0,) * nd) for c in cf + cr]

    def body(*refs):
        ins, refs = refs[:11], refs[11:]
        qf_ref, kf_ref, gf_ref, vf_ref, dof_ref, qr_ref, kr_ref, gr_ref, vr_ref, dor_ref, st_ref = ins
        cfv, crv = [c[...] for c in refs[:nc]], [c[...] for c in refs[nc:2 * nc]]
        outs, dst = refs[2 * nc:-1], refs[-1]

        @pl.when(pl.program_id(0) == 0)
        def _():
            dst[...] = jnp.zeros_like(dst)

        for d, (q_, k_, g_, v_, do_, cv) in enumerate(((qf_ref, kf_ref, gf_ref, vf_ref, dof_ref, cfv),
                                                      (qr_ref, kr_ref, gr_ref, vr_ref, dor_ref, crv))):
            dq_, dk_, dg_, dv_ = outs[4 * d:4 * d + 4]
            for b in range(B):
                _, vjp = jax.vjp(lambda s_, a_, b_, c_, e_, cv=cv: _scan_chunk(s_, a_, b_, c_, e_, *cv),
                                 st_ref[0, d * B + b], q_[b], k_[b], v_[b], g_[b])
                ds, dq, dk, dv, dg = vjp((do_[b], dst[d * B + b]))
                dq_[b], dk_[b], dg_[b], dv_[b] = dq, dk, dg, dv
                dst[d * B + b] = ds

    res = _pallas(
        body, name=name + "_bwd", grid=(sl.ntot,),
        in_specs=[f] * 5 + [r] * 5 + [pl.BlockSpec((1, 2 * B, C_W, C_W), lambda j: (last - j, 0, 0, 0))] + cspecs,
        out_specs=[f] * 4 + [r] * 4,
        out_shape=[jax.ShapeDtypeStruct((B, S, C_W), F32)] * 8,
        scratch_shapes=[pltpu.VMEM((2 * B, C_W, C_W), F32)],
        compiler_params=_cparams(("arbitrary",)),
    )(view(q), view(kf), view(gf), view(v), view(dof), view(q), view(kb), view(gb), view(v), view(dob),
      states, *cf, *cr)
    dq_f, dk_f, dg_f, dv_f, dq_r, dk_r, dg_r, dv_r = [a.reshape(B * S, C_W) for a in res]
    return dq_f + dq_r, dk_f, dg_f, dk_r, dg_r, dv_f + dv_r


def scan_op(sl, name):
    @jax.custom_vjp
    def op(q, kf, gf, kb, gb, v):
        return _scan_fwd(q, kf, gf, kb, gb, v, sl, name)[:2]

    def fwd(q, kf, gf, kb, gb, v):
        of, ob, states = _scan_fwd(q, kf, gf, kb, gb, v, sl, name)
        return (of, ob), (q, kf, gf, kb, gb, v, states)

    def bwd(res, cts):
        return _scan_bwd(*res, cts[0], cts[1], sl, name)

    op.defvjp(fwd, bwd)
    return op


def loss_and_grad(y, target, lay):
    N, D = y.shape

    def body(y_ref, t_ref, dy_ref, l_ref):
        i = pl.program_id(0)

        @pl.when(i == 0)
        def _():
            l_ref[...] = jnp.zeros_like(l_ref)

        is_ctx = i % lay.per < lay.nc

        @pl.when(is_ctx)
        def _():
            dy_ref[...] = jnp.zeros_like(dy_ref)

        @pl.when(jnp.logical_not(is_ctx))
        def _():
            e = y_ref[...] - t_ref[...]
            dy_ref[...] = e * (1.0 / D)
            l_ref[...] += 0.5 * jnp.sum(jnp.sum(e * e, axis=1, keepdims=True) * (1.0 / D), axis=0, keepdims=True)

    def t_index(i):
        return ((i // lay.per) * lay.nl + jnp.maximum(i % lay.per - lay.nc, 0), 0)

    dy, lp = _pallas(
        body, name="loss_head", grid=(lay.n_tiles,),
        in_specs=[pl.BlockSpec((lay.tm, D), lambda i: (i, 0)), pl.BlockSpec((lay.tm, D), t_index)],
        out_specs=[pl.BlockSpec((lay.tm, D), lambda i: (i, 0)), pl.BlockSpec((8, LANES), lambda i: (0, 0))],
        out_shape=[jax.ShapeDtypeStruct((N, D), F32), jax.ShapeDtypeStruct((8, LANES), F32)],
        compiler_params=_cparams(("arbitrary",)),
    )(y, target)
    return lp, dy


def _adam_math(w, g, m, v):
    mn = ADAM_B1 * m + (1.0 - ADAM_B1) * g
    vn = ADAM_B2 * v + (1.0 - ADAM_B2) * jnp.square(g)
    m_hat = mn / (1.0 - ADAM_B1 ** ADAM_STEP)
    v_hat = vn / (1.0 - ADAM_B2 ** ADAM_STEP)
    return -ADAM_LR * (m_hat / (jnp.sqrt(v_hat) + ADAM_EPS) + ADAM_WD * w), mn, vn


ROW_TILES = (512, 256, 128, 64, 32, 16, 8)


def adamw(w, g, m, v, name):
    R, C = w.shape
    tr = _tile(R, ROW_TILES)

    def body(w_ref, g_ref, m_ref, v_ref, d_ref, mo_ref, vo_ref):
        d_ref[...], mo_ref[...], vo_ref[...] = _adam_math(w_ref[...], g_ref[...], m_ref[...], v_ref[...])

    spec = pl.BlockSpec((tr, C), lambda i: (i, 0))
    return _pallas(
        body, name=name, grid=(R // tr,), in_specs=[spec] * 4, out_specs=[spec] * 3,
        out_shape=[jax.ShapeDtypeStruct((R, C), F32)] * 3,
        compiler_params=_cparams(("parallel",)),
    )(w, g, m, v)


def sum_slots(recv, name):
    shape = recv.shape[1:]
    C = shape[-1]
    r3 = recv.reshape(N_DEV, -1, C)
    R = r3.shape[1]
    tr = _tile(R, ROW_TILES[1:])

    def body(r_ref, o_ref):
        g = r_ref[0].astype(F32)
        for k in range(1, N_DEV):
            g = g + r_ref[k].astype(F32)
        o_ref[...] = g

    out = _pallas(
        body, name=name, grid=(R // tr,),
        in_specs=[pl.BlockSpec((N_DEV, tr, C), lambda i: (0, i, 0))],
        out_specs=pl.BlockSpec((tr, C), lambda i: (i, 0)),
        out_shape=jax.ShapeDtypeStruct((R, C), F32),
        compiler_params=_cparams(("parallel",)),
    )(r3)
    return out.reshape(shape)


def adamw_slots(w, recv, m, v, name):
    R, C = w.shape
    tr = _tile(R, ROW_TILES[1:])

    def body(w_ref, r_ref, m_ref, v_ref, g_ref, d_ref, mo_ref, vo_ref):
        g = r_ref[0].astype(F32)
        for k in range(1, N_DEV):
            g = g + r_ref[k].astype(F32)
        g_ref[...] = g
        d_ref[...], mo_ref[...], vo_ref[...] = _adam_math(w_ref[...], g, m_ref[...], v_ref[...])

    spec = pl.BlockSpec((tr, C), lambda i: (i, 0))
    return _pallas(
        body, name=name, grid=(R // tr,),
        in_specs=[spec, pl.BlockSpec((N_DEV, tr, C), lambda i: (0, i, 0)), spec, spec], out_specs=[spec] * 4,
        out_shape=[jax.ShapeDtypeStruct((R, C), F32)] * 4,
        compiler_params=_cparams(("parallel",)),
    )(w, recv, m, v)


def _me():
    return lax.axis_index("x"), lax.axis_index("y"), lax.axis_index("c")


def _gather_many(x_refs, out_refs, send_sems, recv_sems, local_sems):
    x, y, c = _me()
    me, sibling = (x, y, c), (x, y, 1 - c)
    chips = [(1 - x, y), (x, 1 - y), (1 - x, 1 - y)]
    arrs = range(len(x_refs))

    def slot(a, px, py, pc):
        return out_refs[a].at[4 * px + 2 * py + pc]

    def copy(a, k, block, to, src=None):
        return pltpu.make_async_remote_copy(
            src_ref=slot(a, *block) if src is None else src, dst_ref=slot(a, *block),
            send_sem=send_sems.at[7 * a + k], recv_sem=recv_sems.at[7 * a + k], device_id=to, device_id_type=MESH)

    mine = [pltpu.make_async_copy(x_refs[a], slot(a, *me), local_sems.at[a]) for a in arrs]
    for cp in mine:
        cp.start()
    first = []
    for a in arrs:
        first.append(copy(a, 0, me, sibling, src=x_refs[a]))
        first += [copy(a, 1 + j, me, (*chip, c), src=x_refs[a]) for j, chip in enumerate(chips)]
    for cp in first:
        cp.start()
    passed = []
    for j, chip in enumerate(chips):
        for a in arrs:
            copy(a, 1 + j, (*chip, c), me).wait_recv()
            fwd = copy(a, 4 + j, (*chip, c), sibling)
            fwd.start()
            passed.append(fwd)
    for a in arrs:
        copy(a, 0, sibling, me).wait_recv()
    for j, chip in enumerate(chips):
        for a in arrs:
            copy(a, 4 + j, (*chip, 1 - c), me).wait_recv()
    for cp in first + passed:
        cp.wait_send()
    for cp in mine:
        cp.wait()


def _comm_scratch(n):
    return [pltpu.SemaphoreType.DMA((7 * n,)), pltpu.SemaphoreType.DMA((7 * n,)), pltpu.SemaphoreType.DMA((n,))]


def small_gather(xb, name):
    R = xb.shape[0]

    def body(x_ref, out_ref, sum_ref, send_sems, recv_sems, local_sems):
        _gather_many([x_ref], [out_ref], send_sems, recv_sems, local_sems)
        acc = out_ref[0]
        for k in range(1, N_DEV):
            acc = acc + out_ref[k]
        sum_ref[...] = acc

    vm = pl.BlockSpec(memory_space=pltpu.VMEM)
    return _pallas(
        body, name=name, in_specs=[vm], out_specs=[vm, vm],
        out_shape=[jax.ShapeDtypeStruct((N_DEV, R, LANES), xb.dtype), jax.ShapeDtypeStruct((R, LANES), xb.dtype)],
        scratch_shapes=_comm_scratch(1),
        compiler_params=pltpu.CompilerParams(vmem_limit_bytes=VMEM_LIMIT),
    )(xb)


def big_gather(xs, name):
    n = len(xs)

    def body(*refs):
        _gather_many(refs[:n], refs[n:2 * n], *refs[2 * n:])

    hbm = pl.BlockSpec(memory_space=pl.ANY)
    return _pallas(
        body, name=name, in_specs=[hbm] * n, out_specs=[hbm] * n,
        out_shape=[jax.ShapeDtypeStruct((N_DEV,) + a.shape, a.dtype) for a in xs],
        scratch_shapes=_comm_scratch(n),
    )(*xs)


def scatter_exchange(gs, name):
    n = len(gs)
    rels = [(dx, dy, dc) for dx in (0, 1) for dy in (0, 1) for dc in (0, 1) if (dx, dy, dc) != (0, 0, 0)]

    def body(*refs):
        g_refs, r_refs = refs[:n], refs[n:2 * n]
        send_sems, recv_sems, local_sems = refs[2 * n:]
        x, y, c = _me()
        me = 4 * x + 2 * y + c
        mine = [pltpu.make_async_copy(g_refs[a].at[me], r_refs[a].at[me], local_sems.at[a]) for a in range(n)]
        for cp in mine:
            cp.start()
        copies = []
        for r, (dx, dy, dc) in enumerate(rels):
            px, py, pc = (x + dx) % 2, (y + dy) % 2, (c + dc) % 2
            for a in range(n):
                copies.append(pltpu.make_async_remote_copy(
                    src_ref=g_refs[a].at[4 * px + 2 * py + pc], dst_ref=r_refs[a].at[me],
                    send_sem=send_sems.at[7 * a + r], recv_sem=recv_sems.at[7 * a + r],
                    device_id=(px, py, pc), device_id_type=MESH))
        for cp in copies:
            cp.start()
        for cp in copies:
            cp.wait()
        for cp in mine:
            cp.wait()

    hbm = pl.BlockSpec(memory_space=pl.ANY)
    return _pallas(
        body, name=name, in_specs=[hbm] * n, out_specs=[hbm] * n,
        out_shape=[jax.ShapeDtypeStruct(a.shape, a.dtype) for a in gs],
        scratch_shapes=_comm_scratch(n),
    )(*gs)


def _pack(arrs, dtype, row_mult):
    flat = jnp.concatenate([a.astype(dtype).reshape(-1) for a in arrs])
    pad = (-flat.shape[0]) % (LANES * row_mult)
    if pad:
        flat = jnp.concatenate([flat, jnp.zeros((pad,), dtype)])
    return flat.reshape(-1, LANES)


def _unpack(buf, shapes, lead=()):
    flat = buf.reshape(*lead, -1)
    out, off = [], 0
    for s in shapes:
        n = int(np.prod(s))
        out.append(flat[..., off:off + n].reshape(*lead, *s))
        off += n
    return out


def _rope_tables(T, tm):
    pos = np.arange(T)
    row, col = pos // GRID_W, pos % GRID_W

    def tab(rot_dim):
        nf = rot_dim // 4
        inv = ROPE_THETA ** (-np.arange(nf, dtype=np.float32) / nf)
        ang = np.concatenate([row[:, None].astype(np.float32) * inv, col[:, None].astype(np.float32) * inv], axis=-1)
        ang = ang.astype(np.float32)
        cos, sin = np.cos(ang), np.sin(ang)
        return np.concatenate([cos, cos], -1), np.concatenate([-sin, sin], -1)

    c64, s64 = tab(HD)
    c32, s32 = tab(B_ROPE)
    ca, sa = np.tile(c64, (1, A_HEADS)), np.tile(s64, (1, A_HEADS))
    cb = np.concatenate([c32, np.ones((T, LANES - B_ROPE), np.float32)], -1)
    sb = np.concatenate([s32, np.zeros((T, LANES - B_ROPE), np.float32)], -1)
    one, zero = np.ones((T, B_NOPE), np.float32), np.zeros((T, B_NOPE), np.float32)
    tail1, tail0 = np.ones((T, LANES - B_NOPE - B_ROPE), np.float32), np.zeros((T, LANES - B_NOPE - B_ROPE), np.float32)
    cq = np.tile(np.concatenate([one, c32, tail1], -1), (1, B_HEADS))
    sq = np.tile(np.concatenate([zero, s32, tail0], -1), (1, B_HEADS))

    def fin(a, ident):
        return jnp.asarray(np.concatenate([a, np.full((tm, a.shape[1]), ident, np.float32)], 0), F32)

    return fin(ca, 1.0), fin(sa, 0.0), fin(cb, 1.0), fin(sb, 0.0), fin(cq, 1.0), fin(sq, 0.0)


def _swap_matrix(width, starts, half):
    p = np.zeros((width, width), np.float32)
    for s in starts:
        for i in range(half):
            p[s + i, s + half + i] = 1.0
            p[s + half + i, s + i] = 1.0
    return jnp.asarray(p, BF16)


def _seg_matrix(width):
    h = np.arange(width) // HD
    return jnp.asarray((h[:, None] == h[None, :]).astype(np.float32), BF16)


def _key_slot_matrices():
    e1 = np.zeros((B_HEADS * B_NOPE, B_HEADS * LANES), np.float32)
    e2 = np.zeros((LANES, B_HEADS * LANES), np.float32)
    for h in range(B_HEADS):
        for i in range(B_NOPE):
            e1[h * B_NOPE + i, h * LANES + i] = 1.0
        for i in range(B_ROPE):
            e2[i, h * LANES + B_NOPE + i] = 1.0
    return jnp.asarray(e1, BF16), jnp.asarray(e2, BF16)


def _f_premod(x, sh, sc, g):
    return (_rms(x, g) * (1.0 + sc) + sh,)


def _f_post(x, y, gt, g):
    return (x + gt * _rms(y, g),)


def _f_post_pre(x, y, gt, g_post, sh, sc, g_pre):
    x1 = x + gt * _rms(y, g_post)
    return x1, _rms(x1, g_pre) * (1.0 + sc) + sh


def _f_bias(raw, b):
    return (raw + b,)


def _f_silu(x):
    return (_silu(x),)


def _f_readout(of, ob, gate, gain, seg):
    return (_head_rms(of + ob, seg, gain) * _silu(gate),)


def _f_bq(bq, cq, sq, pq):
    return (bq * cq + xdotr2(bq, pq) * sq,)


def _f_bk(bkn, bkr, e1, e2):
    return (xdotr1(bkn, e1) + xdotr1(bkr, e2),)


def _make_f_feat(layer):
    def f(feat, ca, sa, cb, sb, gaq, gak, gbq, gbkv, c00, c01, c10, c11, seg, pa, pb):
        aq = _head_rms(feat[:, 0:512], seg, gaq)
        ak = _head_rms(feat[:, 512:640], seg[0:128, 0:128], gak)
        av = feat[:, 640:768]
        aq = aq * ca + xdotr2(aq, pa) * sa
        ak = ak * ca[:, 0:128] + xdotr2(ak, pa[0:128, 0:128]) * sa[:, 0:128]
        bqn = _rms(feat[:, 768:1024], gbq, B_QR)
        bkvn = _rms(feat[:, 1024:1152], gbkv)
        bkr = feat[:, 1152:1280]
        bkr = bkr * cb + xdotr2(bkr, pb) * sb
        cq = _silu(feat[:, 1280:1536])
        zf, zb = feat[:, 1536:1792], feat[:, 1792:2048]
        if layer == 0:
            lbf = lbb = 0.0
        else:
            def share(c0, c1):
                m = jnp.maximum(c0, c1)
                e0, e1 = jnp.exp(c0 - m), jnp.exp(c1 - m)
                return e1 / (e0 + e1)
            lbf, lbb = share(c00, c10), share(c01, c11)

        def gate(z, lb):
            f_ = lb + (1.0 - lb) * _sigmoid(z)
            return (1.0 - lb) * _sigmoid(-z), jnp.log(jnp.maximum(f_, F_TINY))

        kf, gf = gate(zf, lbf)
        kb, gb = gate(zb, lbb)
        return aq, ak, av, bqn, bkvn, bkr, cq, kf, gf, kb, gb, feat[:, 2048:2304], feat[:, 2304:2560]

    return f


def _pad_w_in(w):
    z = lambda n: jnp.zeros((w.shape[0], n), w.dtype)
    return jnp.concatenate([w[:, 0:960], z(64), w[:, 960:1120], z(96), w[:, 1120:2400]], axis=1)


def _pad_w_q_up(w):
    w4 = w.reshape(B_QR, B_HEADS, B_NOPE + B_ROPE)
    w4 = jnp.pad(w4, ((0, 256 - B_QR), (0, 0), (0, LANES - B_NOPE - B_ROPE)))
    return w4.reshape(256, B_HEADS * LANES)


def _split_w_kv_up(w):
    w4 = w.reshape(B_KVR, B_HEADS, B_NOPE + B_V)
    return w4[:, :, :B_NOPE].reshape(B_KVR, -1), w4[:, :, B_NOPE:].reshape(B_KVR, -1)


def _tile_gain(g, reps, width=None):
    t = jnp.tile(g, reps)
    if width is not None and width > t.shape[0]:
        t = jnp.pad(t, (0, width - t.shape[0]))
    return t[None, :]


def local_forward(dims, p):
    B, T, Tc, D = dims
    tm = min(256, Tc)
    lay_all = Lay(B, T, Tc, tm)
    ca, sa, cb, sb, cq, sq = _rope_tables(T, tm)
    seg512, seg256 = _seg_matrix(512), _seg_matrix(C_W)
    pa = _swap_matrix(512, range(0, 512, HD), HD // 2)
    pb = _swap_matrix(LANES, [0], B_ROPE // 2)
    pq = _swap_matrix(512, [h * LANES + B_NOPE for h in range(B_HEADS)], B_ROPE // 2)
    e1, e2 = _key_slot_matrices()
    sl = ScanLay(B, T, Tc)
    al_a, al_b = AttnLay(B, T, Tc, min(128, Tc)), AttnLay(B, T, Tc, min(256, Tc))
    cfg_a = (A_KV, A_GROUP, HD, HD, HD ** -0.5, 512, 1024)
    cfg_b = (B_HEADS, 1, LANES, B_V, (B_NOPE + B_ROPE) ** -0.5, 512, T + Tc)

    tok = p["tok"]
    depth = p["modraw"].shape[0]
    mods = []
    for l in range(depth):
        bias_lay = Lay(1, 8, 0, 8)
        raw8 = jnp.pad(p["modraw"][l], ((0, 8 - B - 1), (0, 0)))
        mod = ew_op(_f_bias, bias_lay, ("tok", "par"), (True, True), ((6 * D, F32),), f"l{l}_ada_bias")(
            raw8, p["b_ada"][l][None, :])[0]
        seg_rows = jnp.concatenate([mod[0:B], jnp.broadcast_to(mod[B:B + 1], (B, 6 * D))], axis=0)[:, None, :]
        mods.append([seg_rows[:, :, i * D:(i + 1) * D] for i in range(6)])

    post_pre_kinds = ("tok", "tok", "seg", "par", "seg", "seg", "par")
    h = ew_op(_f_premod, lay_all, ("tok", "seg", "seg", "par"), (True,) * 4, ((D, BF16),), "l0_premix")(
        tok, mods[0][0], mods[0][1], p["g_pre_mix"][0][None, :])[0]
    for l in range(depth):
        tag = f"l{l}_"
        sh_m, sc_m, gt_m, sh_f, sc_f, gt_f = mods[l]

        feat = matmul_op(tag + "w_in")(h, _pad_w_in(p["w_in"][l]))
        clb = p["clb"]
        feats = ew_op(
            _make_f_feat(l), lay_all,
            ("tok", "pos", "pos", "pos", "pos") + ("par",) * 11,
            (True,) + (False,) * 4 + (True,) * 8 + (False,) * 3,
            ((512, F32), (128, BF16), (128, BF16), (256, BF16), (128, BF16), (128, F32)) + ((C_W, F32),) * 7,
            tag + "feat")(
            feat, ca, sa, cb, sb,
            _tile_gain(p["a_q_norm"][l], A_HEADS), _tile_gain(p["a_k_norm"][l], A_KV),
            _tile_gain(p["b_q_norm"][l], 1, 256), _tile_gain(p["b_kv_norm"][l], 1),
            clb[0, 0][None, :], clb[0, 1][None, :], clb[1, 0][None, :], clb[1, 1][None, :],
            seg512, pa, pb)
        aq, ak, av, bqn, bkvn, bkr, cqs, kf, gf, kb, gb, cv, cgate = feats
        bq = matmul_op(tag + "w_q_up")(bqn, _pad_w_q_up(p["w_q_up"][l]))
        bq = ew_op(_f_bq, lay_all, ("tok", "pos", "pos", "par"), (True, False, False, False), ((512, F32),),
                   tag + "bq_rope")(bq, cq, sq, pq)[0]
        w_kn, w_v = _split_w_kv_up(p["w_kv_up"][l])
        bkn = matmul_op(tag + "w_k_up")(bkvn, w_kn)
        bv = matmul_op(tag + "w_v_up", BF16)(bkvn, w_v)
        bk = ew_op(_f_bk, lay_all, ("tok", "tok", "par", "par"), (True, True, False, False),
                   ((B_HEADS * LANES, BF16),), tag + "bk_slots")(bkn, bkr, e1, e2)[0]

        if l == 0:
            ya, *got = attn_gather_op(cfg_a, al_a, tag + "attn_a")(aq, ak, av, *[p["shard_" + n] for n in LATE])
            late = {n: _assemble(n, g) for n, g in zip(LATE, got)}
        else:
            ya = attn_op(cfg_a, al_a, tag + "attn_a")(aq, ak, av)
        yb = attn_op(cfg_b, al_b, tag + "attn_b")(bq, bk, bv)
        of, ob = scan_op(sl, tag + "scan")(cqs, kf, gf, kb, gb, cv)
        lay_out = lay_all
        yc = ew_op(_f_readout, lay_out, ("tok", "tok", "tok", "par", "par"), (True, True, True, True, False),
                   ((C_W, BF16),), tag + "readout")(of, ob, cgate, _tile_gain(p["c_out_norm"][l], C_HEADS), seg256)[0]
        ycat = jnp.concatenate([ya, yb, yc], axis=1)
        mixo = matmul_op(tag + "w_out")(ycat, late["w_out"][l])
        tok, hf = ew_op(_f_post_pre, lay_out, post_pre_kinds, (True,) * 7, ((D, F32), (D, BF16)),
                        tag + "postmix_preffn")(
            tok, mixo, gt_m, p["g_post_mix"][l][None, :], sh_f, sc_f, p["g_pre_ffn"][l][None, :])

        z = matmul_op(tag + "w_ff1", BF16, relu2=True)(hf, late["w_ff1"][l])
        yf = matmul_op(tag + "w_ff2")(z, late["w_ff2"][l])
        if l + 1 < depth:
            tok, h = ew_op(_f_post_pre, lay_out, post_pre_kinds, (True,) * 7, ((D, F32), (D, BF16)),
                           tag + "postffn_premix")(
                tok, yf, gt_f, p["g_post_ffn"][l][None, :], mods[l + 1][0], mods[l + 1][1],
                p["g_pre_mix"][l + 1][None, :])
        else:
            tok = ew_op(_f_post, lay_out, ("tok", "tok", "seg", "par"), (True,) * 4, ((D, F32),), tag + "postffn")(
                tok, yf, gt_f, p["g_post_ffn"][l][None, :])[0]
    return tok


EARLY = ("w_in", "w_q_up", "w_kv_up")
LATE = ("w_out", "w_ff1", "w_ff2")
COL_SHARDED = ("w_in", "w_q_up", "w_kv_up", "w_ff1")
SMALL = ("c_ctx", "b_ada", "g_pre_mix", "g_post_mix", "g_pre_ffn", "g_post_ffn", "a_q_norm", "a_k_norm",
         "b_q_norm", "b_kv_norm", "c_out_norm")
WEIGHTS = ("c_ctx", "w_ada", "b_ada", "g_pre_mix", "g_post_mix", "g_pre_ffn", "g_post_ffn", "w_in", "a_q_norm",
           "a_k_norm", "b_q_norm", "w_q_up", "b_kv_norm", "w_kv_up", "c_lower_bounds", "c_out_norm", "w_out",
           "w_ff1", "w_ff2")
SMALL_ROWS = 64


def _assemble(name, a):
    if name in COL_SHARDED:
        return a.transpose(1, 2, 0, 3).reshape(a.shape[1], a.shape[2], N_DEV * a.shape[3])
    return a.transpose(1, 0, 2, 3).reshape(a.shape[1], N_DEV * a.shape[2], a.shape[3])


def kernel(x, c, ctx, c_ctx, w_ada, b_ada, g_pre_mix, g_post_mix, g_pre_ffn, g_post_ffn, w_in, a_q_norm, a_k_norm, b_q_norm, w_q_up, b_kv_norm, w_kv_up, c_lower_bounds, c_out_norm, w_out, w_ff1, w_ff2, loss_target, m_c_ctx, m_w_ada, m_b_ada, m_g_pre_mix, m_g_post_mix, m_g_pre_ffn, m_g_post_ffn, m_w_in, m_a_q_norm, m_a_k_norm, m_b_q_norm, m_w_q_up, m_b_kv_norm, m_w_kv_up, m_c_lower_bounds, m_c_out_norm, m_w_out, m_w_ff1, m_w_ff2, v_c_ctx, v_w_ada, v_b_ada, v_g_pre_mix, v_g_post_mix, v_g_pre_ffn, v_g_post_ffn, v_w_in, v_a_q_norm, v_a_k_norm, v_b_q_norm, v_w_q_up, v_b_kv_norm, v_w_kv_up, v_c_lower_bounds, v_c_out_norm, v_w_out, v_w_ff1, v_w_ff2):
    W = dict(c_ctx=c_ctx, w_ada=w_ada, b_ada=b_ada, g_pre_mix=g_pre_mix, g_post_mix=g_post_mix, g_pre_ffn=g_pre_ffn,
             g_post_ffn=g_post_ffn, w_in=w_in, a_q_norm=a_q_norm, a_k_norm=a_k_norm, b_q_norm=b_q_norm,
             w_q_up=w_q_up, b_kv_norm=b_kv_norm, w_kv_up=w_kv_up, c_lower_bounds=c_lower_bounds,
             c_out_norm=c_out_norm, w_out=w_out, w_ff1=w_ff1, w_ff2=w_ff2)
    M = dict(c_ctx=m_c_ctx, w_ada=m_w_ada, b_ada=m_b_ada, g_pre_mix=m_g_pre_mix, g_post_mix=m_g_post_mix,
             g_pre_ffn=m_g_pre_ffn, g_post_ffn=m_g_post_ffn, w_in=m_w_in, a_q_norm=m_a_q_norm, a_k_norm=m_a_k_norm,
             b_q_norm=m_b_q_norm, w_q_up=m_w_q_up, b_kv_norm=m_b_kv_norm, w_kv_up=m_w_kv_up,
             c_lower_bounds=m_c_lower_bounds, c_out_norm=m_c_out_norm, w_out=m_w_out, w_ff1=m_w_ff1, w_ff2=m_w_ff2)
    V = dict(c_ctx=v_c_ctx, w_ada=v_w_ada, b_ada=v_b_ada, g_pre_mix=v_g_pre_mix, g_post_mix=v_g_post_mix,
             g_pre_ffn=v_g_pre_ffn, g_post_ffn=v_g_post_ffn, w_in=v_w_in, a_q_norm=v_a_q_norm, a_k_norm=v_a_k_norm,
             b_q_norm=v_b_q_norm, w_q_up=v_w_q_up, b_kv_norm=v_b_kv_norm, w_kv_up=v_w_kv_up,
             c_lower_bounds=v_c_lower_bounds, c_out_norm=v_c_out_norm, w_out=v_w_out, w_ff1=v_w_ff1, w_ff2=v_w_ff2)

    B, T, D = x.shape
    Tc = ctx.shape[1]
    depth = w_ada.shape[0]
    ada_cols = w_ada.shape[2]
    idx = 4 * lax.axis_index("x") + 2 * lax.axis_index("y") + lax.axis_index("c")
    n_cond = N_DEV * B
    cond_rows = -(-(n_cond + 1) // 8) * 8

    clb_cols = c_lower_bounds.shape[2]
    g1, _ = small_gather(_pack([c, c_lower_bounds], F32, 8), "gather_cond")
    c_parts, clb_parts = _unpack(g1, [c.shape, c_lower_bounds.shape], lead=(N_DEV,))
    c_all = c_parts.reshape(n_cond, D)
    clb_full = clb_parts.transpose(1, 2, 0, 3).reshape(depth, 2, N_DEV * clb_cols)

    cond_lay = Lay(1, cond_rows, 0, cond_rows)

    def ada_shard(c_ctx_, w_ada_):
        cond = jnp.concatenate([c_all, c_ctx_[None, :], jnp.zeros((cond_rows - n_cond - 1, D), F32)], axis=0)
        sc = ew_op(_f_silu, cond_lay, ("tok",), (True,), ((D, F32),), "cond_silu")(cond)[0]
        return jnp.stack([matmul_op(f"l{l}_w_ada")(sc, w_ada_[l]) for l in range(depth)])

    mod_shard, vjp_ada = jax.vjp(ada_shard, c_ctx, w_ada)

    g2, _ = small_gather(_pack([mod_shard], F32, 8), "gather_mod")
    mod_all = _unpack(g2, [mod_shard.shape], lead=(N_DEV,))[0]
    mod_all = mod_all.transpose(1, 2, 0, 3).reshape(depth, cond_rows, N_DEV * ada_cols)
    mine = lax.dynamic_slice_in_dim(mod_all, idx * B, B, axis=1)
    modraw = jnp.concatenate([mine, mod_all[:, n_cond:n_cond + 1]], axis=1)

    gathered = dict(zip(EARLY, big_gather([W[n].astype(BF16) for n in EARLY], "gather_weights")))

    dims = (B, T, Tc, D)
    small_names = [n for n in SMALL if n != "c_ctx"]
    small_in = {n: W[n] for n in small_names}
    late_in = {n: W[n] for n in LATE}

    def fwd(x_, modraw_, small_, clb_, gathered_, late_):
        p = dict(small_)
        p.update({n: _assemble(n, a) for n, a in gathered_.items()})
        p.update({"shard_" + n: a for n, a in late_.items()})
        p.update(tok=jnp.concatenate([ctx, x_], axis=1).reshape(B * (Tc + T), D), modraw=modraw_, clb=clb_)
        return local_forward(dims, p)

    y, vjp_main = jax.vjp(fwd, x, modraw, small_in, clb_full, gathered, late_in)
    loss_part, dy = loss_and_grad(y, loss_target.reshape(B * T, D), Lay(B, T, Tc, min(256, Tc)))
    dx, dmodraw, dsmall, dclb, dgathered, dlate = vjp_main(dy)

    pay3 = _pack([dmodraw] + [dsmall[n] for n in small_names] + [dclb, loss_part[0, 0:1]], F32, 8)
    g3, s3 = small_gather(pay3, "gather_small_grads")
    dmod_parts = _unpack(g3, [dmodraw.shape], lead=(N_DEV,))[0]
    tot = _unpack(s3, [dmodraw.shape] + [W[n].shape for n in small_names] + [clb_full.shape, (1,)])
    dmod_tot, small_tot, dclb_tot, loss = tot[0], dict(zip(small_names, tot[1:-2])), tot[-2], tot[-1]
    drows = dmod_parts[:, :, 0:B].transpose(1, 0, 2, 3).reshape(depth, n_cond, N_DEV * ada_cols)
    dcond = jnp.concatenate(
        [drows, dmod_tot[:, B:B + 1], jnp.zeros((depth, cond_rows - n_cond - 1, N_DEV * ada_cols), F32)], axis=1)
    dmod_shard = lax.dynamic_slice_in_dim(dcond, idx * ada_cols, ada_cols, axis=2)
    dc_ctx_part, dw_ada = vjp_ada(dmod_shard)

    _, s4 = small_gather(_pack([dc_ctx_part], F32, 8), "gather_c_ctx_grad")
    small_tot["c_ctx"] = _unpack(s4, [c_ctx.shape])[0]

    recv = dict(zip(EARLY, scatter_exchange([dgathered[n] for n in EARLY], "scatter_grads")))
    grads, delta, new_m, new_v = dict(small_tot), {}, {}, {}
    for n in EARLY:
        shape, cols = W[n].shape, W[n].shape[-1]
        flat = lambda a: a.reshape(-1, cols)
        res = adamw_slots(flat(W[n]), recv[n].reshape(N_DEV, -1, cols), flat(M[n]), flat(V[n]), "adamw_" + n)
        grads[n], delta[n], new_m[n], new_v[n] = (a.reshape(shape) for a in res)
    for n in LATE:
        shape, cols = W[n].shape, W[n].shape[-1]
        flat = lambda a: a.reshape(-1, cols)
        grads[n] = dlate[n]
        res = adamw(flat(W[n]), flat(dlate[n]), flat(M[n]), flat(V[n]), "adamw_" + n)
        delta[n], new_m[n], new_v[n] = (a.reshape(shape) for a in res)
    grads["w_ada"] = dw_ada
    grads["c_lower_bounds"] = lax.dynamic_slice_in_dim(dclb_tot, idx * clb_cols, clb_cols, axis=2)

    flat_a = lambda a: a.reshape(-1, ada_cols)
    res = adamw(flat_a(w_ada), flat_a(dw_ada), flat_a(m_w_ada), flat_a(v_w_ada), "adamw_w_ada")
    delta["w_ada"], new_m["w_ada"], new_v["w_ada"] = (a.reshape(w_ada.shape) for a in res)
    names = list(SMALL) + ["c_lower_bounds"]
    shapes = [W[n].shape for n in names]
    res = adamw(*[_pack([src[n] for n in names], F32, SMALL_ROWS) for src in (W, grads, M, V)], "adamw_small")
    for dst, buf in zip((delta, new_m, new_v), res):
        dst.update(zip(names, _unpack(buf, shapes)))

    return (loss.reshape(()), dx, *[grads[n] for n in WEIGHTS], *[delta[n] for n in WEIGHTS],
            *[new_m[n] for n in WEIGHTS], *[new_v[n] for n in WEIGHTS])
```

```python
import math

import numpy as np

import jax
import jax.numpy as jnp
from jax import lax
from jax.experimental import pallas as pl
from jax.experimental.pallas import tpu as pltpu

F32 = jnp.float32
BF16 = jnp.bfloat16

A_HEADS, A_KV, HD = 8, 2, 64
A_GROUP = A_HEADS // A_KV
B_HEADS, B_QR, B_KVR, B_NOPE, B_ROPE, B_V = 4, 192, 128, 64, 32, 64
C_HEADS, C_DK = 4, 64
C_W = C_HEADS * C_DK
GRID_W = 64
CHUNK = 64
ROPE_THETA = 10000.0
EPS = 1e-6
F_TINY = 1e-30
D_IN = 2400
D_IN_PAD = 2560
N_DEV = 8
LANES = 128
NEG = -1e30

ADAM_LR, ADAM_B1, ADAM_B2, ADAM_EPS, ADAM_WD, ADAM_STEP = 0.001, 0.9, 0.999, 1e-08, 0.01, 10

VMEM_LIMIT = 56 * 1024 * 1024
MESH = pl.DeviceIdType.MESH


def _pallas(body, **kw):
    return pl.pallas_call(body, **kw)


def _cparams(sem):
    return pltpu.CompilerParams(dimension_semantics=sem, vmem_limit_bytes=VMEM_LIMIT)


def _split3(x):
    hi = x.astype(BF16)
    r = x - hi.astype(F32)
    mid = r.astype(BF16)
    lo = (r - mid.astype(F32)).astype(BF16)
    return hi, mid, lo


def _nn(a, b):
    return jnp.dot(a, b, preferred_element_type=F32)


def _nt(a, b):
    return lax.dot_general(a, b, (((1,), (1,)), ((), ())), preferred_element_type=F32)


def _tn(a, b):
    return lax.dot_general(a, b, (((0,), (0,)), ((), ())), preferred_element_type=F32)


def _make_xdotr(pieces):
    @jax.custom_vjp
    def op(x, m):
        return sum(_nn(p, m) for p in _split3(x)[:pieces])

    def fwd(x, m):
        return op(x, m), m

    def bwd(m, ct):
        return sum(_nt(p, m) for p in _split3(ct)[:pieces]), None

    op.defvjp(fwd, bwd)
    return op


xdotr, xdotr2, xdotr1 = _make_xdotr(3), _make_xdotr(2), _make_xdotr(1)


@jax.custom_vjp
def xdotl(m, mt, x):
    return sum(_nn(m, p) for p in _split3(x)[:2])


def _xdotl_fwd(m, mt, x):
    return xdotl(m, mt, x), (m, mt)


def _xdotl_bwd(res, ct):
    m, mt = res
    return None, None, sum(_nn(mt, p) for p in _split3(ct)[:2])


xdotl.defvjp(_xdotl_fwd, _xdotl_bwd)


def _sigmoid(x):
    return 1.0 / (1.0 + jnp.exp(-x))


def _silu(x):
    return x * _sigmoid(x)


def _rms(x, gain, n=None):
    n = x.shape[-1] if n is None else n
    ms = jnp.sum(x * x, axis=-1, keepdims=True) * (1.0 / n)
    return x * lax.rsqrt(ms + EPS) * gain


def _head_rms(x, seg, gain):
    ms = xdotr2(x * x, seg) * (1.0 / HD)
    return x * lax.rsqrt(ms + EPS) * gain


class Lay:
    def __init__(self, B, T, Tc, tm):
        self.B, self.T, self.Tc, self.tm = B, T, Tc, tm
        self.nl, self.nc = T // tm, Tc // tm
        self.per = self.nl + self.nc
        self.n_tiles = B * self.per
        self.n_seg = 2 * B
        self.rows = self.n_tiles * tm

    def seg(self, i):
        b, w = i // self.per, i % self.per
        return jnp.where(w < self.nc, self.B + b, b)

    def pos(self, i):
        w = i % self.per
        return jnp.where(w < self.nc, self.nl, w - self.nc)

    def first(self, i):
        w = i % self.per
        return jnp.logical_or(w == 0, w == self.nc)


def _ew_spec(kind, a, lay):
    if kind == "tok":
        return pl.BlockSpec((lay.tm, a.shape[1]), lambda i: (i, 0))
    if kind == "seg":
        return pl.BlockSpec((1, 1, a.shape[2]), lambda i: (lay.seg(i), 0, 0))
    if kind == "pos":
        return pl.BlockSpec((lay.tm, a.shape[1]), lambda i: (lay.pos(i), 0))
    return pl.BlockSpec(a.shape, lambda i: (0,) * a.ndim)


def _ew_load(ref, kind):
    if kind == "seg":
        return ref[0]
    if kind == "tok":
        return ref[...].astype(F32)
    return ref[...]


def _ew_fwd(f, lay, kinds, arrays, outs, name):
    n_in = len(arrays)

    def body(*refs):
        vals = [_ew_load(r, k) for r, k in zip(refs[:n_in], kinds)]
        res = f(*vals)
        for r, o in zip(res, refs[n_in:]):
            o[...] = r.astype(o.dtype)

    return _pallas(
        body, name=name, grid=(lay.n_tiles,),
        in_specs=[_ew_spec(k, a, lay) for k, a in zip(kinds, arrays)],
        out_specs=[pl.BlockSpec((lay.tm, c), lambda i: (i, 0)) for c, _ in outs],
        out_shape=[jax.ShapeDtypeStruct((lay.rows, c), dt) for c, dt in outs],
        compiler_params=_cparams(("parallel",)),
    )(*arrays)


def _ew_bwd(f, lay, kinds, diffs, arrays, cts, name):
    n_in, n_ct = len(arrays), len(cts)
    d_idx = [i for i, d in enumerate(diffs) if d]

    g_shapes, g_specs = [], []
    for i in d_idx:
        a, k = arrays[i], kinds[i]
        if k == "tok":
            g_shapes.append(jax.ShapeDtypeStruct((lay.rows, a.shape[1]), a.dtype))
            g_specs.append(pl.BlockSpec((lay.tm, a.shape[1]), lambda t: (t, 0)))
        elif k == "seg":
            g_shapes.append(jax.ShapeDtypeStruct((lay.n_seg, 1, a.shape[2]), F32))
            g_specs.append(pl.BlockSpec((1, 1, a.shape[2]), lambda t: (lay.seg(t), 0, 0)))
        else:
            g_shapes.append(jax.ShapeDtypeStruct(a.shape, F32))
            g_specs.append(pl.BlockSpec(a.shape, lambda t, nd=a.ndim: (0,) * nd))

    def body(*refs):
        vals = [_ew_load(r, k) for r, k in zip(refs[:n_in], kinds)]
        cvals = tuple(r[...].astype(F32) for r in refs[n_in:n_in + n_ct])
        g_refs = refs[n_in + n_ct:]

        def g(*dv):
            full = list(vals)
            for j, i in enumerate(d_idx):
                full[i] = dv[j]
            return tuple(o.astype(F32) for o in f(*full))

        _, vjp = jax.vjp(g, *[vals[i] for i in d_idx])
        grads = vjp(cvals)
        t = pl.program_id(0)
        for gref, grad, i in zip(g_refs, grads, d_idx):
            k = kinds[i]
            if k == "tok":
                gref[...] = grad.astype(gref.dtype)
            elif k == "seg":
                @pl.when(lay.first(t))
                def _():
                    gref[...] = jnp.zeros_like(gref)

                gref[0] += grad
            else:
                @pl.when(t == 0)
                def _():
                    gref[...] = jnp.zeros_like(gref)

                gref[...] += grad

    res = _pallas(
        body, name=name + "_bwd", grid=(lay.n_tiles,),
        in_specs=[_ew_spec(k, a, lay) for k, a in zip(kinds, arrays)]
        + [pl.BlockSpec((lay.tm, c.shape[1]), lambda i: (i, 0)) for c in cts],
        out_specs=g_specs, out_shape=g_shapes,
        compiler_params=_cparams(("arbitrary",)),
    )(*arrays, *cts)
    out = [None] * n_in
    for gr, i in zip(res, d_idx):
        a = arrays[i]
        if gr.shape != a.shape:
            pad = [(0, a.shape[0] - gr.shape[0])] + [(0, 0)] * (a.ndim - 1)
            gr = jnp.pad(gr, pad)
        out[i] = gr
    return tuple(out)


def ew_op(f, lay, kinds, diffs, outs, name):
    kinds, diffs, outs = tuple(kinds), tuple(diffs), tuple(outs)

    @jax.custom_vjp
    def op(*arrays):
        return tuple(_ew_fwd(f, lay, kinds, arrays, outs, name))

    def fwd(*arrays):
        return op(*arrays), arrays

    def bwd(arrays, cts):
        return _ew_bwd(f, lay, kinds, diffs, arrays, tuple(cts), name)

    op.defvjp(fwd, bwd)
    return op


def _tile(n, cands):
    for c in cands:
        if n % c == 0:
            return c
    return n


TN_ROW_TILES = (2176, 1024, 512, 256)
WEIGHT_TILE_ELEMS = 4 * 1024 * 1024


def _wide_tile(n, depth):
    for c in (2048, 1280, 1024, 512, 256, 128):
        if n % c == 0 and c * depth <= WEIGHT_TILE_ELEMS:
            return c
    return n


def _mm_nn(x, w, name, out_dtype, relu2):
    M, K = x.shape
    N = w.shape[1]
    tm, tn = _tile(M, (512, 256)), _wide_tile(N, K)

    def body(x_ref, w_ref, o_ref):
        acc = _nn(x_ref[...].astype(BF16), w_ref[...].astype(BF16))
        if relu2:
            acc = jnp.square(jnp.maximum(acc, 0.0))
        o_ref[...] = acc.astype(o_ref.dtype)

    return _pallas(
        body, name=name, grid=(N // tn, M // tm),
        in_specs=[pl.BlockSpec((tm, K), lambda j, i: (i, 0)), pl.BlockSpec((K, tn), lambda j, i: (0, j))],
        out_specs=pl.BlockSpec((tm, tn), lambda j, i: (i, j)),
        out_shape=jax.ShapeDtypeStruct((M, N), out_dtype),
        compiler_params=_cparams(("parallel", "parallel")),
    )(x, w)


def _through_relu2(dz_ref, z_ref):
    if z_ref is None:
        return dz_ref[...].astype(BF16)
    z = z_ref[...].astype(F32)
    root = z * lax.rsqrt(jnp.maximum(z, F_TINY))
    return (dz_ref[...].astype(F32) * (2.0 * root)).astype(BF16)


def _mm_nt(dy, w, name, out_dtype, z=None):
    M, N = dy.shape
    K = w.shape[0]
    tm = _tile(M, (256,)) if z is not None else _tile(M, (512, 256))
    tk = K if z is not None else _wide_tile(K, N)
    row = pl.BlockSpec((tm, N), lambda j, i: (i, 0))

    def body(*refs):
        dy_ref, z_ref = (refs[0], refs[1]) if z is not None else (refs[0], None)
        w_ref, o_ref = refs[-2], refs[-1]
        o_ref[...] = _nt(_through_relu2(dy_ref, z_ref), w_ref[...].astype(BF16)).astype(o_ref.dtype)

    return _pallas(
        body, name=name, grid=(K // tk, M // tm),
        in_specs=[row] * (2 if z is not None else 1) + [pl.BlockSpec((tk, N), lambda j, i: (j, 0))],
        out_specs=pl.BlockSpec((tm, tk), lambda j, i: (i, j)),
        out_shape=jax.ShapeDtypeStruct((M, K), out_dtype),
        compiler_params=_cparams(("parallel", "parallel")),
    )(*((dy, z, w) if z is not None else (dy, w)))


def _mm_tn(x, dy, name, out_dtype, z=None):
    M, K = x.shape
    N = dy.shape[1]
    tm = _tile(M, TN_ROW_TILES)
    tk, tn = _tile(K, (1024, 512, 256, 128)), _tile(N, (1024, 512, 256, 128))
    n_m = M // tm
    col = pl.BlockSpec((tm, tn), lambda a, b, m: (m, b))

    def body(*refs):
        x_ref = refs[0]
        dy_ref, z_ref = (refs[1], refs[2]) if z is not None else (refs[1], None)
        o_ref, acc_ref = refs[-2], refs[-1]
        m = pl.program_id(2)

        @pl.when(m == 0)
        def _():
            acc_ref[...] = jnp.zeros_like(acc_ref)

        acc_ref[...] += _tn(x_ref[...].astype(BF16), _through_relu2(dy_ref, z_ref))

        @pl.when(m == n_m - 1)
        def _():
            o_ref[...] = acc_ref[...].astype(o_ref.dtype)

    return _pallas(
        body, name=name, grid=(K // tk, N // tn, n_m),
        in_specs=[pl.BlockSpec((tm, tk), lambda a, b, m: (m, a))] + [col] * (2 if z is not None else 1),
        out_specs=pl.BlockSpec((tk, tn), lambda a, b, m: (a, b)),
        out_shape=jax.ShapeDtypeStruct((K, N), out_dtype),
        scratch_shapes=[pltpu.VMEM((tk, tn), F32)],
        compiler_params=_cparams(("parallel", "parallel", "arbitrary")),
    )(*((x, dy, z) if z is not None else (x, dy)))


def matmul_op(name, out_dtype=F32, relu2=False):
    @jax.custom_vjp
    def op(x, w):
        return _mm_nn(x, w, name, out_dtype, relu2)

    def fwd(x, w):
        y = op(x, w)
        return y, (x, w, y if relu2 else None)

    def bwd(res, dy):
        x, w, z = res
        return _mm_nt(dy, w, name + "_dx", x.dtype, z), _mm_tn(x, dy, name + "_dw", w.dtype, z)

    op.defvjp(fwd, bwd)
    return op


LOG2E = math.log2(math.e)


class AttnLay:
    def __init__(self, B, T, Tc, tq):
        self.B, self.T, self.Tc, self.tq = B, T, Tc, tq
        self.S = T + Tc
        self.nq, self.nqc = self.S // tq, Tc // tq


def _attn_specs(al):
    qs = lambda w: pl.BlockSpec((al.tq, w), lambda b, i: (b * al.nq + i, 0))
    ks = lambda w: pl.BlockSpec((al.S, w), lambda b, i: (b, 0))
    return qs, ks


def _lane_fold(acc, x, op):
    t = x[:, 0:LANES]
    for j in range(1, x.shape[1] // LANES):
        t = op(t, x[:, j * LANES:(j + 1) * LANES])
    return op(acc, t)


def _key_chunks(n, kc):
    return [(c0, min(kc, n - c0)) for c0 in range(0, n, kc)]


def _stack(ref, g, group, width, tq):
    parts = [ref[:, (g * group + j) * width:(g * group + j + 1) * width].astype(F32) for j in range(group)]
    return parts[0] if group == 1 else jnp.concatenate(parts, axis=0)


PEER_RELATIONS = [(dx, dy, dc) for dx in (0, 1) for dy in (0, 1) for dc in (0, 1) if (dx, dy, dc) != (0, 0, 0)]


def _exchange_behind(al, src_refs, dst_refs, sems, gather):
    send_sems, recv_sems, local_sems = sems
    x, y, c = _me()
    me = 4 * x + 2 * y + c

    def copies():
        out = []
        for a, (s, d) in enumerate(zip(src_refs, dst_refs)):
            out.append(pltpu.make_async_copy(s if gather else s.at[me], d.at[me], local_sems.at[a]))
            for r, (dx, dy, dc) in enumerate(PEER_RELATIONS):
                px, py, pc = (x + dx) % 2, (y + dy) % 2, (c + dc) % 2
                out.append(pltpu.make_async_remote_copy(
                    src_ref=s if gather else s.at[4 * px + 2 * py + pc], dst_ref=d.at[me],
                    send_sem=send_sems.at[7 * a + r], recv_sem=recv_sems.at[7 * a + r],
                    device_id=(px, py, pc), device_id_type=MESH))
        return out

    b, i = pl.program_id(0), pl.program_id(1)

    @pl.when(jnp.logical_and(b == 0, i == 0))
    def _():
        for cp in copies():
            cp.start()

    @pl.when(jnp.logical_and(b == al.B - 1, i == al.nq - 1))
    def _():
        for cp in copies():
            cp.wait()


def _attn_fwd(q, k, v, cfg, al, name, shards=()):
    n_kv, group, dq, dv, scale, kc, _ = cfg
    tq = al.tq
    rows = group * tq
    wq, wk, wv, wo = q.shape[1], k.shape[1], v.shape[1], n_kv * group * dv
    qs, ks = _attn_specs(al)
    n = len(shards)

    def body(*refs):
        q_ref, k_ref, v_ref = refs[:3]
        x_refs = refs[3:3 + n]
        o_ref, lse_ref = refs[3 + n:5 + n]
        g_refs = refs[5 + n:5 + 2 * n]
        s_scr = refs[5 + 2 * n]
        if n:
            _exchange_behind(al, x_refs, g_refs, refs[6 + 2 * n:], gather=True)
        lane = lax.broadcasted_iota(jnp.int32, (tq, LANES), 1)

        def run(n_keys):
            chunks = _key_chunks(n_keys, kc)
            lse_all = jnp.zeros((tq, LANES), F32)
            for g in range(n_kv):
                q4 = (_stack(q_ref, g, group, dq, tq) * (scale * LOG2E)).astype(BF16)
                ksl, vsl = slice(g * dq, (g + 1) * dq), slice(g * dv, (g + 1) * dv)
                m_part = jnp.full((rows, LANES), -jnp.inf, F32)
                for c0, w in chunks:
                    s = _nt(q4, k_ref[c0:c0 + w, ksl])
                    s_scr[:, c0:c0 + w] = s
                    m_part = _lane_fold(m_part, s, jnp.maximum)
                m = jnp.max(m_part, axis=1, keepdims=True)
                l_part = jnp.zeros((rows, LANES), F32)
                acc = jnp.zeros((rows, dv), F32)
                for c0, w in chunks:
                    p = jnp.exp2(s_scr[:, c0:c0 + w] - m)
                    l_part = _lane_fold(l_part, p, jnp.add)
                    acc = acc + _nn(p.astype(BF16), v_ref[c0:c0 + w, vsl])
                l = jnp.sum(l_part, axis=1, keepdims=True)
                o = acc / l
                lse = m + jnp.log2(l)
                for j in range(group):
                    h = g * group + j
                    o_ref[:, h * dv:(h + 1) * dv] = o[j * tq:(j + 1) * tq].astype(o_ref.dtype)
                    lse_all = jnp.where(lane == h, lse[j * tq:(j + 1) * tq], lse_all)
            lse_ref[...] = lse_all

        is_ctx = pl.program_id(1) < al.nqc

        @pl.when(is_ctx)
        def _():
            run(al.Tc)

        @pl.when(jnp.logical_not(is_ctx))
        def _():
            run(al.S)

    hbm = pl.BlockSpec(memory_space=pl.ANY)
    res = _pallas(
        body, name=name, grid=(al.B, al.nq),
        in_specs=[qs(wq), ks(wk), ks(wv)] + [hbm] * n,
        out_specs=[qs(wo), qs(LANES)] + [hbm] * n,
        out_shape=[jax.ShapeDtypeStruct((q.shape[0], wo), BF16), jax.ShapeDtypeStruct((q.shape[0], LANES), F32)]
        + [jax.ShapeDtypeStruct((N_DEV,) + a.shape, a.dtype) for a in shards],
        scratch_shapes=[pltpu.VMEM((rows, al.S), F32)] + (_comm_scratch(n) if n else []),
        compiler_params=_cparams(("arbitrary", "arbitrary") if n else ("parallel", "parallel")),
    )(q, k, v, *shards)
    return res[0], res[1], list(res[2:])


def _attn_bwd(q, k, v, o, lse, do, cfg, al, name, partials=()):
    n_kv, group, dq, dv, scale, _, kc = cfg
    tq = al.tq
    rows = group * tq
    wq, wk, wv, wo = q.shape[1], k.shape[1], v.shape[1], n_kv * group * dv
    qs, ks = _attn_specs(al)
    n = len(partials)

    def body(*refs):
        q_ref, k_ref, v_ref, o_ref, lse_ref, do_ref = refs[:6]
        p_refs = refs[6:6 + n]
        dq_ref, dk_ref, dv_ref = refs[6 + n:9 + n]
        r_refs = refs[9 + n:9 + 2 * n]
        ak, av = refs[9 + 2 * n:11 + 2 * n]
        if n:
            _exchange_behind(al, p_refs, r_refs, refs[11 + 2 * n:], gather=False)
        i = pl.program_id(1)

        @pl.when(i == 0)
        def _():
            ak[...] = jnp.zeros_like(ak)
            av[...] = jnp.zeros_like(av)

        lane = lax.broadcasted_iota(jnp.int32, (tq, LANES), 1)

        def run(n_keys):
            lse_tile = lse_ref[...]
            for g in range(n_kv):
                qf = _stack(q_ref, g, group, dq, tq)
                q4l = (qf * (scale * LOG2E)).astype(BF16)
                q4s = (qf * scale).astype(BF16)
                do4 = _stack(do_ref, g, group, dv, tq)
                o4 = _stack(o_ref, g, group, dv, tq)
                cols = [jnp.sum(jnp.where(lane == g * group + j, lse_tile, 0.0), axis=1, keepdims=True)
                        for j in range(group)]
                lse4 = cols[0] if group == 1 else jnp.concatenate(cols, axis=0)
                dl = jnp.sum(do4 * o4, axis=1, keepdims=True)
                dob = do4.astype(BF16)
                ksl, vsl = slice(g * dq, (g + 1) * dq), slice(g * dv, (g + 1) * dv)
                dq4 = jnp.zeros((rows, dq), F32)
                for c0, w in _key_chunks(n_keys, kc):
                    kk = k_ref[c0:c0 + w, ksl]
                    p = jnp.exp2(_nt(q4l, kk) - lse4)
                    dp = _nt(dob, v_ref[c0:c0 + w, vsl])
                    ds = (p * (dp - dl)).astype(BF16)
                    dq4 = dq4 + _nn(ds, kk)
                    ak[c0:c0 + w, ksl] += _tn(ds, q4s)
                    av[c0:c0 + w, vsl] += _tn(p.astype(BF16), dob)
                dq4 = dq4 * scale
                for j in range(group):
                    h = g * group + j
                    dq_ref[:, h * dq:(h + 1) * dq] = dq4[j * tq:(j + 1) * tq]

        is_ctx = i < al.nqc

        @pl.when(is_ctx)
        def _():
            run(al.Tc)

        @pl.when(jnp.logical_not(is_ctx))
        def _():
            run(al.S)

        @pl.when(i == al.nq - 1)
        def _():
            dk_ref[...] = ak[...].astype(dk_ref.dtype)
            dv_ref[...] = av[...].astype(dv_ref.dtype)

    hbm = pl.BlockSpec(memory_space=pl.ANY)
    res = _pallas(
        body, name=name + "_bwd", grid=(al.B, al.nq),
        in_specs=[qs(wq), ks(wk), ks(wv), qs(wo), qs(LANES), qs(wo)] + [hbm] * n,
        out_specs=[qs(wq), ks(wk), ks(wv)] + [hbm] * n,
        out_shape=[jax.ShapeDtypeStruct(q.shape, F32), jax.ShapeDtypeStruct(k.shape, k.dtype),
                   jax.ShapeDtypeStruct(v.shape, v.dtype)]
        + [jax.ShapeDtypeStruct(a.shape, a.dtype) for a in partials],
        scratch_shapes=[pltpu.VMEM((al.S, wk), F32), pltpu.VMEM((al.S, wv), F32)] + (_comm_scratch(n) if n else []),
        compiler_params=_cparams(("arbitrary", "arbitrary") if n else ("parallel", "arbitrary")),
    )(q, k, v, o, lse, do, *partials)
    return res[0], res[1], res[2], list(res[3:])


def attn_op(cfg, al, name):
    @jax.custom_vjp
    def op(q, k, v):
        return _attn_fwd(q, k, v, cfg, al, name)[0]

    def fwd(q, k, v):
        o, lse, _ = _attn_fwd(q, k, v, cfg, al, name)
        return o, (q, k, v, o, lse)

    def bwd(res, do):
        return _attn_bwd(*res, do, cfg, al, name)[:3]

    op.defvjp(fwd, bwd)
    return op


def attn_gather_op(cfg, al, name):
    @jax.custom_vjp
    def op(q, k, v, *shards):
        o, _, gathered = _attn_fwd(q, k, v, cfg, al, name, [s.astype(BF16) for s in shards])
        return (o, *gathered)

    def fwd(q, k, v, *shards):
        o, lse, gathered = _attn_fwd(q, k, v, cfg, al, name, [s.astype(BF16) for s in shards])
        return (o, *gathered), (q, k, v, o, lse)

    def bwd(res, cts):
        dq_, dk_, dv_, received = _attn_bwd(*res, cts[0], cfg, al, name, list(cts[1:]))
        return (dq_, dk_, dv_, *[sum_slots(r, name + f"_sum{a}") for a, r in enumerate(received)])

    op.defvjp(fwd, bwd)
    return op


SCAN_WIDTHS = (32, 16, 8, 4, 2, 1)
N_CM = 2 + 2 * len(SCAN_WIDTHS)


def _scan_consts(reverse):
    C = CHUNK
    t = np.arange(C)[:, None]
    s = np.arange(C)[None, :]
    blocks = [(s <= t), (s > t)]
    for w in SCAN_WIDTHS:
        blocks.append((s <= t) & (s // w == t // w))
    for w in SCAN_WIDTHS:
        blocks.append((s > t) & (s // w == t // w))
    masks = [np.eye(C, dtype=bool)]
    for w in SCAN_WIDTHS:
        masks.append((t // (2 * w) == s // (2 * w)) & ((t // w) % 2 == 1) & ((s // w) % 2 == 0))
    if reverse:
        blocks = [b[::-1, ::-1] for b in blocks]
        masks = [m[::-1, ::-1] for m in masks]
    cm = np.concatenate([b.astype(np.float32) for b in blocks] + [np.ones((8, C), np.float32)], axis=0)
    mw = np.stack([np.tile(m.astype(np.float32), (1, C_HEADS)) for m in masks])
    rows = np.arange(C_HEADS * C)[:, None] // C
    lane = np.arange(C_W)[None, :] // C_DK
    hm = (rows == lane).astype(np.float32)
    bd = (np.arange(C_W)[:, None] // C_DK == lane).astype(np.float32)
    return (jnp.asarray(cm, BF16), jnp.asarray(cm.T.copy(), BF16), jnp.asarray(mw, F32),
            jnp.asarray(hm, F32), jnp.asarray(bd, F32))


def _scan_chunk(st, q, k, v, g, cm, cmt, mw, hm, bd):
    C = CHUNK
    cs = xdotl(cm, cmt, g)
    b = cs[0:C]
    rest = cs[C:2 * C]
    tot = cs[N_CM * C:N_CM * C + 1]

    def per_head(a):
        return (jnp.concatenate([a] * C_HEADS, axis=0) * hm).astype(BF16)

    a = _nt(q.astype(BF16), per_head(k)) * mw[0]
    for i in range(len(SCAN_WIDTHS)):
        eq = jnp.exp(jnp.minimum(cs[(2 + i) * C:(3 + i) * C], 0.0))
        ek = jnp.exp(jnp.minimum(cs[(2 + len(SCAN_WIDTHS) + i) * C:(3 + len(SCAN_WIDTHS) + i) * C], 0.0))
        a = a + _nt((q * eq).astype(BF16), per_head(k * ek)) * mw[i + 1]
    o = _nn(a.astype(BF16), per_head(v))
    o = o + _nt((q * jnp.exp(b)).astype(BF16), st.astype(BF16))
    st_new = st * jnp.exp(tot) + _tn(v.astype(BF16), (k * jnp.exp(rest)).astype(BF16)) * bd
    return o, st_new


class ScanLay:
    def __init__(self, B, T, Tc):
        self.B, self.S = B, T + Tc
        self.ncc, self.ntot = Tc // CHUNK, (T + Tc) // CHUNK

    def chunk(self, j, reverse):
        if not reverse:
            return j
        return jnp.where(j < self.ncc, self.ncc - 1 - j, self.ntot - 1 - (j - self.ncc))


def _scan_specs(sl, step):
    f = pl.BlockSpec((sl.B, CHUNK, C_W), lambda j: (0, sl.chunk(step(j), False), 0))
    r = pl.BlockSpec((sl.B, CHUNK, C_W), lambda j: (0, sl.chunk(step(j), True), 0))
    return f, r


def _scan_fwd(q, kf, gf, kb, gb, v, sl, name):
    B, S = sl.B, sl.S
    view = lambda a: a.reshape(B, S, C_W)
    cf, cr = _scan_consts(False), _scan_consts(True)
    nc = len(cf)
    f, r = _scan_specs(sl, lambda j: j)
    cspecs = [pl.BlockSpec(c.shape, lambda j, nd=c.ndim: (0,) * nd) for c in cf + cr]

    def body(*refs):
        (qf_ref, kf_ref, gf_ref, vf_ref, qr_ref, kr_ref, gr_ref, vr_ref), refs = refs[:8], refs[8:]
        cfv, crv = [c[...] for c in refs[:nc]], [c[...] for c in refs[nc:2 * nc]]
        of_ref, or_ref, st_ref, st = refs[2 * nc:]

        @pl.when(pl.program_id(0) == 0)
        def _():
            st[...] = jnp.zeros_like(st)

        st_ref[0] = st[...]
        dirs = ((qf_ref, kf_ref, gf_ref, vf_ref, of_ref, cfv), (qr_ref, kr_ref, gr_ref, vr_ref, or_ref, crv))
        args = [(st[d * B + b], q_[b], k_[b], v_[b], g_[b]) + tuple(cv)
                for d, (q_, k_, g_, v_, _, cv) in enumerate(dirs) for b in range(B)]
        outs = [_scan_chunk(*a) for a in args]
        for d, (_, _, _, _, o_, _) in enumerate(dirs):
            for b in range(B):
                o_[b], st[d * B + b] = outs[d * B + b]

    of, ob, states = _pallas(
        body, name=name, grid=(sl.ntot,),
        in_specs=[f] * 4 + [r] * 4 + cspecs,
        out_specs=[f, r, pl.BlockSpec((1, 2 * B, C_W, C_W), lambda j: (j, 0, 0, 0))],
        out_shape=[jax.ShapeDtypeStruct((B, S, C_W), F32)] * 2
        + [jax.ShapeDtypeStruct((sl.ntot, 2 * B, C_W, C_W), F32)],
        scratch_shapes=[pltpu.VMEM((2 * B, C_W, C_W), F32)],
        compiler_params=_cparams(("arbitrary",)),
    )(view(q), view(kf), view(gf), view(v), view(q), view(kb), view(gb), view(v), *cf, *cr)
    return of.reshape(B * S, C_W), ob.reshape(B * S, C_W), states


def _scan_bwd(q, kf, gf, kb, gb, v, states, dof, dob, sl, name):
    B, S = sl.B, sl.S
    view = lambda a: a.reshape(B, S, C_W)
    cf, cr = _scan_consts(False), _scan_consts(True)
    nc = len(cf)
    last = sl.ntot - 1
    f, r = _scan_specs(sl, lambda j: last - j)
    cspecs = [pl.BlockSpec(c.shape, lambda j, nd=c.ndim: (0,) * nd) for c in cf + cr]

    def body(*refs):
        ins, refs = refs[:11], refs[11:]
        qf_ref, kf_ref, gf_ref, vf_ref, dof_ref, qr_ref, kr_ref, gr_ref, vr_ref, dor_ref, st_ref = ins
        cfv, crv = [c[...] for c in refs[:nc]], [c[...] for c in refs[nc:2 * nc]]
        outs, dst = refs[2 * nc:-1], refs[-1]

        @pl.when(pl.program_id(0) == 0)
        def _():
            dst[...] = jnp.zeros_like(dst)

        dirs = ((qf_ref, kf_ref, gf_ref, vf_ref, dof_ref, cfv), (qr_ref, kr_ref, gr_ref, vr_ref, dor_ref, crv))
        args = [((st_ref[0, d * B + b], q_[b], k_[b], v_[b], g_[b]), (do_[b], dst[d * B + b]), cv)
                for d, (q_, k_, g_, v_, do_, cv) in enumerate(dirs) for b in range(B)]
        grads = []
        for prim, cts, cv in args:
            _, vjp = jax.vjp(lambda s_, a_, b_, c_, e_, cv=cv: _scan_chunk(s_, a_, b_, c_, e_, *cv), *prim)
            grads.append(vjp(cts))
        for d in range(2):
            dq_, dk_, dg_, dv_ = outs[4 * d:4 * d + 4]
            for b in range(B):
                dst[d * B + b], dq_[b], dk_[b], dv_[b], dg_[b] = grads[d * B + b]

    res = _pallas(
        body, name=name + "_bwd", grid=(sl.ntot,),
        in_specs=[f] * 5 + [r] * 5 + [pl.BlockSpec((1, 2 * B, C_W, C_W), lambda j: (last - j, 0, 0, 0))] + cspecs,
        out_specs=[f] * 4 + [r] * 4,
        out_shape=[jax.ShapeDtypeStruct((B, S, C_W), F32)] * 8,
        scratch_shapes=[pltpu.VMEM((2 * B, C_W, C_W), F32)],
        compiler_params=_cparams(("arbitrary",)),
    )(view(q), view(kf), view(gf), view(v), view(dof), view(q), view(kb), view(gb), view(v), view(dob),
      states, *cf, *cr)
    dq_f, dk_f, dg_f, dv_f, dq_r, dk_r, dg_r, dv_r = [a.reshape(B * S, C_W) for a in res]
    return dq_f + dq_r, dk_f, dg_f, dk_r, dg_r, dv_f + dv_r


def scan_op(sl, name):
    @jax.custom_vjp
    def op(q, kf, gf, kb, gb, v):
        return _scan_fwd(q, kf, gf, kb, gb, v, sl, name)[:2]

    def fwd(q, kf, gf, kb, gb, v):
        of, ob, states = _scan_fwd(q, kf, gf, kb, gb, v, sl, name)
        return (of, ob), (q, kf, gf, kb, gb, v, states)

    def bwd(res, cts):
        return _scan_bwd(*res, cts[0], cts[1], sl, name)

    op.defvjp(fwd, bwd)
    return op


def loss_and_grad(y, target, lay):
    N, D = y.shape

    def body(y_ref, t_ref, dy_ref, l_ref):
        i = pl.program_id(0)

        @pl.when(i == 0)
        def _():
            l_ref[...] = jnp.zeros_like(l_ref)

        is_ctx = i % lay.per < lay.nc

        @pl.when(is_ctx)
        def _():
            dy_ref[...] = jnp.zeros_like(dy_ref)

        @pl.when(jnp.logical_not(is_ctx))
        def _():
            e = y_ref[...] - t_ref[...]
            dy_ref[...] = e * (1.0 / D)
            l_ref[...] += 0.5 * jnp.sum(jnp.sum(e * e, axis=1, keepdims=True) * (1.0 / D), axis=0, keepdims=True)

    def t_index(i):
        return ((i // lay.per) * lay.nl + jnp.maximum(i % lay.per - lay.nc, 0), 0)

    dy, lp = _pallas(
        body, name="loss_head", grid=(lay.n_tiles,),
        in_specs=[pl.BlockSpec((lay.tm, D), lambda i: (i, 0)), pl.BlockSpec((lay.tm, D), t_index)],
        out_specs=[pl.BlockSpec((lay.tm, D), lambda i: (i, 0)), pl.BlockSpec((8, LANES), lambda i: (0, 0))],
        out_shape=[jax.ShapeDtypeStruct((N, D), F32), jax.ShapeDtypeStruct((8, LANES), F32)],
        compiler_params=_cparams(("arbitrary",)),
    )(y, target)
    return lp, dy


def _adam_math(w, g, m, v):
    mn = ADAM_B1 * m + (1.0 - ADAM_B1) * g
    vn = ADAM_B2 * v + (1.0 - ADAM_B2) * jnp.square(g)
    m_hat = mn / (1.0 - ADAM_B1 ** ADAM_STEP)
    v_hat = vn / (1.0 - ADAM_B2 ** ADAM_STEP)
    return -ADAM_LR * (m_hat / (jnp.sqrt(v_hat) + ADAM_EPS) + ADAM_WD * w), mn, vn


ROW_TILES = (512, 256, 128, 64, 32, 16, 8)


def adamw(w, g, m, v, name):
    R, C = w.shape
    tr = _tile(R, ROW_TILES)

    def body(w_ref, g_ref, m_ref, v_ref, d_ref, mo_ref, vo_ref):
        d_ref[...], mo_ref[...], vo_ref[...] = _adam_math(w_ref[...], g_ref[...], m_ref[...], v_ref[...])

    spec = pl.BlockSpec((tr, C), lambda i: (i, 0))
    return _pallas(
        body, name=name, grid=(R // tr,), in_specs=[spec] * 4, out_specs=[spec] * 3,
        out_shape=[jax.ShapeDtypeStruct((R, C), F32)] * 3,
        compiler_params=_cparams(("parallel",)),
    )(w, g, m, v)


def sum_slots(recv, name):
    shape = recv.shape[1:]
    C = shape[-1]
    r3 = recv.reshape(N_DEV, -1, C)
    R = r3.shape[1]
    tr = _tile(R, ROW_TILES[1:])

    def body(r_ref, o_ref):
        g = r_ref[0].astype(F32)
        for k in range(1, N_DEV):
            g = g + r_ref[k].astype(F32)
        o_ref[...] = g

    out = _pallas(
        body, name=name, grid=(R // tr,),
        in_specs=[pl.BlockSpec((N_DEV, tr, C), lambda i: (0, i, 0))],
        out_specs=pl.BlockSpec((tr, C), lambda i: (i, 0)),
        out_shape=jax.ShapeDtypeStruct((R, C), F32),
        compiler_params=_cparams(("parallel",)),
    )(r3)
    return out.reshape(shape)


def adamw_slots(w, recv, m, v, name):
    R, C = w.shape
    tr = _tile(R, ROW_TILES[1:])

    def body(w_ref, r_ref, m_ref, v_ref, g_ref, d_ref, mo_ref, vo_ref):
        g = r_ref[0].astype(F32)
        for k in range(1, N_DEV):
            g = g + r_ref[k].astype(F32)
        g_ref[...] = g
        d_ref[...], mo_ref[...], vo_ref[...] = _adam_math(w_ref[...], g, m_ref[...], v_ref[...])

    spec = pl.BlockSpec((tr, C), lambda i: (i, 0))
    return _pallas(
        body, name=name, grid=(R // tr,),
        in_specs=[spec, pl.BlockSpec((N_DEV, tr, C), lambda i: (0, i, 0)), spec, spec], out_specs=[spec] * 4,
        out_shape=[jax.ShapeDtypeStruct((R, C), F32)] * 4,
        compiler_params=_cparams(("parallel",)),
    )(w, recv, m, v)


def _me():
    return lax.axis_index("x"), lax.axis_index("y"), lax.axis_index("c")


def _gather_many(x_refs, out_refs, send_sems, recv_sems, local_sems):
    x, y, c = _me()
    me, sibling = (x, y, c), (x, y, 1 - c)
    chips = [(1 - x, y), (x, 1 - y), (1 - x, 1 - y)]
    arrs = range(len(x_refs))

    def slot(a, px, py, pc):
        return out_refs[a].at[4 * px + 2 * py + pc]

    def copy(a, k, block, to, src=None):
        return pltpu.make_async_remote_copy(
            src_ref=slot(a, *block) if src is None else src, dst_ref=slot(a, *block),
            send_sem=send_sems.at[7 * a + k], recv_sem=recv_sems.at[7 * a + k], device_id=to, device_id_type=MESH)

    mine = [pltpu.make_async_copy(x_refs[a], slot(a, *me), local_sems.at[a]) for a in arrs]
    for cp in mine:
        cp.start()
    first = []
    for a in arrs:
        first.append(copy(a, 0, me, sibling, src=x_refs[a]))
        first += [copy(a, 1 + j, me, (*chip, c), src=x_refs[a]) for j, chip in enumerate(chips)]
    for cp in first:
        cp.start()
    passed = []
    for j, chip in enumerate(chips):
        for a in arrs:
            copy(a, 1 + j, (*chip, c), me).wait_recv()
            fwd = copy(a, 4 + j, (*chip, c), sibling)
            fwd.start()
            passed.append(fwd)
    for a in arrs:
        copy(a, 0, sibling, me).wait_recv()
    for j, chip in enumerate(chips):
        for a in arrs:
            copy(a, 4 + j, (*chip, 1 - c), me).wait_recv()
    for cp in first + passed:
        cp.wait_send()
    for cp in mine:
        cp.wait()


def _comm_scratch(n):
    return [pltpu.SemaphoreType.DMA((7 * n,)), pltpu.SemaphoreType.DMA((7 * n,)), pltpu.SemaphoreType.DMA((n,))]


def small_gather(xb, name):
    R = xb.shape[0]

    def body(x_ref, out_ref, sum_ref, send_sems, recv_sems, local_sems):
        _gather_many([x_ref], [out_ref], send_sems, recv_sems, local_sems)
        acc = out_ref[0]
        for k in range(1, N_DEV):
            acc = acc + out_ref[k]
        sum_ref[...] = acc

    vm = pl.BlockSpec(memory_space=pltpu.VMEM)
    return _pallas(
        body, name=name, in_specs=[vm], out_specs=[vm, vm],
        out_shape=[jax.ShapeDtypeStruct((N_DEV, R, LANES), xb.dtype), jax.ShapeDtypeStruct((R, LANES), xb.dtype)],
        scratch_shapes=_comm_scratch(1),
        compiler_params=pltpu.CompilerParams(vmem_limit_bytes=VMEM_LIMIT),
    )(xb)


def big_gather(xs, name):
    n = len(xs)

    def body(*refs):
        _gather_many(refs[:n], refs[n:2 * n], *refs[2 * n:])

    hbm = pl.BlockSpec(memory_space=pl.ANY)
    return _pallas(
        body, name=name, in_specs=[hbm] * n, out_specs=[hbm] * n,
        out_shape=[jax.ShapeDtypeStruct((N_DEV,) + a.shape, a.dtype) for a in xs],
        scratch_shapes=_comm_scratch(n),
    )(*xs)


def scatter_exchange(gs, name):
    n = len(gs)
    rels = [(dx, dy, dc) for dx in (0, 1) for dy in (0, 1) for dc in (0, 1) if (dx, dy, dc) != (0, 0, 0)]

    def body(*refs):
        g_refs, r_refs = refs[:n], refs[n:2 * n]
        send_sems, recv_sems, local_sems = refs[2 * n:]
        x, y, c = _me()
        me = 4 * x + 2 * y + c
        mine = [pltpu.make_async_copy(g_refs[a].at[me], r_refs[a].at[me], local_sems.at[a]) for a in range(n)]
        for cp in mine:
            cp.start()
        copies = []
        for r, (dx, dy, dc) in enumerate(rels):
            px, py, pc = (x + dx) % 2, (y + dy) % 2, (c + dc) % 2
            for a in range(n):
                copies.append(pltpu.make_async_remote_copy(
                    src_ref=g_refs[a].at[4 * px + 2 * py + pc], dst_ref=r_refs[a].at[me],
                    send_sem=send_sems.at[7 * a + r], recv_sem=recv_sems.at[7 * a + r],
                    device_id=(px, py, pc), device_id_type=MESH))
        for cp in copies:
            cp.start()
        for cp in copies:
            cp.wait()
        for cp in mine:
            cp.wait()

    hbm = pl.BlockSpec(memory_space=pl.ANY)
    return _pallas(
        body, name=name, in_specs=[hbm] * n, out_specs=[hbm] * n,
        out_shape=[jax.ShapeDtypeStruct(a.shape, a.dtype) for a in gs],
        scratch_shapes=_comm_scratch(n),
    )(*gs)


def _pack(arrs, dtype, row_mult):
    flat = jnp.concatenate([a.astype(dtype).reshape(-1) for a in arrs])
    pad = (-flat.shape[0]) % (LANES * row_mult)
    if pad:
        flat = jnp.concatenate([flat, jnp.zeros((pad,), dtype)])
    return flat.reshape(-1, LANES)


def _unpack(buf, shapes, lead=()):
    flat = buf.reshape(*lead, -1)
    out, off = [], 0
    for s in shapes:
        n = int(np.prod(s))
        out.append(flat[..., off:off + n].reshape(*lead, *s))
        off += n
    return out


def _rope_tables(T, tm):
    pos = np.arange(T)
    row, col = pos // GRID_W, pos % GRID_W

    def tab(rot_dim):
        nf = rot_dim // 4
        inv = ROPE_THETA ** (-np.arange(nf, dtype=np.float32) / nf)
        ang = np.concatenate([row[:, None].astype(np.float32) * inv, col[:, None].astype(np.float32) * inv], axis=-1)
        ang = ang.astype(np.float32)
        cos, sin = np.cos(ang), np.sin(ang)
        return np.concatenate([cos, cos], -1), np.concatenate([-sin, sin], -1)

    c64, s64 = tab(HD)
    c32, s32 = tab(B_ROPE)
    ca, sa = np.tile(c64, (1, A_HEADS)), np.tile(s64, (1, A_HEADS))
    cb = np.concatenate([c32, np.ones((T, LANES - B_ROPE), np.float32)], -1)
    sb = np.concatenate([s32, np.zeros((T, LANES - B_ROPE), np.float32)], -1)
    one, zero = np.ones((T, B_NOPE), np.float32), np.zeros((T, B_NOPE), np.float32)
    tail1, tail0 = np.ones((T, LANES - B_NOPE - B_ROPE), np.float32), np.zeros((T, LANES - B_NOPE - B_ROPE), np.float32)
    cq = np.tile(np.concatenate([one, c32, tail1], -1), (1, B_HEADS))
    sq = np.tile(np.concatenate([zero, s32, tail0], -1), (1, B_HEADS))

    def fin(a, ident):
        return jnp.asarray(np.concatenate([a, np.full((tm, a.shape[1]), ident, np.float32)], 0), F32)

    return fin(ca, 1.0), fin(sa, 0.0), fin(cb, 1.0), fin(sb, 0.0), fin(cq, 1.0), fin(sq, 0.0)


def _swap_matrix(width, starts, half):
    p = np.zeros((width, width), np.float32)
    for s in starts:
        for i in range(half):
            p[s + i, s + half + i] = 1.0
            p[s + half + i, s + i] = 1.0
    return jnp.asarray(p, BF16)


def _seg_matrix(width):
    h = np.arange(width) // HD
    return jnp.asarray((h[:, None] == h[None, :]).astype(np.float32), BF16)


def _key_slot_matrices():
    e1 = np.zeros((B_HEADS * B_NOPE, B_HEADS * LANES), np.float32)
    e2 = np.zeros((LANES, B_HEADS * LANES), np.float32)
    for h in range(B_HEADS):
        for i in range(B_NOPE):
            e1[h * B_NOPE + i, h * LANES + i] = 1.0
        for i in range(B_ROPE):
            e2[i, h * LANES + B_NOPE + i] = 1.0
    return jnp.asarray(e1, BF16), jnp.asarray(e2, BF16)


def _f_premod(x, sh, sc, g):
    return (_rms(x, g) * (1.0 + sc) + sh,)


def _f_post(x, y, gt, g):
    return (x + gt * _rms(y, g),)


def _f_post_pre(x, y, gt, g_post, sh, sc, g_pre):
    x1 = x + gt * _rms(y, g_post)
    return x1, _rms(x1, g_pre) * (1.0 + sc) + sh


def _f_bias(raw, b):
    return (raw + b,)


def _f_silu(x):
    return (_silu(x),)


def _f_readout(of, ob, gate, gain, seg):
    return (_head_rms(of + ob, seg, gain) * _silu(gate),)


def _f_bq(bq, cq, sq, pq):
    return (bq * cq + xdotr2(bq, pq) * sq,)


def _f_bk(bkn, bkr, e1, e2):
    return (xdotr1(bkn, e1) + xdotr1(bkr, e2),)


def _make_f_feat(layer):
    def f(feat, ca, sa, cb, sb, gaq, gak, gbq, gbkv, c00, c01, c10, c11, seg, pa, pb):
        aq = _head_rms(feat[:, 0:512], seg, gaq)
        ak = _head_rms(feat[:, 512:640], seg[0:128, 0:128], gak)
        av = feat[:, 640:768]
        aq = aq * ca + xdotr2(aq, pa) * sa
        ak = ak * ca[:, 0:128] + xdotr2(ak, pa[0:128, 0:128]) * sa[:, 0:128]
        bqn = _rms(feat[:, 768:1024], gbq, B_QR)
        bkvn = _rms(feat[:, 1024:1152], gbkv)
        bkr = feat[:, 1152:1280]
        bkr = bkr * cb + xdotr2(bkr, pb) * sb
        cq = _silu(feat[:, 1280:1536])
        zf, zb = feat[:, 1536:1792], feat[:, 1792:2048]
        if layer == 0:
            lbf = lbb = 0.0
        else:
            def share(c0, c1):
                m = jnp.maximum(c0, c1)
                e0, e1 = jnp.exp(c0 - m), jnp.exp(c1 - m)
                return e1 / (e0 + e1)
            lbf, lbb = share(c00, c10), share(c01, c11)

        def gate(z, lb):
            f_ = lb + (1.0 - lb) * _sigmoid(z)
            return (1.0 - lb) * _sigmoid(-z), jnp.log(jnp.maximum(f_, F_TINY))

        kf, gf = gate(zf, lbf)
        kb, gb = gate(zb, lbb)
        return aq, ak, av, bqn, bkvn, bkr, cq, kf, gf, kb, gb, feat[:, 2048:2304], feat[:, 2304:2560]

    return f


def _pad_w_in(w):
    z = lambda n: jnp.zeros((w.shape[0], n), w.dtype)
    return jnp.concatenate([w[:, 0:960], z(64), w[:, 960:1120], z(96), w[:, 1120:2400]], axis=1)


def _pad_w_q_up(w):
    w4 = w.reshape(B_QR, B_HEADS, B_NOPE + B_ROPE)
    w4 = jnp.pad(w4, ((0, 256 - B_QR), (0, 0), (0, LANES - B_NOPE - B_ROPE)))
    return w4.reshape(256, B_HEADS * LANES)


def _split_w_kv_up(w):
    w4 = w.reshape(B_KVR, B_HEADS, B_NOPE + B_V)
    return w4[:, :, :B_NOPE].reshape(B_KVR, -1), w4[:, :, B_NOPE:].reshape(B_KVR, -1)


def _tile_gain(g, reps, width=None):
    t = jnp.tile(g, reps)
    if width is not None and width > t.shape[0]:
        t = jnp.pad(t, (0, width - t.shape[0]))
    return t[None, :]


def local_forward(dims, p):
    B, T, Tc, D = dims
    tm = min(256, Tc)
    lay_all = Lay(B, T, Tc, tm)
    ca, sa, cb, sb, cq, sq = _rope_tables(T, tm)
    seg512, seg256 = _seg_matrix(512), _seg_matrix(C_W)
    pa = _swap_matrix(512, range(0, 512, HD), HD // 2)
    pb = _swap_matrix(LANES, [0], B_ROPE // 2)
    pq = _swap_matrix(512, [h * LANES + B_NOPE for h in range(B_HEADS)], B_ROPE // 2)
    e1, e2 = _key_slot_matrices()
    sl = ScanLay(B, T, Tc)
    al_a, al_b = AttnLay(B, T, Tc, min(256, Tc)), AttnLay(B, T, Tc, min(256, Tc))
    cfg_a = (A_KV, A_GROUP, HD, HD, HD ** -0.5, 512, 1024)
    cfg_b = (B_HEADS, 1, LANES, B_V, (B_NOPE + B_ROPE) ** -0.5, 512, T + Tc)

    tok = p["tok"]
    depth = p["modraw"].shape[0]
    mods = []
    for l in range(depth):
        bias_lay = Lay(1, 8, 0, 8)
        raw8 = jnp.pad(p["modraw"][l], ((0, 8 - B - 1), (0, 0)))
        mod = ew_op(_f_bias, bias_lay, ("tok", "par"), (True, True), ((6 * D, F32),), f"l{l}_ada_bias")(
            raw8, p["b_ada"][l][None, :])[0]
        seg_rows = jnp.concatenate([mod[0:B], jnp.broadcast_to(mod[B:B + 1], (B, 6 * D))], axis=0)[:, None, :]
        mods.append([seg_rows[:, :, i * D:(i + 1) * D] for i in range(6)])

    post_pre_kinds = ("tok", "tok", "seg", "par", "seg", "seg", "par")
    h = ew_op(_f_premod, lay_all, ("tok", "seg", "seg", "par"), (True,) * 4, ((D, BF16),), "l0_premix")(
        tok, mods[0][0], mods[0][1], p["g_pre_mix"][0][None, :])[0]
    for l in range(depth):
        tag = f"l{l}_"
        sh_m, sc_m, gt_m, sh_f, sc_f, gt_f = mods[l]

        feat = matmul_op(tag + "w_in")(h, _pad_w_in(p["w_in"][l]))
        clb = p["clb"]
        feats = ew_op(
            _make_f_feat(l), lay_all,
            ("tok", "pos", "pos", "pos", "pos") + ("par",) * 11,
            (True,) + (False,) * 4 + (True,) * 8 + (False,) * 3,
            ((512, F32), (128, BF16), (128, BF16), (256, BF16), (128, BF16), (128, F32)) + ((C_W, F32),) * 7,
            tag + "feat")(
            feat, ca, sa, cb, sb,
            _tile_gain(p["a_q_norm"][l], A_HEADS), _tile_gain(p["a_k_norm"][l], A_KV),
            _tile_gain(p["b_q_norm"][l], 1, 256), _tile_gain(p["b_kv_norm"][l], 1),
            clb[0, 0][None, :], clb[0, 1][None, :], clb[1, 0][None, :], clb[1, 1][None, :],
            seg512, pa, pb)
        aq, ak, av, bqn, bkvn, bkr, cqs, kf, gf, kb, gb, cv, cgate = feats
        bq = matmul_op(tag + "w_q_up")(bqn, _pad_w_q_up(p["w_q_up"][l]))
        bq = ew_op(_f_bq, lay_all, ("tok", "pos", "pos", "par"), (True, False, False, False), ((512, F32),),
                   tag + "bq_rope")(bq, cq, sq, pq)[0]
        w_kn, w_v = _split_w_kv_up(p["w_kv_up"][l])
        bkn = matmul_op(tag + "w_k_up")(bkvn, w_kn)
        bv = matmul_op(tag + "w_v_up", BF16)(bkvn, w_v)
        bk = ew_op(_f_bk, lay_all, ("tok", "tok", "par", "par"), (True, True, False, False),
                   ((B_HEADS * LANES, BF16),), tag + "bk_slots")(bkn, bkr, e1, e2)[0]

        if l == 0:
            ya, *got = attn_gather_op(cfg_a, al_a, tag + "attn_a")(aq, ak, av, *[p["shard_" + n] for n in LATE])
            late = {n: _assemble(n, g) for n, g in zip(LATE, got)}
        else:
            ya = attn_op(cfg_a, al_a, tag + "attn_a")(aq, ak, av)
        yb = attn_op(cfg_b, al_b, tag + "attn_b")(bq, bk, bv)
        of, ob = scan_op(sl, tag + "scan")(cqs, kf, gf, kb, gb, cv)
        lay_out = lay_all
        yc = ew_op(_f_readout, lay_out, ("tok", "tok", "tok", "par", "par"), (True, True, True, True, False),
                   ((C_W, BF16),), tag + "readout")(of, ob, cgate, _tile_gain(p["c_out_norm"][l], C_HEADS), seg256)[0]
        ycat = jnp.concatenate([ya, yb, yc], axis=1)
        mixo = matmul_op(tag + "w_out")(ycat, late["w_out"][l])
        tok, hf = ew_op(_f_post_pre, lay_out, post_pre_kinds, (True,) * 7, ((D, F32), (D, BF16)),
                        tag + "postmix_preffn")(
            tok, mixo, gt_m, p["g_post_mix"][l][None, :], sh_f, sc_f, p["g_pre_ffn"][l][None, :])

        z = matmul_op(tag + "w_ff1", BF16, relu2=True)(hf, late["w_ff1"][l])
        yf = matmul_op(tag + "w_ff2")(z, late["w_ff2"][l])
        if l + 1 < depth:
            tok, h = ew_op(_f_post_pre, lay_out, post_pre_kinds, (True,) * 7, ((D, F32), (D, BF16)),
                           tag + "postffn_premix")(
                tok, yf, gt_f, p["g_post_ffn"][l][None, :], mods[l + 1][0], mods[l + 1][1],
                p["g_pre_mix"][l + 1][None, :])
        else:
            tok = ew_op(_f_post, lay_out, ("tok", "tok", "seg", "par"), (True,) * 4, ((D, F32),), tag + "postffn")(
                tok, yf, gt_f, p["g_post_ffn"][l][None, :])[0]
    return tok


EARLY = ("w_in", "w_q_up", "w_kv_up")
LATE = ("w_out", "w_ff1", "w_ff2")
COL_SHARDED = ("w_in", "w_q_up", "w_kv_up", "w_ff1")
SMALL = ("c_ctx", "b_ada", "g_pre_mix", "g_post_mix", "g_pre_ffn", "g_post_ffn", "a_q_norm", "a_k_norm",
         "b_q_norm", "b_kv_norm", "c_out_norm")
WEIGHTS = ("c_ctx", "w_ada", "b_ada", "g_pre_mix", "g_post_mix", "g_pre_ffn", "g_post_ffn", "w_in", "a_q_norm",
           "a_k_norm", "b_q_norm", "w_q_up", "b_kv_norm", "w_kv_up", "c_lower_bounds", "c_out_norm", "w_out",
           "w_ff1", "w_ff2")
SMALL_ROWS = 64


def _assemble(name, a):
    if name in COL_SHARDED:
        return a.transpose(1, 2, 0, 3).reshape(a.shape[1], a.shape[2], N_DEV * a.shape[3])
    return a.transpose(1, 0, 2, 3).reshape(a.shape[1], N_DEV * a.shape[2], a.shape[3])


def kernel(x, c, ctx, c_ctx, w_ada, b_ada, g_pre_mix, g_post_mix, g_pre_ffn, g_post_ffn, w_in, a_q_norm, a_k_norm, b_q_norm, w_q_up, b_kv_norm, w_kv_up, c_lower_bounds, c_out_norm, w_out, w_ff1, w_ff2, loss_target, m_c_ctx, m_w_ada, m_b_ada, m_g_pre_mix, m_g_post_mix, m_g_pre_ffn, m_g_post_ffn, m_w_in, m_a_q_norm, m_a_k_norm, m_b_q_norm, m_w_q_up, m_b_kv_norm, m_w_kv_up, m_c_lower_bounds, m_c_out_norm, m_w_out, m_w_ff1, m_w_ff2, v_c_ctx, v_w_ada, v_b_ada, v_g_pre_mix, v_g_post_mix, v_g_pre_ffn, v_g_post_ffn, v_w_in, v_a_q_norm, v_a_k_norm, v_b_q_norm, v_w_q_up, v_b_kv_norm, v_w_kv_up, v_c_lower_bounds, v_c_out_norm, v_w_out, v_w_ff1, v_w_ff2):
    W = dict(c_ctx=c_ctx, w_ada=w_ada, b_ada=b_ada, g_pre_mix=g_pre_mix, g_post_mix=g_post_mix, g_pre_ffn=g_pre_ffn,
             g_post_ffn=g_post_ffn, w_in=w_in, a_q_norm=a_q_norm, a_k_norm=a_k_norm, b_q_norm=b_q_norm,
             w_q_up=w_q_up, b_kv_norm=b_kv_norm, w_kv_up=w_kv_up, c_lower_bounds=c_lower_bounds,
             c_out_norm=c_out_norm, w_out=w_out, w_ff1=w_ff1, w_ff2=w_ff2)
    M = dict(c_ctx=m_c_ctx, w_ada=m_w_ada, b_ada=m_b_ada, g_pre_mix=m_g_pre_mix, g_post_mix=m_g_post_mix,
             g_pre_ffn=m_g_pre_ffn, g_post_ffn=m_g_post_ffn, w_in=m_w_in, a_q_norm=m_a_q_norm, a_k_norm=m_a_k_norm,
             b_q_norm=m_b_q_norm, w_q_up=m_w_q_up, b_kv_norm=m_b_kv_norm, w_kv_up=m_w_kv_up,
             c_lower_bounds=m_c_lower_bounds, c_out_norm=m_c_out_norm, w_out=m_w_out, w_ff1=m_w_ff1, w_ff2=m_w_ff2)
    V = dict(c_ctx=v_c_ctx, w_ada=v_w_ada, b_ada=v_b_ada, g_pre_mix=v_g_pre_mix, g_post_mix=v_g_post_mix,
             g_pre_ffn=v_g_pre_ffn, g_post_ffn=v_g_post_ffn, w_in=v_w_in, a_q_norm=v_a_q_norm, a_k_norm=v_a_k_norm,
             b_q_norm=v_b_q_norm, w_q_up=v_w_q_up, b_kv_norm=v_b_kv_norm, w_kv_up=v_w_kv_up,
             c_lower_bounds=v_c_lower_bounds, c_out_norm=v_c_out_norm, w_out=v_w_out, w_ff1=v_w_ff1, w_ff2=v_w_ff2)

    B, T, D = x.shape
    Tc = ctx.shape[1]
    depth = w_ada.shape[0]
    ada_cols = w_ada.shape[2]
    idx = 4 * lax.axis_index("x") + 2 * lax.axis_index("y") + lax.axis_index("c")
    n_cond = N_DEV * B
    cond_rows = -(-(n_cond + 1) // 8) * 8

    clb_cols = c_lower_bounds.shape[2]
    g1, _ = small_gather(_pack([c, c_lower_bounds], F32, 8), "gather_cond")
    c_parts, clb_parts = _unpack(g1, [c.shape, c_lower_bounds.shape], lead=(N_DEV,))
    c_all = c_parts.reshape(n_cond, D)
    clb_full = clb_parts.transpose(1, 2, 0, 3).reshape(depth, 2, N_DEV * clb_cols)

    cond_lay = Lay(1, cond_rows, 0, cond_rows)

    def ada_shard(c_ctx_, w_ada_):
        cond = jnp.concatenate([c_all, c_ctx_[None, :], jnp.zeros((cond_rows - n_cond - 1, D), F32)], axis=0)
        sc = ew_op(_f_silu, cond_lay, ("tok",), (True,), ((D, F32),), "cond_silu")(cond)[0]
        return jnp.stack([matmul_op(f"l{l}_w_ada")(sc, w_ada_[l]) for l in range(depth)])

    mod_shard, vjp_ada = jax.vjp(ada_shard, c_ctx, w_ada)

    g2, _ = small_gather(_pack([mod_shard], F32, 8), "gather_mod")
    mod_all = _unpack(g2, [mod_shard.shape], lead=(N_DEV,))[0]
    mod_all = mod_all.transpose(1, 2, 0, 3).reshape(depth, cond_rows, N_DEV * ada_cols)
    mine = lax.dynamic_slice_in_dim(mod_all, idx * B, B, axis=1)
    modraw = jnp.concatenate([mine, mod_all[:, n_cond:n_cond + 1]], axis=1)

    gathered = dict(zip(EARLY, big_gather([W[n].astype(BF16) for n in EARLY], "gather_weights")))

    dims = (B, T, Tc, D)
    small_names = [n for n in SMALL if n != "c_ctx"]
    small_in = {n: W[n] for n in small_names}
    late_in = {n: W[n] for n in LATE}

    def fwd(x_, modraw_, small_, clb_, gathered_, late_):
        p = dict(small_)
        p.update({n: _assemble(n, a) for n, a in gathered_.items()})
        p.update({"shard_" + n: a for n, a in late_.items()})
        p.update(tok=jnp.concatenate([ctx, x_], axis=1).reshape(B * (Tc + T), D), modraw=modraw_, clb=clb_)
        return local_forward(dims, p)

    y, vjp_main = jax.vjp(fwd, x, modraw, small_in, clb_full, gathered, late_in)
    loss_part, dy = loss_and_grad(y, loss_target.reshape(B * T, D), Lay(B, T, Tc, min(256, Tc)))
    dx, dmodraw, dsmall, dclb, dgathered, dlate = vjp_main(dy)

    pay3 = _pack([dmodraw] + [dsmall[n] for n in small_names] + [dclb, loss_part[0, 0:1]], F32, 8)
    g3, s3 = small_gather(pay3, "gather_small_grads")
    dmod_parts = _unpack(g3, [dmodraw.shape], lead=(N_DEV,))[0]
    tot = _unpack(s3, [dmodraw.shape] + [W[n].shape for n in small_names] + [clb_full.shape, (1,)])
    dmod_tot, small_tot, dclb_tot, loss = tot[0], dict(zip(small_names, tot[1:-2])), tot[-2], tot[-1]
    drows = dmod_parts[:, :, 0:B].transpose(1, 0, 2, 3).reshape(depth, n_cond, N_DEV * ada_cols)
    dcond = jnp.concatenate(
        [drows, dmod_tot[:, B:B + 1], jnp.zeros((depth, cond_rows - n_cond - 1, N_DEV * ada_cols), F32)], axis=1)
    dmod_shard = lax.dynamic_slice_in_dim(dcond, idx * ada_cols, ada_cols, axis=2)
    dc_ctx_part, dw_ada = vjp_ada(dmod_shard)

    _, s4 = small_gather(_pack([dc_ctx_part], F32, 8), "gather_c_ctx_grad")
    small_tot["c_ctx"] = _unpack(s4, [c_ctx.shape])[0]

    recv = dict(zip(EARLY, scatter_exchange([dgathered[n] for n in EARLY], "scatter_grads")))
    grads, delta, new_m, new_v = dict(small_tot), {}, {}, {}
    for n in EARLY:
        shape, cols = W[n].shape, W[n].shape[-1]
        flat = lambda a: a.reshape(-1, cols)
        res = adamw_slots(flat(W[n]), recv[n].reshape(N_DEV, -1, cols), flat(M[n]), flat(V[n]), "adamw_" + n)
        grads[n], delta[n], new_m[n], new_v[n] = (a.reshape(shape) for a in res)
    for n in LATE:
        shape, cols = W[n].shape, W[n].shape[-1]
        flat = lambda a: a.reshape(-1, cols)
        grads[n] = dlate[n]
        res = adamw(flat(W[n]), flat(dlate[n]), flat(M[n]), flat(V[n]), "adamw_" + n)
        delta[n], new_m[n], new_v[n] = (a.reshape(shape) for a in res)
    grads["w_ada"] = dw_ada
    grads["c_lower_bounds"] = lax.dynamic_slice_in_dim(dclb_tot, idx * clb_cols, clb_cols, axis=2)

    flat_a = lambda a: a.reshape(-1, ada_cols)
    res = adamw(flat_a(w_ada), flat_a(dw_ada), flat_a(m_w_ada), flat_a(v_w_ada), "adamw_w_ada")
    delta["w_ada"], new_m["w_ada"], new_v["w_ada"] = (a.reshape(w_ada.shape) for a in res)
    names = list(SMALL) + ["c_lower_bounds"]
    shapes = [W[n].shape for n in names]
    res = adamw(*[_pack([src[n] for n in names], F32, SMALL_ROWS) for src in (W, grads, M, V)], "adamw_small")
    for dst, buf in zip((delta, new_m, new_v), res):
        dst.update(zip(names, _unpack(buf, shapes)))

    return (loss.reshape(()), dx, *[grads[n] for n in WEIGHTS], *[delta[n] for n in WEIGHTS],
            *[new_m[n] for n in WEIGHTS], *[new_v[n] for n in WEIGHTS])
```

```python
import math

import numpy as np

import jax
import jax.numpy as jnp
from jax import lax
from jax.experimental import pallas as pl
from jax.experimental.pallas import tpu as pltpu

F32 = jnp.float32
BF16 = jnp.bfloat16

A_HEADS, A_KV, HD = 8, 2, 64
A_GROUP = A_HEADS // A_KV
B_HEADS, B_QR, B_KVR, B_NOPE, B_ROPE, B_V = 4, 192, 128, 64, 32, 64
C_HEADS, C_DK = 4, 64
C_W = C_HEADS * C_DK
GRID_W = 64
CHUNK = 64
ROPE_THETA = 10000.0
EPS = 1e-6
F_TINY = 1e-30
D_IN = 2400
D_IN_PAD = 2560
N_DEV = 8
LANES = 128
NEG = -1e30

ADAM_LR, ADAM_B1, ADAM_B2, ADAM_EPS, ADAM_WD, ADAM_STEP = 0.001, 0.9, 0.999, 1e-08, 0.01, 10

VMEM_LIMIT = 56 * 1024 * 1024
MESH = pl.DeviceIdType.MESH


def _pallas(body, **kw):
    return pl.pallas_call(body, **kw)


def _cparams(sem):
    return pltpu.CompilerParams(dimension_semantics=sem, vmem_limit_bytes=VMEM_LIMIT)


def _split3(x):
    hi = x.astype(BF16)
    r = x - hi.astype(F32)
    mid = r.astype(BF16)
    lo = (r - mid.astype(F32)).astype(BF16)
    return hi, mid, lo


def _nn(a, b):
    return jnp.dot(a, b, preferred_element_type=F32)


def _nt(a, b):
    return lax.dot_general(a, b, (((1,), (1,)), ((), ())), preferred_element_type=F32)


def _tn(a, b):
    return lax.dot_general(a, b, (((0,), (0,)), ((), ())), preferred_element_type=F32)


def _make_xdotr(pieces):
    @jax.custom_vjp
    def op(x, m):
        return sum(_nn(p, m) for p in _split3(x)[:pieces])

    def fwd(x, m):
        return op(x, m), m

    def bwd(m, ct):
        return sum(_nt(p, m) for p in _split3(ct)[:pieces]), None

    op.defvjp(fwd, bwd)
    return op


xdotr, xdotr2, xdotr1 = _make_xdotr(3), _make_xdotr(2), _make_xdotr(1)


@jax.custom_vjp
def xdotl(m, mt, x):
    return sum(_nn(m, p) for p in _split3(x)[:2])


def _xdotl_fwd(m, mt, x):
    return xdotl(m, mt, x), (m, mt)


def _xdotl_bwd(res, ct):
    m, mt = res
    return None, None, sum(_nn(mt, p) for p in _split3(ct)[:2])


xdotl.defvjp(_xdotl_fwd, _xdotl_bwd)


def _sigmoid(x):
    return 1.0 / (1.0 + jnp.exp(-x))


def _silu(x):
    return x * _sigmoid(x)


def _rms(x, gain, n=None):
    n = x.shape[-1] if n is None else n
    ms = jnp.sum(x * x, axis=-1, keepdims=True) * (1.0 / n)
    return x * lax.rsqrt(ms + EPS) * gain


def _head_rms(x, seg, gain):
    ms = xdotr2(x * x, seg) * (1.0 / HD)
    return x * lax.rsqrt(ms + EPS) * gain


class Lay:
    def __init__(self, B, T, Tc, tm):
        self.B, self.T, self.Tc, self.tm = B, T, Tc, tm
        self.nl, self.nc = T // tm, Tc // tm
        self.per = self.nl + self.nc
        self.n_tiles = B * self.per
        self.n_seg = 2 * B
        self.rows = self.n_tiles * tm

    def seg(self, i):
        b, w = i // self.per, i % self.per
        return jnp.where(w < self.nc, self.B + b, b)

    def pos(self, i):
        w = i % self.per
        return jnp.where(w < self.nc, self.nl, w - self.nc)

    def first(self, i):
        w = i % self.per
        return jnp.logical_or(w == 0, w == self.nc)


def _ew_spec(kind, a, lay):
    if kind == "tok":
        return pl.BlockSpec((lay.tm, a.shape[1]), lambda i: (i, 0))
    if kind == "seg":
        return pl.BlockSpec((1, 1, a.shape[2]), lambda i: (lay.seg(i), 0, 0))
    if kind == "pos":
        return pl.BlockSpec((lay.tm, a.shape[1]), lambda i: (lay.pos(i), 0))
    return pl.BlockSpec(a.shape, lambda i: (0,) * a.ndim)


def _ew_load(ref, kind):
    if kind == "seg":
        return ref[0]
    if kind == "tok":
        return ref[...].astype(F32)
    return ref[...]


def _ew_fwd(f, lay, kinds, arrays, outs, name):
    n_in = len(arrays)

    def body(*refs):
        vals = [_ew_load(r, k) for r, k in zip(refs[:n_in], kinds)]
        res = f(*vals)
        for r, o in zip(res, refs[n_in:]):
            o[...] = r.astype(o.dtype)

    return _pallas(
        body, name=name, grid=(lay.n_tiles,),
        in_specs=[_ew_spec(k, a, lay) for k, a in zip(kinds, arrays)],
        out_specs=[pl.BlockSpec((lay.tm, c), lambda i: (i, 0)) for c, _ in outs],
        out_shape=[jax.ShapeDtypeStruct((lay.rows, c), dt) for c, dt in outs],
        compiler_params=_cparams(("parallel",)),
    )(*arrays)


def _ew_bwd(f, lay, kinds, diffs, arrays, cts, name):
    n_in, n_ct = len(arrays), len(cts)
    d_idx = [i for i, d in enumerate(diffs) if d]

    g_shapes, g_specs = [], []
    for i in d_idx:
        a, k = arrays[i], kinds[i]
        if k == "tok":
            g_shapes.append(jax.ShapeDtypeStruct((lay.rows, a.shape[1]), a.dtype))
            g_specs.append(pl.BlockSpec((lay.tm, a.shape[1]), lambda t: (t, 0)))
        elif k == "seg":
            g_shapes.append(jax.ShapeDtypeStruct((lay.n_seg, 1, a.shape[2]), F32))
            g_specs.append(pl.BlockSpec((1, 1, a.shape[2]), lambda t: (lay.seg(t), 0, 0)))
        else:
            g_shapes.append(jax.ShapeDtypeStruct(a.shape, F32))
            g_specs.append(pl.BlockSpec(a.shape, lambda t, nd=a.ndim: (0,) * nd))

    def body(*refs):
        vals = [_ew_load(r, k) for r, k in zip(refs[:n_in], kinds)]
        cvals = tuple(r[...].astype(F32) for r in refs[n_in:n_in + n_ct])
        g_refs = refs[n_in + n_ct:]

        def g(*dv):
            full = list(vals)
            for j, i in enumerate(d_idx):
                full[i] = dv[j]
            return tuple(o.astype(F32) for o in f(*full))

        _, vjp = jax.vjp(g, *[vals[i] for i in d_idx])
        grads = vjp(cvals)
        t = pl.program_id(0)
        for gref, grad, i in zip(g_refs, grads, d_idx):
            k = kinds[i]
            if k == "tok":
                gref[...] = grad.astype(gref.dtype)
            elif k == "seg":
                @pl.when(lay.first(t))
                def _():
                    gref[...] = jnp.zeros_like(gref)

                gref[0] += grad
            else:
                @pl.when(t == 0)
                def _():
                    gref[...] = jnp.zeros_like(gref)

                gref[...] += grad

    res = _pallas(
        body, name=name + "_bwd", grid=(lay.n_tiles,),
        in_specs=[_ew_spec(k, a, lay) for k, a in zip(kinds, arrays)]
        + [pl.BlockSpec((lay.tm, c.shape[1]), lambda i: (i, 0)) for c in cts],
        out_specs=g_specs, out_shape=g_shapes,
        compiler_params=_cparams(("arbitrary",)),
    )(*arrays, *cts)
    out = [None] * n_in
    for gr, i in zip(res, d_idx):
        a = arrays[i]
        if gr.shape != a.shape:
            pad = [(0, a.shape[0] - gr.shape[0])] + [(0, 0)] * (a.ndim - 1)
            gr = jnp.pad(gr, pad)
        out[i] = gr
    return tuple(out)


def ew_op(f, lay, kinds, diffs, outs, name):
    kinds, diffs, outs = tuple(kinds), tuple(diffs), tuple(outs)

    @jax.custom_vjp
    def op(*arrays):
        return tuple(_ew_fwd(f, lay, kinds, arrays, outs, name))

    def fwd(*arrays):
        return op(*arrays), arrays

    def bwd(arrays, cts):
        return _ew_bwd(f, lay, kinds, diffs, arrays, tuple(cts), name)

    op.defvjp(fwd, bwd)
    return op


def _tile(n, cands):
    for c in cands:
        if n % c == 0:
            return c
    return n


TN_ROW_TILES = (2176, 1024, 512, 256)
WEIGHT_TILE_ELEMS = 4 * 1024 * 1024


def _wide_tile(n, depth):
    for c in (2048, 1280, 1024, 512, 256, 128):
        if n % c == 0 and c * depth <= WEIGHT_TILE_ELEMS:
            return c
    return n


def _mm_nn(x, w, name, out_dtype, relu2):
    M, K = x.shape
    N = w.shape[1]
    tm, tn = _tile(M, (512, 256)), _wide_tile(N, K)

    def body(x_ref, w_ref, o_ref):
        acc = _nn(x_ref[...].astype(BF16), w_ref[...].astype(BF16))
        if relu2:
            acc = jnp.square(jnp.maximum(acc, 0.0))
        o_ref[...] = acc.astype(o_ref.dtype)

    return _pallas(
        body, name=name, grid=(N // tn, M // tm),
        in_specs=[pl.BlockSpec((tm, K), lambda j, i: (i, 0)), pl.BlockSpec((K, tn), lambda j, i: (0, j))],
        out_specs=pl.BlockSpec((tm, tn), lambda j, i: (i, j)),
        out_shape=jax.ShapeDtypeStruct((M, N), out_dtype),
        compiler_params=_cparams(("parallel", "parallel")),
    )(x, w)


def _through_relu2(dz_ref, z_ref):
    if z_ref is None:
        return dz_ref[...].astype(BF16)
    z = z_ref[...].astype(F32)
    root = z * lax.rsqrt(jnp.maximum(z, F_TINY))
    return (dz_ref[...].astype(F32) * (2.0 * root)).astype(BF16)


def _mm_nt(dy, w, name, out_dtype, z=None):
    M, N = dy.shape
    K = w.shape[0]
    tm = _tile(M, (256,)) if z is not None else _tile(M, (512, 256))
    tk = K if z is not None else _wide_tile(K, N)
    row = pl.BlockSpec((tm, N), lambda j, i: (i, 0))

    def body(*refs):
        dy_ref, z_ref = (refs[0], refs[1]) if z is not None else (refs[0], None)
        w_ref, o_ref = refs[-2], refs[-1]
        o_ref[...] = _nt(_through_relu2(dy_ref, z_ref), w_ref[...].astype(BF16)).astype(o_ref.dtype)

    return _pallas(
        body, name=name, grid=(K // tk, M // tm),
        in_specs=[row] * (2 if z is not None else 1) + [pl.BlockSpec((tk, N), lambda j, i: (j, 0))],
        out_specs=pl.BlockSpec((tm, tk), lambda j, i: (i, j)),
        out_shape=jax.ShapeDtypeStruct((M, K), out_dtype),
        compiler_params=_cparams(("parallel", "parallel")),
    )(*((dy, z, w) if z is not None else (dy, w)))


def _mm_tn(x, dy, name, out_dtype, z=None):
    M, K = x.shape
    N = dy.shape[1]
    tm = _tile(M, TN_ROW_TILES)
    tk, tn = _tile(K, (1024, 512, 256, 128)), _tile(N, (1024, 512, 256, 128))
    n_m = M // tm
    col = pl.BlockSpec((tm, tn), lambda a, b, m: (m, b))

    def body(*refs):
        x_ref = refs[0]
        dy_ref, z_ref = (refs[1], refs[2]) if z is not None else (refs[1], None)
        o_ref, acc_ref = refs[-2], refs[-1]
        m = pl.program_id(2)

        @pl.when(m == 0)
        def _():
            acc_ref[...] = jnp.zeros_like(acc_ref)

        acc_ref[...] += _tn(x_ref[...].astype(BF16), _through_relu2(dy_ref, z_ref))

        @pl.when(m == n_m - 1)
        def _():
            o_ref[...] = acc_ref[...].astype(o_ref.dtype)

    return _pallas(
        body, name=name, grid=(K // tk, N // tn, n_m),
        in_specs=[pl.BlockSpec((tm, tk), lambda a, b, m: (m, a))] + [col] * (2 if z is not None else 1),
        out_specs=pl.BlockSpec((tk, tn), lambda a, b, m: (a, b)),
        out_shape=jax.ShapeDtypeStruct((K, N), out_dtype),
        scratch_shapes=[pltpu.VMEM((tk, tn), F32)],
        compiler_params=_cparams(("parallel", "parallel", "arbitrary")),
    )(*((x, dy, z) if z is not None else (x, dy)))


def matmul_op(name, out_dtype=F32, relu2=False):
    @jax.custom_vjp
    def op(x, w):
        return _mm_nn(x, w, name, out_dtype, relu2)

    def fwd(x, w):
        y = op(x, w)
        return y, (x, w, y if relu2 else None)

    def bwd(res, dy):
        x, w, z = res
        return _mm_nt(dy, w, name + "_dx", x.dtype, z), _mm_tn(x, dy, name + "_dw", w.dtype, z)

    op.defvjp(fwd, bwd)
    return op


LOG2E = math.log2(math.e)


class AttnLay:
    def __init__(self, B, T, Tc, tq):
        self.B, self.T, self.Tc, self.tq = B, T, Tc, tq
        self.S = T + Tc
        self.nq, self.nqc = self.S // tq, Tc // tq


def _attn_specs(al):
    qs = lambda w: pl.BlockSpec((al.tq, w), lambda b, i: (b * al.nq + i, 0))
    ks = lambda w: pl.BlockSpec((al.S, w), lambda b, i: (b, 0))
    return qs, ks


def _lane_fold(acc, x, op):
    t = x[:, 0:LANES]
    for j in range(1, x.shape[1] // LANES):
        t = op(t, x[:, j * LANES:(j + 1) * LANES])
    return op(acc, t)


def _key_chunks(n, kc):
    return [(c0, min(kc, n - c0)) for c0 in range(0, n, kc)]


def _stack(ref, g, group, width, tq):
    parts = [ref[:, (g * group + j) * width:(g * group + j + 1) * width].astype(F32) for j in range(group)]
    return parts[0] if group == 1 else jnp.concatenate(parts, axis=0)


PEER_RELATIONS = [(dx, dy, dc) for dx in (0, 1) for dy in (0, 1) for dc in (0, 1) if (dx, dy, dc) != (0, 0, 0)]


def _exchange_behind(al, src_refs, dst_refs, sems, gather):
    send_sems, recv_sems, local_sems = sems
    x, y, c = _me()
    me = 4 * x + 2 * y + c

    def copies():
        out = []
        for a, (s, d) in enumerate(zip(src_refs, dst_refs)):
            out.append(pltpu.make_async_copy(s if gather else s.at[me], d.at[me], local_sems.at[a]))
            for r, (dx, dy, dc) in enumerate(PEER_RELATIONS):
                px, py, pc = (x + dx) % 2, (y + dy) % 2, (c + dc) % 2
                out.append(pltpu.make_async_remote_copy(
                    src_ref=s if gather else s.at[4 * px + 2 * py + pc], dst_ref=d.at[me],
                    send_sem=send_sems.at[7 * a + r], recv_sem=recv_sems.at[7 * a + r],
                    device_id=(px, py, pc), device_id_type=MESH))
        return out

    b, i = pl.program_id(0), pl.program_id(1)

    @pl.when(jnp.logical_and(b == 0, i == 0))
    def _():
        for cp in copies():
            cp.start()

    @pl.when(jnp.logical_and(b == al.B - 1, i == al.nq - 1))
    def _():
        for cp in copies():
            cp.wait()


def _attn_fwd(q, k, v, cfg, al, name, shards=()):
    n_kv, group, dq, dv, scale, kc, _ = cfg
    tq = al.tq
    rows = group * tq
    wq, wk, wv, wo = q.shape[1], k.shape[1], v.shape[1], n_kv * group * dv
    qs, ks = _attn_specs(al)
    n = len(shards)

    def body(*refs):
        q_ref, k_ref, v_ref = refs[:3]
        x_refs = refs[3:3 + n]
        o_ref, lse_ref = refs[3 + n:5 + n]
        g_refs = refs[5 + n:5 + 2 * n]
        s_scr = refs[5 + 2 * n]
        if n:
            _exchange_behind(al, x_refs, g_refs, refs[6 + 2 * n:], gather=True)
        lane = lax.broadcasted_iota(jnp.int32, (tq, LANES), 1)

        def run(n_keys):
            chunks = _key_chunks(n_keys, kc)
            lse_all = jnp.zeros((tq, LANES), F32)
            for g in range(n_kv):
                q4 = (_stack(q_ref, g, group, dq, tq) * (scale * LOG2E)).astype(BF16)
                ksl, vsl = slice(g * dq, (g + 1) * dq), slice(g * dv, (g + 1) * dv)
                m_part = jnp.full((rows, LANES), -jnp.inf, F32)
                for c0, w in chunks:
                    s = _nt(q4, k_ref[c0:c0 + w, ksl])
                    s_scr[:, c0:c0 + w] = s
                    m_part = _lane_fold(m_part, s, jnp.maximum)
                m = jnp.max(m_part, axis=1, keepdims=True)
                l_part = jnp.zeros((rows, LANES), F32)
                acc = jnp.zeros((rows, dv), F32)
                for c0, w in chunks:
                    p = jnp.exp2(s_scr[:, c0:c0 + w] - m)
                    l_part = _lane_fold(l_part, p, jnp.add)
                    acc = acc + _nn(p.astype(BF16), v_ref[c0:c0 + w, vsl])
                l = jnp.sum(l_part, axis=1, keepdims=True)
                o = acc / l
                lse = m + jnp.log2(l)
                for j in range(group):
                    h = g * group + j
                    o_ref[:, h * dv:(h + 1) * dv] = o[j * tq:(j + 1) * tq].astype(o_ref.dtype)
                    lse_all = jnp.where(lane == h, lse[j * tq:(j + 1) * tq], lse_all)
            lse_ref[...] = lse_all

        is_ctx = pl.program_id(1) < al.nqc

        @pl.when(is_ctx)
        def _():
            run(al.Tc)

        @pl.when(jnp.logical_not(is_ctx))
        def _():
            run(al.S)

    hbm = pl.BlockSpec(memory_space=pl.ANY)
    res = _pallas(
        body, name=name, grid=(al.B, al.nq),
        in_specs=[qs(wq), ks(wk), ks(wv)] + [hbm] * n,
        out_specs=[qs(wo), qs(LANES)] + [hbm] * n,
        out_shape=[jax.ShapeDtypeStruct((q.shape[0], wo), BF16), jax.ShapeDtypeStruct((q.shape[0], LANES), F32)]
        + [jax.ShapeDtypeStruct((N_DEV,) + a.shape, a.dtype) for a in shards],
        scratch_shapes=[pltpu.VMEM((rows, al.S), F32)] + (_comm_scratch(n) if n else []),
        compiler_params=_cparams(("arbitrary", "arbitrary") if n else ("parallel", "parallel")),
    )(q, k, v, *shards)
    return res[0], res[1], list(res[2:])


def _attn_bwd(q, k, v, o, lse, do, cfg, al, name, partials=()):
    n_kv, group, dq, dv, scale, _, kc = cfg
    tq = al.tq
    rows = group * tq
    wq, wk, wv, wo = q.shape[1], k.shape[1], v.shape[1], n_kv * group * dv
    qs, ks = _attn_specs(al)
    n = len(partials)

    def body(*refs):
        q_ref, k_ref, v_ref, o_ref, lse_ref, do_ref = refs[:6]
        p_refs = refs[6:6 + n]
        dq_ref, dk_ref, dv_ref = refs[6 + n:9 + n]
        r_refs = refs[9 + n:9 + 2 * n]
        ak, av = refs[9 + 2 * n:11 + 2 * n]
        if n:
            _exchange_behind(al, p_refs, r_refs, refs[11 + 2 * n:], gather=False)
        i = pl.program_id(1)

        @pl.when(i == 0)
        def _():
            ak[...] = jnp.zeros_like(ak)
            av[...] = jnp.zeros_like(av)

        lane = lax.broadcasted_iota(jnp.int32, (tq, LANES), 1)

        def run(n_keys):
            lse_tile = lse_ref[...]
            for g in range(n_kv):
                qf = _stack(q_ref, g, group, dq, tq)
                q4l = (qf * (scale * LOG2E)).astype(BF16)
                q4s = (qf * scale).astype(BF16)
                do4 = _stack(do_ref, g, group, dv, tq)
                o4 = _stack(o_ref, g, group, dv, tq)
                cols = [jnp.sum(jnp.where(lane == g * group + j, lse_tile, 0.0), axis=1, keepdims=True)
                        for j in range(group)]
                lse4 = cols[0] if group == 1 else jnp.concatenate(cols, axis=0)
                dl = jnp.sum(do4 * o4, axis=1, keepdims=True)
                dob = do4.astype(BF16)
                ksl, vsl = slice(g * dq, (g + 1) * dq), slice(g * dv, (g + 1) * dv)
                dq4 = jnp.zeros((rows, dq), F32)
                for c0, w in _key_chunks(n_keys, kc):
                    kk = k_ref[c0:c0 + w, ksl]
                    p = jnp.exp2(_nt(q4l, kk) - lse4)
                    dp = _nt(dob, v_ref[c0:c0 + w, vsl])
                    ds = (p * (dp - dl)).astype(BF16)
                    dq4 = dq4 + _nn(ds, kk)
                    ak[c0:c0 + w, ksl] += _tn(ds, q4s)
                    av[c0:c0 + w, vsl] += _tn(p.astype(BF16), dob)
                dq4 = dq4 * scale
                for j in range(group):
                    h = g * group + j
                    dq_ref[:, h * dq:(h + 1) * dq] = dq4[j * tq:(j + 1) * tq]

        is_ctx = i < al.nqc

        @pl.when(is_ctx)
        def _():
            run(al.Tc)

        @pl.when(jnp.logical_not(is_ctx))
        def _():
            run(al.S)

        @pl.when(i == al.nq - 1)
        def _():
            dk_ref[...] = ak[...].astype(dk_ref.dtype)
            dv_ref[...] = av[...].astype(dv_ref.dtype)

    hbm = pl.BlockSpec(memory_space=pl.ANY)
    res = _pallas(
        body, name=name + "_bwd", grid=(al.B, al.nq),
        in_specs=[qs(wq), ks(wk), ks(wv), qs(wo), qs(LANES), qs(wo)] + [hbm] * n,
        out_specs=[qs(wq), ks(wk), ks(wv)] + [hbm] * n,
        out_shape=[jax.ShapeDtypeStruct(q.shape, F32), jax.ShapeDtypeStruct(k.shape, k.dtype),
                   jax.ShapeDtypeStruct(v.shape, v.dtype)]
        + [jax.ShapeDtypeStruct(a.shape, a.dtype) for a in partials],
        scratch_shapes=[pltpu.VMEM((al.S, wk), F32), pltpu.VMEM((al.S, wv), F32)] + (_comm_scratch(n) if n else []),
        compiler_params=_cparams(("arbitrary", "arbitrary") if n else ("parallel", "arbitrary")),
    )(q, k, v, o, lse, do, *partials)
    return res[0], res[1], res[2], list(res[3:])


def attn_op(cfg, al, name):
    @jax.custom_vjp
    def op(q, k, v):
        return _attn_fwd(q, k, v, cfg, al, name)[0]

    def fwd(q, k, v):
        o, lse, _ = _attn_fwd(q, k, v, cfg, al, name)
        return o, (q, k, v, o, lse)

    def bwd(res, do):
        return _attn_bwd(*res, do, cfg, al, name)[:3]

    op.defvjp(fwd, bwd)
    return op


def attn_gather_op(cfg, al, name):
    @jax.custom_vjp
    def op(q, k, v, *shards):
        o, _, gathered = _attn_fwd(q, k, v, cfg, al, name, [s.astype(BF16) for s in shards])
        return (o, *gathered)

    def fwd(q, k, v, *shards):
        o, lse, gathered = _attn_fwd(q, k, v, cfg, al, name, [s.astype(BF16) for s in shards])
        return (o, *gathered), (q, k, v, o, lse)

    def bwd(res, cts):
        dq_, dk_, dv_, received = _attn_bwd(*res, cts[0], cfg, al, name, list(cts[1:]))
        return (dq_, dk_, dv_, *[sum_slots(r, name + f"_sum{a}") for a, r in enumerate(received)])

    op.defvjp(fwd, bwd)
    return op


SCAN_WIDTHS = (32, 16, 8, 4, 2, 1)
N_CM = 2 + 2 * len(SCAN_WIDTHS)


def _scan_consts(reverse):
    C = CHUNK
    t = np.arange(C)[:, None]
    s = np.arange(C)[None, :]
    blocks = [(s <= t), (s > t)]
    for w in SCAN_WIDTHS:
        blocks.append((s <= t) & (s // w == t // w))
    for w in SCAN_WIDTHS:
        blocks.append((s > t) & (s // w == t // w))
    masks = [np.eye(C, dtype=bool)]
    for w in SCAN_WIDTHS:
        masks.append((t // (2 * w) == s // (2 * w)) & ((t // w) % 2 == 1) & ((s // w) % 2 == 0))
    if reverse:
        blocks = [b[::-1, ::-1] for b in blocks]
        masks = [m[::-1, ::-1] for m in masks]
    cm = np.concatenate([b.astype(np.float32) for b in blocks] + [np.ones((8, C), np.float32)], axis=0)
    mw = np.stack([np.tile(m.astype(np.float32), (1, C_HEADS)) for m in masks])
    rows = np.arange(C_HEADS * C)[:, None] // C
    lane = np.arange(C_W)[None, :] // C_DK
    hm = (rows == lane).astype(np.float32)
    bd = (np.arange(C_W)[:, None] // C_DK == lane).astype(np.float32)
    return (jnp.asarray(cm, BF16), jnp.asarray(cm.T.copy(), BF16), jnp.asarray(mw, F32),
            jnp.asarray(hm, F32), jnp.asarray(bd, F32))


def _scan_chunk(st, q, k, v, g, cm, cmt, mw, hm, bd):
    C = CHUNK
    cs = xdotl(cm, cmt, g)
    b = cs[0:C]
    rest = cs[C:2 * C]
    tot = cs[N_CM * C:N_CM * C + 1]

    def per_head(a):
        return (jnp.concatenate([a] * C_HEADS, axis=0) * hm).astype(BF16)

    a = _nt(q.astype(BF16), per_head(k)) * mw[0]
    for i in range(len(SCAN_WIDTHS)):
        eq = jnp.exp(jnp.minimum(cs[(2 + i) * C:(3 + i) * C], 0.0))
        ek = jnp.exp(jnp.minimum(cs[(2 + len(SCAN_WIDTHS) + i) * C:(3 + len(SCAN_WIDTHS) + i) * C], 0.0))
        a = a + _nt((q * eq).astype(BF16), per_head(k * ek)) * mw[i + 1]
    o = _nn(a.astype(BF16), per_head(v))
    o = o + _nt((q * jnp.exp(b)).astype(BF16), st.astype(BF16))
    st_new = st * jnp.exp(tot) + _tn(v.astype(BF16), (k * jnp.exp(rest)).astype(BF16)) * bd
    return o, st_new


class ScanLay:
    def __init__(self, B, T, Tc):
        self.B, self.S = B, T + Tc
        self.ncc, self.ntot = Tc // CHUNK, (T + Tc) // CHUNK

    def chunk(self, j, reverse):
        if not reverse:
            return j
        return jnp.where(j < self.ncc, self.ncc - 1 - j, self.ntot - 1 - (j - self.ncc))


def _scan_specs(sl, step):
    f = pl.BlockSpec((sl.B, CHUNK, C_W), lambda j: (0, sl.chunk(step(j), False), 0))
    r = pl.BlockSpec((sl.B, CHUNK, C_W), lambda j: (0, sl.chunk(step(j), True), 0))
    return f, r


def _scan_fwd(q, kf, gf, kb, gb, v, sl, name):
    B, S = sl.B, sl.S
    view = lambda a: a.reshape(B, S, C_W)
    cf, cr = _scan_consts(False), _scan_consts(True)
    nc = len(cf)
    f, r = _scan_specs(sl, lambda j: j)
    cspecs = [pl.BlockSpec(c.shape, lambda j, nd=c.ndim: (0,) * nd) for c in cf + cr]

    def body(*refs):
        (qf_ref, kf_ref, gf_ref, vf_ref, qr_ref, kr_ref, gr_ref, vr_ref), refs = refs[:8], refs[8:]
        cfv, crv = [c[...] for c in refs[:nc]], [c[...] for c in refs[nc:2 * nc]]
        of_ref, or_ref, st_ref, st = refs[2 * nc:]

        @pl.when(pl.program_id(0) == 0)
        def _():
            st[...] = jnp.zeros_like(st)

        st_ref[0] = st[...]
        dirs = ((qf_ref, kf_ref, gf_ref, vf_ref, of_ref, cfv), (qr_ref, kr_ref, gr_ref, vr_ref, or_ref, crv))
        args = [(st[d * B + b], q_[b], k_[b], v_[b], g_[b]) + tuple(cv)
                for d, (q_, k_, g_, v_, _, cv) in enumerate(dirs) for b in range(B)]
        outs = [_scan_chunk(*a) for a in args]
        for d, (_, _, _, _, o_, _) in enumerate(dirs):
            for b in range(B):
                o_[b], st[d * B + b] = outs[d * B + b]

    of, ob, states = _pallas(
        body, name=name, grid=(sl.ntot,),
        in_specs=[f] * 4 + [r] * 4 + cspecs,
        out_specs=[f, r, pl.BlockSpec((1, 2 * B, C_W, C_W), lambda j: (j, 0, 0, 0))],
        out_shape=[jax.ShapeDtypeStruct((B, S, C_W), F32)] * 2
        + [jax.ShapeDtypeStruct((sl.ntot, 2 * B, C_W, C_W), F32)],
        scratch_shapes=[pltpu.VMEM((2 * B, C_W, C_W), F32)],
        compiler_params=_cparams(("arbitrary",)),
    )(view(q), view(kf), view(gf), view(v), view(q), view(kb), view(gb), view(v), *cf, *cr)
    return of.reshape(B * S, C_W), ob.reshape(B * S, C_W), states


def _scan_bwd(q, kf, gf, kb, gb, v, states, dof, dob, sl, name):
    B, S = sl.B, sl.S
    view = lambda a: a.reshape(B, S, C_W)
    cf, cr = _scan_consts(False), _scan_consts(True)
    nc = len(cf)
    last = sl.ntot - 1
    f, r = _scan_specs(sl, lambda j: last - j)
    cspecs = [pl.BlockSpec(c.shape, lambda j, nd=c.ndim: (0,) * nd) for c in cf + cr]

    def body(*refs):
        ins, refs = refs[:11], refs[11:]
        qf_ref, kf_ref, gf_ref, vf_ref, dof_ref, qr_ref, kr_ref, gr_ref, vr_ref, dor_ref, st_ref = ins
        cfv, crv = [c[...] for c in refs[:nc]], [c[...] for c in refs[nc:2 * nc]]
        outs, dst = refs[2 * nc:-1], refs[-1]

        @pl.when(pl.program_id(0) == 0)
        def _():
            dst[...] = jnp.zeros_like(dst)

        dirs = ((qf_ref, kf_ref, gf_ref, vf_ref, dof_ref, cfv), (qr_ref, kr_ref, gr_ref, vr_ref, dor_ref, crv))
        args = [((st_ref[0, d * B + b], q_[b], k_[b], v_[b], g_[b]), (do_[b], dst[d * B + b]), cv)
                for d, (q_, k_, g_, v_, do_, cv) in enumerate(dirs) for b in range(B)]
        grads = []
        for prim, cts, cv in args:
            _, vjp = jax.vjp(lambda s_, a_, b_, c_, e_, cv=cv: _scan_chunk(s_, a_, b_, c_, e_, *cv), *prim)
            grads.append(vjp(cts))
        for d in range(2):
            dq_, dk_, dg_, dv_ = outs[4 * d:4 * d + 4]
            for b in range(B):
                dst[d * B + b], dq_[b], dk_[b], dv_[b], dg_[b] = grads[d * B + b]

    res = _pallas(
        body, name=name + "_bwd", grid=(sl.ntot,),
        in_specs=[f] * 5 + [r] * 5 + [pl.BlockSpec((1, 2 * B, C_W, C_W), lambda j: (last - j, 0, 0, 0))] + cspecs,
        out_specs=[f] * 4 + [r] * 4,
        out_shape=[jax.ShapeDtypeStruct((B, S, C_W), F32)] * 8,
        scratch_shapes=[pltpu.VMEM((2 * B, C_W, C_W), F32)],
        compiler_params=_cparams(("arbitrary",)),
    )(view(q), view(kf), view(gf), view(v), view(dof), view(q), view(kb), view(gb), view(v), view(dob),
      states, *cf, *cr)
    dq_f, dk_f, dg_f, dv_f, dq_r, dk_r, dg_r, dv_r = [a.reshape(B * S, C_W) for a in res]
    return dq_f + dq_r, dk_f, dg_f, dk_r, dg_r, dv_f + dv_r


def scan_op(sl, name):
    @jax.custom_vjp
    def op(q, kf, gf, kb, gb, v):
        return _scan_fwd(q, kf, gf, kb, gb, v, sl, name)[:2]

    def fwd(q, kf, gf, kb, gb, v):
        of, ob, states = _scan_fwd(q, kf, gf, kb, gb, v, sl, name)
        return (of, ob), (q, kf, gf, kb, gb, v, states)

    def bwd(res, cts):
        return _scan_bwd(*res, cts[0], cts[1], sl, name)

    op.defvjp(fwd, bwd)
    return op


def loss_and_grad(y, target, lay):
    N, D = y.shape

    def body(y_ref, t_ref, dy_ref, l_ref):
        i = pl.program_id(0)

        @pl.when(i == 0)
        def _():
            l_ref[...] = jnp.zeros_like(l_ref)

        is_ctx = i % lay.per < lay.nc

        @pl.when(is_ctx)
        def _():
            dy_ref[...] = jnp.zeros_like(dy_ref)

        @pl.when(jnp.logical_not(is_ctx))
        def _():
            e = y_ref[...] - t_ref[...]
            dy_ref[...] = e * (1.0 / D)
            l_ref[...] += 0.5 * jnp.sum(jnp.sum(e * e, axis=1, keepdims=True) * (1.0 / D), axis=0, keepdims=True)

    def t_index(i):
        return ((i // lay.per) * lay.nl + jnp.maximum(i % lay.per - lay.nc, 0), 0)

    dy, lp = _pallas(
        body, name="loss_head", grid=(lay.n_tiles,),
        in_specs=[pl.BlockSpec((lay.tm, D), lambda i: (i, 0)), pl.BlockSpec((lay.tm, D), t_index)],
        out_specs=[pl.BlockSpec((lay.tm, D), lambda i: (i, 0)), pl.BlockSpec((8, LANES), lambda i: (0, 0))],
        out_shape=[jax.ShapeDtypeStruct((N, D), F32), jax.ShapeDtypeStruct((8, LANES), F32)],
        compiler_params=_cparams(("arbitrary",)),
    )(y, target)
    return lp, dy


def _adam_math(w, g, m, v):
    mn = ADAM_B1 * m + (1.0 - ADAM_B1) * g
    vn = ADAM_B2 * v + (1.0 - ADAM_B2) * jnp.square(g)
    m_hat = mn / (1.0 - ADAM_B1 ** ADAM_STEP)
    v_hat = vn / (1.0 - ADAM_B2 ** ADAM_STEP)
    return -ADAM_LR * (m_hat / (jnp.sqrt(v_hat) + ADAM_EPS) + ADAM_WD * w), mn, vn


ROW_TILES = (512, 256, 128, 64, 32, 16, 8)


def adamw(w, g, m, v, name):
    R, C = w.shape
    tr = _tile(R, ROW_TILES)

    def body(w_ref, g_ref, m_ref, v_ref, d_ref, mo_ref, vo_ref):
        d_ref[...], mo_ref[...], vo_ref[...] = _adam_math(w_ref[...], g_ref[...], m_ref[...], v_ref[...])

    spec = pl.BlockSpec((tr, C), lambda i: (i, 0))
    return _pallas(
        body, name=name, grid=(R // tr,), in_specs=[spec] * 4, out_specs=[spec] * 3,
        out_shape=[jax.ShapeDtypeStruct((R, C), F32)] * 3,
        compiler_params=_cparams(("parallel",)),
    )(w, g, m, v)


def sum_slots(recv, name):
    shape = recv.shape[1:]
    C = shape[-1]
    r3 = recv.reshape(N_DEV, -1, C)
    R = r3.shape[1]
    tr = _tile(R, ROW_TILES[1:])

    def body(r_ref, o_ref):
        g = r_ref[0].astype(F32)
        for k in range(1, N_DEV):
            g = g + r_ref[k].astype(F32)
        o_ref[...] = g

    out = _pallas(
        body, name=name, grid=(R // tr,),
        in_specs=[pl.BlockSpec((N_DEV, tr, C), lambda i: (0, i, 0))],
        out_specs=pl.BlockSpec((tr, C), lambda i: (i, 0)),
        out_shape=jax.ShapeDtypeStruct((R, C), F32),
        compiler_params=_cparams(("parallel",)),
    )(r3)
    return out.reshape(shape)


def adamw_slots(w, recv, m, v, name):
    R, C = w.shape
    tr = _tile(R, ROW_TILES[1:])

    def body(w_ref, r_ref, m_ref, v_ref, g_ref, d_ref, mo_ref, vo_ref):
        g = r_ref[0].astype(F32)
        for k in range(1, N_DEV):
            g = g + r_ref[k].astype(F32)
        g_ref[...] = g
        d_ref[...], mo_ref[...], vo_ref[...] = _adam_math(w_ref[...], g, m_ref[...], v_ref[...])

    spec = pl.BlockSpec((tr, C), lambda i: (i, 0))
    return _pallas(
        body, name=name, grid=(R // tr,),
        in_specs=[spec, pl.BlockSpec((N_DEV, tr, C), lambda i: (0, i, 0)), spec, spec], out_specs=[spec] * 4,
        out_shape=[jax.ShapeDtypeStruct((R, C), F32)] * 4,
        compiler_params=_cparams(("parallel",)),
    )(w, recv, m, v)


def _me():
    return lax.axis_index("x"), lax.axis_index("y"), lax.axis_index("c")


def _gather_many(x_refs, out_refs, send_sems, recv_sems, local_sems):
    x, y, c = _me()
    me, sibling = (x, y, c), (x, y, 1 - c)
    chips = [(1 - x, y), (x, 1 - y), (1 - x, 1 - y)]
    arrs = range(len(x_refs))

    def slot(a, px, py, pc):
        return out_refs[a].at[4 * px + 2 * py + pc]

    def copy(a, k, block, to, src=None):
        return pltpu.make_async_remote_copy(
            src_ref=slot(a, *block) if src is None else src, dst_ref=slot(a, *block),
            send_sem=send_sems.at[7 * a + k], recv_sem=recv_sems.at[7 * a + k], device_id=to, device_id_type=MESH)

    mine = [pltpu.make_async_copy(x_refs[a], slot(a, *me), local_sems.at[a]) for a in arrs]
    for cp in mine:
        cp.start()
    first = []
    for a in arrs:
        first.append(copy(a, 0, me, sibling, src=x_refs[a]))
        first += [copy(a, 1 + j, me, (*chip, c), src=x_refs[a]) for j, chip in enumerate(chips)]
    for cp in first:
        cp.start()
    passed = []
    for j, chip in enumerate(chips):
        for a in arrs:
            copy(a, 1 + j, (*chip, c), me).wait_recv()
            fwd = copy(a, 4 + j, (*chip, c), sibling)
            fwd.start()
            passed.append(fwd)
    for a in arrs:
        copy(a, 0, sibling, me).wait_recv()
    for j, chip in enumerate(chips):
        for a in arrs:
            copy(a, 4 + j, (*chip, 1 - c), me).wait_recv()
    for cp in first + passed:
        cp.wait_send()
    for cp in mine:
        cp.wait()


def _comm_scratch(n):
    return [pltpu.SemaphoreType.DMA((7 * n,)), pltpu.SemaphoreType.DMA((7 * n,)), pltpu.SemaphoreType.DMA((n,))]


def small_gather(xb, name):
    R = xb.shape[0]

    def body(x_ref, out_ref, sum_ref, send_sems, recv_sems, local_sems):
        _gather_many([x_ref], [out_ref], send_sems, recv_sems, local_sems)
        acc = out_ref[0]
        for k in range(1, N_DEV):
            acc = acc + out_ref[k]
        sum_ref[...] = acc

    vm = pl.BlockSpec(memory_space=pltpu.VMEM)
    return _pallas(
        body, name=name, in_specs=[vm], out_specs=[vm, vm],
        out_shape=[jax.ShapeDtypeStruct((N_DEV, R, LANES), xb.dtype), jax.ShapeDtypeStruct((R, LANES), xb.dtype)],
        scratch_shapes=_comm_scratch(1),
        compiler_params=pltpu.CompilerParams(vmem_limit_bytes=VMEM_LIMIT),
    )(xb)


def big_gather(xs, name):
    n = len(xs)

    def body(*refs):
        _gather_many(refs[:n], refs[n:2 * n], *refs[2 * n:])

    hbm = pl.BlockSpec(memory_space=pl.ANY)
    return _pallas(
        body, name=name, in_specs=[hbm] * n, out_specs=[hbm] * n,
        out_shape=[jax.ShapeDtypeStruct((N_DEV,) + a.shape, a.dtype) for a in xs],
        scratch_shapes=_comm_scratch(n),
    )(*xs)


def scatter_exchange(gs, name):
    n = len(gs)
    rels = [(dx, dy, dc) for dx in (0, 1) for dy in (0, 1) for dc in (0, 1) if (dx, dy, dc) != (0, 0, 0)]

    def body(*refs):
        g_refs, r_refs = refs[:n], refs[n:2 * n]
        send_sems, recv_sems, local_sems = refs[2 * n:]
        x, y, c = _me()
        me = 4 * x + 2 * y + c
        mine = [pltpu.make_async_copy(g_refs[a].at[me], r_refs[a].at[me], local_sems.at[a]) for a in range(n)]
        for cp in mine:
            cp.start()
        copies = []
        for r, (dx, dy, dc) in enumerate(rels):
            px, py, pc = (x + dx) % 2, (y + dy) % 2, (c + dc) % 2
            for a in range(n):
                copies.append(pltpu.make_async_remote_copy(
                    src_ref=g_refs[a].at[4 * px + 2 * py + pc], dst_ref=r_refs[a].at[me],
                    send_sem=send_sems.at[7 * a + r], recv_sem=recv_sems.at[7 * a + r],
                    device_id=(px, py, pc), device_id_type=MESH))
        for cp in copies:
            cp.start()
        for cp in copies:
            cp.wait()
        for cp in mine:
            cp.wait()

    hbm = pl.BlockSpec(memory_space=pl.ANY)
    return _pallas(
        body, name=name, in_specs=[hbm] * n, out_specs=[hbm] * n,
        out_shape=[jax.ShapeDtypeStruct(a.shape, a.dtype) for a in gs],
        scratch_shapes=_comm_scratch(n),
    )(*gs)


def _pack(arrs, dtype, row_mult):
    flat = jnp.concatenate([a.astype(dtype).reshape(-1) for a in arrs])
    pad = (-flat.shape[0]) % (LANES * row_mult)
    if pad:
        flat = jnp.concatenate([flat, jnp.zeros((pad,), dtype)])
    return flat.reshape(-1, LANES)


def _unpack(buf, shapes, lead=()):
    flat = buf.reshape(*lead, -1)
    out, off = [], 0
    for s in shapes:
        n = int(np.prod(s))
        out.append(flat[..., off:off + n].reshape(*lead, *s))
        off += n
    return out


def _rope_tables(T, tm):
    pos = np.arange(T)
    row, col = pos // GRID_W, pos % GRID_W

    def tab(rot_dim):
        nf = rot_dim // 4
        inv = ROPE_THETA ** (-np.arange(nf, dtype=np.float32) / nf)
        ang = np.concatenate([row[:, None].astype(np.float32) * inv, col[:, None].astype(np.float32) * inv], axis=-1)
        ang = ang.astype(np.float32)
        cos, sin = np.cos(ang), np.sin(ang)
        return np.concatenate([cos, cos], -1), np.concatenate([-sin, sin], -1)

    c64, s64 = tab(HD)
    c32, s32 = tab(B_ROPE)
    ca, sa = np.tile(c64, (1, A_HEADS)), np.tile(s64, (1, A_HEADS))
    cb = np.concatenate([c32, np.ones((T, LANES - B_ROPE), np.float32)], -1)
    sb = np.concatenate([s32, np.zeros((T, LANES - B_ROPE), np.float32)], -1)
    one, zero = np.ones((T, B_NOPE), np.float32), np.zeros((T, B_NOPE), np.float32)
    tail1, tail0 = np.ones((T, LANES - B_NOPE - B_ROPE), np.float32), np.zeros((T, LANES - B_NOPE - B_ROPE), np.float32)
    cq = np.tile(np.concatenate([one, c32, tail1], -1), (1, B_HEADS))
    sq = np.tile(np.concatenate([zero, s32, tail0], -1), (1, B_HEADS))

    def fin(a, ident):
        return jnp.asarray(np.concatenate([a, np.full((tm, a.shape[1]), ident, np.float32)], 0), F32)

    return fin(ca, 1.0), fin(sa, 0.0), fin(cb, 1.0), fin(sb, 0.0), fin(cq, 1.0), fin(sq, 0.0)


def _swap_matrix(width, starts, half):
    p = np.zeros((width, width), np.float32)
    for s in starts:
        for i in range(half):
            p[s + i, s + half + i] = 1.0
            p[s + half + i, s + i] = 1.0
    return jnp.asarray(p, BF16)


def _seg_matrix(width):
    h = np.arange(width) // HD
    return jnp.asarray((h[:, None] == h[None, :]).astype(np.float32), BF16)


def _key_slot_matrices():
    e1 = np.zeros((B_HEADS * B_NOPE, B_HEADS * LANES), np.float32)
    e2 = np.zeros((LANES, B_HEADS * LANES), np.float32)
    for h in range(B_HEADS):
        for i in range(B_NOPE):
            e1[h * B_NOPE + i, h * LANES + i] = 1.0
        for i in range(B_ROPE):
            e2[i, h * LANES + B_NOPE + i] = 1.0
    return jnp.asarray(e1, BF16), jnp.asarray(e2, BF16)


def _f_premod(x, sh, sc, g):
    return (_rms(x, g) * (1.0 + sc) + sh,)


def _f_post(x, y, gt, g):
    return (x + gt * _rms(y, g),)


def _f_post_pre(x, y, gt, g_post, sh, sc, g_pre):
    x1 = x + gt * _rms(y, g_post)
    return x1, _rms(x1, g_pre) * (1.0 + sc) + sh


def _f_bias(raw, b):
    return (raw + b,)


def _f_silu(x):
    return (_silu(x),)


def _f_readout(of, ob, gate, gain, seg):
    return (_head_rms(of + ob, seg, gain) * _silu(gate),)


def _f_bq(bq, cq, sq, pq):
    return (bq * cq + xdotr2(bq, pq) * sq,)


def _f_bk(bkn, bkr, e1, e2):
    return (xdotr1(bkn, e1) + xdotr1(bkr, e2),)


def _make_f_feat(layer):
    def f(feat, ca, sa, cb, sb, gaq, gak, gbq, gbkv, c00, c01, c10, c11, seg, pa, pb):
        aq = _head_rms(feat[:, 0:512], seg, gaq)
        ak = _head_rms(feat[:, 512:640], seg[0:128, 0:128], gak)
        av = feat[:, 640:768]
        aq = aq * ca + xdotr2(aq, pa) * sa
        ak = ak * ca[:, 0:128] + xdotr2(ak, pa[0:128, 0:128]) * sa[:, 0:128]
        bqn = _rms(feat[:, 768:1024], gbq, B_QR)
        bkvn = _rms(feat[:, 1024:1152], gbkv)
        bkr = feat[:, 1152:1280]
        bkr = bkr * cb + xdotr2(bkr, pb) * sb
        cq = _silu(feat[:, 1280:1536])
        zf, zb = feat[:, 1536:1792], feat[:, 1792:2048]
        if layer == 0:
            lbf = lbb = 0.0
        else:
            def share(c0, c1):
                m = jnp.maximum(c0, c1)
                e0, e1 = jnp.exp(c0 - m), jnp.exp(c1 - m)
                return e1 / (e0 + e1)
            lbf, lbb = share(c00, c10), share(c01, c11)

        def gate(z, lb):
            f_ = lb + (1.0 - lb) * _sigmoid(z)
            return (1.0 - lb) * _sigmoid(-z), jnp.log(jnp.maximum(f_, F_TINY))

        kf, gf = gate(zf, lbf)
        kb, gb = gate(zb, lbb)
        return aq, ak, av, bqn, bkvn, bkr, cq, kf, gf, kb, gb, feat[:, 2048:2304], feat[:, 2304:2560]

    return f


def _pad_w_in(w):
    z = lambda n: jnp.zeros((w.shape[0], n), w.dtype)
    return jnp.concatenate([w[:, 0:960], z(64), w[:, 960:1120], z(96), w[:, 1120:2400]], axis=1)


def _pad_w_q_up(w):
    w4 = w.reshape(B_QR, B_HEADS, B_NOPE + B_ROPE)
    w4 = jnp.pad(w4, ((0, 256 - B_QR), (0, 0), (0, LANES - B_NOPE - B_ROPE)))
    return w4.reshape(256, B_HEADS * LANES)


def _split_w_kv_up(w):
    w4 = w.reshape(B_KVR, B_HEADS, B_NOPE + B_V)
    return w4[:, :, :B_NOPE].reshape(B_KVR, -1), w4[:, :, B_NOPE:].reshape(B_KVR, -1)


def _tile_gain(g, reps, width=None):
    t = jnp.tile(g, reps)
    if width is not None and width > t.shape[0]:
        t = jnp.pad(t, (0, width - t.shape[0]))
    return t[None, :]


def local_forward(dims, p):
    B, T, Tc, D = dims
    tm = min(256, Tc)
    lay_all = Lay(B, T, Tc, tm)
    ca, sa, cb, sb, cq, sq = _rope_tables(T, tm)
    seg512, seg256 = _seg_matrix(512), _seg_matrix(C_W)
    pa = _swap_matrix(512, range(0, 512, HD), HD // 2)
    pb = _swap_matrix(LANES, [0], B_ROPE // 2)
    pq = _swap_matrix(512, [h * LANES + B_NOPE for h in range(B_HEADS)], B_ROPE // 2)
    e1, e2 = _key_slot_matrices()
    sl = ScanLay(B, T, Tc)
    al_a, al_b = AttnLay(B, T, Tc, min(256, Tc)), AttnLay(B, T, Tc, min(256, Tc))
    cfg_a = (A_KV, A_GROUP, HD, HD, HD ** -0.5, 512, 1024)
    cfg_b = (B_HEADS, 1, LANES, B_V, (B_NOPE + B_ROPE) ** -0.5, 512, T + Tc)

    tok = p["tok"]
    depth = p["modraw"].shape[0]
    mods = []
    for l in range(depth):
        bias_lay = Lay(1, 8, 0, 8)
        raw8 = jnp.pad(p["modraw"][l], ((0, 8 - B - 1), (0, 0)))
        mod = ew_op(_f_bias, bias_lay, ("tok", "par"), (True, True), ((6 * D, F32),), f"l{l}_ada_bias")(
            raw8, p["b_ada"][l][None, :])[0]
        seg_rows = jnp.concatenate([mod[0:B], jnp.broadcast_to(mod[B:B + 1], (B, 6 * D))], axis=0)[:, None, :]
        mods.append([seg_rows[:, :, i * D:(i + 1) * D] for i in range(6)])

    post_pre_kinds = ("tok", "tok", "seg", "par", "seg", "seg", "par")
    h = ew_op(_f_premod, lay_all, ("tok", "seg", "seg", "par"), (True,) * 4, ((D, BF16),), "l0_premix")(
        tok, mods[0][0], mods[0][1], p["g_pre_mix"][0][None, :])[0]
    for l in range(depth):
        tag = f"l{l}_"
        sh_m, sc_m, gt_m, sh_f, sc_f, gt_f = mods[l]

        def early(n):
            return p[n][0] if l == 0 else late[n + REST][l - 1]

        feat = matmul_op(tag + "w_in")(h, _pad_w_in(early("w_in")))
        clb = p["clb"]
        feats = ew_op(
            _make_f_feat(l), lay_all,
            ("tok", "pos", "pos", "pos", "pos") + ("par",) * 11,
            (True,) + (False,) * 4 + (True,) * 8 + (False,) * 3,
            ((512, F32), (128, BF16), (128, BF16), (256, BF16), (128, BF16), (128, F32)) + ((C_W, F32),) * 7,
            tag + "feat")(
            feat, ca, sa, cb, sb,
            _tile_gain(p["a_q_norm"][l], A_HEADS), _tile_gain(p["a_k_norm"][l], A_KV),
            _tile_gain(p["b_q_norm"][l], 1, 256), _tile_gain(p["b_kv_norm"][l], 1),
            clb[0, 0][None, :], clb[0, 1][None, :], clb[1, 0][None, :], clb[1, 1][None, :],
            seg512, pa, pb)
        aq, ak, av, bqn, bkvn, bkr, cqs, kf, gf, kb, gb, cv, cgate = feats
        bq = matmul_op(tag + "w_q_up")(bqn, _pad_w_q_up(early("w_q_up")))
        bq = ew_op(_f_bq, lay_all, ("tok", "pos", "pos", "par"), (True, False, False, False), ((512, F32),),
                   tag + "bq_rope")(bq, cq, sq, pq)[0]
        w_kn, w_v = _split_w_kv_up(early("w_kv_up"))
        bkn = matmul_op(tag + "w_k_up")(bkvn, w_kn)
        bv = matmul_op(tag + "w_v_up", BF16)(bkvn, w_v)
        bk = ew_op(_f_bk, lay_all, ("tok", "tok", "par", "par"), (True, True, False, False),
                   ((B_HEADS * LANES, BF16),), tag + "bk_slots")(bkn, bkr, e1, e2)[0]

        if l == 0:
            ya, *got = attn_gather_op(cfg_a, al_a, tag + "attn_a")(aq, ak, av, *[p["shard_" + n] for n in RIDE])
            late = {n: _assemble(n, g) for n, g in zip(RIDE, got)}
        else:
            ya = attn_op(cfg_a, al_a, tag + "attn_a")(aq, ak, av)
        yb = attn_op(cfg_b, al_b, tag + "attn_b")(bq, bk, bv)
        of, ob = scan_op(sl, tag + "scan")(cqs, kf, gf, kb, gb, cv)
        lay_out = lay_all
        yc = ew_op(_f_readout, lay_out, ("tok", "tok", "tok", "par", "par"), (True, True, True, True, False),
                   ((C_W, BF16),), tag + "readout")(of, ob, cgate, _tile_gain(p["c_out_norm"][l], C_HEADS), seg256)[0]
        ycat = jnp.concatenate([ya, yb, yc], axis=1)
        mixo = matmul_op(tag + "w_out")(ycat, late["w_out"][l])
        tok, hf = ew_op(_f_post_pre, lay_out, post_pre_kinds, (True,) * 7, ((D, F32), (D, BF16)),
                        tag + "postmix_preffn")(
            tok, mixo, gt_m, p["g_post_mix"][l][None, :], sh_f, sc_f, p["g_pre_ffn"][l][None, :])

        z = matmul_op(tag + "w_ff1", BF16, relu2=True)(hf, late["w_ff1"][l])
        yf = matmul_op(tag + "w_ff2")(z, late["w_ff2"][l])
        if l + 1 < depth:
            tok, h = ew_op(_f_post_pre, lay_out, post_pre_kinds, (True,) * 7, ((D, F32), (D, BF16)),
                           tag + "postffn_premix")(
                tok, yf, gt_f, p["g_post_ffn"][l][None, :], mods[l + 1][0], mods[l + 1][1],
                p["g_pre_mix"][l + 1][None, :])
        else:
            tok = ew_op(_f_post, lay_out, ("tok", "tok", "seg", "par"), (True,) * 4, ((D, F32),), tag + "postffn")(
                tok, yf, gt_f, p["g_post_ffn"][l][None, :])[0]
    return tok


EARLY = ("w_in", "w_q_up", "w_kv_up")
LATE = ("w_out", "w_ff1", "w_ff2")
REST = "_rest"
RIDE = LATE + tuple(n + REST for n in EARLY)
COL_SHARDED = ("w_in", "w_q_up", "w_kv_up", "w_ff1")
SMALL = ("c_ctx", "b_ada", "g_pre_mix", "g_post_mix", "g_pre_ffn", "g_post_ffn", "a_q_norm", "a_k_norm",
         "b_q_norm", "b_kv_norm", "c_out_norm")
WEIGHTS = ("c_ctx", "w_ada", "b_ada", "g_pre_mix", "g_post_mix", "g_pre_ffn", "g_post_ffn", "w_in", "a_q_norm",
           "a_k_norm", "b_q_norm", "w_q_up", "b_kv_norm", "w_kv_up", "c_lower_bounds", "c_out_norm", "w_out",
           "w_ff1", "w_ff2")
SMALL_ROWS = 64


def _assemble(name, a):
    if name.removesuffix(REST) in COL_SHARDED:
        return a.transpose(1, 2, 0, 3).reshape(a.shape[1], a.shape[2], N_DEV * a.shape[3])
    return a.transpose(1, 0, 2, 3).reshape(a.shape[1], N_DEV * a.shape[2], a.shape[3])


def kernel(x, c, ctx, c_ctx, w_ada, b_ada, g_pre_mix, g_post_mix, g_pre_ffn, g_post_ffn, w_in, a_q_norm, a_k_norm, b_q_norm, w_q_up, b_kv_norm, w_kv_up, c_lower_bounds, c_out_norm, w_out, w_ff1, w_ff2, loss_target, m_c_ctx, m_w_ada, m_b_ada, m_g_pre_mix, m_g_post_mix, m_g_pre_ffn, m_g_post_ffn, m_w_in, m_a_q_norm, m_a_k_norm, m_b_q_norm, m_w_q_up, m_b_kv_norm, m_w_kv_up, m_c_lower_bounds, m_c_out_norm, m_w_out, m_w_ff1, m_w_ff2, v_c_ctx, v_w_ada, v_b_ada, v_g_pre_mix, v_g_post_mix, v_g_pre_ffn, v_g_post_ffn, v_w_in, v_a_q_norm, v_a_k_norm, v_b_q_norm, v_w_q_up, v_b_kv_norm, v_w_kv_up, v_c_lower_bounds, v_c_out_norm, v_w_out, v_w_ff1, v_w_ff2):
    W = dict(c_ctx=c_ctx, w_ada=w_ada, b_ada=b_ada, g_pre_mix=g_pre_mix, g_post_mix=g_post_mix, g_pre_ffn=g_pre_ffn,
             g_post_ffn=g_post_ffn, w_in=w_in, a_q_norm=a_q_norm, a_k_norm=a_k_norm, b_q_norm=b_q_norm,
             w_q_up=w_q_up, b_kv_norm=b_kv_norm, w_kv_up=w_kv_up, c_lower_bounds=c_lower_bounds,
             c_out_norm=c_out_norm, w_out=w_out, w_ff1=w_ff1, w_ff2=w_ff2)
    M = dict(c_ctx=m_c_ctx, w_ada=m_w_ada, b_ada=m_b_ada, g_pre_mix=m_g_pre_mix, g_post_mix=m_g_post_mix,
             g_pre_ffn=m_g_pre_ffn, g_post_ffn=m_g_post_ffn, w_in=m_w_in, a_q_norm=m_a_q_norm, a_k_norm=m_a_k_norm,
             b_q_norm=m_b_q_norm, w_q_up=m_w_q_up, b_kv_norm=m_b_kv_norm, w_kv_up=m_w_kv_up,
             c_lower_bounds=m_c_lower_bounds, c_out_norm=m_c_out_norm, w_out=m_w_out, w_ff1=m_w_ff1, w_ff2=m_w_ff2)
    V = dict(c_ctx=v_c_ctx, w_ada=v_w_ada, b_ada=v_b_ada, g_pre_mix=v_g_pre_mix, g_post_mix=v_g_post_mix,
             g_pre_ffn=v_g_pre_ffn, g_post_ffn=v_g_post_ffn, w_in=v_w_in, a_q_norm=v_a_q_norm, a_k_norm=v_a_k_norm,
             b_q_norm=v_b_q_norm, w_q_up=v_w_q_up, b_kv_norm=v_b_kv_norm, w_kv_up=v_w_kv_up,
             c_lower_bounds=v_c_lower_bounds, c_out_norm=v_c_out_norm, w_out=v_w_out, w_ff1=v_w_ff1, w_ff2=v_w_ff2)

    B, T, D = x.shape
    Tc = ctx.shape[1]
    depth = w_ada.shape[0]
    ada_cols = w_ada.shape[2]
    idx = 4 * lax.axis_index("x") + 2 * lax.axis_index("y") + lax.axis_index("c")
    n_cond = N_DEV * B
    cond_rows = -(-(n_cond + 1) // 8) * 8

    clb_cols = c_lower_bounds.shape[2]
    g1, _ = small_gather(_pack([c, c_lower_bounds], F32, 8), "gather_cond")
    c_parts, clb_parts = _unpack(g1, [c.shape, c_lower_bounds.shape], lead=(N_DEV,))
    c_all = c_parts.reshape(n_cond, D)
    clb_full = clb_parts.transpose(1, 2, 0, 3).reshape(depth, 2, N_DEV * clb_cols)

    cond_lay = Lay(1, cond_rows, 0, cond_rows)

    def ada_shard(c_ctx_, w_ada_):
        cond = jnp.concatenate([c_all, c_ctx_[None, :], jnp.zeros((cond_rows - n_cond - 1, D), F32)], axis=0)
        sc = ew_op(_f_silu, cond_lay, ("tok",), (True,), ((D, F32),), "cond_silu")(cond)[0]
        return jnp.stack([matmul_op(f"l{l}_w_ada")(sc, w_ada_[l]) for l in range(depth)])

    mod_shard, vjp_ada = jax.vjp(ada_shard, c_ctx, w_ada)

    g2, _ = small_gather(_pack([mod_shard], F32, 8), "gather_mod")
    mod_all = _unpack(g2, [mod_shard.shape], lead=(N_DEV,))[0]
    mod_all = mod_all.transpose(1, 2, 0, 3).reshape(depth, cond_rows, N_DEV * ada_cols)
    mine = lax.dynamic_slice_in_dim(mod_all, idx * B, B, axis=1)
    modraw = jnp.concatenate([mine, mod_all[:, n_cond:n_cond + 1]], axis=1)

    gathered = dict(zip(EARLY, big_gather([W[n][0:1].astype(BF16) for n in EARLY], "gather_weights")))

    dims = (B, T, Tc, D)
    small_names = [n for n in SMALL if n != "c_ctx"]
    small_in = {n: W[n] for n in small_names}
    late_in = {n: W[n] for n in LATE}
    late_in.update({n + REST: W[n][1:] for n in EARLY})

    def fwd(x_, modraw_, small_, clb_, gathered_, late_):
        p = dict(small_)
        p.update({n: _assemble(n, a) for n, a in gathered_.items()})
        p.update({"shard_" + n: a for n, a in late_.items()})
        p.update(tok=jnp.concatenate([ctx, x_], axis=1).reshape(B * (Tc + T), D), modraw=modraw_, clb=clb_)
        return local_forward(dims, p)

    y, vjp_main = jax.vjp(fwd, x, modraw, small_in, clb_full, gathered, late_in)
    loss_part, dy = loss_and_grad(y, loss_target.reshape(B * T, D), Lay(B, T, Tc, min(256, Tc)))
    dx, dmodraw, dsmall, dclb, dgathered, dlate = vjp_main(dy)

    pay3 = _pack([dmodraw] + [dsmall[n] for n in small_names] + [dclb, loss_part[0, 0:1]], F32, 8)
    g3, s3 = small_gather(pay3, "gather_small_grads")
    dmod_parts = _unpack(g3, [dmodraw.shape], lead=(N_DEV,))[0]
    tot = _unpack(s3, [dmodraw.shape] + [W[n].shape for n in small_names] + [clb_full.shape, (1,)])
    dmod_tot, small_tot, dclb_tot, loss = tot[0], dict(zip(small_names, tot[1:-2])), tot[-2], tot[-1]
    drows = dmod_parts[:, :, 0:B].transpose(1, 0, 2, 3).reshape(depth, n_cond, N_DEV * ada_cols)
    dcond = jnp.concatenate(
        [drows, dmod_tot[:, B:B + 1], jnp.zeros((depth, cond_rows - n_cond - 1, N_DEV * ada_cols), F32)], axis=1)
    dmod_shard = lax.dynamic_slice_in_dim(dcond, idx * ada_cols, ada_cols, axis=2)
    dc_ctx_part, dw_ada = vjp_ada(dmod_shard)

    _, s4 = small_gather(_pack([dc_ctx_part], F32, 8), "gather_c_ctx_grad")
    small_tot["c_ctx"] = _unpack(s4, [c_ctx.shape])[0]

    recv = dict(zip(EARLY, scatter_exchange([dgathered[n] for n in EARLY], "scatter_grads")))
    grads, delta, new_m, new_v = dict(small_tot), {}, {}, {}
    for n in EARLY:
        cols = W[n].shape[-1]
        flat = lambda a: a.reshape(-1, cols)
        first = adamw_slots(flat(W[n][0:1]), recv[n].reshape(N_DEV, -1, cols), flat(M[n][0:1]), flat(V[n][0:1]),
                            "adamw_" + n + "_first")
        g_rest = dlate[n + REST]
        rest = adamw(flat(W[n][1:]), flat(g_rest), flat(M[n][1:]), flat(V[n][1:]), "adamw_" + n + REST)
        join = lambda a, b: jnp.concatenate([a.reshape(W[n][0:1].shape), b.reshape(W[n][1:].shape)], axis=0)
        grads[n] = join(first[0], g_rest)
        delta[n], new_m[n], new_v[n] = (join(a, b) for a, b in zip(first[1:], rest))
    for n in LATE:
        shape, cols = W[n].shape, W[n].shape[-1]
        flat = lambda a: a.reshape(-1, cols)
        grads[n] = dlate[n]
        res = adamw(flat(W[n]), flat(dlate[n]), flat(M[n]), flat(V[n]), "adamw_" + n)
        delta[n], new_m[n], new_v[n] = (a.reshape(shape) for a in res)
    grads["w_ada"] = dw_ada
    grads["c_lower_bounds"] = lax.dynamic_slice_in_dim(dclb_tot, idx * clb_cols, clb_cols, axis=2)

    flat_a = lambda a: a.reshape(-1, ada_cols)
    res = adamw(flat_a(w_ada), flat_a(dw_ada), flat_a(m_w_ada), flat_a(v_w_ada), "adamw_w_ada")
    delta["w_ada"], new_m["w_ada"], new_v["w_ada"] = (a.reshape(w_ada.shape) for a in res)
    names = list(SMALL) + ["c_lower_bounds"]
    shapes = [W[n].shape for n in names]
    res = adamw(*[_pack([src[n] for n in names], F32, SMALL_ROWS) for src in (W, grads, M, V)], "adamw_small")
    for dst, buf in zip((delta, new_m, new_v), res):
        dst.update(zip(names, _unpack(buf, shapes)))

    return (loss.reshape(()), dx, *[grads[n] for n in WEIGHTS], *[delta[n] for n in WEIGHTS],
            *[new_m[n] for n in WEIGHTS], *[new_v[n] for n in WEIGHTS])
```

```python
import math

import numpy as np

import jax
import jax.numpy as jnp
from jax import lax
from jax.experimental import pallas as pl
from jax.experimental.pallas import tpu as pltpu

F32 = jnp.float32
BF16 = jnp.bfloat16

A_HEADS, A_KV, HD = 8, 2, 64
A_GROUP = A_HEADS // A_KV
B_HEADS, B_QR, B_KVR, B_NOPE, B_ROPE, B_V = 4, 192, 128, 64, 32, 64
C_HEADS, C_DK = 4, 64
C_W = C_HEADS * C_DK
GRID_W = 64
CHUNK = 64
ROPE_THETA = 10000.0
EPS = 1e-6
F_TINY = 1e-30
D_IN = 2400
D_IN_PAD = 2560
N_DEV = 8
LANES = 128
NEG = -1e30

ADAM_LR, ADAM_B1, ADAM_B2, ADAM_EPS, ADAM_WD, ADAM_STEP = 0.001, 0.9, 0.999, 1e-08, 0.01, 10

VMEM_LIMIT = 56 * 1024 * 1024
MESH = pl.DeviceIdType.MESH


def _pallas(body, **kw):
    return pl.pallas_call(body, **kw)


def _cparams(sem):
    return pltpu.CompilerParams(dimension_semantics=sem, vmem_limit_bytes=VMEM_LIMIT)


def _split3(x):
    hi = x.astype(BF16)
    r = x - hi.astype(F32)
    mid = r.astype(BF16)
    lo = (r - mid.astype(F32)).astype(BF16)
    return hi, mid, lo


def _nn(a, b):
    return jnp.dot(a, b, preferred_element_type=F32)


def _nt(a, b):
    return lax.dot_general(a, b, (((1,), (1,)), ((), ())), preferred_element_type=F32)


def _tn(a, b):
    return lax.dot_general(a, b, (((0,), (0,)), ((), ())), preferred_element_type=F32)


def _make_xdotr(pieces):
    @jax.custom_vjp
    def op(x, m):
        return sum(_nn(p, m) for p in _split3(x)[:pieces])

    def fwd(x, m):
        return op(x, m), m

    def bwd(m, ct):
        return sum(_nt(p, m) for p in _split3(ct)[:pieces]), None

    op.defvjp(fwd, bwd)
    return op


xdotr, xdotr2, xdotr1 = _make_xdotr(3), _make_xdotr(2), _make_xdotr(1)


@jax.custom_vjp
def xdotl(m, mt, x):
    return sum(_nn(m, p) for p in _split3(x)[:2])


def _xdotl_fwd(m, mt, x):
    return xdotl(m, mt, x), (m, mt)


def _xdotl_bwd(res, ct):
    m, mt = res
    return None, None, sum(_nn(mt, p) for p in _split3(ct)[:2])


xdotl.defvjp(_xdotl_fwd, _xdotl_bwd)


def _sigmoid(x):
    return 1.0 / (1.0 + jnp.exp(-x))


def _silu(x):
    return x * _sigmoid(x)


def _rms(x, gain, n=None):
    n = x.shape[-1] if n is None else n
    ms = jnp.sum(x * x, axis=-1, keepdims=True) * (1.0 / n)
    return x * lax.rsqrt(ms + EPS) * gain


def _head_rms(x, seg, gain):
    ms = xdotr2(x * x, seg) * (1.0 / HD)
    return x * lax.rsqrt(ms + EPS) * gain


class Lay:
    def __init__(self, B, T, Tc, tm):
        self.B, self.T, self.Tc, self.tm = B, T, Tc, tm
        self.nl, self.nc = T // tm, Tc // tm
        self.per = self.nl + self.nc
        self.n_tiles = B * self.per
        self.n_seg = 2 * B
        self.rows = self.n_tiles * tm

    def seg(self, i):
        b, w = i // self.per, i % self.per
        return jnp.where(w < self.nc, self.B + b, b)

    def pos(self, i):
        w = i % self.per
        return jnp.where(w < self.nc, self.nl, w - self.nc)

    def first(self, i):
        w = i % self.per
        return jnp.logical_or(w == 0, w == self.nc)


def _ew_spec(kind, a, lay):
    if kind == "tok":
        return pl.BlockSpec((lay.tm, a.shape[1]), lambda i: (i, 0))
    if kind == "seg":
        return pl.BlockSpec((1, 1, a.shape[2]), lambda i: (lay.seg(i), 0, 0))
    if kind == "pos":
        return pl.BlockSpec((lay.tm, a.shape[1]), lambda i: (lay.pos(i), 0))
    return pl.BlockSpec(a.shape, lambda i: (0,) * a.ndim)


def _ew_load(ref, kind):
    if kind == "seg":
        return ref[0]
    if kind == "tok":
        return ref[...].astype(F32)
    return ref[...]


def _ew_fwd(f, lay, kinds, arrays, outs, name):
    n_in = len(arrays)

    def body(*refs):
        vals = [_ew_load(r, k) for r, k in zip(refs[:n_in], kinds)]
        res = f(*vals)
        for r, o in zip(res, refs[n_in:]):
            o[...] = r.astype(o.dtype)

    return _pallas(
        body, name=name, grid=(lay.n_tiles,),
        in_specs=[_ew_spec(k, a, lay) for k, a in zip(kinds, arrays)],
        out_specs=[pl.BlockSpec((lay.tm, c), lambda i: (i, 0)) for c, _ in outs],
        out_shape=[jax.ShapeDtypeStruct((lay.rows, c), dt) for c, dt in outs],
        compiler_params=_cparams(("parallel",)),
    )(*arrays)


def _ew_bwd(f, lay, kinds, diffs, arrays, cts, name):
    n_in, n_ct = len(arrays), len(cts)
    d_idx = [i for i, d in enumerate(diffs) if d]

    g_shapes, g_specs = [], []
    for i in d_idx:
        a, k = arrays[i], kinds[i]
        if k == "tok":
            g_shapes.append(jax.ShapeDtypeStruct((lay.rows, a.shape[1]), a.dtype))
            g_specs.append(pl.BlockSpec((lay.tm, a.shape[1]), lambda t: (t, 0)))
        elif k == "seg":
            g_shapes.append(jax.ShapeDtypeStruct((lay.n_seg, 1, a.shape[2]), F32))
            g_specs.append(pl.BlockSpec((1, 1, a.shape[2]), lambda t: (lay.seg(t), 0, 0)))
        else:
            g_shapes.append(jax.ShapeDtypeStruct(a.shape, F32))
            g_specs.append(pl.BlockSpec(a.shape, lambda t, nd=a.ndim: (0,) * nd))

    def body(*refs):
        vals = [_ew_load(r, k) for r, k in zip(refs[:n_in], kinds)]
        cvals = tuple(r[...].astype(F32) for r in refs[n_in:n_in + n_ct])
        g_refs = refs[n_in + n_ct:]

        def g(*dv):
            full = list(vals)
            for j, i in enumerate(d_idx):
                full[i] = dv[j]
            return tuple(o.astype(F32) for o in f(*full))

        _, vjp = jax.vjp(g, *[vals[i] for i in d_idx])
        grads = vjp(cvals)
        t = pl.program_id(0)
        for gref, grad, i in zip(g_refs, grads, d_idx):
            k = kinds[i]
            if k == "tok":
                gref[...] = grad.astype(gref.dtype)
            elif k == "seg":
                @pl.when(lay.first(t))
                def _():
                    gref[...] = jnp.zeros_like(gref)

                gref[0] += grad
            else:
                @pl.when(t == 0)
                def _():
                    gref[...] = jnp.zeros_like(gref)

                gref[...] += grad

    res = _pallas(
        body, name=name + "_bwd", grid=(lay.n_tiles,),
        in_specs=[_ew_spec(k, a, lay) for k, a in zip(kinds, arrays)]
        + [pl.BlockSpec((lay.tm, c.shape[1]), lambda i: (i, 0)) for c in cts],
        out_specs=g_specs, out_shape=g_shapes,
        compiler_params=_cparams(("arbitrary",)),
    )(*arrays, *cts)
    out = [None] * n_in
    for gr, i in zip(res, d_idx):
        a = arrays[i]
        if gr.shape != a.shape:
            pad = [(0, a.shape[0] - gr.shape[0])] + [(0, 0)] * (a.ndim - 1)
            gr = jnp.pad(gr, pad)
        out[i] = gr
    return tuple(out)


def ew_op(f, lay, kinds, diffs, outs, name):
    kinds, diffs, outs = tuple(kinds), tuple(diffs), tuple(outs)

    @jax.custom_vjp
    def op(*arrays):
        return tuple(_ew_fwd(f, lay, kinds, arrays, outs, name))

    def fwd(*arrays):
        return op(*arrays), arrays

    def bwd(arrays, cts):
        return _ew_bwd(f, lay, kinds, diffs, arrays, tuple(cts), name)

    op.defvjp(fwd, bwd)
    return op


def _tile(n, cands):
    for c in cands:
        if n % c == 0:
            return c
    return n


TN_ROW_TILES = (2176, 1024, 512, 256)
WEIGHT_TILE_ELEMS = 4 * 1024 * 1024


def _wide_tile(n, depth):
    for c in (2048, 1280, 1024, 512, 256, 128):
        if n % c == 0 and c * depth <= WEIGHT_TILE_ELEMS:
            return c
    return n


def _mm_nn(x, w, name, out_dtype, relu2):
    M, K = x.shape
    N = w.shape[1]
    tm, tn = _tile(M, (512, 256)), _wide_tile(N, K)

    def body(x_ref, w_ref, o_ref):
        acc = _nn(x_ref[...].astype(BF16), w_ref[...].astype(BF16))
        if relu2:
            acc = jnp.square(jnp.maximum(acc, 0.0))
        o_ref[...] = acc.astype(o_ref.dtype)

    return _pallas(
        body, name=name, grid=(N // tn, M // tm),
        in_specs=[pl.BlockSpec((tm, K), lambda j, i: (i, 0)), pl.BlockSpec((K, tn), lambda j, i: (0, j))],
        out_specs=pl.BlockSpec((tm, tn), lambda j, i: (i, j)),
        out_shape=jax.ShapeDtypeStruct((M, N), out_dtype),
        compiler_params=_cparams(("parallel", "parallel")),
    )(x, w)


def _through_relu2(dz_ref, z_ref):
    if z_ref is None:
        return dz_ref[...].astype(BF16)
    z = z_ref[...].astype(F32)
    root = z * lax.rsqrt(jnp.maximum(z, F_TINY))
    return (dz_ref[...].astype(F32) * (2.0 * root)).astype(BF16)


def _mm_nt(dy, w, name, out_dtype, z=None):
    M, N = dy.shape
    K = w.shape[0]
    tm = _tile(M, (256,)) if z is not None else _tile(M, (512, 256))
    tk = K if z is not None else _wide_tile(K, N)
    row = pl.BlockSpec((tm, N), lambda j, i: (i, 0))

    def body(*refs):
        dy_ref, z_ref = (refs[0], refs[1]) if z is not None else (refs[0], None)
        w_ref, o_ref = refs[-2], refs[-1]
        o_ref[...] = _nt(_through_relu2(dy_ref, z_ref), w_ref[...].astype(BF16)).astype(o_ref.dtype)

    return _pallas(
        body, name=name, grid=(K // tk, M // tm),
        in_specs=[row] * (2 if z is not None else 1) + [pl.BlockSpec((tk, N), lambda j, i: (j, 0))],
        out_specs=pl.BlockSpec((tm, tk), lambda j, i: (i, j)),
        out_shape=jax.ShapeDtypeStruct((M, K), out_dtype),
        compiler_params=_cparams(("parallel", "parallel")),
    )(*((dy, z, w) if z is not None else (dy, w)))


def _mm_tn(x, dy, name, out_dtype, z=None):
    M, K = x.shape
    N = dy.shape[1]
    tm = _tile(M, TN_ROW_TILES)
    tk, tn = _tile(K, (1024, 512, 256, 128)), _tile(N, (1024, 512, 256, 128))
    n_m = M // tm
    col = pl.BlockSpec((tm, tn), lambda a, b, m: (m, b))

    def body(*refs):
        x_ref = refs[0]
        dy_ref, z_ref = (refs[1], refs[2]) if z is not None else (refs[1], None)
        o_ref, acc_ref = refs[-2], refs[-1]
        m = pl.program_id(2)

        @pl.when(m == 0)
        def _():
            acc_ref[...] = jnp.zeros_like(acc_ref)

        acc_ref[...] += _tn(x_ref[...].astype(BF16), _through_relu2(dy_ref, z_ref))

        @pl.when(m == n_m - 1)
        def _():
            o_ref[...] = acc_ref[...].astype(o_ref.dtype)

    return _pallas(
        body, name=name, grid=(K // tk, N // tn, n_m),
        in_specs=[pl.BlockSpec((tm, tk), lambda a, b, m: (m, a))] + [col] * (2 if z is not None else 1),
        out_specs=pl.BlockSpec((tk, tn), lambda a, b, m: (a, b)),
        out_shape=jax.ShapeDtypeStruct((K, N), out_dtype),
        scratch_shapes=[pltpu.VMEM((tk, tn), F32)],
        compiler_params=_cparams(("parallel", "parallel", "arbitrary")),
    )(*((x, dy, z) if z is not None else (x, dy)))


def matmul_op(name, out_dtype=F32, relu2=False):
    @jax.custom_vjp
    def op(x, w):
        return _mm_nn(x, w, name, out_dtype, relu2)

    def fwd(x, w):
        y = op(x, w)
        return y, (x, w, y if relu2 else None)

    def bwd(res, dy):
        x, w, z = res
        return _mm_nt(dy, w, name + "_dx", x.dtype, z), _mm_tn(x, dy, name + "_dw", w.dtype, z)

    op.defvjp(fwd, bwd)
    return op


LOG2E = math.log2(math.e)


class AttnLay:
    def __init__(self, B, T, Tc, tq):
        self.B, self.T, self.Tc, self.tq = B, T, Tc, tq
        self.S = T + Tc
        self.nq, self.nqc = self.S // tq, Tc // tq


def _attn_specs(al):
    qs = lambda w: pl.BlockSpec((al.tq, w), lambda b, i: (b * al.nq + i, 0))
    ks = lambda w: pl.BlockSpec((al.S, w), lambda b, i: (b, 0))
    return qs, ks


def _lane_fold(acc, x, op):
    t = x[:, 0:LANES]
    for j in range(1, x.shape[1] // LANES):
        t = op(t, x[:, j * LANES:(j + 1) * LANES])
    return op(acc, t)


def _key_chunks(n, kc):
    return [(c0, min(kc, n - c0)) for c0 in range(0, n, kc)]


def _stack(ref, g, group, width, tq):
    parts = [ref[:, (g * group + j) * width:(g * group + j + 1) * width].astype(F32) for j in range(group)]
    return parts[0] if group == 1 else jnp.concatenate(parts, axis=0)


PEER_RELATIONS = [(dx, dy, dc) for dx in (0, 1) for dy in (0, 1) for dc in (0, 1) if (dx, dy, dc) != (0, 0, 0)]


def _exchange_behind(al, src_refs, dst_refs, sems, gather):
    send_sems, recv_sems, local_sems = sems
    x, y, c = _me()
    me = 4 * x + 2 * y + c

    def copies():
        out = []
        for a, (s, d) in enumerate(zip(src_refs, dst_refs)):
            out.append(pltpu.make_async_copy(s if gather else s.at[me], d.at[me], local_sems.at[a]))
            for r, (dx, dy, dc) in enumerate(PEER_RELATIONS):
                px, py, pc = (x + dx) % 2, (y + dy) % 2, (c + dc) % 2
                out.append(pltpu.make_async_remote_copy(
                    src_ref=s if gather else s.at[4 * px + 2 * py + pc], dst_ref=d.at[me],
                    send_sem=send_sems.at[7 * a + r], recv_sem=recv_sems.at[7 * a + r],
                    device_id=(px, py, pc), device_id_type=MESH))
        return out

    b, i = pl.program_id(0), pl.program_id(1)

    @pl.when(jnp.logical_and(b == 0, i == 0))
    def _():
        for cp in copies():
            cp.start()

    @pl.when(jnp.logical_and(b == al.B - 1, i == al.nq - 1))
    def _():
        for cp in copies():
            cp.wait()


def _attn_fwd(q, k, v, cfg, al, name, shards=()):
    n_kv, group, dq, dv, scale, kc, _ = cfg
    tq = al.tq
    rows = group * tq
    wq, wk, wv, wo = q.shape[1], k.shape[1], v.shape[1], n_kv * group * dv
    qs, ks = _attn_specs(al)
    n = len(shards)

    def body(*refs):
        q_ref, k_ref, v_ref = refs[:3]
        x_refs = refs[3:3 + n]
        o_ref, lse_ref = refs[3 + n:5 + n]
        g_refs = refs[5 + n:5 + 2 * n]
        s_scr = refs[5 + 2 * n]
        if n:
            _exchange_behind(al, x_refs, g_refs, refs[6 + 2 * n:], gather=True)
        lane = lax.broadcasted_iota(jnp.int32, (tq, LANES), 1)

        def run(n_keys):
            chunks = _key_chunks(n_keys, kc)
            lse_all = jnp.zeros((tq, LANES), F32)
            for g in range(n_kv):
                q4 = (_stack(q_ref, g, group, dq, tq) * (scale * LOG2E)).astype(BF16)
                ksl, vsl = slice(g * dq, (g + 1) * dq), slice(g * dv, (g + 1) * dv)
                m_part = jnp.full((rows, LANES), -jnp.inf, F32)
                for c0, w in chunks:
                    s = _nt(q4, k_ref[c0:c0 + w, ksl])
                    s_scr[:, c0:c0 + w] = s
                    m_part = _lane_fold(m_part, s, jnp.maximum)
                m = jnp.max(m_part, axis=1, keepdims=True)
                l_part = jnp.zeros((rows, LANES), F32)
                acc = jnp.zeros((rows, dv), F32)
                for c0, w in chunks:
                    p = jnp.exp2(s_scr[:, c0:c0 + w] - m)
                    l_part = _lane_fold(l_part, p, jnp.add)
                    acc = acc + _nn(p.astype(BF16), v_ref[c0:c0 + w, vsl])
                l = jnp.sum(l_part, axis=1, keepdims=True)
                o = acc / l
                lse = m + jnp.log2(l)
                for j in range(group):
                    h = g * group + j
                    o_ref[:, h * dv:(h + 1) * dv] = o[j * tq:(j + 1) * tq].astype(o_ref.dtype)
                    lse_all = jnp.where(lane == h, lse[j * tq:(j + 1) * tq], lse_all)
            lse_ref[...] = lse_all

        is_ctx = pl.program_id(1) < al.nqc

        @pl.when(is_ctx)
        def _():
            run(al.Tc)

        @pl.when(jnp.logical_not(is_ctx))
        def _():
            run(al.S)

    hbm = pl.BlockSpec(memory_space=pl.ANY)
    res = _pallas(
        body, name=name, grid=(al.B, al.nq),
        in_specs=[qs(wq), ks(wk), ks(wv)] + [hbm] * n,
        out_specs=[qs(wo), qs(LANES)] + [hbm] * n,
        out_shape=[jax.ShapeDtypeStruct((q.shape[0], wo), BF16), jax.ShapeDtypeStruct((q.shape[0], LANES), F32)]
        + [jax.ShapeDtypeStruct((N_DEV,) + a.shape, a.dtype) for a in shards],
        scratch_shapes=[pltpu.VMEM((rows, al.S), F32)] + (_comm_scratch(n) if n else []),
        compiler_params=_cparams(("arbitrary", "arbitrary") if n else ("parallel", "parallel")),
    )(q, k, v, *shards)
    return res[0], res[1], list(res[2:])


def _attn_bwd(q, k, v, o, lse, do, cfg, al, name, partials=()):
    n_kv, group, dq, dv, scale, _, kc = cfg
    tq = al.tq
    rows = group * tq
    wq, wk, wv, wo = q.shape[1], k.shape[1], v.shape[1], n_kv * group * dv
    qs, ks = _attn_specs(al)
    n = len(partials)

    def body(*refs):
        q_ref, k_ref, v_ref, o_ref, lse_ref, do_ref = refs[:6]
        p_refs = refs[6:6 + n]
        dq_ref, dk_ref, dv_ref = refs[6 + n:9 + n]
        r_refs = refs[9 + n:9 + 2 * n]
        ak, av = refs[9 + 2 * n:11 + 2 * n]
        if n:
            _exchange_behind(al, p_refs, r_refs, refs[11 + 2 * n:], gather=False)
        i = pl.program_id(1)

        @pl.when(i == 0)
        def _():
            ak[...] = jnp.zeros_like(ak)
            av[...] = jnp.zeros_like(av)

        lane = lax.broadcasted_iota(jnp.int32, (tq, LANES), 1)

        def run(n_keys):
            lse_tile = lse_ref[...]
            for g in range(n_kv):
                qf = _stack(q_ref, g, group, dq, tq)
                q4l = (qf * (scale * LOG2E)).astype(BF16)
                q4s = (qf * scale).astype(BF16)
                do4 = _stack(do_ref, g, group, dv, tq)
                o4 = _stack(o_ref, g, group, dv, tq)
                cols = [jnp.sum(jnp.where(lane == g * group + j, lse_tile, 0.0), axis=1, keepdims=True)
                        for j in range(group)]
                lse4 = cols[0] if group == 1 else jnp.concatenate(cols, axis=0)
                dl = jnp.sum(do4 * o4, axis=1, keepdims=True)
                dob = do4.astype(BF16)
                ksl, vsl = slice(g * dq, (g + 1) * dq), slice(g * dv, (g + 1) * dv)
                dq4 = jnp.zeros((rows, dq), F32)
                for c0, w in _key_chunks(n_keys, kc):
                    kk = k_ref[c0:c0 + w, ksl]
                    p = jnp.exp2(_nt(q4l, kk) - lse4)
                    dp = _nt(dob, v_ref[c0:c0 + w, vsl])
                    ds = (p * (dp - dl)).astype(BF16)
                    dq4 = dq4 + _nn(ds, kk)
                    ak[c0:c0 + w, ksl] += _tn(ds, q4s)
                    av[c0:c0 + w, vsl] += _tn(p.astype(BF16), dob)
                dq4 = dq4 * scale
                for j in range(group):
                    h = g * group + j
                    dq_ref[:, h * dq:(h + 1) * dq] = dq4[j * tq:(j + 1) * tq]

        is_ctx = i < al.nqc

        @pl.when(is_ctx)
        def _():
            run(al.Tc)

        @pl.when(jnp.logical_not(is_ctx))
        def _():
            run(al.S)

        @pl.when(i == al.nq - 1)
        def _():
            dk_ref[...] = ak[...].astype(dk_ref.dtype)
            dv_ref[...] = av[...].astype(dv_ref.dtype)

    hbm = pl.BlockSpec(memory_space=pl.ANY)
    res = _pallas(
        body, name=name + "_bwd", grid=(al.B, al.nq),
        in_specs=[qs(wq), ks(wk), ks(wv), qs(wo), qs(LANES), qs(wo)] + [hbm] * n,
        out_specs=[qs(wq), ks(wk), ks(wv)] + [hbm] * n,
        out_shape=[jax.ShapeDtypeStruct(q.shape, F32), jax.ShapeDtypeStruct(k.shape, k.dtype),
                   jax.ShapeDtypeStruct(v.shape, v.dtype)]
        + [jax.ShapeDtypeStruct(a.shape, a.dtype) for a in partials],
        scratch_shapes=[pltpu.VMEM((al.S, wk), F32), pltpu.VMEM((al.S, wv), F32)] + (_comm_scratch(n) if n else []),
        compiler_params=_cparams(("arbitrary", "arbitrary") if n else ("parallel", "arbitrary")),
    )(q, k, v, o, lse, do, *partials)
    return res[0], res[1], res[2], list(res[3:])


def attn_op(cfg, al, name):
    @jax.custom_vjp
    def op(q, k, v):
        return _attn_fwd(q, k, v, cfg, al, name)[0]

    def fwd(q, k, v):
        o, lse, _ = _attn_fwd(q, k, v, cfg, al, name)
        return o, (q, k, v, o, lse)

    def bwd(res, do):
        return _attn_bwd(*res, do, cfg, al, name)[:3]

    op.defvjp(fwd, bwd)
    return op


def attn_gather_op(cfg, al, name):
    @jax.custom_vjp
    def op(q, k, v, *shards):
        o, _, gathered = _attn_fwd(q, k, v, cfg, al, name, [s.astype(BF16) for s in shards])
        return (o, *gathered)

    def fwd(q, k, v, *shards):
        o, lse, gathered = _attn_fwd(q, k, v, cfg, al, name, [s.astype(BF16) for s in shards])
        return (o, *gathered), (q, k, v, o, lse)

    def bwd(res, cts):
        dq_, dk_, dv_, received = _attn_bwd(*res, cts[0], cfg, al, name, list(cts[1:]))
        return (dq_, dk_, dv_, *[sum_slots(r, name + f"_sum{a}") for a, r in enumerate(received)])

    op.defvjp(fwd, bwd)
    return op


SCAN_WIDTHS = (32, 16, 8, 4, 2, 1)
N_CM = 2 + 2 * len(SCAN_WIDTHS)


def _scan_consts(reverse):
    C = CHUNK
    t = np.arange(C)[:, None]
    s = np.arange(C)[None, :]
    blocks = [(s <= t), (s > t)]
    for w in SCAN_WIDTHS:
        blocks.append((s <= t) & (s // w == t // w))
    for w in SCAN_WIDTHS:
        blocks.append((s > t) & (s // w == t // w))
    masks = [np.eye(C, dtype=bool)]
    for w in SCAN_WIDTHS:
        masks.append((t // (2 * w) == s // (2 * w)) & ((t // w) % 2 == 1) & ((s // w) % 2 == 0))
    if reverse:
        blocks = [b[::-1, ::-1] for b in blocks]
        masks = [m[::-1, ::-1] for m in masks]
    cm = np.concatenate([b.astype(np.float32) for b in blocks] + [np.ones((8, C), np.float32)], axis=0)
    mw = np.stack([np.tile(m.astype(np.float32), (1, C_HEADS)) for m in masks])
    rows = np.arange(C_HEADS * C)[:, None] // C
    lane = np.arange(C_W)[None, :] // C_DK
    hm = (rows == lane).astype(np.float32)
    bd = (np.arange(C_W)[:, None] // C_DK == lane).astype(np.float32)
    return (jnp.asarray(cm, BF16), jnp.asarray(cm.T.copy(), BF16), jnp.asarray(mw, F32),
            jnp.asarray(hm, F32), jnp.asarray(bd, F32))


def _scan_chunk(st, q, k, v, g, cm, cmt, mw, hm, bd):
    C = CHUNK
    cs = xdotl(cm, cmt, g)
    b = cs[0:C]
    rest = cs[C:2 * C]
    tot = cs[N_CM * C:N_CM * C + 1]

    def per_head(a):
        return (jnp.concatenate([a] * C_HEADS, axis=0) * hm).astype(BF16)

    a = _nt(q.astype(BF16), per_head(k)) * mw[0]
    for i in range(len(SCAN_WIDTHS)):
        eq = jnp.exp(jnp.minimum(cs[(2 + i) * C:(3 + i) * C], 0.0))
        ek = jnp.exp(jnp.minimum(cs[(2 + len(SCAN_WIDTHS) + i) * C:(3 + len(SCAN_WIDTHS) + i) * C], 0.0))
        a = a + _nt((q * eq).astype(BF16), per_head(k * ek)) * mw[i + 1]
    o = _nn(a.astype(BF16), per_head(v))
    o = o + _nt((q * jnp.exp(b)).astype(BF16), st.astype(BF16))
    st_new = st * jnp.exp(tot) + _tn(v.astype(BF16), (k * jnp.exp(rest)).astype(BF16)) * bd
    return o, st_new


class ScanLay:
    def __init__(self, B, T, Tc):
        self.B, self.S = B, T + Tc
        self.ncc, self.ntot = Tc // CHUNK, (T + Tc) // CHUNK

    def chunk(self, j, reverse):
        if not reverse:
            return j
        return jnp.where(j < self.ncc, self.ncc - 1 - j, self.ntot - 1 - (j - self.ncc))


def _scan_specs(sl, step):
    f = pl.BlockSpec((sl.B, CHUNK, C_W), lambda j: (0, sl.chunk(step(j), False), 0))
    r = pl.BlockSpec((sl.B, CHUNK, C_W), lambda j: (0, sl.chunk(step(j), True), 0))
    return f, r


def _scan_fwd(q, kf, gf, kb, gb, v, sl, name):
    B, S = sl.B, sl.S
    view = lambda a: a.reshape(B, S, C_W)
    cf, cr = _scan_consts(False), _scan_consts(True)
    nc = len(cf)
    f, r = _scan_specs(sl, lambda j: j)
    cspecs = [pl.BlockSpec(c.shape, lambda j, nd=c.ndim: (0,) * nd) for c in cf + cr]

    def body(*refs):
        (qf_ref, kf_ref, gf_ref, vf_ref, qr_ref, kr_ref, gr_ref, vr_ref), refs = refs[:8], refs[8:]
        cfv, crv = [c[...] for c in refs[:nc]], [c[...] for c in refs[nc:2 * nc]]
        of_ref, or_ref, st_ref, st = refs[2 * nc:]

        @pl.when(pl.program_id(0) == 0)
        def _():
            st[...] = jnp.zeros_like(st)

        st_ref[0] = st[...]
        dirs = ((qf_ref, kf_ref, gf_ref, vf_ref, of_ref, cfv), (qr_ref, kr_ref, gr_ref, vr_ref, or_ref, crv))
        args = [(st[d * B + b], q_[b], k_[b], v_[b], g_[b]) + tuple(cv)
                for d, (q_, k_, g_, v_, _, cv) in enumerate(dirs) for b in range(B)]
        outs = [_scan_chunk(*a) for a in args]
        for d, (_, _, _, _, o_, _) in enumerate(dirs):
            for b in range(B):
                o_[b], st[d * B + b] = outs[d * B + b]

    of, ob, states = _pallas(
        body, name=name, grid=(sl.ntot,),
        in_specs=[f] * 4 + [r] * 4 + cspecs,
        out_specs=[f, r, pl.BlockSpec((1, 2 * B, C_W, C_W), lambda j: (j, 0, 0, 0))],
        out_shape=[jax.ShapeDtypeStruct((B, S, C_W), F32)] * 2
        + [jax.ShapeDtypeStruct((sl.ntot, 2 * B, C_W, C_W), F32)],
        scratch_shapes=[pltpu.VMEM((2 * B, C_W, C_W), F32)],
        compiler_params=_cparams(("arbitrary",)),
    )(view(q), view(kf), view(gf), view(v), view(q), view(kb), view(gb), view(v), *cf, *cr)
    return of.reshape(B * S, C_W), ob.reshape(B * S, C_W), states


def _scan_bwd(q, kf, gf, kb, gb, v, states, dof, dob, sl, name):
    B, S = sl.B, sl.S
    view = lambda a: a.reshape(B, S, C_W)
    cf, cr = _scan_consts(False), _scan_consts(True)
    nc = len(cf)
    last = sl.ntot - 1
    f, r = _scan_specs(sl, lambda j: last - j)
    cspecs = [pl.BlockSpec(c.shape, lambda j, nd=c.ndim: (0,) * nd) for c in cf + cr]

    def body(*refs):
        ins, refs = refs[:11], refs[11:]
        qf_ref, kf_ref, gf_ref, vf_ref, dof_ref, qr_ref, kr_ref, gr_ref, vr_ref, dor_ref, st_ref = ins
        cfv, crv = [c[...] for c in refs[:nc]], [c[...] for c in refs[nc:2 * nc]]
        outs, dst = refs[2 * nc:-1], refs[-1]

        @pl.when(pl.program_id(0) == 0)
        def _():
            dst[...] = jnp.zeros_like(dst)

        dirs = ((qf_ref, kf_ref, gf_ref, vf_ref, dof_ref, cfv), (qr_ref, kr_ref, gr_ref, vr_ref, dor_ref, crv))
        args = [((st_ref[0, d * B + b], q_[b], k_[b], v_[b], g_[b]), (do_[b], dst[d * B + b]), cv)
                for d, (q_, k_, g_, v_, do_, cv) in enumerate(dirs) for b in range(B)]
        grads = []
        for prim, cts, cv in args:
            _, vjp = jax.vjp(lambda s_, a_, b_, c_, e_, cv=cv: _scan_chunk(s_, a_, b_, c_, e_, *cv), *prim)
            grads.append(vjp(cts))
        for d in range(2):
            dq_, dk_, dg_, dv_ = outs[4 * d:4 * d + 4]
            for b in range(B):
                dst[d * B + b], dq_[b], dk_[b], dv_[b], dg_[b] = grads[d * B + b]

    res = _pallas(
        body, name=name + "_bwd", grid=(sl.ntot,),
        in_specs=[f] * 5 + [r] * 5 + [pl.BlockSpec((1, 2 * B, C_W, C_W), lambda j: (last - j, 0, 0, 0))] + cspecs,
        out_specs=[f] * 4 + [r] * 4,
        out_shape=[jax.ShapeDtypeStruct((B, S, C_W), F32)] * 8,
        scratch_shapes=[pltpu.VMEM((2 * B, C_W, C_W), F32)],
        compiler_params=_cparams(("arbitrary",)),
    )(view(q), view(kf), view(gf), view(v), view(dof), view(q), view(kb), view(gb), view(v), view(dob),
      states, *cf, *cr)
    dq_f, dk_f, dg_f, dv_f, dq_r, dk_r, dg_r, dv_r = [a.reshape(B * S, C_W) for a in res]
    return dq_f + dq_r, dk_f, dg_f, dk_r, dg_r, dv_f + dv_r


def scan_op(sl, name):
    @jax.custom_vjp
    def op(q, kf, gf, kb, gb, v):
        return _scan_fwd(q, kf, gf, kb, gb, v, sl, name)[:2]

    def fwd(q, kf, gf, kb, gb, v):
        of, ob, states = _scan_fwd(q, kf, gf, kb, gb, v, sl, name)
        return (of, ob), (q, kf, gf, kb, gb, v, states)

    def bwd(res, cts):
        return _scan_bwd(*res, cts[0], cts[1], sl, name)

    op.defvjp(fwd, bwd)
    return op


def loss_and_grad(y, target, lay):
    N, D = y.shape

    def body(y_ref, t_ref, dy_ref, l_ref):
        i = pl.program_id(0)

        @pl.when(i == 0)
        def _():
            l_ref[...] = jnp.zeros_like(l_ref)

        is_ctx = i % lay.per < lay.nc

        @pl.when(is_ctx)
        def _():
            dy_ref[...] = jnp.zeros_like(dy_ref)

        @pl.when(jnp.logical_not(is_ctx))
        def _():
            e = y_ref[...] - t_ref[...]
            dy_ref[...] = e * (1.0 / D)
            l_ref[...] += 0.5 * jnp.sum(jnp.sum(e * e, axis=1, keepdims=True) * (1.0 / D), axis=0, keepdims=True)

    def t_index(i):
        return ((i // lay.per) * lay.nl + jnp.maximum(i % lay.per - lay.nc, 0), 0)

    dy, lp = _pallas(
        body, name="loss_head", grid=(lay.n_tiles,),
        in_specs=[pl.BlockSpec((lay.tm, D), lambda i: (i, 0)), pl.BlockSpec((lay.tm, D), t_index)],
        out_specs=[pl.BlockSpec((lay.tm, D), lambda i: (i, 0)), pl.BlockSpec((8, LANES), lambda i: (0, 0))],
        out_shape=[jax.ShapeDtypeStruct((N, D), F32), jax.ShapeDtypeStruct((8, LANES), F32)],
        compiler_params=_cparams(("arbitrary",)),
    )(y, target)
    return lp, dy


def _adam_math(w, g, m, v):
    mn = ADAM_B1 * m + (1.0 - ADAM_B1) * g
    vn = ADAM_B2 * v + (1.0 - ADAM_B2) * jnp.square(g)
    m_hat = mn / (1.0 - ADAM_B1 ** ADAM_STEP)
    v_hat = vn / (1.0 - ADAM_B2 ** ADAM_STEP)
    return -ADAM_LR * (m_hat / (jnp.sqrt(v_hat) + ADAM_EPS) + ADAM_WD * w), mn, vn


ROW_TILES = (512, 256, 128, 64, 32, 16, 8)


def adamw(w, g, m, v, name):
    R, C = w.shape
    tr = _tile(R, ROW_TILES)

    def body(w_ref, g_ref, m_ref, v_ref, d_ref, mo_ref, vo_ref):
        d_ref[...], mo_ref[...], vo_ref[...] = _adam_math(w_ref[...], g_ref[...], m_ref[...], v_ref[...])

    spec = pl.BlockSpec((tr, C), lambda i: (i, 0))
    return _pallas(
        body, name=name, grid=(R // tr,), in_specs=[spec] * 4, out_specs=[spec] * 3,
        out_shape=[jax.ShapeDtypeStruct((R, C), F32)] * 3,
        compiler_params=_cparams(("parallel",)),
    )(w, g, m, v)


def sum_slots(recv, name):
    shape = recv.shape[1:]
    C = shape[-1]
    r3 = recv.reshape(N_DEV, -1, C)
    R = r3.shape[1]
    tr = _tile(R, ROW_TILES[1:])

    def body(r_ref, o_ref):
        g = r_ref[0].astype(F32)
        for k in range(1, N_DEV):
            g = g + r_ref[k].astype(F32)
        o_ref[...] = g

    out = _pallas(
        body, name=name, grid=(R // tr,),
        in_specs=[pl.BlockSpec((N_DEV, tr, C), lambda i: (0, i, 0))],
        out_specs=pl.BlockSpec((tr, C), lambda i: (i, 0)),
        out_shape=jax.ShapeDtypeStruct((R, C), F32),
        compiler_params=_cparams(("parallel",)),
    )(r3)
    return out.reshape(shape)


def adamw_slots(w, recv, m, v, name):
    R, C = w.shape
    tr = _tile(R, ROW_TILES[1:])

    def body(w_ref, r_ref, m_ref, v_ref, g_ref, d_ref, mo_ref, vo_ref):
        g = r_ref[0].astype(F32)
        for k in range(1, N_DEV):
            g = g + r_ref[k].astype(F32)
        g_ref[...] = g
        d_ref[...], mo_ref[...], vo_ref[...] = _adam_math(w_ref[...], g, m_ref[...], v_ref[...])

    spec = pl.BlockSpec((tr, C), lambda i: (i, 0))
    return _pallas(
        body, name=name, grid=(R // tr,),
        in_specs=[spec, pl.BlockSpec((N_DEV, tr, C), lambda i: (0, i, 0)), spec, spec], out_specs=[spec] * 4,
        out_shape=[jax.ShapeDtypeStruct((R, C), F32)] * 4,
        compiler_params=_cparams(("parallel",)),
    )(w, recv, m, v)


def _me():
    return lax.axis_index("x"), lax.axis_index("y"), lax.axis_index("c")


def _gather_many(x_refs, out_refs, send_sems, recv_sems, local_sems):
    x, y, c = _me()
    me, sibling = (x, y, c), (x, y, 1 - c)
    chips = [(1 - x, y), (x, 1 - y), (1 - x, 1 - y)]
    arrs = range(len(x_refs))

    def slot(a, px, py, pc):
        return out_refs[a].at[4 * px + 2 * py + pc]

    def copy(a, k, block, to, src=None):
        return pltpu.make_async_remote_copy(
            src_ref=slot(a, *block) if src is None else src, dst_ref=slot(a, *block),
            send_sem=send_sems.at[7 * a + k], recv_sem=recv_sems.at[7 * a + k], device_id=to, device_id_type=MESH)

    mine = [pltpu.make_async_copy(x_refs[a], slot(a, *me), local_sems.at[a]) for a in arrs]
    for cp in mine:
        cp.start()
    first = []
    for a in arrs:
        first.append(copy(a, 0, me, sibling, src=x_refs[a]))
        first += [copy(a, 1 + j, me, (*chip, c), src=x_refs[a]) for j, chip in enumerate(chips)]
    for cp in first:
        cp.start()
    passed = []
    for j, chip in enumerate(chips):
        for a in arrs:
            copy(a, 1 + j, (*chip, c), me).wait_recv()
            fwd = copy(a, 4 + j, (*chip, c), sibling)
            fwd.start()
            passed.append(fwd)
    for a in arrs:
        copy(a, 0, sibling, me).wait_recv()
    for j, chip in enumerate(chips):
        for a in arrs:
            copy(a, 4 + j, (*chip, 1 - c), me).wait_recv()
    for cp in first + passed:
        cp.wait_send()
    for cp in mine:
        cp.wait()


def _comm_scratch(n):
    return [pltpu.SemaphoreType.DMA((7 * n,)), pltpu.SemaphoreType.DMA((7 * n,)), pltpu.SemaphoreType.DMA((n,))]


def small_gather(xb, name):
    R = xb.shape[0]

    def body(x_ref, out_ref, sum_ref, send_sems, recv_sems, local_sems):
        _gather_many([x_ref], [out_ref], send_sems, recv_sems, local_sems)
        acc = out_ref[0]
        for k in range(1, N_DEV):
            acc = acc + out_ref[k]
        sum_ref[...] = acc

    vm = pl.BlockSpec(memory_space=pltpu.VMEM)
    return _pallas(
        body, name=name, in_specs=[vm], out_specs=[vm, vm],
        out_shape=[jax.ShapeDtypeStruct((N_DEV, R, LANES), xb.dtype), jax.ShapeDtypeStruct((R, LANES), xb.dtype)],
        scratch_shapes=_comm_scratch(1),
        compiler_params=pltpu.CompilerParams(vmem_limit_bytes=VMEM_LIMIT),
    )(xb)


def big_gather(xs, name):
    n = len(xs)

    def body(*refs):
        _gather_many(refs[:n], refs[n:2 * n], *refs[2 * n:])

    hbm = pl.BlockSpec(memory_space=pl.ANY)
    return _pallas(
        body, name=name, in_specs=[hbm] * n, out_specs=[hbm] * n,
        out_shape=[jax.ShapeDtypeStruct((N_DEV,) + a.shape, a.dtype) for a in xs],
        scratch_shapes=_comm_scratch(n),
    )(*xs)


def scatter_exchange(gs, name):
    n = len(gs)
    rels = [(dx, dy, dc) for dx in (0, 1) for dy in (0, 1) for dc in (0, 1) if (dx, dy, dc) != (0, 0, 0)]

    def body(*refs):
        g_refs, r_refs = refs[:n], refs[n:2 * n]
        send_sems, recv_sems, local_sems = refs[2 * n:]
        x, y, c = _me()
        me = 4 * x + 2 * y + c
        mine = [pltpu.make_async_copy(g_refs[a].at[me], r_refs[a].at[me], local_sems.at[a]) for a in range(n)]
        for cp in mine:
            cp.start()
        copies = []
        for r, (dx, dy, dc) in enumerate(rels):
            px, py, pc = (x + dx) % 2, (y + dy) % 2, (c + dc) % 2
            for a in range(n):
                copies.append(pltpu.make_async_remote_copy(
                    src_ref=g_refs[a].at[4 * px + 2 * py + pc], dst_ref=r_refs[a].at[me],
                    send_sem=send_sems.at[7 * a + r], recv_sem=recv_sems.at[7 * a + r],
                    device_id=(px, py, pc), device_id_type=MESH))
        for cp in copies:
            cp.start()
        for cp in copies:
            cp.wait()
        for cp in mine:
            cp.wait()

    hbm = pl.BlockSpec(memory_space=pl.ANY)
    return _pallas(
        body, name=name, in_specs=[hbm] * n, out_specs=[hbm] * n,
        out_shape=[jax.ShapeDtypeStruct(a.shape, a.dtype) for a in gs],
        scratch_shapes=_comm_scratch(n),
    )(*gs)


def _pack(arrs, dtype, row_mult):
    flat = jnp.concatenate([a.astype(dtype).reshape(-1) for a in arrs])
    pad = (-flat.shape[0]) % (LANES * row_mult)
    if pad:
        flat = jnp.concatenate([flat, jnp.zeros((pad,), dtype)])
    return flat.reshape(-1, LANES)


def _unpack(buf, shapes, lead=()):
    flat = buf.reshape(*lead, -1)
    out, off = [], 0
    for s in shapes:
        n = int(np.prod(s))
        out.append(flat[..., off:off + n].reshape(*lead, *s))
        off += n
    return out


def _rope_tables(T, tm):
    pos = np.arange(T)
    row, col = pos // GRID_W, pos % GRID_W

    def tab(rot_dim):
        nf = rot_dim // 4
        inv = ROPE_THETA ** (-np.arange(nf, dtype=np.float32) / nf)
        ang = np.concatenate([row[:, None].astype(np.float32) * inv, col[:, None].astype(np.float32) * inv], axis=-1)
        ang = ang.astype(np.float32)
        cos, sin = np.cos(ang), np.sin(ang)
        return np.concatenate([cos, cos], -1), np.concatenate([-sin, sin], -1)

    c64, s64 = tab(HD)
    c32, s32 = tab(B_ROPE)
    ca, sa = np.tile(c64, (1, A_HEADS)), np.tile(s64, (1, A_HEADS))
    cb = np.concatenate([c32, np.ones((T, LANES - B_ROPE), np.float32)], -1)
    sb = np.concatenate([s32, np.zeros((T, LANES - B_ROPE), np.float32)], -1)
    one, zero = np.ones((T, B_NOPE), np.float32), np.zeros((T, B_NOPE), np.float32)
    tail1, tail0 = np.ones((T, LANES - B_NOPE - B_ROPE), np.float32), np.zeros((T, LANES - B_NOPE - B_ROPE), np.float32)
    cq = np.tile(np.concatenate([one, c32, tail1], -1), (1, B_HEADS))
    sq = np.tile(np.concatenate([zero, s32, tail0], -1), (1, B_HEADS))

    def fin(a, ident):
        return jnp.asarray(np.concatenate([a, np.full((tm, a.shape[1]), ident, np.float32)], 0), F32)

    return fin(ca, 1.0), fin(sa, 0.0), fin(cb, 1.0), fin(sb, 0.0), fin(cq, 1.0), fin(sq, 0.0)


def _swap_matrix(width, starts, half):
    p = np.zeros((width, width), np.float32)
    for s in starts:
        for i in range(half):
            p[s + i, s + half + i] = 1.0
            p[s + half + i, s + i] = 1.0
    return jnp.asarray(p, BF16)


def _seg_matrix(width):
    h = np.arange(width) // HD
    return jnp.asarray((h[:, None] == h[None, :]).astype(np.float32), BF16)


def _key_slot_matrices():
    e1 = np.zeros((B_HEADS * B_NOPE, B_HEADS * LANES), np.float32)
    e2 = np.zeros((LANES, B_HEADS * LANES), np.float32)
    for h in range(B_HEADS):
        for i in range(B_NOPE):
            e1[h * B_NOPE + i, h * LANES + i] = 1.0
        for i in range(B_ROPE):
            e2[i, h * LANES + B_NOPE + i] = 1.0
    return jnp.asarray(e1, BF16), jnp.asarray(e2, BF16)


def _unit(x):
    return x * lax.rsqrt(jnp.sum(x * x, axis=-1, keepdims=True) * (1.0 / x.shape[-1]) + EPS)


def _f_premod(x, sh, sc, g):
    return (_unit(x) * (g * (1.0 + sc)) + sh,)


def _f_post(x, y, gt, g):
    return (x + _unit(y) * (gt * g),)


def _f_post_pre(x, y, gt, g_post, sh, sc, g_pre):
    x1 = x + _unit(y) * (gt * g_post)
    return x1, _unit(x1) * (g_pre * (1.0 + sc)) + sh


def _f_bias(raw, b):
    return (raw + b,)


def _f_silu(x):
    return (_silu(x),)


def _f_readout(of, ob, gate, ya, yb, gain, seg):
    yc = _head_rms(of + ob, seg, gain) * _silu(gate)
    return (jnp.concatenate([ya, yb, yc], axis=1),)


def _f_bq(bq, cq, sq, pq):
    return (bq * cq + xdotr2(bq, pq) * sq,)


def _f_bk(bkn, bkr, e1, e2):
    return (xdotr1(bkn, e1) + xdotr1(bkr, e2),)


def _make_f_feat(layer):
    def f(feat, ca, sa, cb, sb, gaq, gak, gbq, gbkv, c00, c01, c10, c11, seg, pa, pb):
        aq = _head_rms(feat[:, 0:512], seg, gaq)
        ak = _head_rms(feat[:, 512:640], seg[0:128, 0:128], gak)
        av = feat[:, 640:768]
        aq = aq * ca + xdotr2(aq, pa) * sa
        ak = ak * ca[:, 0:128] + xdotr2(ak, pa[0:128, 0:128]) * sa[:, 0:128]
        bqn = _rms(feat[:, 768:1024], gbq, B_QR)
        bkvn = _rms(feat[:, 1024:1152], gbkv)
        bkr = feat[:, 1152:1280]
        bkr = bkr * cb + xdotr2(bkr, pb) * sb
        cq = _silu(feat[:, 1280:1536])
        zf, zb = feat[:, 1536:1792], feat[:, 1792:2048]
        if layer == 0:
            lbf = lbb = 0.0
        else:
            def share(c0, c1):
                m = jnp.maximum(c0, c1)
                e0, e1 = jnp.exp(c0 - m), jnp.exp(c1 - m)
                return e1 / (e0 + e1)
            lbf, lbb = share(c00, c10), share(c01, c11)

        def gate(z, lb):
            f_ = lb + (1.0 - lb) * _sigmoid(z)
            return (1.0 - lb) * _sigmoid(-z), jnp.log(jnp.maximum(f_, F_TINY))

        kf, gf = gate(zf, lbf)
        kb, gb = gate(zb, lbb)
        return aq, ak, av, bqn, bkvn, bkr, cq, kf, gf, kb, gb, feat[:, 2048:2304], feat[:, 2304:2560]

    return f


def _pad_w_in(w):
    z = lambda n: jnp.zeros((w.shape[0], n), w.dtype)
    return jnp.concatenate([w[:, 0:960], z(64), w[:, 960:1120], z(96), w[:, 1120:2400]], axis=1)


def _pad_w_q_up(w):
    w4 = w.reshape(B_QR, B_HEADS, B_NOPE + B_ROPE)
    w4 = jnp.pad(w4, ((0, 256 - B_QR), (0, 0), (0, LANES - B_NOPE - B_ROPE)))
    return w4.reshape(256, B_HEADS * LANES)


def _split_w_kv_up(w):
    w4 = w.reshape(B_KVR, B_HEADS, B_NOPE + B_V)
    return w4[:, :, :B_NOPE].reshape(B_KVR, -1), w4[:, :, B_NOPE:].reshape(B_KVR, -1)


def _tile_gain(g, reps, width=None):
    t = jnp.tile(g, reps)
    if width is not None and width > t.shape[0]:
        t = jnp.pad(t, (0, width - t.shape[0]))
    return t[None, :]


def local_forward(dims, p):
    B, T, Tc, D = dims
    tm = min(256, Tc)
    lay_all = Lay(B, T, Tc, tm)
    ca, sa, cb, sb, cq, sq = _rope_tables(T, tm)
    seg512, seg256 = _seg_matrix(512), _seg_matrix(C_W)
    pa = _swap_matrix(512, range(0, 512, HD), HD // 2)
    pb = _swap_matrix(LANES, [0], B_ROPE // 2)
    pq = _swap_matrix(512, [h * LANES + B_NOPE for h in range(B_HEADS)], B_ROPE // 2)
    e1, e2 = _key_slot_matrices()
    sl = ScanLay(B, T, Tc)
    al_a, al_b = AttnLay(B, T, Tc, min(256, Tc)), AttnLay(B, T, Tc, min(256, Tc))
    cfg_a = (A_KV, A_GROUP, HD, HD, HD ** -0.5, 512, 1024)
    cfg_b = (B_HEADS, 1, LANES, B_V, (B_NOPE + B_ROPE) ** -0.5, 512, T + Tc)

    tok = p["tok"]
    depth = p["modraw"].shape[0]
    mods = []
    for l in range(depth):
        bias_lay = Lay(1, 8, 0, 8)
        raw8 = jnp.pad(p["modraw"][l], ((0, 8 - B - 1), (0, 0)))
        mod = ew_op(_f_bias, bias_lay, ("tok", "par"), (True, True), ((6 * D, F32),), f"l{l}_ada_bias")(
            raw8, p["b_ada"][l][None, :])[0]
        seg_rows = jnp.concatenate([mod[0:B], jnp.broadcast_to(mod[B:B + 1], (B, 6 * D))], axis=0)[:, None, :]
        mods.append([seg_rows[:, :, i * D:(i + 1) * D] for i in range(6)])

    post_pre_kinds = ("tok", "tok", "seg", "par", "seg", "seg", "par")
    h = ew_op(_f_premod, lay_all, ("tok", "seg", "seg", "par"), (True,) * 4, ((D, BF16),), "l0_premix")(
        tok, mods[0][0], mods[0][1], p["g_pre_mix"][0][None, :])[0]
    for l in range(depth):
        tag = f"l{l}_"
        sh_m, sc_m, gt_m, sh_f, sc_f, gt_f = mods[l]

        def early(n):
            return p[n][0] if l == 0 else late[n + REST][l - 1]

        feat = matmul_op(tag + "w_in")(h, _pad_w_in(early("w_in")))
        clb = p["clb"]
        feats = ew_op(
            _make_f_feat(l), lay_all,
            ("tok", "pos", "pos", "pos", "pos") + ("par",) * 11,
            (True,) + (False,) * 4 + (True,) * 8 + (False,) * 3,
            ((512, F32), (128, BF16), (128, BF16), (256, BF16), (128, BF16), (128, F32)) + ((C_W, F32),) * 7,
            tag + "feat")(
            feat, ca, sa, cb, sb,
            _tile_gain(p["a_q_norm"][l], A_HEADS), _tile_gain(p["a_k_norm"][l], A_KV),
            _tile_gain(p["b_q_norm"][l], 1, 256), _tile_gain(p["b_kv_norm"][l], 1),
            clb[0, 0][None, :], clb[0, 1][None, :], clb[1, 0][None, :], clb[1, 1][None, :],
            seg512, pa, pb)
        aq, ak, av, bqn, bkvn, bkr, cqs, kf, gf, kb, gb, cv, cgate = feats
        bq = matmul_op(tag + "w_q_up")(bqn, _pad_w_q_up(early("w_q_up")))
        bq = ew_op(_f_bq, lay_all, ("tok", "pos", "pos", "par"), (True, False, False, False), ((512, F32),),
                   tag + "bq_rope")(bq, cq, sq, pq)[0]
        w_kn, w_v = _split_w_kv_up(early("w_kv_up"))
        bkn = matmul_op(tag + "w_k_up")(bkvn, w_kn)
        bv = matmul_op(tag + "w_v_up", BF16)(bkvn, w_v)
        bk = ew_op(_f_bk, lay_all, ("tok", "tok", "par", "par"), (True, True, False, False),
                   ((B_HEADS * LANES, BF16),), tag + "bk_slots")(bkn, bkr, e1, e2)[0]

        if l == 0:
            ya, *got = attn_gather_op(cfg_a, al_a, tag + "attn_a")(aq, ak, av, *[p["shard_" + n] for n in RIDE])
            late = {n: _assemble(n, g) for n, g in zip(RIDE, got)}
        else:
            ya = attn_op(cfg_a, al_a, tag + "attn_a")(aq, ak, av)
        yb = attn_op(cfg_b, al_b, tag + "attn_b")(bq, bk, bv)
        of, ob = scan_op(sl, tag + "scan")(cqs, kf, gf, kb, gb, cv)
        lay_out = lay_all
        ycat = ew_op(_f_readout, lay_out, ("tok",) * 5 + ("par", "par"), (True,) * 6 + (False,),
                     ((A_HEADS * HD + B_HEADS * B_V + C_W, BF16),), tag + "readout")(
            of, ob, cgate, ya, yb, _tile_gain(p["c_out_norm"][l], C_HEADS), seg256)[0]
        mixo = matmul_op(tag + "w_out")(ycat, late["w_out"][l])
        tok, hf = ew_op(_f_post_pre, lay_out, post_pre_kinds, (True,) * 7, ((D, F32), (D, BF16)),
                        tag + "postmix_preffn")(
            tok, mixo, gt_m, p["g_post_mix"][l][None, :], sh_f, sc_f, p["g_pre_ffn"][l][None, :])

        z = matmul_op(tag + "w_ff1", BF16, relu2=True)(hf, late["w_ff1"][l])
        yf = matmul_op(tag + "w_ff2")(z, late["w_ff2"][l])
        if l + 1 < depth:
            tok, h = ew_op(_f_post_pre, lay_out, post_pre_kinds, (True,) * 7, ((D, F32), (D, BF16)),
                           tag + "postffn_premix")(
                tok, yf, gt_f, p["g_post_ffn"][l][None, :], mods[l + 1][0], mods[l + 1][1],
                p["g_pre_mix"][l + 1][None, :])
        else:
            tok = ew_op(_f_post, lay_out, ("tok", "tok", "seg", "par"), (True,) * 4, ((D, F32),), tag + "postffn")(
                tok, yf, gt_f, p["g_post_ffn"][l][None, :])[0]
    return tok


EARLY = ("w_in", "w_q_up", "w_kv_up")
LATE = ("w_out", "w_ff1", "w_ff2")
REST = "_rest"
RIDE = LATE + tuple(n + REST for n in EARLY)
COL_SHARDED = ("w_in", "w_q_up", "w_kv_up", "w_ff1")
SMALL = ("c_ctx", "b_ada", "g_pre_mix", "g_post_mix", "g_pre_ffn", "g_post_ffn", "a_q_norm", "a_k_norm",
         "b_q_norm", "b_kv_norm", "c_out_norm")
WEIGHTS = ("c_ctx", "w_ada", "b_ada", "g_pre_mix", "g_post_mix", "g_pre_ffn", "g_post_ffn", "w_in", "a_q_norm",
           "a_k_norm", "b_q_norm", "w_q_up", "b_kv_norm", "w_kv_up", "c_lower_bounds", "c_out_norm", "w_out",
           "w_ff1", "w_ff2")
SMALL_ROWS = 64


def _assemble(name, a):
    if name.removesuffix(REST) in COL_SHARDED:
        return a.transpose(1, 2, 0, 3).reshape(a.shape[1], a.shape[2], N_DEV * a.shape[3])
    return a.transpose(1, 0, 2, 3).reshape(a.shape[1], N_DEV * a.shape[2], a.shape[3])


def kernel(x, c, ctx, c_ctx, w_ada, b_ada, g_pre_mix, g_post_mix, g_pre_ffn, g_post_ffn, w_in, a_q_norm, a_k_norm, b_q_norm, w_q_up, b_kv_norm, w_kv_up, c_lower_bounds, c_out_norm, w_out, w_ff1, w_ff2, loss_target, m_c_ctx, m_w_ada, m_b_ada, m_g_pre_mix, m_g_post_mix, m_g_pre_ffn, m_g_post_ffn, m_w_in, m_a_q_norm, m_a_k_norm, m_b_q_norm, m_w_q_up, m_b_kv_norm, m_w_kv_up, m_c_lower_bounds, m_c_out_norm, m_w_out, m_w_ff1, m_w_ff2, v_c_ctx, v_w_ada, v_b_ada, v_g_pre_mix, v_g_post_mix, v_g_pre_ffn, v_g_post_ffn, v_w_in, v_a_q_norm, v_a_k_norm, v_b_q_norm, v_w_q_up, v_b_kv_norm, v_w_kv_up, v_c_lower_bounds, v_c_out_norm, v_w_out, v_w_ff1, v_w_ff2):
    W = dict(c_ctx=c_ctx, w_ada=w_ada, b_ada=b_ada, g_pre_mix=g_pre_mix, g_post_mix=g_post_mix, g_pre_ffn=g_pre_ffn,
             g_post_ffn=g_post_ffn, w_in=w_in, a_q_norm=a_q_norm, a_k_norm=a_k_norm, b_q_norm=b_q_norm,
             w_q_up=w_q_up, b_kv_norm=b_kv_norm, w_kv_up=w_kv_up, c_lower_bounds=c_lower_bounds,
             c_out_norm=c_out_norm, w_out=w_out, w_ff1=w_ff1, w_ff2=w_ff2)
    M = dict(c_ctx=m_c_ctx, w_ada=m_w_ada, b_ada=m_b_ada, g_pre_mix=m_g_pre_mix, g_post_mix=m_g_post_mix,
             g_pre_ffn=m_g_pre_ffn, g_post_ffn=m_g_post_ffn, w_in=m_w_in, a_q_norm=m_a_q_norm, a_k_norm=m_a_k_norm,
             b_q_norm=m_b_q_norm, w_q_up=m_w_q_up, b_kv_norm=m_b_kv_norm, w_kv_up=m_w_kv_up,
             c_lower_bounds=m_c_lower_bounds, c_out_norm=m_c_out_norm, w_out=m_w_out, w_ff1=m_w_ff1, w_ff2=m_w_ff2)
    V = dict(c_ctx=v_c_ctx, w_ada=v_w_ada, b_ada=v_b_ada, g_pre_mix=v_g_pre_mix, g_post_mix=v_g_post_mix,
             g_pre_ffn=v_g_pre_ffn, g_post_ffn=v_g_post_ffn, w_in=v_w_in, a_q_norm=v_a_q_norm, a_k_norm=v_a_k_norm,
             b_q_norm=v_b_q_norm, w_q_up=v_w_q_up, b_kv_norm=v_b_kv_norm, w_kv_up=v_w_kv_up,
             c_lower_bounds=v_c_lower_bounds, c_out_norm=v_c_out_norm, w_out=v_w_out, w_ff1=v_w_ff1, w_ff2=v_w_ff2)

    B, T, D = x.shape
    Tc = ctx.shape[1]
    depth = w_ada.shape[0]
    ada_cols = w_ada.shape[2]
    idx = 4 * lax.axis_index("x") + 2 * lax.axis_index("y") + lax.axis_index("c")
    n_cond = N_DEV * B
    cond_rows = -(-(n_cond + 1) // 8) * 8

    clb_cols = c_lower_bounds.shape[2]
    g1, _ = small_gather(_pack([c, c_lower_bounds], F32, 8), "gather_cond")
    c_parts, clb_parts = _unpack(g1, [c.shape, c_lower_bounds.shape], lead=(N_DEV,))
    c_all = c_parts.reshape(n_cond, D)
    clb_full = clb_parts.transpose(1, 2, 0, 3).reshape(depth, 2, N_DEV * clb_cols)

    cond_lay = Lay(1, cond_rows, 0, cond_rows)

    def ada_shard(c_ctx_, w_ada_):
        cond = jnp.concatenate([c_all, c_ctx_[None, :], jnp.zeros((cond_rows - n_cond - 1, D), F32)], axis=0)
        sc = ew_op(_f_silu, cond_lay, ("tok",), (True,), ((D, F32),), "cond_silu")(cond)[0]
        return jnp.stack([matmul_op(f"l{l}_w_ada")(sc, w_ada_[l]) for l in range(depth)])

    mod_shard, vjp_ada = jax.vjp(ada_shard, c_ctx, w_ada)

    g2, _ = small_gather(_pack([mod_shard], F32, 8), "gather_mod")
    mod_all = _unpack(g2, [mod_shard.shape], lead=(N_DEV,))[0]
    mod_all = mod_all.transpose(1, 2, 0, 3).reshape(depth, cond_rows, N_DEV * ada_cols)
    mine = lax.dynamic_slice_in_dim(mod_all, idx * B, B, axis=1)
    modraw = jnp.concatenate([mine, mod_all[:, n_cond:n_cond + 1]], axis=1)

    gathered = dict(zip(EARLY, big_gather([W[n][0:1].astype(BF16) for n in EARLY], "gather_weights")))

    dims = (B, T, Tc, D)
    small_names = [n for n in SMALL if n != "c_ctx"]
    small_in = {n: W[n] for n in small_names}
    late_in = {n: W[n] for n in LATE}
    late_in.update({n + REST: W[n][1:] for n in EARLY})

    def fwd(x_, modraw_, small_, clb_, gathered_, late_):
        p = dict(small_)
        p.update({n: _assemble(n, a) for n, a in gathered_.items()})
        p.update({"shard_" + n: a for n, a in late_.items()})
        p.update(tok=jnp.concatenate([ctx, x_], axis=1).reshape(B * (Tc + T), D), modraw=modraw_, clb=clb_)
        return local_forward(dims, p)

    y, vjp_main = jax.vjp(fwd, x, modraw, small_in, clb_full, gathered, late_in)
    loss_part, dy = loss_and_grad(y, loss_target.reshape(B * T, D), Lay(B, T, Tc, min(256, Tc)))
    dx, dmodraw, dsmall, dclb, dgathered, dlate = vjp_main(dy)

    pay3 = _pack([dmodraw] + [dsmall[n] for n in small_names] + [dclb, loss_part[0, 0:1]], F32, 8)
    g3, s3 = small_gather(pay3, "gather_small_grads")
    dmod_parts = _unpack(g3, [dmodraw.shape], lead=(N_DEV,))[0]
    tot = _unpack(s3, [dmodraw.shape] + [W[n].shape for n in small_names] + [clb_full.shape, (1,)])
    dmod_tot, small_tot, dclb_tot, loss = tot[0], dict(zip(small_names, tot[1:-2])), tot[-2], tot[-1]
    drows = dmod_parts[:, :, 0:B].transpose(1, 0, 2, 3).reshape(depth, n_cond, N_DEV * ada_cols)
    dcond = jnp.concatenate(
        [drows, dmod_tot[:, B:B + 1], jnp.zeros((depth, cond_rows - n_cond - 1, N_DEV * ada_cols), F32)], axis=1)
    dmod_shard = lax.dynamic_slice_in_dim(dcond, idx * ada_cols, ada_cols, axis=2)
    dc_ctx_part, dw_ada = vjp_ada(dmod_shard)

    _, s4 = small_gather(_pack([dc_ctx_part], F32, 8), "gather_c_ctx_grad")
    small_tot["c_ctx"] = _unpack(s4, [c_ctx.shape])[0]

    recv = dict(zip(EARLY, scatter_exchange([dgathered[n] for n in EARLY], "scatter_grads")))
    grads, delta, new_m, new_v = dict(small_tot), {}, {}, {}
    for n in EARLY:
        cols = W[n].shape[-1]
        flat = lambda a: a.reshape(-1, cols)
        first = adamw_slots(flat(W[n][0:1]), recv[n].reshape(N_DEV, -1, cols), flat(M[n][0:1]), flat(V[n][0:1]),
                            "adamw_" + n + "_first")
        g_rest = dlate[n + REST]
        rest = adamw(flat(W[n][1:]), flat(g_rest), flat(M[n][1:]), flat(V[n][1:]), "adamw_" + n + REST)
        join = lambda a, b: jnp.concatenate([a.reshape(W[n][0:1].shape), b.reshape(W[n][1:].shape)], axis=0)
        grads[n] = join(first[0], g_rest)
        delta[n], new_m[n], new_v[n] = (join(a, b) for a, b in zip(first[1:], rest))
    for n in LATE:
        shape, cols = W[n].shape, W[n].shape[-1]
        flat = lambda a: a.reshape(-1, cols)
        grads[n] = dlate[n]
        res = adamw(flat(W[n]), flat(dlate[n]), flat(M[n]), flat(V[n]), "adamw_" + n)
        delta[n], new_m[n], new_v[n] = (a.reshape(shape) for a in res)
    grads["w_ada"] = dw_ada
    grads["c_lower_bounds"] = lax.dynamic_slice_in_dim(dclb_tot, idx * clb_cols, clb_cols, axis=2)

    flat_a = lambda a: a.reshape(-1, ada_cols)
    res = adamw(flat_a(w_ada), flat_a(dw_ada), flat_a(m_w_ada), flat_a(v_w_ada), "adamw_w_ada")
    delta["w_ada"], new_m["w_ada"], new_v["w_ada"] = (a.reshape(w_ada.shape) for a in res)
    names = list(SMALL) + ["c_lower_bounds"]
    shapes = [W[n].shape for n in names]
    res = adamw(*[_pack([src[n] for n in names], F32, SMALL_ROWS) for src in (W, grads, M, V)], "adamw_small")
    for dst, buf in zip((delta, new_m, new_v), res):
        dst.update(zip(names, _unpack(buf, shapes)))

    return (loss.reshape(()), dx, *[grads[n] for n in WEIGHTS], *[delta[n] for n in WEIGHTS],
            *[new_m[n] for n in WEIGHTS], *[new_v[n] for n in WEIGHTS])
```

```python
import math

import numpy as np

import jax
import jax.numpy as jnp
from jax import lax
from jax.experimental import pallas as pl
from jax.experimental.pallas import tpu as pltpu

F32 = jnp.float32
BF16 = jnp.bfloat16

A_HEADS, A_KV, HD = 8, 2, 64
A_GROUP = A_HEADS // A_KV
B_HEADS, B_QR, B_KVR, B_NOPE, B_ROPE, B_V = 4, 192, 128, 64, 32, 64
C_HEADS, C_DK = 4, 64
C_W = C_HEADS * C_DK
GRID_W = 64
CHUNK = 64
ROPE_THETA = 10000.0
EPS = 1e-6
F_TINY = 1e-30
D_IN = 2400
D_IN_PAD = 2560
N_DEV = 8
LANES = 128
NEG = -1e30

ADAM_LR, ADAM_B1, ADAM_B2, ADAM_EPS, ADAM_WD, ADAM_STEP = 0.001, 0.9, 0.999, 1e-08, 0.01, 10

VMEM_LIMIT = 56 * 1024 * 1024
MESH = pl.DeviceIdType.MESH


def _pallas(body, **kw):
    return pl.pallas_call(body, **kw)


def _cparams(sem):
    return pltpu.CompilerParams(dimension_semantics=sem, vmem_limit_bytes=VMEM_LIMIT)


def _split3(x):
    hi = x.astype(BF16)
    r = x - hi.astype(F32)
    mid = r.astype(BF16)
    lo = (r - mid.astype(F32)).astype(BF16)
    return hi, mid, lo


def _nn(a, b):
    return jnp.dot(a, b, preferred_element_type=F32)


def _nt(a, b):
    return lax.dot_general(a, b, (((1,), (1,)), ((), ())), preferred_element_type=F32)


def _tn(a, b):
    return lax.dot_general(a, b, (((0,), (0,)), ((), ())), preferred_element_type=F32)


def _make_xdotr(pieces):
    @jax.custom_vjp
    def op(x, m):
        return sum(_nn(p, m) for p in _split3(x)[:pieces])

    def fwd(x, m):
        return op(x, m), m

    def bwd(m, ct):
        return sum(_nt(p, m) for p in _split3(ct)[:pieces]), None

    op.defvjp(fwd, bwd)
    return op


xdotr, xdotr2, xdotr1 = _make_xdotr(3), _make_xdotr(2), _make_xdotr(1)


@jax.custom_vjp
def xdotl(m, mt, x):
    return sum(_nn(m, p) for p in _split3(x)[:2])


def _xdotl_fwd(m, mt, x):
    return xdotl(m, mt, x), (m, mt)


def _xdotl_bwd(res, ct):
    m, mt = res
    return None, None, sum(_nn(mt, p) for p in _split3(ct)[:2])


xdotl.defvjp(_xdotl_fwd, _xdotl_bwd)


def _sigmoid(x):
    return 1.0 / (1.0 + jnp.exp(-x))


def _silu(x):
    return x * _sigmoid(x)


def _rms(x, gain, n=None):
    n = x.shape[-1] if n is None else n
    ms = jnp.sum(x * x, axis=-1, keepdims=True) * (1.0 / n)
    return x * lax.rsqrt(ms + EPS) * gain


def _head_rms(x, seg, gain):
    ms = xdotr2(x * x, seg) * (1.0 / HD)
    return x * lax.rsqrt(ms + EPS) * gain


class Lay:
    def __init__(self, B, T, Tc, tm):
        self.B, self.T, self.Tc, self.tm = B, T, Tc, tm
        self.nl, self.nc = T // tm, Tc // tm
        self.per = self.nl + self.nc
        self.n_tiles = B * self.per
        self.n_seg = 2 * B
        self.rows = self.n_tiles * tm

    def seg(self, i):
        b, w = i // self.per, i % self.per
        return jnp.where(w < self.nc, self.B + b, b)

    def pos(self, i):
        w = i % self.per
        return jnp.where(w < self.nc, self.nl, w - self.nc)

    def first(self, i):
        w = i % self.per
        return jnp.logical_or(w == 0, w == self.nc)


def _ew_spec(kind, a, lay):
    if kind == "tok":
        return pl.BlockSpec((lay.tm, a.shape[1]), lambda i: (i, 0))
    if kind == "seg":
        return pl.BlockSpec((1, 1, a.shape[2]), lambda i: (lay.seg(i), 0, 0))
    if kind == "pos":
        return pl.BlockSpec((lay.tm, a.shape[1]), lambda i: (lay.pos(i), 0))
    return pl.BlockSpec(a.shape, lambda i: (0,) * a.ndim)


def _ew_load(ref, kind):
    if kind == "seg":
        return ref[0]
    if kind == "tok":
        return ref[...].astype(F32)
    return ref[...]


def _ew_fwd(f, lay, kinds, arrays, outs, name):
    n_in = len(arrays)

    def body(*refs):
        vals = [_ew_load(r, k) for r, k in zip(refs[:n_in], kinds)]
        res = f(*vals)
        for r, o in zip(res, refs[n_in:]):
            o[...] = r.astype(o.dtype)

    return _pallas(
        body, name=name, grid=(lay.n_tiles,),
        in_specs=[_ew_spec(k, a, lay) for k, a in zip(kinds, arrays)],
        out_specs=[pl.BlockSpec((lay.tm, c), lambda i: (i, 0)) for c, _ in outs],
        out_shape=[jax.ShapeDtypeStruct((lay.rows, c), dt) for c, dt in outs],
        compiler_params=_cparams(("parallel",)),
    )(*arrays)


def _ew_bwd(f, lay, kinds, diffs, arrays, cts, name):
    n_in, n_ct = len(arrays), len(cts)
    d_idx = [i for i, d in enumerate(diffs) if d]

    g_shapes, g_specs = [], []
    for i in d_idx:
        a, k = arrays[i], kinds[i]
        if k == "tok":
            g_shapes.append(jax.ShapeDtypeStruct((lay.rows, a.shape[1]), a.dtype))
            g_specs.append(pl.BlockSpec((lay.tm, a.shape[1]), lambda t: (t, 0)))
        elif k == "seg":
            g_shapes.append(jax.ShapeDtypeStruct((lay.n_seg, 1, a.shape[2]), F32))
            g_specs.append(pl.BlockSpec((1, 1, a.shape[2]), lambda t: (lay.seg(t), 0, 0)))
        else:
            g_shapes.append(jax.ShapeDtypeStruct(a.shape, F32))
            g_specs.append(pl.BlockSpec(a.shape, lambda t, nd=a.ndim: (0,) * nd))

    def body(*refs):
        vals = [_ew_load(r, k) for r, k in zip(refs[:n_in], kinds)]
        cvals = tuple(r[...].astype(F32) for r in refs[n_in:n_in + n_ct])
        g_refs = refs[n_in + n_ct:]

        def g(*dv):
            full = list(vals)
            for j, i in enumerate(d_idx):
                full[i] = dv[j]
            return tuple(o.astype(F32) for o in f(*full))

        _, vjp = jax.vjp(g, *[vals[i] for i in d_idx])
        grads = vjp(cvals)
        t = pl.program_id(0)
        for gref, grad, i in zip(g_refs, grads, d_idx):
            k = kinds[i]
            if k == "tok":
                gref[...] = grad.astype(gref.dtype)
            elif k == "seg":
                @pl.when(lay.first(t))
                def _():
                    gref[...] = jnp.zeros_like(gref)

                gref[0] += grad
            else:
                @pl.when(t == 0)
                def _():
                    gref[...] = jnp.zeros_like(gref)

                gref[...] += grad

    res = _pallas(
        body, name=name + "_bwd", grid=(lay.n_tiles,),
        in_specs=[_ew_spec(k, a, lay) for k, a in zip(kinds, arrays)]
        + [pl.BlockSpec((lay.tm, c.shape[1]), lambda i: (i, 0)) for c in cts],
        out_specs=g_specs, out_shape=g_shapes,
        compiler_params=_cparams(("arbitrary",)),
    )(*arrays, *cts)
    out = [None] * n_in
    for gr, i in zip(res, d_idx):
        a = arrays[i]
        if gr.shape != a.shape:
            pad = [(0, a.shape[0] - gr.shape[0])] + [(0, 0)] * (a.ndim - 1)
            gr = jnp.pad(gr, pad)
        out[i] = gr
    return tuple(out)


def ew_op(f, lay, kinds, diffs, outs, name):
    kinds, diffs, outs = tuple(kinds), tuple(diffs), tuple(outs)

    @jax.custom_vjp
    def op(*arrays):
        return tuple(_ew_fwd(f, lay, kinds, arrays, outs, name))

    def fwd(*arrays):
        return op(*arrays), arrays

    def bwd(arrays, cts):
        return _ew_bwd(f, lay, kinds, diffs, arrays, tuple(cts), name)

    op.defvjp(fwd, bwd)
    return op


def _tile(n, cands):
    for c in cands:
        if n % c == 0:
            return c
    return n


TN_ROW_TILES = (2176, 1024, 512, 256)
WEIGHT_TILE_ELEMS = 4 * 1024 * 1024


def _wide_tile(n, depth):
    for c in (2048, 1280, 1024, 512, 256, 128):
        if n % c == 0 and c * depth <= WEIGHT_TILE_ELEMS:
            return c
    return n


def _mm_nn(x, w, name, out_dtype, relu2):
    M, K = x.shape
    N = w.shape[1]
    tm, tn = _tile(M, (512, 256)), _wide_tile(N, K)

    def body(x_ref, w_ref, o_ref):
        acc = _nn(x_ref[...].astype(BF16), w_ref[...].astype(BF16))
        if relu2:
            acc = jnp.square(jnp.maximum(acc, 0.0))
        o_ref[...] = acc.astype(o_ref.dtype)

    return _pallas(
        body, name=name, grid=(N // tn, M // tm),
        in_specs=[pl.BlockSpec((tm, K), lambda j, i: (i, 0)), pl.BlockSpec((K, tn), lambda j, i: (0, j))],
        out_specs=pl.BlockSpec((tm, tn), lambda j, i: (i, j)),
        out_shape=jax.ShapeDtypeStruct((M, N), out_dtype),
        compiler_params=_cparams(("parallel", "parallel")),
    )(x, w)


def _through_relu2(dz_ref, z_ref):
    if z_ref is None:
        return dz_ref[...].astype(BF16)
    z = z_ref[...].astype(F32)
    root = z * lax.rsqrt(jnp.maximum(z, F_TINY))
    return (dz_ref[...].astype(F32) * (2.0 * root)).astype(BF16)


def _mm_nt(dy, w, name, out_dtype, z=None):
    M, N = dy.shape
    K = w.shape[0]
    tm = _tile(M, (256,)) if z is not None else _tile(M, (512, 256))
    tk = K if z is not None else _wide_tile(K, N)
    row = pl.BlockSpec((tm, N), lambda j, i: (i, 0))

    def body(*refs):
        dy_ref, z_ref = (refs[0], refs[1]) if z is not None else (refs[0], None)
        w_ref, o_ref = refs[-2], refs[-1]
        o_ref[...] = _nt(_through_relu2(dy_ref, z_ref), w_ref[...].astype(BF16)).astype(o_ref.dtype)

    return _pallas(
        body, name=name, grid=(K // tk, M // tm),
        in_specs=[row] * (2 if z is not None else 1) + [pl.BlockSpec((tk, N), lambda j, i: (j, 0))],
        out_specs=pl.BlockSpec((tm, tk), lambda j, i: (i, j)),
        out_shape=jax.ShapeDtypeStruct((M, K), out_dtype),
        compiler_params=_cparams(("parallel", "parallel")),
    )(*((dy, z, w) if z is not None else (dy, w)))


def _mm_tn(x, dy, name, out_dtype, z=None):
    M, K = x.shape
    N = dy.shape[1]
    tm = _tile(M, TN_ROW_TILES)
    tk, tn = _tile(K, (1024, 512, 256, 128)), _tile(N, (1024, 512, 256, 128))
    n_m = M // tm
    col = pl.BlockSpec((tm, tn), lambda a, b, m: (m, b))

    def body(*refs):
        x_ref = refs[0]
        dy_ref, z_ref = (refs[1], refs[2]) if z is not None else (refs[1], None)
        o_ref, acc_ref = refs[-2], refs[-1]
        m = pl.program_id(2)

        @pl.when(m == 0)
        def _():
            acc_ref[...] = jnp.zeros_like(acc_ref)

        acc_ref[...] += _tn(x_ref[...].astype(BF16), _through_relu2(dy_ref, z_ref))

        @pl.when(m == n_m - 1)
        def _():
            o_ref[...] = acc_ref[...].astype(o_ref.dtype)

    return _pallas(
        body, name=name, grid=(K // tk, N // tn, n_m),
        in_specs=[pl.BlockSpec((tm, tk), lambda a, b, m: (m, a))] + [col] * (2 if z is not None else 1),
        out_specs=pl.BlockSpec((tk, tn), lambda a, b, m: (a, b)),
        out_shape=jax.ShapeDtypeStruct((K, N), out_dtype),
        scratch_shapes=[pltpu.VMEM((tk, tn), F32)],
        compiler_params=_cparams(("parallel", "parallel", "arbitrary")),
    )(*((x, dy, z) if z is not None else (x, dy)))


def matmul_op(name, out_dtype=F32, relu2=False):
    @jax.custom_vjp
    def op(x, w):
        return _mm_nn(x, w, name, out_dtype, relu2)

    def fwd(x, w):
        y = op(x, w)
        return y, (x, w, y if relu2 else None)

    def bwd(res, dy):
        x, w, z = res
        return _mm_nt(dy, w, name + "_dx", x.dtype, z), _mm_tn(x, dy, name + "_dw", w.dtype, z)

    op.defvjp(fwd, bwd)
    return op


LOG2E = math.log2(math.e)


class AttnLay:
    def __init__(self, B, T, Tc, tq):
        self.B, self.T, self.Tc, self.tq = B, T, Tc, tq
        self.S = T + Tc
        self.nq, self.nqc = self.S // tq, Tc // tq


def _attn_specs(al):
    qs = lambda w: pl.BlockSpec((al.tq, w), lambda b, i: (b * al.nq + i, 0))
    ks = lambda w: pl.BlockSpec((al.S, w), lambda b, i: (b, 0))
    return qs, ks


def _lane_fold(acc, x, op):
    t = x[:, 0:LANES]
    for j in range(1, x.shape[1] // LANES):
        t = op(t, x[:, j * LANES:(j + 1) * LANES])
    return op(acc, t)


def _key_chunks(n, kc):
    return [(c0, min(kc, n - c0)) for c0 in range(0, n, kc)]


def _stack(ref, g, group, width, tq):
    parts = [ref[:, (g * group + j) * width:(g * group + j + 1) * width].astype(F32) for j in range(group)]
    return parts[0] if group == 1 else jnp.concatenate(parts, axis=0)


PEER_RELATIONS = [(dx, dy, dc) for dx in (0, 1) for dy in (0, 1) for dc in (0, 1) if (dx, dy, dc) != (0, 0, 0)]


def _exchange_behind(al, src_refs, dst_refs, sems, gather):
    send_sems, recv_sems, local_sems = sems
    x, y, c = _me()
    me = 4 * x + 2 * y + c

    def copies():
        out = []
        for a, (s, d) in enumerate(zip(src_refs, dst_refs)):
            out.append(pltpu.make_async_copy(s if gather else s.at[me], d.at[me], local_sems.at[a]))
            for r, (dx, dy, dc) in enumerate(PEER_RELATIONS):
                px, py, pc = (x + dx) % 2, (y + dy) % 2, (c + dc) % 2
                out.append(pltpu.make_async_remote_copy(
                    src_ref=s if gather else s.at[4 * px + 2 * py + pc], dst_ref=d.at[me],
                    send_sem=send_sems.at[7 * a + r], recv_sem=recv_sems.at[7 * a + r],
                    device_id=(px, py, pc), device_id_type=MESH))
        return out

    b, i = pl.program_id(0), pl.program_id(1)

    @pl.when(jnp.logical_and(b == 0, i == 0))
    def _():
        for cp in copies():
            cp.start()

    @pl.when(jnp.logical_and(b == al.B - 1, i == al.nq - 1))
    def _():
        for cp in copies():
            cp.wait()


def _attn_fwd(q, k, v, cfg, al, name, shards=()):
    n_kv, group, dq, dv, scale, kc, _ = cfg
    tq = al.tq
    rows = group * tq
    wq, wk, wv, wo = q.shape[1], k.shape[1], v.shape[1], n_kv * group * dv
    qs, ks = _attn_specs(al)
    n = len(shards)

    def body(*refs):
        q_ref, k_ref, v_ref = refs[:3]
        x_refs = refs[3:3 + n]
        o_ref, lse_ref = refs[3 + n:5 + n]
        g_refs = refs[5 + n:5 + 2 * n]
        s_scr = refs[5 + 2 * n]
        if n:
            _exchange_behind(al, x_refs, g_refs, refs[6 + 2 * n:], gather=True)
        lane = lax.broadcasted_iota(jnp.int32, (tq, LANES), 1)

        def run(n_keys):
            chunks = _key_chunks(n_keys, kc)
            lse_all = jnp.zeros((tq, LANES), F32)
            for g in range(n_kv):
                q4 = (_stack(q_ref, g, group, dq, tq) * (scale * LOG2E)).astype(BF16)
                ksl, vsl = slice(g * dq, (g + 1) * dq), slice(g * dv, (g + 1) * dv)
                m_part = jnp.full((rows, LANES), -jnp.inf, F32)
                for c0, w in chunks:
                    s = _nt(q4, k_ref[c0:c0 + w, ksl])
                    s_scr[:, c0:c0 + w] = s
                    m_part = _lane_fold(m_part, s, jnp.maximum)
                m = jnp.max(m_part, axis=1, keepdims=True)
                l_part = jnp.zeros((rows, LANES), F32)
                acc = jnp.zeros((rows, dv), F32)
                for c0, w in chunks:
                    p = jnp.exp2(s_scr[:, c0:c0 + w] - m)
                    l_part = _lane_fold(l_part, p, jnp.add)
                    acc = acc + _nn(p.astype(BF16), v_ref[c0:c0 + w, vsl])
                l = jnp.sum(l_part, axis=1, keepdims=True)
                o = acc / l
                lse = m + jnp.log2(l)
                for j in range(group):
                    h = g * group + j
                    o_ref[:, h * dv:(h + 1) * dv] = o[j * tq:(j + 1) * tq].astype(o_ref.dtype)
                    lse_all = jnp.where(lane == h, lse[j * tq:(j + 1) * tq], lse_all)
            lse_ref[...] = lse_all

        is_ctx = pl.program_id(1) < al.nqc

        @pl.when(is_ctx)
        def _():
            run(al.Tc)

        @pl.when(jnp.logical_not(is_ctx))
        def _():
            run(al.S)

    hbm = pl.BlockSpec(memory_space=pl.ANY)
    res = _pallas(
        body, name=name, grid=(al.B, al.nq),
        in_specs=[qs(wq), ks(wk), ks(wv)] + [hbm] * n,
        out_specs=[qs(wo), qs(LANES)] + [hbm] * n,
        out_shape=[jax.ShapeDtypeStruct((q.shape[0], wo), BF16), jax.ShapeDtypeStruct((q.shape[0], LANES), F32)]
        + [jax.ShapeDtypeStruct((N_DEV,) + a.shape, a.dtype) for a in shards],
        scratch_shapes=[pltpu.VMEM((rows, al.S), F32)] + (_comm_scratch(n) if n else []),
        compiler_params=_cparams(("arbitrary", "arbitrary") if n else ("parallel", "parallel")),
    )(q, k, v, *shards)
    return res[0], res[1], list(res[2:])


def _attn_bwd(q, k, v, o, lse, do, cfg, al, name, partials=()):
    n_kv, group, dq, dv, scale, _, kc = cfg
    tq = al.tq
    rows = group * tq
    wq, wk, wv, wo = q.shape[1], k.shape[1], v.shape[1], n_kv * group * dv
    qs, ks = _attn_specs(al)
    n = len(partials)

    def body(*refs):
        q_ref, k_ref, v_ref, o_ref, lse_ref, do_ref = refs[:6]
        p_refs = refs[6:6 + n]
        dq_ref, dk_ref, dv_ref = refs[6 + n:9 + n]
        r_refs = refs[9 + n:9 + 2 * n]
        ak, av = refs[9 + 2 * n:11 + 2 * n]
        if n:
            _exchange_behind(al, p_refs, r_refs, refs[11 + 2 * n:], gather=False)
        i = pl.program_id(1)

        @pl.when(i == 0)
        def _():
            ak[...] = jnp.zeros_like(ak)
            av[...] = jnp.zeros_like(av)

        lane = lax.broadcasted_iota(jnp.int32, (tq, LANES), 1)

        def run(n_keys):
            lse_tile = lse_ref[...]
            for g in range(n_kv):
                qf = _stack(q_ref, g, group, dq, tq)
                q4l = (qf * (scale * LOG2E)).astype(BF16)
                q4s = (qf * scale).astype(BF16)
                do4 = _stack(do_ref, g, group, dv, tq)
                o4 = _stack(o_ref, g, group, dv, tq)
                cols = [jnp.sum(jnp.where(lane == g * group + j, lse_tile, 0.0), axis=1, keepdims=True)
                        for j in range(group)]
                lse4 = cols[0] if group == 1 else jnp.concatenate(cols, axis=0)
                dl = jnp.sum(do4 * o4, axis=1, keepdims=True)
                dob = do4.astype(BF16)
                ksl, vsl = slice(g * dq, (g + 1) * dq), slice(g * dv, (g + 1) * dv)
                dq4 = jnp.zeros((rows, dq), F32)
                for c0, w in _key_chunks(n_keys, kc):
                    kk = k_ref[c0:c0 + w, ksl]
                    p = jnp.exp2(_nt(q4l, kk) - lse4)
                    dp = _nt(dob, v_ref[c0:c0 + w, vsl])
                    ds = (p * (dp - dl)).astype(BF16)
                    dq4 = dq4 + _nn(ds, kk)
                    ak[c0:c0 + w, ksl] += _tn(ds, q4s)
                    av[c0:c0 + w, vsl] += _tn(p.astype(BF16), dob)
                dq4 = dq4 * scale
                for j in range(group):
                    h = g * group + j
                    dq_ref[:, h * dq:(h + 1) * dq] = dq4[j * tq:(j + 1) * tq]

        is_ctx = i < al.nqc

        @pl.when(is_ctx)
        def _():
            run(al.Tc)

        @pl.when(jnp.logical_not(is_ctx))
        def _():
            run(al.S)

        @pl.when(i == al.nq - 1)
        def _():
            dk_ref[...] = ak[...].astype(dk_ref.dtype)
            dv_ref[...] = av[...].astype(dv_ref.dtype)

    hbm = pl.BlockSpec(memory_space=pl.ANY)
    res = _pallas(
        body, name=name + "_bwd", grid=(al.B, al.nq),
        in_specs=[qs(wq), ks(wk), ks(wv), qs(wo), qs(LANES), qs(wo)] + [hbm] * n,
        out_specs=[qs(wq), ks(wk), ks(wv)] + [hbm] * n,
        out_shape=[jax.ShapeDtypeStruct(q.shape, F32), jax.ShapeDtypeStruct(k.shape, k.dtype),
                   jax.ShapeDtypeStruct(v.shape, v.dtype)]
        + [jax.ShapeDtypeStruct(a.shape, a.dtype) for a in partials],
        scratch_shapes=[pltpu.VMEM((al.S, wk), F32), pltpu.VMEM((al.S, wv), F32)] + (_comm_scratch(n) if n else []),
        compiler_params=_cparams(("arbitrary", "arbitrary") if n else ("parallel", "arbitrary")),
    )(q, k, v, o, lse, do, *partials)
    return res[0], res[1], res[2], list(res[3:])


def attn_op(cfg, al, name):
    @jax.custom_vjp
    def op(q, k, v):
        return _attn_fwd(q, k, v, cfg, al, name)[0]

    def fwd(q, k, v):
        o, lse, _ = _attn_fwd(q, k, v, cfg, al, name)
        return o, (q, k, v, o, lse)

    def bwd(res, do):
        return _attn_bwd(*res, do, cfg, al, name)[:3]

    op.defvjp(fwd, bwd)
    return op


def attn_gather_op(cfg, al, name):
    @jax.custom_vjp
    def op(q, k, v, *shards):
        o, _, gathered = _attn_fwd(q, k, v, cfg, al, name, [s.astype(BF16) for s in shards])
        return (o, *gathered)

    def fwd(q, k, v, *shards):
        o, lse, gathered = _attn_fwd(q, k, v, cfg, al, name, [s.astype(BF16) for s in shards])
        return (o, *gathered), (q, k, v, o, lse)

    def bwd(res, cts):
        dq_, dk_, dv_, received = _attn_bwd(*res, cts[0], cfg, al, name, list(cts[1:]))
        return (dq_, dk_, dv_, *[sum_slots(r, name + f"_sum{a}") for a, r in enumerate(received)])

    op.defvjp(fwd, bwd)
    return op


SCAN_WIDTHS = (32, 16, 8, 4, 2, 1)
N_CM = 2 + 2 * len(SCAN_WIDTHS)


def _scan_consts(reverse):
    C = CHUNK
    t = np.arange(C)[:, None]
    s = np.arange(C)[None, :]
    blocks = [(s <= t), (s > t)]
    for w in SCAN_WIDTHS:
        blocks.append((s <= t) & (s // w == t // w))
    for w in SCAN_WIDTHS:
        blocks.append((s > t) & (s // w == t // w))
    masks = [np.eye(C, dtype=bool)]
    for w in SCAN_WIDTHS:
        masks.append((t // (2 * w) == s // (2 * w)) & ((t // w) % 2 == 1) & ((s // w) % 2 == 0))
    if reverse:
        blocks = [b[::-1, ::-1] for b in blocks]
        masks = [m[::-1, ::-1] for m in masks]
    cm = np.concatenate([b.astype(np.float32) for b in blocks] + [np.ones((8, C), np.float32)], axis=0)
    mw = np.stack([np.tile(m.astype(np.float32), (1, C_HEADS)) for m in masks])
    rows = np.arange(C_HEADS * C)[:, None] // C
    lane = np.arange(C_W)[None, :] // C_DK
    hm = (rows == lane).astype(np.float32)
    bd = (np.arange(C_W)[:, None] // C_DK == lane).astype(np.float32)
    return (jnp.asarray(cm, BF16), jnp.asarray(cm.T.copy(), BF16), jnp.asarray(mw, F32),
            jnp.asarray(hm, F32), jnp.asarray(bd, F32))


def _scan_chunk(st, q, k, v, g, cm, cmt, mw, hm, bd):
    C = CHUNK
    cs = xdotl(cm, cmt, g)
    b = cs[0:C]
    rest = cs[C:2 * C]
    tot = cs[N_CM * C:N_CM * C + 1]

    def per_head(a):
        return (jnp.concatenate([a] * C_HEADS, axis=0) * hm).astype(BF16)

    a = _nt(q.astype(BF16), per_head(k)) * mw[0]
    for i in range(len(SCAN_WIDTHS)):
        eq = jnp.exp(jnp.minimum(cs[(2 + i) * C:(3 + i) * C], 0.0))
        ek = jnp.exp(jnp.minimum(cs[(2 + len(SCAN_WIDTHS) + i) * C:(3 + len(SCAN_WIDTHS) + i) * C], 0.0))
        a = a + _nt((q * eq).astype(BF16), per_head(k * ek)) * mw[i + 1]
    o = _nn(a.astype(BF16), per_head(v))
    o = o + _nt((q * jnp.exp(b)).astype(BF16), st.astype(BF16))
    st_new = st * jnp.exp(tot) + _tn(v.astype(BF16), (k * jnp.exp(rest)).astype(BF16)) * bd
    return o, st_new


class ScanLay:
    def __init__(self, B, T, Tc):
        self.B, self.S = B, T + Tc
        self.ncc, self.ntot = Tc // CHUNK, (T + Tc) // CHUNK

    def chunk(self, j, reverse):
        if not reverse:
            return j
        return jnp.where(j < self.ncc, self.ncc - 1 - j, self.ntot - 1 - (j - self.ncc))


def _scan_specs(sl, step):
    f = pl.BlockSpec((sl.B, CHUNK, C_W), lambda j: (0, sl.chunk(step(j), False), 0))
    r = pl.BlockSpec((sl.B, CHUNK, C_W), lambda j: (0, sl.chunk(step(j), True), 0))
    return f, r


def _scan_fwd(q, kf, gf, kb, gb, v, sl, name):
    B, S = sl.B, sl.S
    view = lambda a: a.reshape(B, S, C_W)
    cf, cr = _scan_consts(False), _scan_consts(True)
    nc = len(cf)
    f, r = _scan_specs(sl, lambda j: j)
    cspecs = [pl.BlockSpec(c.shape, lambda j, nd=c.ndim: (0,) * nd) for c in cf + cr]

    def body(*refs):
        (qf_ref, kf_ref, gf_ref, vf_ref, qr_ref, kr_ref, gr_ref, vr_ref), refs = refs[:8], refs[8:]
        cfv, crv = [c[...] for c in refs[:nc]], [c[...] for c in refs[nc:2 * nc]]
        of_ref, or_ref, st_ref, st = refs[2 * nc:]

        @pl.when(pl.program_id(0) == 0)
        def _():
            st[...] = jnp.zeros_like(st)

        st_ref[0] = st[...]
        dirs = ((qf_ref, kf_ref, gf_ref, vf_ref, of_ref, cfv), (qr_ref, kr_ref, gr_ref, vr_ref, or_ref, crv))
        args = [(st[d * B + b], q_[b], k_[b], v_[b], g_[b]) + tuple(cv)
                for d, (q_, k_, g_, v_, _, cv) in enumerate(dirs) for b in range(B)]
        outs = [_scan_chunk(*a) for a in args]
        for d, (_, _, _, _, o_, _) in enumerate(dirs):
            for b in range(B):
                o_[b], st[d * B + b] = outs[d * B + b]

    of, ob, states = _pallas(
        body, name=name, grid=(sl.ntot,),
        in_specs=[f] * 4 + [r] * 4 + cspecs,
        out_specs=[f, r, pl.BlockSpec((1, 2 * B, C_W, C_W), lambda j: (j, 0, 0, 0))],
        out_shape=[jax.ShapeDtypeStruct((B, S, C_W), F32)] * 2
        + [jax.ShapeDtypeStruct((sl.ntot, 2 * B, C_W, C_W), F32)],
        scratch_shapes=[pltpu.VMEM((2 * B, C_W, C_W), F32)],
        compiler_params=_cparams(("arbitrary",)),
    )(view(q), view(kf), view(gf), view(v), view(q), view(kb), view(gb), view(v), *cf, *cr)
    return of.reshape(B * S, C_W), ob.reshape(B * S, C_W), states


def _scan_bwd(q, kf, gf, kb, gb, v, states, dof, dob, sl, name):
    B, S = sl.B, sl.S
    view = lambda a: a.reshape(B, S, C_W)
    cf, cr = _scan_consts(False), _scan_consts(True)
    nc = len(cf)
    last = sl.ntot - 1
    f, r = _scan_specs(sl, lambda j: last - j)
    cspecs = [pl.BlockSpec(c.shape, lambda j, nd=c.ndim: (0,) * nd) for c in cf + cr]

    def body(*refs):
        ins, refs = refs[:11], refs[11:]
        qf_ref, kf_ref, gf_ref, vf_ref, dof_ref, qr_ref, kr_ref, gr_ref, vr_ref, dor_ref, st_ref = ins
        cfv, crv = [c[...] for c in refs[:nc]], [c[...] for c in refs[nc:2 * nc]]
        outs, dst = refs[2 * nc:-1], refs[-1]

        @pl.when(pl.program_id(0) == 0)
        def _():
            dst[...] = jnp.zeros_like(dst)

        dirs = ((qf_ref, kf_ref, gf_ref, vf_ref, dof_ref, cfv), (qr_ref, kr_ref, gr_ref, vr_ref, dor_ref, crv))
        args = [((st_ref[0, d * B + b], q_[b], k_[b], v_[b], g_[b]), (do_[b], dst[d * B + b]), cv)
                for d, (q_, k_, g_, v_, do_, cv) in enumerate(dirs) for b in range(B)]
        grads = []
        for prim, cts, cv in args:
            _, vjp = jax.vjp(lambda s_, a_, b_, c_, e_, cv=cv: _scan_chunk(s_, a_, b_, c_, e_, *cv), *prim)
            grads.append(vjp(cts))
        for d in range(2):
            dq_, dk_, dg_, dv_ = outs[4 * d:4 * d + 4]
            for b in range(B):
                dst[d * B + b], dq_[b], dk_[b], dv_[b], dg_[b] = grads[d * B + b]

    res = _pallas(
        body, name=name + "_bwd", grid=(sl.ntot,),
        in_specs=[f] * 5 + [r] * 5 + [pl.BlockSpec((1, 2 * B, C_W, C_W), lambda j: (last - j, 0, 0, 0))] + cspecs,
        out_specs=[f] * 4 + [r] * 4,
        out_shape=[jax.ShapeDtypeStruct((B, S, C_W), F32)] * 8,
        scratch_shapes=[pltpu.VMEM((2 * B, C_W, C_W), F32)],
        compiler_params=_cparams(("arbitrary",)),
    )(view(q), view(kf), view(gf), view(v), view(dof), view(q), view(kb), view(gb), view(v), view(dob),
      states, *cf, *cr)
    dq_f, dk_f, dg_f, dv_f, dq_r, dk_r, dg_r, dv_r = [a.reshape(B * S, C_W) for a in res]
    return dq_f + dq_r, dk_f, dg_f, dk_r, dg_r, dv_f + dv_r


def scan_op(sl, name):
    @jax.custom_vjp
    def op(q, kf, gf, kb, gb, v):
        return _scan_fwd(q, kf, gf, kb, gb, v, sl, name)[:2]

    def fwd(q, kf, gf, kb, gb, v):
        of, ob, states = _scan_fwd(q, kf, gf, kb, gb, v, sl, name)
        return (of, ob), (q, kf, gf, kb, gb, v, states)

    def bwd(res, cts):
        return _scan_bwd(*res, cts[0], cts[1], sl, name)

    op.defvjp(fwd, bwd)
    return op


def loss_and_grad(y, target, lay):
    N, D = y.shape

    def body(y_ref, t_ref, dy_ref, l_ref):
        i = pl.program_id(0)

        @pl.when(i == 0)
        def _():
            l_ref[...] = jnp.zeros_like(l_ref)

        is_ctx = i % lay.per < lay.nc

        @pl.when(is_ctx)
        def _():
            dy_ref[...] = jnp.zeros_like(dy_ref)

        @pl.when(jnp.logical_not(is_ctx))
        def _():
            e = y_ref[...] - t_ref[...]
            dy_ref[...] = e * (1.0 / D)
            l_ref[...] += 0.5 * jnp.sum(jnp.sum(e * e, axis=1, keepdims=True) * (1.0 / D), axis=0, keepdims=True)

    def t_index(i):
        return ((i // lay.per) * lay.nl + jnp.maximum(i % lay.per - lay.nc, 0), 0)

    dy, lp = _pallas(
        body, name="loss_head", grid=(lay.n_tiles,),
        in_specs=[pl.BlockSpec((lay.tm, D), lambda i: (i, 0)), pl.BlockSpec((lay.tm, D), t_index)],
        out_specs=[pl.BlockSpec((lay.tm, D), lambda i: (i, 0)), pl.BlockSpec((8, LANES), lambda i: (0, 0))],
        out_shape=[jax.ShapeDtypeStruct((N, D), F32), jax.ShapeDtypeStruct((8, LANES), F32)],
        compiler_params=_cparams(("arbitrary",)),
    )(y, target)
    return lp, dy


def _adam_math(w, g, m, v):
    mn = ADAM_B1 * m + (1.0 - ADAM_B1) * g
    vn = ADAM_B2 * v + (1.0 - ADAM_B2) * jnp.square(g)
    m_hat = mn / (1.0 - ADAM_B1 ** ADAM_STEP)
    v_hat = vn / (1.0 - ADAM_B2 ** ADAM_STEP)
    return -ADAM_LR * (m_hat / (jnp.sqrt(v_hat) + ADAM_EPS) + ADAM_WD * w), mn, vn


ROW_TILES = (512, 256, 128, 64, 32, 16, 8)


def adamw(w, g, m, v, name):
    R, C = w.shape
    tr = _tile(R, ROW_TILES)

    def body(w_ref, g_ref, m_ref, v_ref, d_ref, mo_ref, vo_ref):
        d_ref[...], mo_ref[...], vo_ref[...] = _adam_math(w_ref[...], g_ref[...], m_ref[...], v_ref[...])

    spec = pl.BlockSpec((tr, C), lambda i: (i, 0))
    return _pallas(
        body, name=name, grid=(R // tr,), in_specs=[spec] * 4, out_specs=[spec] * 3,
        out_shape=[jax.ShapeDtypeStruct((R, C), F32)] * 3,
        compiler_params=_cparams(("parallel",)),
    )(w, g, m, v)


def sum_slots(recv, name):
    shape = recv.shape[1:]
    C = shape[-1]
    r3 = recv.reshape(N_DEV, -1, C)
    R = r3.shape[1]
    tr = _tile(R, ROW_TILES[1:])

    def body(r_ref, o_ref):
        g = r_ref[0].astype(F32)
        for k in range(1, N_DEV):
            g = g + r_ref[k].astype(F32)
        o_ref[...] = g

    out = _pallas(
        body, name=name, grid=(R // tr,),
        in_specs=[pl.BlockSpec((N_DEV, tr, C), lambda i: (0, i, 0))],
        out_specs=pl.BlockSpec((tr, C), lambda i: (i, 0)),
        out_shape=jax.ShapeDtypeStruct((R, C), F32),
        compiler_params=_cparams(("parallel",)),
    )(r3)
    return out.reshape(shape)


def adamw_slots(w, recv, m, v, name):
    R, C = w.shape
    tr = _tile(R, ROW_TILES[1:])

    def body(w_ref, r_ref, m_ref, v_ref, g_ref, d_ref, mo_ref, vo_ref):
        g = r_ref[0].astype(F32)
        for k in range(1, N_DEV):
            g = g + r_ref[k].astype(F32)
        g_ref[...] = g
        d_ref[...], mo_ref[...], vo_ref[...] = _adam_math(w_ref[...], g, m_ref[...], v_ref[...])

    spec = pl.BlockSpec((tr, C), lambda i: (i, 0))
    return _pallas(
        body, name=name, grid=(R // tr,),
        in_specs=[spec, pl.BlockSpec((N_DEV, tr, C), lambda i: (0, i, 0)), spec, spec], out_specs=[spec] * 4,
        out_shape=[jax.ShapeDtypeStruct((R, C), F32)] * 4,
        compiler_params=_cparams(("parallel",)),
    )(w, recv, m, v)


def _me():
    return lax.axis_index("x"), lax.axis_index("y"), lax.axis_index("c")


def _gather_many(x_refs, out_refs, send_sems, recv_sems, local_sems):
    x, y, c = _me()
    me, sibling = (x, y, c), (x, y, 1 - c)
    chips = [(1 - x, y), (x, 1 - y), (1 - x, 1 - y)]
    arrs = range(len(x_refs))

    def slot(a, px, py, pc):
        return out_refs[a].at[4 * px + 2 * py + pc]

    def copy(a, k, block, to, src=None):
        return pltpu.make_async_remote_copy(
            src_ref=slot(a, *block) if src is None else src, dst_ref=slot(a, *block),
            send_sem=send_sems.at[7 * a + k], recv_sem=recv_sems.at[7 * a + k], device_id=to, device_id_type=MESH)

    mine = [pltpu.make_async_copy(x_refs[a], slot(a, *me), local_sems.at[a]) for a in arrs]
    for cp in mine:
        cp.start()
    first = []
    for a in arrs:
        first.append(copy(a, 0, me, sibling, src=x_refs[a]))
        first += [copy(a, 1 + j, me, (*chip, c), src=x_refs[a]) for j, chip in enumerate(chips)]
    for cp in first:
        cp.start()
    passed = []
    for j, chip in enumerate(chips):
        for a in arrs:
            copy(a, 1 + j, (*chip, c), me).wait_recv()
            fwd = copy(a, 4 + j, (*chip, c), sibling)
            fwd.start()
            passed.append(fwd)
    for a in arrs:
        copy(a, 0, sibling, me).wait_recv()
    for j, chip in enumerate(chips):
        for a in arrs:
            copy(a, 4 + j, (*chip, 1 - c), me).wait_recv()
    for cp in first + passed:
        cp.wait_send()
    for cp in mine:
        cp.wait()


def _comm_scratch(n):
    return [pltpu.SemaphoreType.DMA((7 * n,)), pltpu.SemaphoreType.DMA((7 * n,)), pltpu.SemaphoreType.DMA((n,))]


def small_gather(xb, name):
    R = xb.shape[0]

    def body(x_ref, out_ref, sum_ref, send_sems, recv_sems, local_sems):
        _gather_many([x_ref], [out_ref], send_sems, recv_sems, local_sems)
        acc = out_ref[0]
        for k in range(1, N_DEV):
            acc = acc + out_ref[k]
        sum_ref[...] = acc

    vm = pl.BlockSpec(memory_space=pltpu.VMEM)
    return _pallas(
        body, name=name, in_specs=[vm], out_specs=[vm, vm],
        out_shape=[jax.ShapeDtypeStruct((N_DEV, R, LANES), xb.dtype), jax.ShapeDtypeStruct((R, LANES), xb.dtype)],
        scratch_shapes=_comm_scratch(1),
        compiler_params=pltpu.CompilerParams(vmem_limit_bytes=VMEM_LIMIT),
    )(xb)


def big_gather(xs, name):
    n = len(xs)

    def body(*refs):
        _gather_many(refs[:n], refs[n:2 * n], *refs[2 * n:])

    hbm = pl.BlockSpec(memory_space=pl.ANY)
    return _pallas(
        body, name=name, in_specs=[hbm] * n, out_specs=[hbm] * n,
        out_shape=[jax.ShapeDtypeStruct((N_DEV,) + a.shape, a.dtype) for a in xs],
        scratch_shapes=_comm_scratch(n),
    )(*xs)


def scatter_exchange(gs, name):
    n = len(gs)
    rels = [(dx, dy, dc) for dx in (0, 1) for dy in (0, 1) for dc in (0, 1) if (dx, dy, dc) != (0, 0, 0)]

    def body(*refs):
        g_refs, r_refs = refs[:n], refs[n:2 * n]
        send_sems, recv_sems, local_sems = refs[2 * n:]
        x, y, c = _me()
        me = 4 * x + 2 * y + c
        mine = [pltpu.make_async_copy(g_refs[a].at[me], r_refs[a].at[me], local_sems.at[a]) for a in range(n)]
        for cp in mine:
            cp.start()
        copies = []
        for r, (dx, dy, dc) in enumerate(rels):
            px, py, pc = (x + dx) % 2, (y + dy) % 2, (c + dc) % 2
            for a in range(n):
                copies.append(pltpu.make_async_remote_copy(
                    src_ref=g_refs[a].at[4 * px + 2 * py + pc], dst_ref=r_refs[a].at[me],
                    send_sem=send_sems.at[7 * a + r], recv_sem=recv_sems.at[7 * a + r],
                    device_id=(px, py, pc), device_id_type=MESH))
        for cp in copies:
            cp.start()
        for cp in copies:
            cp.wait()
        for cp in mine:
            cp.wait()

    hbm = pl.BlockSpec(memory_space=pl.ANY)
    return _pallas(
        body, name=name, in_specs=[hbm] * n, out_specs=[hbm] * n,
        out_shape=[jax.ShapeDtypeStruct(a.shape, a.dtype) for a in gs],
        scratch_shapes=_comm_scratch(n),
    )(*gs)


def gather_and_scatter(xb, gs, name):
    R, n = xb.shape[0], len(gs)

    def body(*refs):
        x_ref, g_refs = refs[0], refs[1:1 + n]
        out_ref, sum_ref = refs[1 + n:3 + n]
        r_refs = refs[3 + n:3 + 2 * n]
        g_send, g_recv, g_local, s_send, s_recv, s_local = refs[3 + 2 * n:]
        x, y, c = _me()
        me = 4 * x + 2 * y + c
        copies = [pltpu.make_async_copy(g_refs[a].at[me], r_refs[a].at[me], s_local.at[a]) for a in range(n)]
        for r, (dx, dy, dc) in enumerate(PEER_RELATIONS):
            px, py, pc = (x + dx) % 2, (y + dy) % 2, (c + dc) % 2
            for a in range(n):
                copies.append(pltpu.make_async_remote_copy(
                    src_ref=g_refs[a].at[4 * px + 2 * py + pc], dst_ref=r_refs[a].at[me],
                    send_sem=s_send.at[7 * a + r], recv_sem=s_recv.at[7 * a + r],
                    device_id=(px, py, pc), device_id_type=MESH))
        for cp in copies:
            cp.start()
        _gather_many([x_ref], [out_ref], g_send, g_recv, g_local)
        acc = out_ref[0]
        for k in range(1, N_DEV):
            acc = acc + out_ref[k]
        sum_ref[...] = acc
        for cp in copies:
            cp.wait()

    vm, hbm = pl.BlockSpec(memory_space=pltpu.VMEM), pl.BlockSpec(memory_space=pl.ANY)
    res = _pallas(
        body, name=name, in_specs=[vm] + [hbm] * n, out_specs=[vm, vm] + [hbm] * n,
        out_shape=[jax.ShapeDtypeStruct((N_DEV, R, LANES), xb.dtype), jax.ShapeDtypeStruct((R, LANES), xb.dtype)]
        + [jax.ShapeDtypeStruct(a.shape, a.dtype) for a in gs],
        scratch_shapes=_comm_scratch(1) + _comm_scratch(n),
        compiler_params=pltpu.CompilerParams(vmem_limit_bytes=VMEM_LIMIT),
    )(xb, *gs)
    return res[0], res[1], list(res[2:])


def _pack(arrs, dtype, row_mult):
    flat = jnp.concatenate([a.astype(dtype).reshape(-1) for a in arrs])
    pad = (-flat.shape[0]) % (LANES * row_mult)
    if pad:
        flat = jnp.concatenate([flat, jnp.zeros((pad,), dtype)])
    return flat.reshape(-1, LANES)


def _unpack(buf, shapes, lead=()):
    flat = buf.reshape(*lead, -1)
    out, off = [], 0
    for s in shapes:
        n = int(np.prod(s))
        out.append(flat[..., off:off + n].reshape(*lead, *s))
        off += n
    return out


def _rope_tables(T, tm):
    pos = np.arange(T)
    row, col = pos // GRID_W, pos % GRID_W

    def tab(rot_dim):
        nf = rot_dim // 4
        inv = ROPE_THETA ** (-np.arange(nf, dtype=np.float32) / nf)
        ang = np.concatenate([row[:, None].astype(np.float32) * inv, col[:, None].astype(np.float32) * inv], axis=-1)
        ang = ang.astype(np.float32)
        cos, sin = np.cos(ang), np.sin(ang)
        return np.concatenate([cos, cos], -1), np.concatenate([-sin, sin], -1)

    c64, s64 = tab(HD)
    c32, s32 = tab(B_ROPE)
    ca, sa = np.tile(c64, (1, A_HEADS)), np.tile(s64, (1, A_HEADS))
    cb = np.concatenate([c32, np.ones((T, LANES - B_ROPE), np.float32)], -1)
    sb = np.concatenate([s32, np.zeros((T, LANES - B_ROPE), np.float32)], -1)
    one, zero = np.ones((T, B_NOPE), np.float32), np.zeros((T, B_NOPE), np.float32)
    tail1, tail0 = np.ones((T, LANES - B_NOPE - B_ROPE), np.float32), np.zeros((T, LANES - B_NOPE - B_ROPE), np.float32)
    cq = np.tile(np.concatenate([one, c32, tail1], -1), (1, B_HEADS))
    sq = np.tile(np.concatenate([zero, s32, tail0], -1), (1, B_HEADS))

    def fin(a, ident):
        return jnp.asarray(np.concatenate([a, np.full((tm, a.shape[1]), ident, np.float32)], 0), F32)

    return fin(ca, 1.0), fin(sa, 0.0), fin(cb, 1.0), fin(sb, 0.0), fin(cq, 1.0), fin(sq, 0.0)


def _swap_matrix(width, starts, half):
    p = np.zeros((width, width), np.float32)
    for s in starts:
        for i in range(half):
            p[s + i, s + half + i] = 1.0
            p[s + half + i, s + i] = 1.0
    return jnp.asarray(p, BF16)


def _seg_matrix(width):
    h = np.arange(width) // HD
    return jnp.asarray((h[:, None] == h[None, :]).astype(np.float32), BF16)


def _key_slot_matrices():
    e1 = np.zeros((B_HEADS * B_NOPE, B_HEADS * LANES), np.float32)
    e2 = np.zeros((LANES, B_HEADS * LANES), np.float32)
    for h in range(B_HEADS):
        for i in range(B_NOPE):
            e1[h * B_NOPE + i, h * LANES + i] = 1.0
        for i in range(B_ROPE):
            e2[i, h * LANES + B_NOPE + i] = 1.0
    return jnp.asarray(e1, BF16), jnp.asarray(e2, BF16)


def _f_premod(x, sh, sc, g):
    return (_rms(x, g) * (1.0 + sc) + sh,)


def _f_post(x, y, gt, g):
    return (x + gt * _rms(y, g),)


def _f_post_pre(x, y, gt, g_post, sh, sc, g_pre):
    x1 = x + gt * _rms(y, g_post)
    return x1, _rms(x1, g_pre) * (1.0 + sc) + sh


def _f_bias(raw, b):
    return (raw + b,)


def _f_silu(x):
    return (_silu(x),)


def _f_readout(of, ob, gate, gain, seg):
    return (_head_rms(of + ob, seg, gain) * _silu(gate),)


def _f_bq(bq, cq, sq, pq):
    return (bq * cq + xdotr2(bq, pq) * sq,)


def _f_bk(bkn, bkr, e1, e2):
    return (xdotr1(bkn, e1) + xdotr1(bkr, e2),)


def _make_f_feat(layer):
    def f(feat, ca, sa, cb, sb, gaq, gak, gbq, gbkv, c00, c01, c10, c11, seg, pa, pb):
        aq = _head_rms(feat[:, 0:512], seg, gaq)
        ak = _head_rms(feat[:, 512:640], seg[0:128, 0:128], gak)
        av = feat[:, 640:768]
        aq = aq * ca + xdotr2(aq, pa) * sa
        ak = ak * ca[:, 0:128] + xdotr2(ak, pa[0:128, 0:128]) * sa[:, 0:128]
        bqn = _rms(feat[:, 768:1024], gbq, B_QR)
        bkvn = _rms(feat[:, 1024:1152], gbkv)
        bkr = feat[:, 1152:1280]
        bkr = bkr * cb + xdotr2(bkr, pb) * sb
        cq = _silu(feat[:, 1280:1536])
        zf, zb = feat[:, 1536:1792], feat[:, 1792:2048]
        if layer == 0:
            lbf = lbb = 0.0
        else:
            def share(c0, c1):
                m = jnp.maximum(c0, c1)
                e0, e1 = jnp.exp(c0 - m), jnp.exp(c1 - m)
                return e1 / (e0 + e1)
            lbf, lbb = share(c00, c10), share(c01, c11)

        def gate(z, lb):
            f_ = lb + (1.0 - lb) * _sigmoid(z)
            return (1.0 - lb) * _sigmoid(-z), jnp.log(jnp.maximum(f_, F_TINY))

        kf, gf = gate(zf, lbf)
        kb, gb = gate(zb, lbb)
        return aq, ak, av, bqn, bkvn, bkr, cq, kf, gf, kb, gb, feat[:, 2048:2304], feat[:, 2304:2560]

    return f


def _pad_w_in(w):
    z = lambda n: jnp.zeros((w.shape[0], n), w.dtype)
    return jnp.concatenate([w[:, 0:960], z(64), w[:, 960:1120], z(96), w[:, 1120:2400]], axis=1)


def _pad_w_q_up(w):
    w4 = w.reshape(B_QR, B_HEADS, B_NOPE + B_ROPE)
    w4 = jnp.pad(w4, ((0, 256 - B_QR), (0, 0), (0, LANES - B_NOPE - B_ROPE)))
    return w4.reshape(256, B_HEADS * LANES)


def _split_w_kv_up(w):
    w4 = w.reshape(B_KVR, B_HEADS, B_NOPE + B_V)
    return w4[:, :, :B_NOPE].reshape(B_KVR, -1), w4[:, :, B_NOPE:].reshape(B_KVR, -1)


def _tile_gain(g, reps, width=None):
    t = jnp.tile(g, reps)
    if width is not None and width > t.shape[0]:
        t = jnp.pad(t, (0, width - t.shape[0]))
    return t[None, :]


def local_forward(dims, p):
    B, T, Tc, D = dims
    tm = min(256, Tc)
    lay_all = Lay(B, T, Tc, tm)
    ca, sa, cb, sb, cq, sq = _rope_tables(T, tm)
    seg512, seg256 = _seg_matrix(512), _seg_matrix(C_W)
    pa = _swap_matrix(512, range(0, 512, HD), HD // 2)
    pb = _swap_matrix(LANES, [0], B_ROPE // 2)
    pq = _swap_matrix(512, [h * LANES + B_NOPE for h in range(B_HEADS)], B_ROPE // 2)
    e1, e2 = _key_slot_matrices()
    sl = ScanLay(B, T, Tc)
    al_a, al_b = AttnLay(B, T, Tc, min(256, Tc)), AttnLay(B, T, Tc, min(256, Tc))
    cfg_a = (A_KV, A_GROUP, HD, HD, HD ** -0.5, 512, 1024)
    cfg_b = (B_HEADS, 1, LANES, B_V, (B_NOPE + B_ROPE) ** -0.5, 512, T + Tc)

    tok = p["tok"]
    depth = p["modraw"].shape[0]
    mods = []
    for l in range(depth):
        bias_lay = Lay(1, 8, 0, 8)
        raw8 = jnp.pad(p["modraw"][l], ((0, 8 - B - 1), (0, 0)))
        mod = ew_op(_f_bias, bias_lay, ("tok", "par"), (True, True), ((6 * D, F32),), f"l{l}_ada_bias")(
            raw8, p["b_ada"][l][None, :])[0]
        seg_rows = jnp.concatenate([mod[0:B], jnp.broadcast_to(mod[B:B + 1], (B, 6 * D))], axis=0)[:, None, :]
        mods.append([seg_rows[:, :, i * D:(i + 1) * D] for i in range(6)])

    post_pre_kinds = ("tok", "tok", "seg", "par", "seg", "seg", "par")
    h = ew_op(_f_premod, lay_all, ("tok", "seg", "seg", "par"), (True,) * 4, ((D, BF16),), "l0_premix")(
        tok, mods[0][0], mods[0][1], p["g_pre_mix"][0][None, :])[0]
    for l in range(depth):
        tag = f"l{l}_"
        sh_m, sc_m, gt_m, sh_f, sc_f, gt_f = mods[l]

        def early(n):
            return p[n][0] if l == 0 else late[n + REST][l - 1]

        feat = matmul_op(tag + "w_in")(h, _pad_w_in(early("w_in")))
        clb = p["clb"]
        feats = ew_op(
            _make_f_feat(l), lay_all,
            ("tok", "pos", "pos", "pos", "pos") + ("par",) * 11,
            (True,) + (False,) * 4 + (True,) * 8 + (False,) * 3,
            ((512, F32), (128, BF16), (128, BF16), (256, BF16), (128, BF16), (128, F32)) + ((C_W, F32),) * 7,
            tag + "feat")(
            feat, ca, sa, cb, sb,
            _tile_gain(p["a_q_norm"][l], A_HEADS), _tile_gain(p["a_k_norm"][l], A_KV),
            _tile_gain(p["b_q_norm"][l], 1, 256), _tile_gain(p["b_kv_norm"][l], 1),
            clb[0, 0][None, :], clb[0, 1][None, :], clb[1, 0][None, :], clb[1, 1][None, :],
            seg512, pa, pb)
        aq, ak, av, bqn, bkvn, bkr, cqs, kf, gf, kb, gb, cv, cgate = feats
        bq = matmul_op(tag + "w_q_up")(bqn, _pad_w_q_up(early("w_q_up")))
        bq = ew_op(_f_bq, lay_all, ("tok", "pos", "pos", "par"), (True, False, False, False), ((512, F32),),
                   tag + "bq_rope")(bq, cq, sq, pq)[0]
        w_kn, w_v = _split_w_kv_up(early("w_kv_up"))
        bkn = matmul_op(tag + "w_k_up")(bkvn, w_kn)
        bv = matmul_op(tag + "w_v_up", BF16)(bkvn, w_v)
        bk = ew_op(_f_bk, lay_all, ("tok", "tok", "par", "par"), (True, True, False, False),
                   ((B_HEADS * LANES, BF16),), tag + "bk_slots")(bkn, bkr, e1, e2)[0]

        if l == 0:
            ya, *got = attn_gather_op(cfg_a, al_a, tag + "attn_a")(aq, ak, av, *[p["shard_" + n] for n in RIDE])
            late = {n: _assemble(n, g) for n, g in zip(RIDE, got)}
        else:
            ya = attn_op(cfg_a, al_a, tag + "attn_a")(aq, ak, av)
        yb = attn_op(cfg_b, al_b, tag + "attn_b")(bq, bk, bv)
        of, ob = scan_op(sl, tag + "scan")(cqs, kf, gf, kb, gb, cv)
        lay_out = lay_all
        yc = ew_op(_f_readout, lay_out, ("tok", "tok", "tok", "par", "par"), (True, True, True, True, False),
                   ((C_W, BF16),), tag + "readout")(of, ob, cgate, _tile_gain(p["c_out_norm"][l], C_HEADS), seg256)[0]
        ycat = jnp.concatenate([ya, yb, yc], axis=1)
        mixo = matmul_op(tag + "w_out")(ycat, late["w_out"][l])
        tok, hf = ew_op(_f_post_pre, lay_out, post_pre_kinds, (True,) * 7, ((D, F32), (D, BF16)),
                        tag + "postmix_preffn")(
            tok, mixo, gt_m, p["g_post_mix"][l][None, :], sh_f, sc_f, p["g_pre_ffn"][l][None, :])

        z = matmul_op(tag + "w_ff1", BF16, relu2=True)(hf, late["w_ff1"][l])
        yf = matmul_op(tag + "w_ff2")(z, late["w_ff2"][l])
        if l + 1 < depth:
            tok, h = ew_op(_f_post_pre, lay_out, post_pre_kinds, (True,) * 7, ((D, F32), (D, BF16)),
                           tag + "postffn_premix")(
                tok, yf, gt_f, p["g_post_ffn"][l][None, :], mods[l + 1][0], mods[l + 1][1],
                p["g_pre_mix"][l + 1][None, :])
        else:
            tok = ew_op(_f_post, lay_out, ("tok", "tok", "seg", "par"), (True,) * 4, ((D, F32),), tag + "postffn")(
                tok, yf, gt_f, p["g_post_ffn"][l][None, :])[0]
    return tok


EARLY = ("w_in", "w_q_up", "w_kv_up")
LATE = ("w_out", "w_ff1", "w_ff2")
REST = "_rest"
RIDE = LATE + tuple(n + REST for n in EARLY)
COL_SHARDED = ("w_in", "w_q_up", "w_kv_up", "w_ff1")
SMALL = ("c_ctx", "b_ada", "g_pre_mix", "g_post_mix", "g_pre_ffn", "g_post_ffn", "a_q_norm", "a_k_norm",
         "b_q_norm", "b_kv_norm", "c_out_norm")
WEIGHTS = ("c_ctx", "w_ada", "b_ada", "g_pre_mix", "g_post_mix", "g_pre_ffn", "g_post_ffn", "w_in", "a_q_norm",
           "a_k_norm", "b_q_norm", "w_q_up", "b_kv_norm", "w_kv_up", "c_lower_bounds", "c_out_norm", "w_out",
           "w_ff1", "w_ff2")
SMALL_ROWS = 64


def _assemble(name, a):
    if name.removesuffix(REST) in COL_SHARDED:
        return a.transpose(1, 2, 0, 3).reshape(a.shape[1], a.shape[2], N_DEV * a.shape[3])
    return a.transpose(1, 0, 2, 3).reshape(a.shape[1], N_DEV * a.shape[2], a.shape[3])


def kernel(x, c, ctx, c_ctx, w_ada, b_ada, g_pre_mix, g_post_mix, g_pre_ffn, g_post_ffn, w_in, a_q_norm, a_k_norm, b_q_norm, w_q_up, b_kv_norm, w_kv_up, c_lower_bounds, c_out_norm, w_out, w_ff1, w_ff2, loss_target, m_c_ctx, m_w_ada, m_b_ada, m_g_pre_mix, m_g_post_mix, m_g_pre_ffn, m_g_post_ffn, m_w_in, m_a_q_norm, m_a_k_norm, m_b_q_norm, m_w_q_up, m_b_kv_norm, m_w_kv_up, m_c_lower_bounds, m_c_out_norm, m_w_out, m_w_ff1, m_w_ff2, v_c_ctx, v_w_ada, v_b_ada, v_g_pre_mix, v_g_post_mix, v_g_pre_ffn, v_g_post_ffn, v_w_in, v_a_q_norm, v_a_k_norm, v_b_q_norm, v_w_q_up, v_b_kv_norm, v_w_kv_up, v_c_lower_bounds, v_c_out_norm, v_w_out, v_w_ff1, v_w_ff2):
    W = dict(c_ctx=c_ctx, w_ada=w_ada, b_ada=b_ada, g_pre_mix=g_pre_mix, g_post_mix=g_post_mix, g_pre_ffn=g_pre_ffn,
             g_post_ffn=g_post_ffn, w_in=w_in, a_q_norm=a_q_norm, a_k_norm=a_k_norm, b_q_norm=b_q_norm,
             w_q_up=w_q_up, b_kv_norm=b_kv_norm, w_kv_up=w_kv_up, c_lower_bounds=c_lower_bounds,
             c_out_norm=c_out_norm, w_out=w_out, w_ff1=w_ff1, w_ff2=w_ff2)
    M = dict(c_ctx=m_c_ctx, w_ada=m_w_ada, b_ada=m_b_ada, g_pre_mix=m_g_pre_mix, g_post_mix=m_g_post_mix,
             g_pre_ffn=m_g_pre_ffn, g_post_ffn=m_g_post_ffn, w_in=m_w_in, a_q_norm=m_a_q_norm, a_k_norm=m_a_k_norm,
             b_q_norm=m_b_q_norm, w_q_up=m_w_q_up, b_kv_norm=m_b_kv_norm, w_kv_up=m_w_kv_up,
             c_lower_bounds=m_c_lower_bounds, c_out_norm=m_c_out_norm, w_out=m_w_out, w_ff1=m_w_ff1, w_ff2=m_w_ff2)
    V = dict(c_ctx=v_c_ctx, w_ada=v_w_ada, b_ada=v_b_ada, g_pre_mix=v_g_pre_mix, g_post_mix=v_g_post_mix,
             g_pre_ffn=v_g_pre_ffn, g_post_ffn=v_g_post_ffn, w_in=v_w_in, a_q_norm=v_a_q_norm, a_k_norm=v_a_k_norm,
             b_q_norm=v_b_q_norm, w_q_up=v_w_q_up, b_kv_norm=v_b_kv_norm, w_kv_up=v_w_kv_up,
             c_lower_bounds=v_c_lower_bounds, c_out_norm=v_c_out_norm, w_out=v_w_out, w_ff1=v_w_ff1, w_ff2=v_w_ff2)

    B, T, D = x.shape
    Tc = ctx.shape[1]
    depth = w_ada.shape[0]
    ada_cols = w_ada.shape[2]
    idx = 4 * lax.axis_index("x") + 2 * lax.axis_index("y") + lax.axis_index("c")
    n_cond = N_DEV * B
    cond_rows = -(-(n_cond + 1) // 8) * 8

    clb_cols = c_lower_bounds.shape[2]
    g1, _ = small_gather(_pack([c, c_lower_bounds], F32, 8), "gather_cond")
    c_parts, clb_parts = _unpack(g1, [c.shape, c_lower_bounds.shape], lead=(N_DEV,))
    c_all = c_parts.reshape(n_cond, D)
    clb_full = clb_parts.transpose(1, 2, 0, 3).reshape(depth, 2, N_DEV * clb_cols)

    cond_lay = Lay(1, cond_rows, 0, cond_rows)

    def ada_shard(c_ctx_, w_ada_):
        cond = jnp.concatenate([c_all, c_ctx_[None, :], jnp.zeros((cond_rows - n_cond - 1, D), F32)], axis=0)
        sc = ew_op(_f_silu, cond_lay, ("tok",), (True,), ((D, F32),), "cond_silu")(cond)[0]
        return jnp.stack([matmul_op(f"l{l}_w_ada")(sc, w_ada_[l]) for l in range(depth)])

    mod_shard, vjp_ada = jax.vjp(ada_shard, c_ctx, w_ada)

    g2, _ = small_gather(_pack([mod_shard], F32, 8), "gather_mod")
    mod_all = _unpack(g2, [mod_shard.shape], lead=(N_DEV,))[0]
    mod_all = mod_all.transpose(1, 2, 0, 3).reshape(depth, cond_rows, N_DEV * ada_cols)
    mine = lax.dynamic_slice_in_dim(mod_all, idx * B, B, axis=1)
    modraw = jnp.concatenate([mine, mod_all[:, n_cond:n_cond + 1]], axis=1)

    gathered = dict(zip(EARLY, big_gather([W[n][0:1].astype(BF16) for n in EARLY], "gather_weights")))

    dims = (B, T, Tc, D)
    small_names = [n for n in SMALL if n != "c_ctx"]
    small_in = {n: W[n] for n in small_names}
    late_in = {n: W[n] for n in LATE}
    late_in.update({n + REST: W[n][1:] for n in EARLY})

    def fwd(x_, modraw_, small_, clb_, gathered_, late_):
        p = dict(small_)
        p.update({n: _assemble(n, a) for n, a in gathered_.items()})
        p.update({"shard_" + n: a for n, a in late_.items()})
        p.update(tok=jnp.concatenate([ctx, x_], axis=1).reshape(B * (Tc + T), D), modraw=modraw_, clb=clb_)
        return local_forward(dims, p)

    y, vjp_main = jax.vjp(fwd, x, modraw, small_in, clb_full, gathered, late_in)
    loss_part, dy = loss_and_grad(y, loss_target.reshape(B * T, D), Lay(B, T, Tc, min(256, Tc)))
    dx, dmodraw, dsmall, dclb, dgathered, dlate = vjp_main(dy)

    pay3 = _pack([dmodraw] + [dsmall[n] for n in small_names] + [dclb, loss_part[0, 0:1]], F32, 8)
    g3, s3, recv_list = gather_and_scatter(pay3, [dgathered[n] for n in EARLY], "exchange_grads")
    dmod_parts = _unpack(g3, [dmodraw.shape], lead=(N_DEV,))[0]
    tot = _unpack(s3, [dmodraw.shape] + [W[n].shape for n in small_names] + [clb_full.shape, (1,)])
    dmod_tot, small_tot, dclb_tot, loss = tot[0], dict(zip(small_names, tot[1:-2])), tot[-2], tot[-1]
    drows = dmod_parts[:, :, 0:B].transpose(1, 0, 2, 3).reshape(depth, n_cond, N_DEV * ada_cols)
    dcond = jnp.concatenate(
        [drows, dmod_tot[:, B:B + 1], jnp.zeros((depth, cond_rows - n_cond - 1, N_DEV * ada_cols), F32)], axis=1)
    dmod_shard = lax.dynamic_slice_in_dim(dcond, idx * ada_cols, ada_cols, axis=2)
    dc_ctx_part, dw_ada = vjp_ada(dmod_shard)

    _, s4 = small_gather(_pack([dc_ctx_part], F32, 8), "gather_c_ctx_grad")
    small_tot["c_ctx"] = _unpack(s4, [c_ctx.shape])[0]

    recv = dict(zip(EARLY, recv_list))
    grads, delta, new_m, new_v = dict(small_tot), {}, {}, {}
    for n in EARLY:
        cols = W[n].shape[-1]
        flat = lambda a: a.reshape(-1, cols)
        first = adamw_slots(flat(W[n][0:1]), recv[n].reshape(N_DEV, -1, cols), flat(M[n][0:1]), flat(V[n][0:1]),
                            "adamw_" + n + "_first")
        g_rest = dlate[n + REST]
        rest = adamw(flat(W[n][1:]), flat(g_rest), flat(M[n][1:]), flat(V[n][1:]), "adamw_" + n + REST)
        join = lambda a, b: jnp.concatenate([a.reshape(W[n][0:1].shape), b.reshape(W[n][1:].shape)], axis=0)
        grads[n] = join(first[0], g_rest)
        delta[n], new_m[n], new_v[n] = (join(a, b) for a, b in zip(first[1:], rest))
    for n in LATE:
        shape, cols = W[n].shape, W[n].shape[-1]
        flat = lambda a: a.reshape(-1, cols)
        grads[n] = dlate[n]
        res = adamw(flat(W[n]), flat(dlate[n]), flat(M[n]), flat(V[n]), "adamw_" + n)
        delta[n], new_m[n], new_v[n] = (a.reshape(shape) for a in res)
    grads["w_ada"] = dw_ada
    grads["c_lower_bounds"] = lax.dynamic_slice_in_dim(dclb_tot, idx * clb_cols, clb_cols, axis=2)

    flat_a = lambda a: a.reshape(-1, ada_cols)
    res = adamw(flat_a(w_ada), flat_a(dw_ada), flat_a(m_w_ada), flat_a(v_w_ada), "adamw_w_ada")
    delta["w_ada"], new_m["w_ada"], new_v["w_ada"] = (a.reshape(w_ada.shape) for a in res)
    names = list(SMALL) + ["c_lower_bounds"]
    shapes = [W[n].shape for n in names]
    res = adamw(*[_pack([src[n] for n in names], F32, SMALL_ROWS) for src in (W, grads, M, V)], "adamw_small")
    for dst, buf in zip((delta, new_m, new_v), res):
        dst.update(zip(names, _unpack(buf, shapes)))

    return (loss.reshape(()), dx, *[grads[n] for n in WEIGHTS], *[delta[n] for n in WEIGHTS],
            *[new_m[n] for n in WEIGHTS], *[new_v[n] for n in WEIGHTS])
```
